```python
import jax, jax.numpy as jnp
from jax import lax
import numpy as np

D_MODEL = 1024
BATCH = 8
SEQ = 2048
DEPTH = 2
DEC_BATCH = 32
DEC_SEQ = 64
PAST_LEN = 2048

CHUNK = 64
GROUP_W = D_MODEL // 4
HEAD_DIM = 64
N_HEADS = GROUP_W // HEAD_DIM
MIX_W = 4 * GROUP_W
POOL_WINDOWS = (2, 4, 8, 16)
POOL_GROUP = GROUP_W // len(POOL_WINDOWS)
POOL_BUF = max(POOL_WINDOWS) - 1
RWKV_W_LORA = 64
RWKV_A_LORA = 64
RWKV_G_LORA = 128
RWKV_COLS = 3 * GROUP_W + RWKV_W_LORA + RWKV_A_LORA + RWKV_G_LORA
RET_COLS = 4 * GROUP_W
FOX_COLS = 4 * GROUP_W + N_HEADS
OFF_RWKV = GROUP_W
OFF_RET = OFF_RWKV + RWKV_COLS
OFF_FOX = OFF_RET + RET_COLS
N_IN = OFF_FOX + FOX_COLS
RET_GAMMA = tuple(1.0 - 2.0 ** (-5 - h) for h in range(N_HEADS))
ROPE_BASE = 10000.0
Q_BLOCK = 128
N_GROUPS = 4
EXPERTS_PER_GROUP = 4
N_EXPERTS = N_GROUPS * EXPERTS_PER_GROUP
D_EXPERT = 512
TOP_K = 2
RMS_EPS = 1e-6
RWKV_GN_EPS = 64e-5
RET_GN_EPS = 1e-5

kernel_name = 'hybrid_streaming_encoder_step'


def _f(a):
    return a.astype(jnp.float32)


def rmsnorm(x, g):
    xf = _f(x)
    y = xf * lax.rsqrt(jnp.mean(xf * xf, axis=-1, keepdims=True) + RMS_EPS)
    return (y * _f(g)).astype(x.dtype)


def head_groupnorm(x, gain, eps):
    B, T, H, d = x.shape
    xf = _f(x)
    mu = jnp.mean(xf, axis=-1, keepdims=True)
    var = jnp.mean(jnp.square(xf - mu), axis=-1, keepdims=True)
    y = (xf - mu) * lax.rsqrt(var + eps)
    return y.reshape(B, T, H * d) * _f(gain)


def rope(x, pos):
    half = x.shape[-1] // 2
    inv = ROPE_BASE ** (-jnp.arange(half, dtype=jnp.float32) / half)
    ang = pos.astype(jnp.float32)[:, None] * inv[None, :]
    cos = jnp.cos(ang)[None, :, None, :]
    sin = jnp.sin(ang)[None, :, None, :]
    x1, x2 = x[..., :half], x[..., half:]
    return jnp.concatenate([x1 * cos - x2 * sin, x1 * sin + x2 * cos], axis=-1)


def pool_mixer(u, buf, pos0, w_pool, scale):
    B, T, C = u.shape
    ext = jnp.concatenate([buf.astype(u.dtype), u], axis=1)
    cs = jnp.pad(jnp.cumsum(_f(ext), axis=1), ((0, 0), (1, 0), (0, 0)))
    end = cs[:, POOL_BUF + 1:]
    pos = pos0 + jnp.arange(T)
    means = []
    for gi, w in enumerate(POOL_WINDOWS):
        sl = slice(gi * POOL_GROUP, (gi + 1) * POOL_GROUP)
        start = cs[:, POOL_BUF + 1 - w:POOL_BUF + 1 - w + T, sl]
        cnt = jnp.minimum(w, pos + 1).astype(jnp.float32)[None, :, None]
        means.append((end[..., sl] - start) / cnt)
    d = (jnp.concatenate(means, axis=-1) - _f(u)).reshape(B, T, len(POOL_WINDOWS), POOL_GROUP)
    y = jnp.einsum('btgc,gce->btge', d, _f(w_pool)).reshape(B, T, C) * _f(scale)
    return y, ext[:, -POOL_BUF:]


def _wkv_step(S, inp):
    r_t, w_t, k_t, v_t, kk_t, a_t = inp
    sa = jnp.einsum('bhvk,bhk->bhv', S, -kk_t)
    S = (S * w_t[:, :, None, :] + sa[..., :, None] * (kk_t * a_t)[:, :, None, :]
         + v_t[..., :, None] * k_t[:, :, None, :])
    return S, jnp.einsum('bhvk,bhk->bhv', S, r_t)


def rwkv_mixer(z, shift_buf, s0, mu, w0, w2, a0, a2, g2, k_k, k_a, r_k, lnx):
    B, T, _ = z.shape
    zf = _f(z)
    prev = jnp.concatenate([_f(shift_buf), zf[:, :-1]], axis=1)
    zs = zf + (prev - zf) * _f(mu)
    c = GROUP_W
    r, k, v = zs[..., :c], zs[..., c:2 * c], zs[..., 2 * c:3 * c]
    o = 3 * c
    wl = zs[..., o:o + RWKV_W_LORA]
    o += RWKV_W_LORA
    al = zs[..., o:o + RWKV_A_LORA]
    o += RWKV_A_LORA
    gl = zs[..., o:o + RWKV_G_LORA]
    w = -jax.nn.softplus(-(_f(w0) + jnp.tanh(wl) @ _f(w2))) - 0.5
    decay = jnp.exp(-jnp.exp(w))
    a = jax.nn.sigmoid(_f(a0) + al @ _f(a2))
    g = jax.nn.sigmoid(gl) @ _f(g2)
    hs = lambda t: t.reshape(B, T, N_HEADS, HEAD_DIM)
    kk = hs(k * _f(k_k))
    kk = kk / jnp.maximum(jnp.sqrt(jnp.sum(kk * kk, axis=-1, keepdims=True)), 1e-12)
    k = k * (1.0 + (a - 1.0) * _f(k_a))
    r_h, k_h, v_h = hs(r), hs(k), hs(v)
    xs = (jnp.moveaxis(r_h, 1, 0), jnp.moveaxis(hs(decay), 1, 0), jnp.moveaxis(k_h, 1, 0),
          jnp.moveaxis(v_h, 1, 0), jnp.moveaxis(kk, 1, 0), jnp.moveaxis(hs(a), 1, 0))
    s_fin, y = lax.scan(_wkv_step, _f(s0), xs)
    y = head_groupnorm(jnp.moveaxis(y, 0, 1), lnx, RWKV_GN_EPS)
    bonus = jnp.sum(r_h * k_h * _f(r_k), axis=-1, keepdims=True) * v_h
    y = (y + bonus.reshape(B, T, c)) * g
    return y, z[:, -1:], s_fin


def retention_mixer(z, s0, pos0, gn_g):
    B, T, _ = z.shape
    zf = _f(z)
    hs = lambda t: t.reshape(B, T, N_HEADS, HEAD_DIM)
    pos = pos0 + jnp.arange(T)
    q = rope(hs(zf[..., :GROUP_W]), pos)
    k = rope(hs(zf[..., GROUP_W:2 * GROUP_W]), pos) * HEAD_DIM ** -0.5
    v = hs(zf[..., 2 * GROUP_W:3 * GROUP_W])
    g = zf[..., 3 * GROUP_W:]
    L = min(CHUNK, T)
    n = T // L
    log_g = jnp.log(jnp.array(RET_GAMMA, jnp.float32))
    idx = jnp.arange(L, dtype=jnp.float32)
    dmask = jnp.exp(log_g[:, None, None] * jnp.abs(idx[:, None] - idx[None, :]))
    qc = q.reshape(B, n, L, N_HEADS, HEAD_DIM)
    kc = k.reshape(B, n, L, N_HEADS, HEAD_DIM)
    vc = v.reshape(B, n, L, N_HEADS, HEAD_DIM)
    att = jnp.einsum('bnihd,bnjhd->bnhij', qc, kc) * dmask
    o_in = jnp.einsum('bnhij,bnjhe->bnihe', att, vc)
    kdec = jnp.exp(log_g[None, :] * (L - 1.0 - idx)[:, None])
    kv = jnp.einsum('bnjhd,bnjhe->bnhde', kc * kdec[:, :, None], vc)
    cdec = jnp.exp(log_g * L)[None, :, None, None]

    def step(s, kv_n):
        return s * cdec + kv_n, s

    s_fin, s_prev = lax.scan(step, _f(s0), jnp.moveaxis(kv, 1, 0))
    s_prev = jnp.moveaxis(s_prev, 0, 1)
    qdec = jnp.exp(log_g[None, :] * (idx + 1.0)[:, None])
    o_x = jnp.einsum('bnihd,bnhde->bnihe', qc * qdec[:, :, None], s_prev)
    o = (o_in + o_x).reshape(B, T, N_HEADS, HEAD_DIM)
    y = jax.nn.silu(g) * head_groupnorm(o, gn_g, RET_GN_EPS)
    return y, s_fin


def fox_mixer(z, k_past, v_past, lf_past, b_f):
    B, T, _ = z.shape
    dt = z.dtype
    hs = lambda t: t.reshape(B, T, N_HEADS, HEAD_DIM)
    q = hs(z[..., :GROUP_W])
    k = hs(z[..., GROUP_W:2 * GROUP_W])
    v = hs(z[..., 2 * GROUP_W:3 * GROUP_W])
    og = z[..., 3 * GROUP_W:4 * GROUP_W]
    logf = jax.nn.log_sigmoid(_f(z[..., 4 * GROUP_W:]) + _f(b_f))
    P = k_past.shape[1]
    k_all = jnp.concatenate([k_past.astype(dt), k], axis=1)
    v_all = _f(jnp.concatenate([v_past.astype(dt), v], axis=1))
    c = jnp.cumsum(jnp.concatenate([_f(lf_past), logf], axis=1), axis=1)
    c_k = jnp.transpose(c, (0, 2, 1))
    kpos = jnp.arange(P + T)
    blk = min(Q_BLOCK, T)
    nb = T // blk

    def attend(args):
        qb, cqb, qpb = args
        s = _f(jnp.einsum('bqhd,bkhd->bhqk', qb, k_all)) * HEAD_DIM ** -0.5
        s = s + jnp.transpose(cqb, (0, 2, 1))[..., None] - c_k[:, :, None, :]
        s = jnp.where(kpos[None, None, None, :] <= qpb[None, None, :, None], s, -jnp.inf)
        p = jax.nn.softmax(s, axis=-1)
        return jnp.einsum('bhqk,bkhd->bqhd', p, v_all)

    qs = jnp.moveaxis(q.reshape(B, nb, blk, N_HEADS, HEAD_DIM), 1, 0)
    cqs = jnp.moveaxis(c[:, P:].reshape(B, nb, blk, N_HEADS), 1, 0)
    qps = (P + jnp.arange(T)).reshape(nb, blk)
    o = lax.map(attend, (qs, cqs, qps))
    o = jnp.moveaxis(o, 0, 1).reshape(B, T, GROUP_W)
    y = jax.nn.sigmoid(_f(og)) * o
    return y, k, v, logf


def hier_moe(h, wc, bc, wf, bfine, w1, w3, w2):
    B, T, D = h.shape
    x = h.reshape(B * T, D)
    pc_all = jax.nn.softmax(_f(x @ wc) + _f(bc), axis=-1)
    gi = jnp.argmax(pc_all, axis=-1)
    pc = jnp.max(pc_all, axis=-1)
    gsel = jax.nn.one_hot(gi, N_GROUPS, dtype=jnp.float32)
    fine = _f(jnp.einsum('nd,gde->nge', x, wf)) + _f(bfine)
    fine = jnp.einsum('nge,ng->ne', fine, gsel)
    pf = jax.nn.softmax(fine, axis=-1)
    topv, topi = lax.top_k(pf, TOP_K)
    topv = topv / jnp.sum(topv, axis=-1, keepdims=True)
    eid = gi[:, None] * EXPERTS_PER_GROUP + topi
    gate = jnp.einsum('nke,nk->ne', jax.nn.one_hot(eid, N_EXPERTS, dtype=jnp.float32),
                      pc[:, None] * topv).astype(h.dtype)
    y = jnp.zeros_like(x)
    for e in range(N_EXPERTS):
        he = jax.nn.silu(x @ w1[e]) * (x @ w3[e])
        y = y + gate[:, e:e + 1] * (he @ w2[e])
    return y.reshape(B, T, D)


def layer(x, pos0, st, prm):
    (n1, w_in, pool_w, pool_scale, mu, w0, rw_w2, a0, a2, g2, k_k, k_a, r_k, lnx,
     ret_gn, fox_bf, w_out, n2, wc, bc, wf, bfine, e_w1, e_w3, e_w2) = prm
    pool_buf, shift_buf, s_wkv, s_ret, k_past, v_past, lf_past = st
    z = rmsnorm(x, n1) @ w_in
    y_a, pool_new = pool_mixer(z[..., :OFF_RWKV], pool_buf, pos0, pool_w, pool_scale)
    y_b, shift_new, wkv_new = rwkv_mixer(z[..., OFF_RWKV:OFF_RET], shift_buf, s_wkv, mu, w0, rw_w2,
                                         a0, a2, g2, k_k, k_a, r_k, lnx)
    y_c, ret_new = retention_mixer(z[..., OFF_RET:OFF_FOX], s_ret, pos0, ret_gn)
    y_d, k_new, v_new, lf_new = fox_mixer(z[..., OFF_FOX:], k_past, v_past, lf_past, fox_bf)
    mix = jnp.concatenate([y_a, y_b, y_c, y_d], axis=-1).astype(x.dtype)
    x = x + mix @ w_out
    x = x + hier_moe(rmsnorm(x, n2), wc, bc, wf, bfine, e_w1, e_w3, e_w2)
    dt = x.dtype
    new = (pool_new.astype(dt), shift_new.astype(dt), wkv_new.astype(dt), ret_new.astype(dt),
           k_new.astype(dt), v_new.astype(dt), lf_new.astype(dt))
    return x, new


def trunk(x, pos0, init_states, weights, norm_f):
    outs = [[] for _ in range(7)]
    for l in range(DEPTH):
        x, st = layer(x, pos0, init_states[l], [w[l] for w in weights])
        for lst, s in zip(outs, st):
            lst.append(s)
    return rmsnorm(x, norm_f), [jnp.stack(lst) for lst in outs]


def setup_inputs(seed: int = 0) -> dict:
    key = jax.random.key(seed)
    ks = jax.random.split(key, 36)

    def nrm(i, shape, s=1.0):
        return s * jax.random.normal(ks[i], shape, jnp.float32)

    H, N, G, E, NE, F = N_HEADS, HEAD_DIM, N_GROUPS, EXPERTS_PER_GROUP, N_EXPERTS, D_EXPERT
    return {
        'x_prompt': nrm(0, (BATCH, SEQ, D_MODEL)),
        'x_sample': nrm(1, (DEC_BATCH, DEC_SEQ, D_MODEL)),
        'state_pool': nrm(2, (DEPTH, DEC_BATCH, POOL_BUF, GROUP_W)),
        'state_shift': nrm(3, (DEPTH, DEC_BATCH, 1, RWKV_COLS)),
        'state_wkv': nrm(4, (DEPTH, DEC_BATCH, H, N, N), 0.5),
        'state_ret': nrm(5, (DEPTH, DEC_BATCH, H, N, N), 0.5),
        'cache_fox_k': nrm(6, (DEPTH, DEC_BATCH, PAST_LEN, H, N)),
        'cache_fox_v': nrm(7, (DEPTH, DEC_BATCH, PAST_LEN, H, N)),
        'cache_fox_logf': jax.nn.log_sigmoid(2.0 + nrm(8, (DEPTH, DEC_BATCH, PAST_LEN, H))),
        'norm1_g': 1.0 + nrm(9, (DEPTH, D_MODEL), 0.05),
        'w_in': nrm(10, (DEPTH, D_MODEL, N_IN), D_MODEL ** -0.5),
        'pool_w': nrm(11, (DEPTH, len(POOL_WINDOWS), POOL_GROUP, POOL_GROUP), POOL_GROUP ** -0.5),
        'pool_scale': 1.0 + nrm(12, (DEPTH, GROUP_W), 0.1),
        'rwkv_mu': jax.random.uniform(ks[13], (DEPTH, RWKV_COLS), jnp.float32),
        'rwkv_w0': nrm(14, (DEPTH, GROUP_W), 0.5),
        'rwkv_w2': nrm(15, (DEPTH, RWKV_W_LORA, GROUP_W), 0.5 * RWKV_W_LORA ** -0.5),
        'rwkv_a0': nrm(16, (DEPTH, GROUP_W), 0.1),
        'rwkv_a2': nrm(17, (DEPTH, RWKV_A_LORA, GROUP_W), RWKV_A_LORA ** -0.5),
        'rwkv_g2': nrm(18, (DEPTH, RWKV_G_LORA, GROUP_W), RWKV_G_LORA ** -0.5),
        'rwkv_kk': 0.85 + nrm(19, (DEPTH, GROUP_W), 0.05),
        'rwkv_ka': 1.0 + nrm(20, (DEPTH, GROUP_W), 0.05),
        'rwkv_rk': nrm(21, (DEPTH, H, N), 0.1),
        'rwkv_lnx': 1.0 + nrm(22, (DEPTH, GROUP_W), 0.05),
        'ret_gn': 1.0 + nrm(23, (DEPTH, GROUP_W), 0.05),
        'fox_bf': 2.0 + nrm(24, (DEPTH, H), 0.1),
        'w_out': nrm(25, (DEPTH, MIX_W, D_MODEL), MIX_W ** -0.5),
        'norm2_g': 1.0 + nrm(26, (DEPTH, D_MODEL), 0.05),
        'moe_wc': nrm(27, (DEPTH, D_MODEL, G), D_MODEL ** -0.5),
        'moe_bc': nrm(28, (DEPTH, G), 0.01),
        'moe_wf': nrm(29, (DEPTH, G, D_MODEL, E), D_MODEL ** -0.5),
        'moe_bf': nrm(30, (DEPTH, G, E), 0.01),
        'moe_w1': nrm(31, (DEPTH, NE, D_MODEL, F), D_MODEL ** -0.5),
        'moe_w3': nrm(32, (DEPTH, NE, D_MODEL, F), D_MODEL ** -0.5),
        'moe_w2': nrm(33, (DEPTH, NE, F, D_MODEL), F ** -0.5),
        'norm_f': 1.0 + nrm(34, (D_MODEL,), 0.05),
    }


def reference(x_prompt, x_sample, state_pool, state_shift, state_wkv, state_ret, cache_fox_k,
              cache_fox_v, cache_fox_logf, norm1_g, w_in, pool_w, pool_scale, rwkv_mu, rwkv_w0,
              rwkv_w2, rwkv_a0, rwkv_a2, rwkv_g2, rwkv_kk, rwkv_ka, rwkv_rk, rwkv_lnx, ret_gn,
              fox_bf, w_out, norm2_g, moe_wc, moe_bc, moe_wf, moe_bf, moe_w1, moe_w3, moe_w2,
              norm_f):
    weights = [norm1_g, w_in, pool_w, pool_scale, rwkv_mu, rwkv_w0, rwkv_w2, rwkv_a0, rwkv_a2,
               rwkv_g2, rwkv_kk, rwkv_ka, rwkv_rk, rwkv_lnx, ret_gn, fox_bf, w_out, norm2_g,
               moe_wc, moe_bc, moe_wf, moe_bf, moe_w1, moe_w3, moe_w2]
    B = x_prompt.shape[0]
    dt = x_prompt.dtype
    prompt_init = [(jnp.zeros((B, POOL_BUF, GROUP_W), dt),
                    jnp.zeros((B, 1, RWKV_COLS), dt),
                    jnp.zeros((B, N_HEADS, HEAD_DIM, HEAD_DIM), dt),
                    jnp.zeros((B, N_HEADS, HEAD_DIM, HEAD_DIM), dt),
                    jnp.zeros((B, 0, N_HEADS, HEAD_DIM), dt),
                    jnp.zeros((B, 0, N_HEADS, HEAD_DIM), dt),
                    jnp.zeros((B, 0, N_HEADS), dt)) for _ in range(DEPTH)]
    sample_init = [(state_pool[l], state_shift[l], state_wkv[l], state_ret[l], cache_fox_k[l],
                    cache_fox_v[l], cache_fox_logf[l]) for l in range(DEPTH)]
    past = cache_fox_k.shape[2]
    y_prompt, new_p = trunk(x_prompt, 0, prompt_init, weights, norm_f)
    y_sample, new_s = trunk(x_sample, past, sample_init, weights, norm_f)
    p_pool, p_shift, p_wkv, p_ret, p_fox_k, p_fox_v, p_fox_logf = new_p
    s_pool, s_shift, s_wkv, s_ret, s_fox_k, s_fox_v, s_fox_logf = new_s
    return (y_prompt, y_sample, p_pool, p_shift, p_wkv, p_ret, p_fox_k, p_fox_v, p_fox_logf,
            s_pool, s_shift, s_wkv, s_ret, s_fox_k, s_fox_v, s_fox_logf)
```

```python
import functools

import jax
import jax.numpy as jnp
from jax import lax
from jax.experimental import pallas as pl
from jax.experimental.pallas import tpu as pltpu

F32 = jnp.float32
BF16 = jnp.bfloat16
HIGHEST = lax.Precision.HIGHEST

D_MODEL = 1024
DEPTH = 2
CHUNK = 64
GROUP_W = 256
HEAD_DIM = 64
N_HEADS = 4
POOL_WINDOWS = (2, 4, 8, 16)
POOL_BUF = 15
POOL_PAD = 16
RWKV_COLS = 1024
N_IN = 3332
FF_PAD = 128
RET_GAMMA = tuple(1.0 - 2.0 ** (-5 - h) for h in range(N_HEADS))
ROPE_BASE = 10000.0
N_GROUPS = 4
EXPERTS_PER_GROUP = 4
N_EXPERTS = 16
D_EXPERT = 512
RMS_EPS = 1e-6
RWKV_GN_EPS = 64e-5
RET_GN_EPS = 1e-5
SUB = 16
ROUTER_PAD = 128
VMEM_LIMIT = 48 * 1024 * 1024


def _cp(sem):
    return pltpu.CompilerParams(dimension_semantics=sem, vmem_limit_bytes=VMEM_LIMIT)


def _dot(a, b):
    return jnp.dot(a.astype(BF16), b.astype(BF16), preferred_element_type=F32)


def _dot_nt(a, b):
    return lax.dot_general(a.astype(BF16), b.astype(BF16), (((1,), (1,)), ((), ())),
                           preferred_element_type=F32)


def _dot_tn(a, b):
    return lax.dot_general(a.astype(BF16), b.astype(BF16), (((0,), (0,)), ((), ())),
                           preferred_element_type=F32)


def _split(a):
    hi = a.astype(BF16)
    return hi, (a - hi.astype(F32)).astype(BF16)


def _dg3(a, b, dims):
    ah, al = _split(a)
    bh, bl = _split(b)
    dg = lambda x, y: lax.dot_general(x, y, (dims, ((), ())), preferred_element_type=F32)
    return dg(ah, bh) + (dg(ah, bl) + dg(al, bh))


class _MM:
    def __init__(self, precise):
        if precise:
            self.nn = lambda a, b: _dg3(a, b, ((1,), (0,)))
            self.nt = lambda a, b: _dg3(a, b, ((1,), (1,)))
            self.tn = lambda a, b: _dg3(a, b, ((0,), (0,)))
        else:
            self.nn, self.nt, self.tn = _dot, _dot_nt, _dot_tn


def _dot_hi(a, b):
    return jnp.dot(a, b, precision=HIGHEST, preferred_element_type=F32)


def _dot_nt_hi(a, b):
    return lax.dot_general(a, b, (((1,), (1,)), ((), ())), precision=HIGHEST,
                           preferred_element_type=F32)


def _sigmoid(x):
    return 1.0 / (1.0 + jnp.exp(-x))


def _softplus(x):
    return jnp.maximum(x, 0.0) + jnp.log(1.0 + jnp.exp(-jnp.abs(x)))


def _rmsnorm(x, g):
    return x * lax.rsqrt(jnp.mean(x * x, axis=-1, keepdims=True) + RMS_EPS) * g


def _head_groupnorm(x, seg, eps):
    mu = _dot_hi(x, seg)
    xc = x - mu
    var = _dot_hi(xc * xc, seg)
    return xc * lax.rsqrt(var + eps)


_IN_SPLITS = ((0, 1024), (1024, 2048), (2048, 3072), (3072, 3328), (3328, 3456))


def _in_kernel(x_ref, g_ref, *refs, precise):
    nw = 2 if precise else 1
    w_refs, o_refs = refs[:nw], refs[nw:]
    h = _rmsnorm(x_ref[...], g_ref[...])
    dot = lambda a, b: jnp.dot(a, b, preferred_element_type=F32)
    if precise:
        hh, hl = _split(h)
    else:
        hh = h.astype(BF16)
    for o_ref, (c0, c1) in zip(o_refs, _IN_SPLITS):
        acc = dot(hh, w_refs[0][:, c0:c1])
        if precise:
            acc = acc + (dot(hh, w_refs[1][:, c0:c1]) + dot(hl, w_refs[0][:, c0:c1]))
        o_ref[...] = acc


def _in_proj(x, g, ws):
    n = x.shape[0]
    tm = 256
    ncol = ws[0].shape[1]
    widths = [c1 - c0 for c0, c1 in _IN_SPLITS]
    return pl.pallas_call(
        functools.partial(_in_kernel, precise=len(ws) == 2),
        grid=(n // tm,),
        in_specs=[pl.BlockSpec((tm, D_MODEL), lambda i: (i, 0)),
                  pl.BlockSpec((1, D_MODEL), lambda i: (0, 0))]
                 + [pl.BlockSpec((D_MODEL, ncol), lambda i: (0, 0)) for _ in ws],
        out_specs=[pl.BlockSpec((tm, wd), lambda i: (i, 0)) for wd in widths],
        out_shape=[jax.ShapeDtypeStruct((n, wd), F32) for wd in widths],
        compiler_params=_cp(("parallel",)),
        name="in_proj",
    )(x, g, *ws)


def _pool_kernel(u_ref, buf_ref, w_ref, sc_ref, y_ref, nb_ref, ext_ref, *, tt, pos0, precise):
    mm = _MM(precise)
    j = pl.program_id(1)

    @pl.when(j == 0)
    def _():
        ext_ref[0:POOL_PAD, :] = buf_ref[...]

    u = u_ref[...]
    ext_ref[POOL_PAD:POOL_PAD + tt, :] = u
    acc = u
    sums = []
    for k in range(1, POOL_PAD):
        acc = acc + ext_ref[POOL_PAD - k:POOL_PAD - k + tt, :]
        if k + 1 in POOL_WINDOWS:
            sums.append(acc)
    lane = lax.broadcasted_iota(jnp.int32, (tt, GROUP_W), 1)
    grp = lane // (GROUP_W // len(POOL_WINDOWS))
    pos = (pos0 + j * tt + lax.broadcasted_iota(jnp.int32, (tt, GROUP_W), 0)).astype(F32)
    mean = jnp.zeros((tt, GROUP_W), F32)
    for gi, w in enumerate(POOL_WINDOWS):
        cnt = jnp.minimum(float(w), pos + 1.0)
        mean = jnp.where(grp == gi, sums[gi] / cnt, mean)
    d = mean - u
    y_ref[...] = (mm.nn(d, w_ref[...]) * sc_ref[...]).astype(y_ref.dtype)
    nb_ref[...] = ext_ref[tt + 1:tt + POOL_PAD, :]
    ext_ref[0:POOL_PAD, :] = ext_ref[tt:tt + POOL_PAD, :]


def _pool_mixer(u, buf16, w_bd, scale, pos0, precise):
    b, t, _ = u.shape
    tt = min(t, 512)
    return pl.pallas_call(
        functools.partial(_pool_kernel, tt=tt, pos0=pos0, precise=precise),
        grid=(b, t // tt),
        in_specs=[pl.BlockSpec((None, tt, GROUP_W), lambda i, j: (i, j, 0)),
                  pl.BlockSpec((None, POOL_PAD, GROUP_W), lambda i, j: (i, 0, 0)),
                  pl.BlockSpec((GROUP_W, GROUP_W), lambda i, j: (0, 0)),
                  pl.BlockSpec((1, GROUP_W), lambda i, j: (0, 0))],
        out_specs=[pl.BlockSpec((None, tt, GROUP_W), lambda i, j: (i, j, 0)),
                   pl.BlockSpec((None, POOL_BUF, GROUP_W), lambda i, j: (i, 0, 0))],
        out_shape=[jax.ShapeDtypeStruct((b, t, GROUP_W), F32 if precise else BF16),
                   jax.ShapeDtypeStruct((b, POOL_BUF, GROUP_W), F32)],
        scratch_shapes=[pltpu.VMEM((POOL_PAD + tt, GROUP_W), F32)],
        compiler_params=_cp(("parallel", "arbitrary")),
        name="pool_mixer",
    )(u, buf16, w_bd, scale)


def _rwkv_kernel(z_ref, sh_ref, s0_ref, mu_ref, w0_ref, w2_ref, a0_ref, a2_ref, g2_ref, kk_ref,
                 ka_ref, rk_ref, lnx_ref, seg_ref, tril_ref,
                 y_ref, so_ref, sho_ref, zext_ref, s_ref, ybuf_ref, *, precise):
    mm = _MM(precise)
    j = pl.program_id(1)
    L = CHUNK

    @pl.when(j == 0)
    def _():
        zext_ref[7:8, :] = sh_ref[...]
        s_ref[...] = s0_ref[...]

    z = z_ref[...]
    zext_ref[8:8 + L, :] = z
    prev = zext_ref[7:7 + L, :]
    zs = z + (prev - z) * mu_ref[...]
    c = GROUP_W
    r, k, v = zs[:, 0:c], zs[:, c:2 * c], zs[:, 2 * c:3 * c]
    wl, al, gl = zs[:, 768:832], zs[:, 832:896], zs[:, 896:1024]
    w = -_softplus(-(w0_ref[...] + _dot_hi(jnp.tanh(wl), w2_ref[...]))) - 0.5
    lw = -jnp.exp(w)
    a = _sigmoid(a0_ref[...] + _dot_hi(al, a2_ref[...]))
    g = _dot_hi(_sigmoid(gl), g2_ref[...])
    seg = seg_ref[...]
    kk = k * kk_ref[...]
    kk = kk / jnp.maximum(jnp.sqrt(_dot_hi(kk * kk, seg) * float(HEAD_DIM)), 1e-12)
    k2 = k * (1.0 + (a - 1.0) * ka_ref[...])
    bonus = _dot_hi(r * k2 * rk_ref[...], seg) * float(HEAD_DIM) * v
    b = kk * a

    cum = _dot_hi(tril_ref[...], lw)
    pend = cum[L - 1:L, :]
    e_neg = jnp.exp(-cum)
    e_end = jnp.exp(pend - cum)
    at = -kk * jnp.exp(cum - lw)
    rt = r * jnp.exp(cum)
    bt, kt = b * e_neg, k2 * e_neg
    bh, kh = b * e_end, k2 * e_end
    dend = jnp.exp(pend)

    ri = lax.broadcasted_iota(jnp.int32, (L, L), 0)
    ci = lax.broadcasted_iota(jnp.int32, (L, L), 1)
    strict, incl, eye = ri > ci, ri >= ci, ri == ci
    blk = (ri // SUB) == (ci // SUB)
    eye_f = jnp.where(eye, 1.0, 0.0)

    for h in range(N_HEADS):
        hs = slice(h * HEAD_DIM, (h + 1) * HEAD_DIM)
        ath, rth, vh = at[:, hs], rt[:, hs], v[:, hs]
        gm = mm.nt(jnp.concatenate([ath, rth], axis=0),
                   jnp.concatenate([bt[:, hs], kt[:, hs]], axis=0))
        a_ab = jnp.where(strict, gm[:L, :L], 0.0)
        a_ak = jnp.where(strict, gm[:L, L:], 0.0)
        g_b = jnp.where(incl, gm[L:, :L], 0.0)
        g_k = jnp.where(incl, gm[L:, L:], 0.0)
        dg = jnp.where(blk, a_ab, 0.0)
        aoff = jnp.where(blk, 0.0, a_ab)
        md = eye_f + dg
        dp = dg
        for _ in range(3):
            dp = mm.nn(dp, dp)
            md = md + mm.nn(md, dp)
        nm = mm.nn(md, aoff)
        nm2 = mm.nn(nm, nm)
        x = mm.nn(md, jnp.concatenate([ath, mm.nn(a_ak, vh)], axis=1))
        x = x + mm.nn(nm2, x)
        x = x + mm.nn(nm, x)
        gx = mm.nn(g_b, x)
        rw = rth + gx[:, :HEAD_DIM]
        y0 = gx[:, HEAD_DIM:] + mm.nn(g_k, vh)
        xtb = mm.tn(x, bh[:, hs])
        tm = jnp.where(eye, jnp.broadcast_to(dend[:, hs], (L, L)), 0.0) + xtb[:HEAD_DIM]
        cm = xtb[HEAD_DIM:] + mm.tn(vh, kh[:, hs])
        s_prev = s_ref[h]
        ybuf_ref[:, hs] = _dot_nt_hi(rw, s_prev) + y0
        s_ref[h] = _dot_hi(s_prev, tm) + cm

    y = _head_groupnorm(ybuf_ref[...], seg, RWKV_GN_EPS) * lnx_ref[...]
    y_ref[...] = ((y + bonus) * g).astype(y_ref.dtype)
    so_ref[...] = s_ref[...]
    sho_ref[...] = z[L - 1:L, :]
    zext_ref[7:8, :] = z[L - 1:L, :]


def _rwkv_mixer(z, shift, s0, prm, seg, tril, precise):
    b, t, _ = z.shape
    L = CHUNK
    full = lambda a: pl.BlockSpec(a.shape, lambda i, j: (0,) * a.ndim)
    st_spec = pl.BlockSpec((None, N_HEADS, HEAD_DIM, HEAD_DIM), lambda i, j: (i, 0, 0, 0))
    sh_spec = pl.BlockSpec((None, 1, RWKV_COLS), lambda i, j: (i, 0, 0))
    consts = list(prm) + [seg, tril]
    return pl.pallas_call(
        functools.partial(_rwkv_kernel, precise=precise),
        grid=(b, t // L),
        in_specs=[pl.BlockSpec((None, L, RWKV_COLS), lambda i, j: (i, j, 0)), sh_spec, st_spec]
                 + [full(a) for a in consts],
        out_specs=[pl.BlockSpec((None, L, GROUP_W), lambda i, j: (i, j, 0)), st_spec, sh_spec],
        out_shape=[jax.ShapeDtypeStruct((b, t, GROUP_W), F32 if precise else BF16),
                   jax.ShapeDtypeStruct((b, N_HEADS, HEAD_DIM, HEAD_DIM), F32),
                   jax.ShapeDtypeStruct((b, 1, RWKV_COLS), F32)],
        scratch_shapes=[pltpu.VMEM((8 + L, RWKV_COLS), F32),
                        pltpu.VMEM((N_HEADS, HEAD_DIM, HEAD_DIM), F32),
                        pltpu.VMEM((L, GROUP_W), F32)],
        compiler_params=_cp(("parallel", "arbitrary")),
        name="rwkv_mixer",
    )(z, shift, s0, *consts)


def _ret_kernel(z_ref, cos_ref, sin_ref, dm_ref, kdec_ref, qdec_ref, cdec_ref, gn_ref, seg_ref, s0_ref,
                y_ref, so_ref, s_ref, obuf_ref, *, precise):
    mm = _MM(precise)
    j = pl.program_id(1)
    L = CHUNK

    @pl.when(j == 0)
    def _():
        s_ref[...] = s0_ref[...]

    c = GROUP_W
    lane = lax.broadcasted_iota(jnp.int32, (L, c), 1)
    first_half = (lane % HEAD_DIM) < (HEAD_DIM // 2)
    cs, sn = cos_ref[...], sin_ref[...]

    def rope(x):
        partner = jnp.where(first_half, pltpu.roll(x, c - HEAD_DIM // 2, 1), pltpu.roll(x, HEAD_DIM // 2, 1))
        return x * cs + partner * sn

    q = rope(z_ref[:, 0:c])
    k = rope(z_ref[:, c:2 * c]) * HEAD_DIM ** -0.5
    v = z_ref[:, 2 * c:3 * c]
    g = z_ref[:, 3 * c:4 * c]
    kd = k * kdec_ref[...]
    qd = q * qdec_ref[...]
    for h in range(N_HEADS):
        hs = slice(h * HEAD_DIM, (h + 1) * HEAD_DIM)
        vh = v[:, hs]
        att = mm.nt(q[:, hs], k[:, hs]) * dm_ref[h]
        s_prev = s_ref[h]
        obuf_ref[:, hs] = mm.nn(att, vh) + mm.nn(qd[:, hs], s_prev)
        s_ref[h] = s_prev * cdec_ref[h] + mm.tn(kd[:, hs], vh)
    o = _head_groupnorm(obuf_ref[...], seg_ref[...], RET_GN_EPS) * gn_ref[...]
    y_ref[...] = (g * _sigmoid(g) * o).astype(y_ref.dtype)
    so_ref[...] = s_ref[...]


def _ret_mixer(z, s0, cos, sin, dmask, kdec, qdec, cdec, gn, seg, precise):
    b, t, _ = z.shape
    L = CHUNK
    full = lambda a: pl.BlockSpec(a.shape, lambda i, j: (0,) * a.ndim)
    st_spec = pl.BlockSpec((None, N_HEADS, HEAD_DIM, HEAD_DIM), lambda i, j: (i, 0, 0, 0))
    tab_spec = pl.BlockSpec((L, GROUP_W), lambda i, j: (j, 0))
    return pl.pallas_call(
        functools.partial(_ret_kernel, precise=precise),
        grid=(b, t // L),
        in_specs=[pl.BlockSpec((None, L, 4 * GROUP_W), lambda i, j: (i, j, 0)), tab_spec, tab_spec,
                  full(dmask), full(kdec), full(qdec), full(cdec), full(gn), full(seg), st_spec],
        out_specs=[pl.BlockSpec((None, L, GROUP_W), lambda i, j: (i, j, 0)), st_spec],
        out_shape=[jax.ShapeDtypeStruct((b, t, GROUP_W), F32 if precise else BF16),
                   jax.ShapeDtypeStruct((b, N_HEADS, HEAD_DIM, HEAD_DIM), F32)],
        scratch_shapes=[pltpu.VMEM((N_HEADS, HEAD_DIM, HEAD_DIM), F32),
                        pltpu.VMEM((L, GROUP_W), F32)],
        compiler_params=_cp(("parallel", "arbitrary")),
        name="ret_mixer",
    )(z, cos, sin, dmask, kdec, qdec, cdec, gn, seg, s0)


def _cumsum_lanes(x_ref, o_ref, tri, carry, width):
    blk = 128
    for c0 in range(0, width, blk):
        wd = min(blk, width - c0)
        cs = _dot_hi(x_ref[:, c0:c0 + wd], tri[:wd, :wd]) + carry
        o_ref[:, c0:c0 + wd] = cs
        carry = cs[:, wd - 1:wd]
    return carry


def _fox_gate_kernel(*refs, past, t):
    if past:
        zf_ref, b_ref, tri_ref, lfp_ref, lf_ref, cn_ref, cp_ref = refs
    else:
        zf_ref, b_ref, tri_ref, lf_ref, cn_ref = refs
    tri = tri_ref[...]
    x = zf_ref[...] + b_ref[...]
    lf_ref[...] = jnp.minimum(x, 0.0) - jnp.log(1.0 + jnp.exp(-jnp.abs(x)))
    carry = jnp.zeros((zf_ref.shape[0], 1), F32)
    if past:
        carry = _cumsum_lanes(lfp_ref, cp_ref, tri, carry, past)
    _cumsum_lanes(lf_ref, cn_ref, tri, carry, t)


def _fox_gate(zf_t, bias_rows, tri, lf_past_t):
    rows, t = zf_t.shape
    rt = min(rows, 32)
    past = 0 if lf_past_t is None else lf_past_t.shape[1]
    row_spec = lambda wd: pl.BlockSpec((rt, wd), lambda i: (i, 0))
    in_specs = [row_spec(t), row_spec(1), pl.BlockSpec(tri.shape, lambda i: (0, 0))]
    out_specs = [row_spec(t), row_spec(t)]
    out_shape = [jax.ShapeDtypeStruct((rows, t), F32), jax.ShapeDtypeStruct((rows, t), F32)]
    args = [zf_t, bias_rows, tri]
    if past:
        in_specs.append(row_spec(past))
        out_specs.append(row_spec(past))
        out_shape.append(jax.ShapeDtypeStruct((rows, past), F32))
        args.append(lf_past_t)
    return pl.pallas_call(
        functools.partial(_fox_gate_kernel, past=past, t=t),
        grid=(rows // rt,),
        in_specs=in_specs, out_specs=out_specs, out_shape=out_shape,
        compiler_params=_cp(("parallel",)),
        name="fox_gate",
    )(*args)


def _fox_attn_kernel(*refs, nkp, tq, tkn, precise):
    if nkp:
        (q_ref, kn_ref, vn_ref, og_ref, cq_ref, ckn_ref, kp_ref, vp_ref, ckp_ref,
         y_ref, m_ref, l_ref, acc_ref) = refs
    else:
        q_ref, kn_ref, vn_ref, og_ref, cq_ref, ckn_ref, y_ref, m_ref, l_ref, acc_ref = refs
    mm = _MM(precise)
    i, j = pl.program_id(1), pl.program_id(2)
    nk = pl.num_programs(2)

    @pl.when(j == 0)
    def _():
        m_ref[...] = jnp.full(m_ref.shape, -jnp.inf, F32)
        l_ref[...] = jnp.zeros(l_ref.shape, F32)
        acc_ref[...] = jnp.zeros(acc_ref.shape, F32)

    def block(k_ref, v_ref, ck_ref, mask):
        q = q_ref[...]
        k = k_ref[...]
        v = v_ref[...]
        cq = cq_ref[...]
        ck = ck_ref[...]
        for h in range(N_HEADS):
            hs = slice(h * HEAD_DIM, (h + 1) * HEAD_DIM)
            s = mm.nt(q[:, hs], k[:, hs]) * HEAD_DIM ** -0.5 + cq[:, h:h + 1] - ck[h:h + 1, :]
            if mask is not None:
                s = jnp.where(mask, s, -jnp.inf)
            m_prev = m_ref[h]
            m_new = jnp.maximum(m_prev, jnp.max(s, axis=1, keepdims=True))
            alpha = jnp.exp(m_prev - m_new)
            p = jnp.exp(s - m_new)
            l_ref[h] = alpha * l_ref[h] + jnp.sum(p, axis=1, keepdims=True)
            acc_ref[h] = alpha * acc_ref[h] + mm.nn(p, v[:, hs])
            m_ref[h] = m_new

    if nkp:
        @pl.when(j < nkp)
        def _():
            block(kp_ref, vp_ref, ckp_ref, None)

    jn = j - nkp

    @pl.when(jnp.logical_and(jn >= 0, jn <= i))
    def _():
        rowpos = i * tq + lax.broadcasted_iota(jnp.int32, (tq, tkn), 0)
        colpos = jn * tkn + lax.broadcasted_iota(jnp.int32, (tq, tkn), 1)
        block(kn_ref, vn_ref, ckn_ref, colpos <= rowpos)

    @pl.when(j == nk - 1)
    def _():
        og = og_ref[...]
        for h in range(N_HEADS):
            hs = slice(h * HEAD_DIM, (h + 1) * HEAD_DIM)
            o = acc_ref[h] / l_ref[h]
            y_ref[:, hs] = (_sigmoid(og[:, hs]) * o).astype(y_ref.dtype)


def _fox_attn(zfox, cq, ck_new, precise, k_past=None, v_past=None, ck_past=None):
    b, t, _ = zfox.shape
    tq = tkn = min(t, 256)
    nq = t // tq
    nkp = 0
    tkp = 512
    if k_past is not None:
        nkp = k_past.shape[1] // tkp
    jn = lambda i, j: jnp.minimum(jnp.maximum(j - nkp, 0), i)
    jp = lambda j: jnp.minimum(j, nkp - 1)
    in_specs = [pl.BlockSpec((None, tq, GROUP_W), lambda bb, i, j: (bb, i, 0)),
                pl.BlockSpec((None, tkn, GROUP_W), lambda bb, i, j: (bb, jn(i, j), 1)),
                pl.BlockSpec((None, tkn, GROUP_W), lambda bb, i, j: (bb, jn(i, j), 2)),
                pl.BlockSpec((None, tq, GROUP_W), lambda bb, i, j: (bb, i, 3)),
                pl.BlockSpec((None, tq, N_HEADS), lambda bb, i, j: (bb, i, 0)),
                pl.BlockSpec((None, N_HEADS, tkn), lambda bb, i, j: (bb, 0, jn(i, j)))]
    args = [zfox, zfox, zfox, zfox, cq, ck_new]
    if nkp:
        in_specs += [pl.BlockSpec((None, tkp, GROUP_W), lambda bb, i, j: (bb, jp(j), 0)),
                     pl.BlockSpec((None, tkp, GROUP_W), lambda bb, i, j: (bb, jp(j), 0)),
                     pl.BlockSpec((None, N_HEADS, tkp), lambda bb, i, j: (bb, 0, jp(j)))]
        args += [k_past, v_past, ck_past]
    return pl.pallas_call(
        functools.partial(_fox_attn_kernel, nkp=nkp, tq=tq, tkn=tkn, precise=precise),
        grid=(b, nq, nkp + nq),
        in_specs=in_specs,
        out_specs=pl.BlockSpec((None, tq, GROUP_W), lambda bb, i, j: (bb, i, 0)),
        out_shape=jax.ShapeDtypeStruct((b, t, GROUP_W), F32 if precise else BF16),
        scratch_shapes=[pltpu.VMEM((N_HEADS, tq, 1), F32), pltpu.VMEM((N_HEADS, tq, 1), F32),
                        pltpu.VMEM((N_HEADS, tq, HEAD_DIM), F32)],
        compiler_params=_cp(("parallel", "parallel", "arbitrary")),
        name="fox_attn",
    )(*args)


def _out_kernel(x_ref, ya_ref, yb_ref, yc_ref, yd_ref, *refs, precise):
    w_refs, o_ref = refs[:-1], refs[-1]
    dot = lambda a, b: jnp.dot(a, b, preferred_element_type=F32)
    acc = x_ref[...]
    for i, y_ref in enumerate((ya_ref, yb_ref, yc_ref, yd_ref)):
        if precise:
            yh, yl = _split(y_ref[...])
            acc = acc + (dot(yh, w_refs[0][i]) + (dot(yh, w_refs[1][i]) + dot(yl, w_refs[0][i])))
        else:
            acc = acc + dot(y_ref[...], w_refs[0][i])
    o_ref[...] = acc


def _out_proj(x, ys, ws):
    n = x.shape[0]
    tm = 512
    y_spec = pl.BlockSpec((tm, GROUP_W), lambda i: (i, 0))
    x_spec = pl.BlockSpec((tm, D_MODEL), lambda i: (i, 0))
    return pl.pallas_call(
        functools.partial(_out_kernel, precise=len(ws) == 2),
        grid=(n // tm,),
        in_specs=[x_spec, y_spec, y_spec, y_spec, y_spec]
                 + [pl.BlockSpec(w.shape, lambda i: (0, 0, 0)) for w in ws],
        out_specs=x_spec,
        out_shape=jax.ShapeDtypeStruct((n, D_MODEL), F32),
        compiler_params=_cp(("parallel",)),
        name="out_proj",
    )(x, *ys, *ws)


def _router_kernel(x_ref, g_ref, w_ref, b_ref, h_ref, gate_ref):
    h = _rmsnorm(x_ref[...], g_ref[...])
    h_ref[...] = h.astype(BF16)
    logits = _dot_hi(h, w_ref[...]) + b_ref[...]
    tm = logits.shape[0]
    lane_i = lax.broadcasted_iota(jnp.int32, (tm, ROUTER_PAD), 1)
    lane = lane_i.astype(F32)
    big = float(ROUTER_PAD)
    rmax = lambda a: jnp.max(a, axis=1, keepdims=True)
    rsum = lambda a: jnp.sum(a, axis=1, keepdims=True)
    first = lambda m: jnp.min(jnp.where(m, lane, big), axis=1, keepdims=True)

    is_c = jnp.logical_and(lane_i >= N_EXPERTS, lane_i < N_EXPERTS + N_GROUPS)
    lc = jnp.where(is_c, logits, -jnp.inf)
    ec = jnp.exp(lc - rmax(lc))
    pc_all = ec / rsum(ec)
    pc = rmax(pc_all)
    gi = first(jnp.logical_and(is_c, pc_all == pc)) - float(N_EXPERTS)

    is_f = (lane_i // EXPERTS_PER_GROUP).astype(F32) == gi
    lf = jnp.where(is_f, logits, -jnp.inf)
    ef = jnp.exp(lf - rmax(lf))
    pf = jnp.where(is_f, ef / rsum(ef), -1.0)
    t1 = rmax(pf)
    i1 = first(pf == t1)
    pf2 = jnp.where(lane == i1, -1.0, pf)
    t2 = rmax(pf2)
    i2 = first(pf2 == t2)
    den = t1 + t2
    gate_ref[...] = jnp.where(lane == i1, pc * (t1 / den), jnp.where(lane == i2, pc * (t2 / den), 0.0))


def _router(x, g, wr, br):
    n = x.shape[0]
    tm = 512
    return pl.pallas_call(
        _router_kernel,
        grid=(n // tm,),
        in_specs=[pl.BlockSpec((tm, D_MODEL), lambda i: (i, 0)),
                  pl.BlockSpec((1, D_MODEL), lambda i: (0, 0)),
                  pl.BlockSpec((D_MODEL, ROUTER_PAD), lambda i: (0, 0)),
                  pl.BlockSpec((1, ROUTER_PAD), lambda i: (0, 0))],
        out_specs=[pl.BlockSpec((tm, D_MODEL), lambda i: (i, 0)),
                   pl.BlockSpec((tm, ROUTER_PAD), lambda i: (i, 0))],
        out_shape=[jax.ShapeDtypeStruct((n, D_MODEL), BF16),
                   jax.ShapeDtypeStruct((n, ROUTER_PAD), F32)],
        compiler_params=_cp(("parallel",)),
        name="moe_router",
    )(x, g, wr, br)


def _moe_kernel(x_ref, h_ref, gate_ref, w1_ref, w3_ref, w2_ref, nf_ref, o_ref, *, final):
    e = pl.program_id(1)

    @pl.when(e == 0)
    def _():
        o_ref[...] = x_ref[...]

    h = h_ref[...]
    a = jnp.dot(h, w1_ref[...], preferred_element_type=F32)
    b = jnp.dot(h, w3_ref[...], preferred_element_type=F32)
    he = (a * _sigmoid(a) * b).astype(BF16)
    ye = jnp.dot(he, w2_ref[...], preferred_element_type=F32)
    gate = gate_ref[...]
    lane = lax.broadcasted_iota(jnp.int32, gate.shape, 1)
    gcol = jnp.sum(jnp.where(lane == e, gate, 0.0), axis=1, keepdims=True)
    o_ref[...] += gcol * ye

    if final:
        @pl.when(e == N_EXPERTS - 1)
        def _():
            o_ref[...] = _rmsnorm(o_ref[...], nf_ref[...])


def _moe(x, h, gate, w1, w3, w2, nf, final):
    n = x.shape[0]
    tm = 512
    return pl.pallas_call(
        functools.partial(_moe_kernel, final=final),
        grid=(n // tm, N_EXPERTS),
        in_specs=[pl.BlockSpec((tm, D_MODEL), lambda i, e: (i, 0)),
                  pl.BlockSpec((tm, D_MODEL), lambda i, e: (i, 0)),
                  pl.BlockSpec((tm, ROUTER_PAD), lambda i, e: (i, 0)),
                  pl.BlockSpec((None, D_MODEL, D_EXPERT), lambda i, e: (e, 0, 0)),
                  pl.BlockSpec((None, D_MODEL, D_EXPERT), lambda i, e: (e, 0, 0)),
                  pl.BlockSpec((None, D_EXPERT, D_MODEL), lambda i, e: (e, 0, 0)),
                  pl.BlockSpec((1, D_MODEL), lambda i, e: (0, 0))],
        out_specs=pl.BlockSpec((tm, D_MODEL), lambda i, e: (i, 0)),
        out_shape=jax.ShapeDtypeStruct((n, D_MODEL), F32),
        compiler_params=_cp(("parallel", "arbitrary")),
        name="moe_experts",
    )(x, h, gate, w1, w3, w2, nf)


def _prep_layer(l, p):
    w_in = p["w_in"][l]
    o_rwkv, o_ret, o_fox = GROUP_W, GROUP_W + RWKV_COLS, GROUP_W + RWKV_COLS + 4 * GROUP_W
    ff = jnp.pad(w_in[:, o_fox + 4 * GROUP_W:], ((0, 0), (0, FF_PAD - N_HEADS)))
    w_in_r = jnp.concatenate([w_in[:, o_rwkv:o_ret], w_in[:, o_ret:o_fox],
                              w_in[:, o_fox:o_fox + 4 * GROUP_W], w_in[:, :GROUP_W], ff], axis=1)
    pw = p["pool_w"][l]
    pg = GROUP_W // len(POOL_WINDOWS)
    w_bd = jnp.zeros((GROUP_W, GROUP_W), F32)
    for gi in range(len(POOL_WINDOWS)):
        w_bd = w_bd.at[gi * pg:(gi + 1) * pg, gi * pg:(gi + 1) * pg].set(pw[gi])
    row = lambda a: a.reshape(1, -1)
    rwkv = (row(p["rwkv_mu"][l]), row(p["rwkv_w0"][l]), p["rwkv_w2"][l], row(p["rwkv_a0"][l]),
            p["rwkv_a2"][l], p["rwkv_g2"][l], row(p["rwkv_kk"][l]), row(p["rwkv_ka"][l]),
            row(p["rwkv_rk"][l]), row(p["rwkv_lnx"][l]))
    wf = jnp.transpose(p["moe_wf"][l], (1, 0, 2)).reshape(D_MODEL, N_EXPERTS)
    wr = jnp.pad(jnp.concatenate([wf, p["moe_wc"][l]], axis=1),
                 ((0, 0), (0, ROUTER_PAD - N_EXPERTS - N_GROUPS)))
    br = jnp.pad(jnp.concatenate([p["moe_bf"][l].reshape(-1), p["moe_bc"][l]]),
                 (0, ROUTER_PAD - N_EXPERTS - N_GROUPS)).reshape(1, ROUTER_PAD)
    w_out4 = p["w_out"][l].reshape(4, GROUP_W, D_MODEL)
    precise = l < p["w_in"].shape[0] - 1
    split = lambda w: _split(w) if precise else (w.astype(BF16),)
    return dict(
        precise=precise, n1=row(p["norm1_g"][l]), w_in=split(w_in_r), pool_w=w_bd,
        pool_scale=row(p["pool_scale"][l]),
        rwkv=rwkv, ret_gn=row(p["ret_gn"][l]), fox_bf=p["fox_bf"][l],
        w_out=split(w_out4), n2=row(p["norm2_g"][l]),
        wr=wr, br=br, w1=p["moe_w1"][l].astype(BF16), w3=p["moe_w3"][l].astype(BF16),
        w2=p["moe_w2"][l].astype(BF16))


def _tables(t, pos0):
    half = HEAD_DIM // 2
    inv = ROPE_BASE ** (-jnp.arange(half, dtype=F32) / half)
    ang = (pos0 + jnp.arange(t)).astype(F32)[:, None] * inv[None, :]
    cos, sin = jnp.cos(ang), jnp.sin(ang)
    cos_t = jnp.tile(jnp.concatenate([cos, cos], axis=1), (1, N_HEADS))
    sin_t = jnp.tile(jnp.concatenate([-sin, sin], axis=1), (1, N_HEADS))
    L = CHUNK
    log_g = jnp.log(jnp.array(RET_GAMMA, F32))
    idx = jnp.arange(L, dtype=F32)
    dmask = jnp.exp(log_g[:, None, None] * jnp.abs(idx[:, None] - idx[None, :]))
    lanes = lambda a: jnp.repeat(a, HEAD_DIM, axis=1)
    kdec = lanes(jnp.exp(log_g[None, :] * (L - 1.0 - idx)[:, None]))
    qdec = lanes(jnp.exp(log_g[None, :] * (idx + 1.0)[:, None]))
    cdec = jnp.broadcast_to(jnp.exp(log_g * L)[:, None, None], (N_HEADS, HEAD_DIM, HEAD_DIM))
    hid = jnp.arange(GROUP_W) // HEAD_DIM
    seg = jnp.where(hid[:, None] == hid[None, :], 1.0 / HEAD_DIM, 0.0).astype(F32)
    tril = jnp.tril(jnp.ones((L, L), F32))
    triu = jnp.triu(jnp.ones((128, 128), F32))
    return dict(cos=cos_t, sin=sin_t, dmask=dmask, kdec=kdec, qdec=qdec, cdec=cdec, seg=seg,
                tril=tril, triu=triu)


def _trunk(x, pos0, states, layers, norm_f):
    b, t, _ = x.shape
    n = b * t
    tb = _tables(t, pos0)
    xf = x.reshape(n, D_MODEL)
    nf = norm_f.reshape(1, D_MODEL)
    outs = [[] for _ in range(7)]
    for l, lp in enumerate(layers):
        pool_buf, shift, wkv, ret, k_past, v_past, lf_past = states[l]
        precise = lp["precise"]
        z_rwkv, z_ret, z_fox, z_pool, z_ff = _in_proj(xf, lp["n1"], lp["w_in"])
        buf16 = jnp.pad(pool_buf, ((0, 0), (POOL_PAD - POOL_BUF, 0), (0, 0)))
        y_a, pool_new = _pool_mixer(z_pool.reshape(b, t, GROUP_W), buf16, lp["pool_w"], lp["pool_scale"], pos0,
                                    precise)
        y_b, wkv_new, shift_new = _rwkv_mixer(z_rwkv.reshape(b, t, RWKV_COLS), shift, wkv, lp["rwkv"],
                                              tb["seg"], tb["tril"], precise)
        y_c, ret_new = _ret_mixer(z_ret.reshape(b, t, 4 * GROUP_W), ret, tb["cos"], tb["sin"], tb["dmask"],
                                  tb["kdec"], tb["qdec"], tb["cdec"], lp["ret_gn"], tb["seg"], precise)
        zf_t = jnp.transpose(z_ff[:, :N_HEADS].reshape(b, t, N_HEADS), (0, 2, 1)).reshape(b * N_HEADS, t)
        bias_rows = jnp.tile(lp["fox_bf"], b).reshape(b * N_HEADS, 1)
        zfox3 = z_fox.reshape(b, t, 4 * GROUP_W)
        if k_past is None:
            lf_t, c_new = _fox_gate(zf_t, bias_rows, tb["triu"], None)
            cn3 = c_new.reshape(b, N_HEADS, t)
            y_d = _fox_attn(zfox3, jnp.transpose(cn3, (0, 2, 1)), cn3, precise)
        else:
            p = k_past.shape[1]
            lfp_t = jnp.transpose(lf_past, (0, 2, 1)).reshape(b * N_HEADS, p)
            lf_t, c_new, c_past = _fox_gate(zf_t, bias_rows, tb["triu"], lfp_t)
            cn3 = c_new.reshape(b, N_HEADS, t)
            y_d = _fox_attn(zfox3, jnp.transpose(cn3, (0, 2, 1)), cn3, precise,
                            k_past.reshape(b, p, GROUP_W), v_past.reshape(b, p, GROUP_W),
                            c_past.reshape(b, N_HEADS, p))
        x1 = _out_proj(xf, [y.reshape(n, GROUP_W) for y in (y_a, y_b, y_c, y_d)], lp["w_out"])
        h, gate = _router(x1, lp["n2"], lp["wr"], lp["br"])
        xf = _moe(x1, h, gate, lp["w1"], lp["w3"], lp["w2"], nf, final=(l == len(layers) - 1))
        k_new = zfox3[:, :, GROUP_W:2 * GROUP_W].reshape(b, t, N_HEADS, HEAD_DIM)
        v_new = zfox3[:, :, 2 * GROUP_W:3 * GROUP_W].reshape(b, t, N_HEADS, HEAD_DIM)
        lf_new = jnp.transpose(lf_t.reshape(b, N_HEADS, t), (0, 2, 1))
        for lst, s in zip(outs, (pool_new, shift_new, wkv_new, ret_new, k_new, v_new, lf_new)):
            lst.append(s)
    return xf.reshape(b, t, D_MODEL), [jnp.stack(lst) for lst in outs]


def kernel(x_prompt, x_sample, state_pool, state_shift, state_wkv, state_ret, cache_fox_k, cache_fox_v, cache_fox_logf, norm1_g, w_in, pool_w, pool_scale, rwkv_mu, rwkv_w0, rwkv_w2, rwkv_a0, rwkv_a2, rwkv_g2, rwkv_kk, rwkv_ka, rwkv_rk, rwkv_lnx, ret_gn, fox_bf, w_out, norm2_g, moe_wc, moe_bc, moe_wf, moe_bf, moe_w1, moe_w3, moe_w2, norm_f):
    p = dict(norm1_g=norm1_g, w_in=w_in, pool_w=pool_w, pool_scale=pool_scale, rwkv_mu=rwkv_mu,
             rwkv_w0=rwkv_w0, rwkv_w2=rwkv_w2, rwkv_a0=rwkv_a0, rwkv_a2=rwkv_a2, rwkv_g2=rwkv_g2,
             rwkv_kk=rwkv_kk, rwkv_ka=rwkv_ka, rwkv_rk=rwkv_rk, rwkv_lnx=rwkv_lnx, ret_gn=ret_gn,
             fox_bf=fox_bf, w_out=w_out, norm2_g=norm2_g, moe_wc=moe_wc, moe_bc=moe_bc, moe_wf=moe_wf,
             moe_bf=moe_bf, moe_w1=moe_w1, moe_w3=moe_w3, moe_w2=moe_w2)
    depth = w_in.shape[0]
    layers = [_prep_layer(l, p) for l in range(depth)]
    b = x_prompt.shape[0]
    dt = x_prompt.dtype
    prompt_init = [(jnp.zeros((b, POOL_BUF, GROUP_W), dt), jnp.zeros((b, 1, RWKV_COLS), dt),
                    jnp.zeros((b, N_HEADS, HEAD_DIM, HEAD_DIM), dt),
                    jnp.zeros((b, N_HEADS, HEAD_DIM, HEAD_DIM), dt), None, None, None)
                   for _ in range(depth)]
    sample_init = [(state_pool[l], state_shift[l], state_wkv[l], state_ret[l], cache_fox_k[l],
                    cache_fox_v[l], cache_fox_logf[l]) for l in range(depth)]
    past = cache_fox_k.shape[2]
    y_prompt, new_p = _trunk(x_prompt, 0, prompt_init, layers, norm_f)
    y_sample, new_s = _trunk(x_sample, past, sample_init, layers, norm_f)
    return (y_prompt, y_sample, *new_p, *new_s)
```

```python
import functools

import jax
import jax.numpy as jnp
from jax import lax
from jax.experimental import pallas as pl
from jax.experimental.pallas import tpu as pltpu

F32 = jnp.float32
BF16 = jnp.bfloat16
HIGHEST = lax.Precision.HIGHEST

D_MODEL = 1024
DEPTH = 2
CHUNK = 64
GROUP_W = 256
HEAD_DIM = 64
N_HEADS = 4
POOL_WINDOWS = (2, 4, 8, 16)
POOL_BUF = 15
POOL_PAD = 16
RWKV_COLS = 1024
N_IN = 3332
FF_PAD = 128
RET_GAMMA = tuple(1.0 - 2.0 ** (-5 - h) for h in range(N_HEADS))
ROPE_BASE = 10000.0
N_GROUPS = 4
EXPERTS_PER_GROUP = 4
N_EXPERTS = 16
D_EXPERT = 512
RMS_EPS = 1e-6
RWKV_GN_EPS = 64e-5
RET_GN_EPS = 1e-5
SUB = 16
ROUTER_PAD = 128
VMEM_LIMIT = 48 * 1024 * 1024


def _cp(sem):
    return pltpu.CompilerParams(dimension_semantics=sem, vmem_limit_bytes=VMEM_LIMIT)


def _dot(a, b):
    return jnp.dot(a.astype(BF16), b.astype(BF16), preferred_element_type=F32)


def _dot_nt(a, b):
    return lax.dot_general(a.astype(BF16), b.astype(BF16), (((1,), (1,)), ((), ())),
                           preferred_element_type=F32)


def _dot_tn(a, b):
    return lax.dot_general(a.astype(BF16), b.astype(BF16), (((0,), (0,)), ((), ())),
                           preferred_element_type=F32)


def _split(a):
    hi = a.astype(BF16)
    return hi, (a - hi.astype(F32)).astype(BF16)


def _split_bits(w):
    bits = lax.bitcast_convert_type(w, jnp.uint32) & jnp.uint32(0xFFFF0000)
    hi = lax.bitcast_convert_type(bits, F32)
    return hi.astype(BF16), (w - hi).astype(BF16)


def _dg3(a, b, dims):
    ah, al = _split(a)
    bh, bl = _split(b)
    dg = lambda x, y: lax.dot_general(x, y, (dims, ((), ())), preferred_element_type=F32)
    return dg(ah, bh) + (dg(ah, bl) + dg(al, bh))


class _MM:
    def __init__(self, precise):
        if precise:
            self.nn = lambda a, b: _dg3(a, b, ((1,), (0,)))
            self.nt = lambda a, b: _dg3(a, b, ((1,), (1,)))
            self.tn = lambda a, b: _dg3(a, b, ((0,), (0,)))
        else:
            self.nn, self.nt, self.tn = _dot, _dot_nt, _dot_tn


def _pieces(a, n):
    out = []
    for i in range(n):
        p = a.astype(BF16)
        out.append(p)
        if i + 1 < n:
            a = a - p.astype(F32)
    return out


def _dot_pieces(a, b, n, lhs_exact):
    dot = lambda x, y: jnp.dot(x, y, preferred_element_type=F32)
    terms = [dot(a, p) for p in _pieces(b, n)] if lhs_exact else [dot(p, b) for p in _pieces(a, n)]
    acc = terms[-1]
    for t in terms[-2::-1]:
        acc = acc + t
    return acc


def _dot_hi(a, b):
    return jnp.dot(a, b, precision=HIGHEST, preferred_element_type=F32)


def _dot_nt_hi(a, b):
    return lax.dot_general(a, b, (((1,), (1,)), ((), ())), precision=HIGHEST,
                           preferred_element_type=F32)


def _sigmoid(x):
    return 1.0 / (1.0 + jnp.exp(-x))


def _softplus(x):
    return jnp.maximum(x, 0.0) + jnp.log(1.0 + jnp.exp(-jnp.abs(x)))


def _rmsnorm(x, g):
    return x * lax.rsqrt(jnp.mean(x * x, axis=-1, keepdims=True) + RMS_EPS) * g


_IN_SPLITS = ((0, 1024), (1024, 2048), (2048, 3072), (3072, 3328), (3328, 3456))


def _in_kernel(x_ref, g_ref, *refs, precise):
    nw = 2 if precise else 1
    w_refs, o_refs = refs[:nw], refs[nw:]
    h = _rmsnorm(x_ref[...], g_ref[...])
    dot = lambda a, b: jnp.dot(a, b, preferred_element_type=F32)
    if precise:
        hh, hl = _split(h)
    else:
        hh = h.astype(BF16)
    for o_ref, (c0, c1) in zip(o_refs, _IN_SPLITS):
        acc = dot(hh, w_refs[0][:, c0:c1])
        if precise:
            acc = acc + (dot(hh, w_refs[1][:, c0:c1]) + dot(hl, w_refs[0][:, c0:c1]))
        o_ref[...] = acc


def _in_proj(x, g, ws):
    n = x.shape[0]
    tm = 256
    ncol = ws[0].shape[1]
    widths = [c1 - c0 for c0, c1 in _IN_SPLITS]
    return pl.pallas_call(
        functools.partial(_in_kernel, precise=len(ws) == 2),
        grid=(n // tm,),
        in_specs=[pl.BlockSpec((tm, D_MODEL), lambda i: (i, 0)),
                  pl.BlockSpec((1, D_MODEL), lambda i: (0, 0))]
                 + [pl.BlockSpec((D_MODEL, ncol), lambda i: (0, 0)) for _ in ws],
        out_specs=[pl.BlockSpec((tm, wd), lambda i: (i, 0)) for wd in widths],
        out_shape=[jax.ShapeDtypeStruct((n, wd), F32) for wd in widths],
        compiler_params=_cp(("parallel",)),
        name="in_proj",
    )(x, g, *ws)


def _pool_kernel(u_ref, buf_ref, w_ref, sc_ref, y_ref, nb_ref, ext_ref, *, tt, pos0, precise):
    mm = _MM(precise)
    j = pl.program_id(1)

    @pl.when(j == 0)
    def _():
        ext_ref[0:POOL_PAD, :] = buf_ref[...]

    u = u_ref[...]
    ext_ref[POOL_PAD:POOL_PAD + tt, :] = u
    acc = u
    sums = []
    for k in range(1, POOL_PAD):
        acc = acc + ext_ref[POOL_PAD - k:POOL_PAD - k + tt, :]
        if k + 1 in POOL_WINDOWS:
            sums.append(acc)
    lane = lax.broadcasted_iota(jnp.int32, (tt, GROUP_W), 1)
    grp = lane // (GROUP_W // len(POOL_WINDOWS))
    pos = (pos0 + j * tt + lax.broadcasted_iota(jnp.int32, (tt, GROUP_W), 0)).astype(F32)
    mean = jnp.zeros((tt, GROUP_W), F32)
    for gi, w in enumerate(POOL_WINDOWS):
        cnt = jnp.minimum(float(w), pos + 1.0)
        mean = jnp.where(grp == gi, sums[gi] / cnt, mean)
    d = mean - u
    y_ref[...] = (mm.nn(d, w_ref[...]) * sc_ref[...]).astype(y_ref.dtype)
    nb_ref[...] = ext_ref[tt + 1:tt + POOL_PAD, :]
    ext_ref[0:POOL_PAD, :] = ext_ref[tt:tt + POOL_PAD, :]


def _pool_mixer(u, buf16, w_bd, scale, pos0, precise):
    b, t, _ = u.shape
    tt = min(t, 512)
    return pl.pallas_call(
        functools.partial(_pool_kernel, tt=tt, pos0=pos0, precise=precise),
        grid=(b, t // tt),
        in_specs=[pl.BlockSpec((None, tt, GROUP_W), lambda i, j: (i, j, 0)),
                  pl.BlockSpec((None, POOL_PAD, GROUP_W), lambda i, j: (i, 0, 0)),
                  pl.BlockSpec((GROUP_W, GROUP_W), lambda i, j: (0, 0)),
                  pl.BlockSpec((1, GROUP_W), lambda i, j: (0, 0))],
        out_specs=[pl.BlockSpec((None, tt, GROUP_W), lambda i, j: (i, j, 0)),
                   pl.BlockSpec((None, POOL_BUF, GROUP_W), lambda i, j: (i, 0, 0))],
        out_shape=[jax.ShapeDtypeStruct((b, t, GROUP_W), F32 if precise else BF16),
                   jax.ShapeDtypeStruct((b, POOL_BUF, GROUP_W), F32)],
        scratch_shapes=[pltpu.VMEM((POOL_PAD + tt, GROUP_W), F32)],
        compiler_params=_cp(("parallel", "arbitrary")),
        name="pool_mixer",
    )(u, buf16, w_bd, scale)


def _rwkv_kernel(z_ref, sh_ref, s0_ref, mu_ref, w0_ref, w2_ref, a0_ref, a2_ref, g2_ref, kk_ref,
                 ka_ref, rk_ref, lnx_ref, seg_ref, tril_ref,
                 y_ref, so_ref, sho_ref, zext_ref, s_ref, ybuf_ref, pre_ref, *, precise, nb, nc):
    mm = _MM(precise)
    lora = _dot_hi if precise else _dot
    j = pl.program_id(1)
    L = CHUNK
    rows = nc * L
    n = nb * rows

    @pl.when(j == 0)
    def _():
        zext_ref[:, 7:8, :] = sh_ref[...]
        s_ref[...] = s0_ref[...]

    z3 = z_ref[...]
    zext_ref[:, 8:8 + rows, :] = z3
    prev3 = zext_ref[:, 7:7 + rows, :]
    zs = (z3 + (prev3 - z3) * mu_ref[...]).reshape(n, RWKV_COLS)
    c = GROUP_W
    r, k, v = zs[:, 0:c], zs[:, c:2 * c], zs[:, 2 * c:3 * c]
    wl, al, gl = zs[:, 768:832], zs[:, 832:896], zs[:, 896:1024]
    w = -_softplus(-(w0_ref[...] + lora(jnp.tanh(wl), w2_ref[...]))) - 0.5
    lw = -jnp.exp(w)
    a = _sigmoid(a0_ref[...] + lora(al, a2_ref[...]))
    g = lora(_sigmoid(gl), g2_ref[...])
    seg = seg_ref[...]
    segsum = lambda x: _dot_pieces(x, seg, 2 if precise else 1, lhs_exact=False)
    kk = k * kk_ref[...]
    kk = kk / jnp.maximum(jnp.sqrt(segsum(kk * kk) * float(HEAD_DIM)), 1e-12)
    k2 = k * (1.0 + (a - 1.0) * ka_ref[...])
    bonus = segsum(r * k2 * rk_ref[...]) * float(HEAD_DIM) * v
    b = kk * a

    ri = lax.broadcasted_iota(jnp.int32, (L, L), 0)
    ci = lax.broadcasted_iota(jnp.int32, (L, L), 1)
    strict, incl, eye = ri > ci, ri >= ci, ri == ci
    blk = (ri // SUB) == (ci // SUB)
    eye_f = jnp.where(eye, 1.0, 0.0)
    tril = tril_ref[...]

    nch = nb * nc
    chains = [(ch, h) for ch in range(nch) for h in range(N_HEADS)]
    each = lambda f, *lists: [f(*vals) for vals in zip(*lists)]
    at, rt, bt, kt, bh, kh, dend, vv = [], [], [], [], [], [], [], []
    for ch in range(nch):
        rs = slice(ch * L, (ch + 1) * L)
        lwc = lw[rs]
        cum = _dot_pieces(tril, lwc, 3 if precise else 2, lhs_exact=True)
        pend = cum[L - 1:L, :]
        e_neg = jnp.exp(-cum)
        e_end = jnp.exp(pend - cum)
        at_c, rt_c = -kk[rs] * jnp.exp(cum - lwc), r[rs] * jnp.exp(cum)
        bt_c, kt_c, bh_c, kh_c = b[rs] * e_neg, k2[rs] * e_neg, b[rs] * e_end, k2[rs] * e_end
        dend_c, v_c = jnp.exp(pend), v[rs]
        for h in range(N_HEADS):
            hs = slice(h * HEAD_DIM, (h + 1) * HEAD_DIM)
            for lst, val in ((at, at_c), (rt, rt_c), (bt, bt_c), (kt, kt_c), (bh, bh_c), (kh, kh_c),
                             (dend, dend_c), (vv, v_c)):
                lst.append(val[:, hs])
    gm = each(lambda a_, r_, b_, k_: mm.nt(jnp.concatenate([a_, r_], axis=0),
                                           jnp.concatenate([b_, k_], axis=0)), at, rt, bt, kt)
    gkv = each(lambda g_, v_: mm.nn(jnp.concatenate([jnp.where(strict, g_[:L, L:], 0.0),
                                                     jnp.where(incl, g_[L:, L:], 0.0)], axis=0), v_),
               gm, vv)
    a_ab = each(lambda g_: jnp.where(strict, g_[:L, :L], 0.0), gm)
    g_b = each(lambda g_: jnp.where(incl, g_[L:, :L], 0.0), gm)
    dp = each(lambda a_: jnp.where(blk, a_, 0.0), a_ab)
    aoff = each(lambda a_: jnp.where(blk, 0.0, a_), a_ab)
    md = each(lambda d_: eye_f + d_, dp)
    for _ in range(3):
        dp = each(lambda d_: mm.nn(d_, d_), dp)
        md = each(lambda m_, d_: m_ + mm.nn(m_, d_), md, dp)
    nm = each(mm.nn, md, aoff)
    nm2 = each(lambda n_: mm.nn(n_, n_), nm)
    x = each(lambda m_, a_, g_: mm.nn(m_, jnp.concatenate([a_, g_[:L]], axis=1)), md, at, gkv)
    x = each(lambda n_, x_: x_ + mm.nn(n_, x_), nm2, x)
    x = each(lambda n_, x_: x_ + mm.nn(n_, x_), nm, x)
    gx = each(mm.nn, g_b, x)
    xtb = each(mm.tn, x, bh)
    vtk = each(mm.tn, vv, kh)
    for i, (ch, h) in enumerate(chains):
        pre_ref[ch, h, 0] = rt[i] + gx[i][:, :HEAD_DIM]
        pre_ref[ch, h, 1] = gx[i][:, HEAD_DIM:] + gkv[i][L:]
        pre_ref[ch, h, 2] = jnp.where(eye, jnp.broadcast_to(dend[i], (L, L)), 0.0) + xtb[i][:HEAD_DIM]
        pre_ref[ch, h, 3] = xtb[i][HEAD_DIM:] + vtk[i]

    seqs = [(bi, h) for bi in range(nb) for h in range(N_HEADS)]
    s = [s_ref[bi, h] for bi, h in seqs]
    for ci_ in range(nc):
        for i, (bi, h) in enumerate(seqs):
            ch = bi * nc + ci_
            ybuf_ref[ch * L:(ch + 1) * L, h * HEAD_DIM:(h + 1) * HEAD_DIM] = (
                _dot_nt_hi(pre_ref[ch, h, 0], s[i]) + pre_ref[ch, h, 1])
        s = [_dot_hi(s[i], pre_ref[bi * nc + ci_, h, 2]) + pre_ref[bi * nc + ci_, h, 3]
             for i, (bi, h) in enumerate(seqs)]
    for i, (bi, h) in enumerate(seqs):
        s_ref[bi, h] = s[i]

    yb = ybuf_ref[...]
    mu = segsum(yb)
    yc = yb - mu
    y = yc * lax.rsqrt(segsum(yc * yc) + RWKV_GN_EPS) * lnx_ref[...]
    y_ref[...] = ((y + bonus) * g).astype(y_ref.dtype).reshape(nb, rows, GROUP_W)
    so_ref[...] = s_ref[...]
    sho_ref[...] = z3[:, rows - 1:rows, :]
    zext_ref[:, 7:8, :] = z3[:, rows - 1:rows, :]


def _rwkv_mixer(z, shift, s0, prm, seg, tril, precise):
    b, t, _ = z.shape
    L = CHUNK
    nc = min(t // L, 4)
    nb = 1 if nc > 1 else min(b, 4)
    rows = nc * L
    full = lambda a: pl.BlockSpec(a.shape, lambda i, j: (0,) * a.ndim)
    st_spec = pl.BlockSpec((nb, N_HEADS, HEAD_DIM, HEAD_DIM), lambda i, j: (i, 0, 0, 0))
    sh_spec = pl.BlockSpec((nb, 1, RWKV_COLS), lambda i, j: (i, 0, 0))
    consts = list(prm) + [seg.astype(BF16), tril.astype(BF16)]
    return pl.pallas_call(
        functools.partial(_rwkv_kernel, precise=precise, nb=nb, nc=nc),
        grid=(b // nb, t // rows),
        in_specs=[pl.BlockSpec((nb, rows, RWKV_COLS), lambda i, j: (i, j, 0)), sh_spec, st_spec]
                 + [full(a) for a in consts],
        out_specs=[pl.BlockSpec((nb, rows, GROUP_W), lambda i, j: (i, j, 0)), st_spec, sh_spec],
        out_shape=[jax.ShapeDtypeStruct((b, t, GROUP_W), F32 if precise else BF16),
                   jax.ShapeDtypeStruct((b, N_HEADS, HEAD_DIM, HEAD_DIM), F32),
                   jax.ShapeDtypeStruct((b, 1, RWKV_COLS), F32)],
        scratch_shapes=[pltpu.VMEM((nb, 8 + rows, RWKV_COLS), F32),
                        pltpu.VMEM((nb, N_HEADS, HEAD_DIM, HEAD_DIM), F32),
                        pltpu.VMEM((nb * rows, GROUP_W), F32),
                        pltpu.VMEM((nb * nc, N_HEADS, 4, HEAD_DIM, HEAD_DIM), F32)],
        compiler_params=_cp(("parallel", "arbitrary")),
        name="rwkv_mixer",
    )(z, shift, s0, *consts)


def _ret_kernel(z_ref, cos_ref, sin_ref, dm_ref, kdec_ref, qdec_ref, cdec_ref, gn_ref, seg_ref, s0_ref,
                y_ref, so_ref, s_ref, obuf_ref, *, precise, nb, nc):
    mm = _MM(precise)
    j = pl.program_id(1)
    L = CHUNK
    rows = nc * L

    @pl.when(j == 0)
    def _():
        s_ref[...] = s0_ref[...]

    c = GROUP_W
    lane = lax.broadcasted_iota(jnp.int32, (rows, c), 1)
    first_half = (lane % HEAD_DIM) < (HEAD_DIM // 2)
    cs, sn = cos_ref[...], sin_ref[...]
    kdec, qdec = kdec_ref[...], qdec_ref[...]

    def rope(x):
        partner = jnp.where(first_half, pltpu.roll(x, c - HEAD_DIM // 2, 1), pltpu.roll(x, HEAD_DIM // 2, 1))
        return x * cs + partner * sn

    each = lambda f, *lists: [f(*vals) for vals in zip(*lists)]
    keys = [(bi, ci, h) for bi in range(nb) for ci in range(nc) for h in range(N_HEADS)]
    qs, ks, vs, kds, qds, g_all = [], [], [], [], [], []
    for bi in range(nb):
        q = rope(z_ref[bi, :, 0:c])
        k = rope(z_ref[bi, :, c:2 * c]) * HEAD_DIM ** -0.5
        v = z_ref[bi, :, 2 * c:3 * c]
        g_all.append(z_ref[bi, :, 3 * c:4 * c])
        for ci in range(nc):
            rs = slice(ci * L, (ci + 1) * L)
            kd, qd = k[rs] * kdec, q[rs] * qdec
            for h in range(N_HEADS):
                hs = slice(h * HEAD_DIM, (h + 1) * HEAD_DIM)
                for lst, val in ((qs, q[rs]), (ks, k[rs]), (vs, v[rs]), (kds, kd), (qds, qd)):
                    lst.append(val[:, hs])
    att = [mm.nt(q_, k_) * dm_ref[h] for q_, k_, (_, _, h) in zip(qs, ks, keys)]
    o_in = each(mm.nn, att, vs)
    kv = each(mm.tn, kds, vs)
    s_prev = {}
    for bi in range(nb):
        for h in range(N_HEADS):
            s = s_ref[bi, h]
            for ci in range(nc):
                s_prev[(bi, ci, h)] = s
                s = s * cdec_ref[h] + kv[keys.index((bi, ci, h))]
            s_ref[bi, h] = s
    o_x = [mm.nn(qd_, s_prev[key]) for qd_, key in zip(qds, keys)]
    for i, (bi, ci, h) in enumerate(keys):
        r0 = bi * rows + ci * L
        obuf_ref[r0:r0 + L, h * HEAD_DIM:(h + 1) * HEAD_DIM] = o_in[i] + o_x[i]
    seg = seg_ref[...]
    segsum = lambda x: _dot_pieces(x, seg, 2 if precise else 1, lhs_exact=False)
    ob = obuf_ref[...]
    oc = ob - segsum(ob)
    o = oc * lax.rsqrt(segsum(oc * oc) + RET_GN_EPS) * gn_ref[...]
    g = jnp.concatenate(g_all, axis=0) if nb > 1 else g_all[0]
    y_ref[...] = (g * _sigmoid(g) * o).astype(y_ref.dtype).reshape(nb, rows, c)
    so_ref[...] = s_ref[...]


def _ret_mixer(z, s0, cos, sin, dmask, kdec, qdec, cdec, gn, seg, precise):
    b, t, _ = z.shape
    L = CHUNK
    nc = min(t // L, 4)
    nb = 1 if nc > 1 else min(b, 4)
    rows = nc * L
    full = lambda a: pl.BlockSpec(a.shape, lambda i, j: (0,) * a.ndim)
    st_spec = pl.BlockSpec((nb, N_HEADS, HEAD_DIM, HEAD_DIM), lambda i, j: (i, 0, 0, 0))
    tab_spec = pl.BlockSpec((rows, GROUP_W), lambda i, j: (j, 0))
    seg = seg.astype(BF16)
    return pl.pallas_call(
        functools.partial(_ret_kernel, precise=precise, nb=nb, nc=nc),
        grid=(b // nb, t // rows),
        in_specs=[pl.BlockSpec((nb, rows, 4 * GROUP_W), lambda i, j: (i, j, 0)), tab_spec, tab_spec,
                  full(dmask), full(kdec), full(qdec), full(cdec), full(gn), full(seg), st_spec],
        out_specs=[pl.BlockSpec((nb, rows, GROUP_W), lambda i, j: (i, j, 0)), st_spec],
        out_shape=[jax.ShapeDtypeStruct((b, t, GROUP_W), F32 if precise else BF16),
                   jax.ShapeDtypeStruct((b, N_HEADS, HEAD_DIM, HEAD_DIM), F32)],
        scratch_shapes=[pltpu.VMEM((nb, N_HEADS, HEAD_DIM, HEAD_DIM), F32),
                        pltpu.VMEM((nb * rows, GROUP_W), F32)],
        compiler_params=_cp(("parallel", "arbitrary")),
        name="ret_mixer",
    )(z, cos, sin, dmask, kdec, qdec, cdec, gn, seg, s0)


def _cumsum_lanes(x_ref, o_ref, tri, carry, width):
    blk = 128
    for c0 in range(0, width, blk):
        wd = min(blk, width - c0)
        cs = _dot_hi(x_ref[:, c0:c0 + wd], tri[:wd, :wd]) + carry
        o_ref[:, c0:c0 + wd] = cs
        carry = cs[:, wd - 1:wd]
    return carry


def _fox_gate_kernel(*refs, past, t):
    if past:
        zf_ref, b_ref, tri_ref, lfp_ref, lf_ref, cn_ref, cp_ref = refs
    else:
        zf_ref, b_ref, tri_ref, lf_ref, cn_ref = refs
    tri = tri_ref[...]
    x = zf_ref[...] + b_ref[...]
    lf_ref[...] = jnp.minimum(x, 0.0) - jnp.log(1.0 + jnp.exp(-jnp.abs(x)))
    carry = jnp.zeros((zf_ref.shape[0], 1), F32)
    if past:
        carry = _cumsum_lanes(lfp_ref, cp_ref, tri, carry, past)
    _cumsum_lanes(lf_ref, cn_ref, tri, carry, t)


def _fox_gate(zf_t, bias_rows, tri, lf_past_t):
    rows, t = zf_t.shape
    rt = min(rows, 32)
    past = 0 if lf_past_t is None else lf_past_t.shape[1]
    row_spec = lambda wd: pl.BlockSpec((rt, wd), lambda i: (i, 0))
    in_specs = [row_spec(t), row_spec(1), pl.BlockSpec(tri.shape, lambda i: (0, 0))]
    out_specs = [row_spec(t), row_spec(t)]
    out_shape = [jax.ShapeDtypeStruct((rows, t), F32), jax.ShapeDtypeStruct((rows, t), F32)]
    args = [zf_t, bias_rows, tri]
    if past:
        in_specs.append(row_spec(past))
        out_specs.append(row_spec(past))
        out_shape.append(jax.ShapeDtypeStruct((rows, past), F32))
        args.append(lf_past_t)
    return pl.pallas_call(
        functools.partial(_fox_gate_kernel, past=past, t=t),
        grid=(rows // rt,),
        in_specs=in_specs, out_specs=out_specs, out_shape=out_shape,
        compiler_params=_cp(("parallel",)),
        name="fox_gate",
    )(*args)


def _fox_attn_kernel(*refs, nkp, tq, tkn, precise):
    if nkp:
        (q_ref, kn_ref, vn_ref, og_ref, cq_ref, ckn_ref, kp_ref, vp_ref, ckp_ref,
         y_ref, m_ref, l_ref, acc_ref) = refs
    else:
        q_ref, kn_ref, vn_ref, og_ref, cq_ref, ckn_ref, y_ref, m_ref, l_ref, acc_ref = refs
    mm = _MM(precise)
    i, j = pl.program_id(1), pl.program_id(2)
    nk = pl.num_programs(2)

    @pl.when(j == 0)
    def _():
        m_ref[...] = jnp.full(m_ref.shape, -jnp.inf, F32)
        l_ref[...] = jnp.zeros(l_ref.shape, F32)
        acc_ref[...] = jnp.zeros(acc_ref.shape, F32)

    def block(k_ref, v_ref, ck_ref, mask):
        q = q_ref[...]
        k = k_ref[...]
        v = v_ref[...]
        cq = cq_ref[...]
        ck = ck_ref[...]
        heads = range(N_HEADS)
        hsl = [slice(h * HEAD_DIM, (h + 1) * HEAD_DIM) for h in heads]
        s = [mm.nt(q[:, hs], k[:, hs]) for hs in hsl]
        s = [s[h] * HEAD_DIM ** -0.5 + cq[:, h:h + 1] - ck[h:h + 1, :] for h in heads]
        if mask is not None:
            s = [jnp.where(mask, s_h, -jnp.inf) for s_h in s]
        m_prev = [m_ref[h] for h in heads]
        m_new = [jnp.maximum(m_prev[h], jnp.max(s[h], axis=1, keepdims=True)) for h in heads]
        alpha = [jnp.exp(m_prev[h] - m_new[h]) for h in heads]
        p = [jnp.exp(s[h] - m_new[h]) for h in heads]
        pv = [mm.nn(p[h], v[:, hsl[h]]) for h in heads]
        for h in heads:
            l_ref[h] = alpha[h] * l_ref[h] + jnp.sum(p[h], axis=1, keepdims=True)
            acc_ref[h] = alpha[h] * acc_ref[h] + pv[h]
            m_ref[h] = m_new[h]

    if nkp:
        @pl.when(j < nkp)
        def _():
            block(kp_ref, vp_ref, ckp_ref, None)

    jn = j - nkp

    @pl.when(jnp.logical_and(jn >= 0, jn <= i))
    def _():
        rowpos = i * tq + lax.broadcasted_iota(jnp.int32, (tq, tkn), 0)
        colpos = jn * tkn + lax.broadcasted_iota(jnp.int32, (tq, tkn), 1)
        block(kn_ref, vn_ref, ckn_ref, colpos <= rowpos)

    @pl.when(j == nk - 1)
    def _():
        og = og_ref[...]
        for h in range(N_HEADS):
            hs = slice(h * HEAD_DIM, (h + 1) * HEAD_DIM)
            o = acc_ref[h] / l_ref[h]
            y_ref[:, hs] = (_sigmoid(og[:, hs]) * o).astype(y_ref.dtype)


def _fox_attn(zfox, cq, ck_new, precise, k_past=None, v_past=None, ck_past=None):
    b, t, _ = zfox.shape
    tq = tkn = min(t, 256)
    nq = t // tq
    nkp = 0
    tkp = 512
    if k_past is not None:
        nkp = k_past.shape[1] // tkp
    jn = lambda i, j: jnp.minimum(jnp.maximum(j - nkp, 0), i)
    jp = lambda j: jnp.minimum(j, nkp - 1)
    in_specs = [pl.BlockSpec((None, tq, GROUP_W), lambda bb, i, j: (bb, i, 0)),
                pl.BlockSpec((None, tkn, GROUP_W), lambda bb, i, j: (bb, jn(i, j), 1)),
                pl.BlockSpec((None, tkn, GROUP_W), lambda bb, i, j: (bb, jn(i, j), 2)),
                pl.BlockSpec((None, tq, GROUP_W), lambda bb, i, j: (bb, i, 3)),
                pl.BlockSpec((None, tq, N_HEADS), lambda bb, i, j: (bb, i, 0)),
                pl.BlockSpec((None, N_HEADS, tkn), lambda bb, i, j: (bb, 0, jn(i, j)))]
    args = [zfox, zfox, zfox, zfox, cq, ck_new]
    if nkp:
        in_specs += [pl.BlockSpec((None, tkp, GROUP_W), lambda bb, i, j: (bb, jp(j), 0)),
                     pl.BlockSpec((None, tkp, GROUP_W), lambda bb, i, j: (bb, jp(j), 0)),
                     pl.BlockSpec((None, N_HEADS, tkp), lambda bb, i, j: (bb, 0, jp(j)))]
        args += [k_past, v_past, ck_past]
    return pl.pallas_call(
        functools.partial(_fox_attn_kernel, nkp=nkp, tq=tq, tkn=tkn, precise=precise),
        grid=(b, nq, nkp + nq),
        in_specs=in_specs,
        out_specs=pl.BlockSpec((None, tq, GROUP_W), lambda bb, i, j: (bb, i, 0)),
        out_shape=jax.ShapeDtypeStruct((b, t, GROUP_W), F32 if precise else BF16),
        scratch_shapes=[pltpu.VMEM((N_HEADS, tq, 1), F32), pltpu.VMEM((N_HEADS, tq, 1), F32),
                        pltpu.VMEM((N_HEADS, tq, HEAD_DIM), F32)],
        compiler_params=_cp(("parallel", "parallel", "arbitrary")),
        name="fox_attn",
    )(*args)


def _out_kernel(x_ref, ya_ref, yb_ref, yc_ref, yd_ref, *refs, precise):
    w_refs, o_ref = refs[:-1], refs[-1]
    dot = lambda a, b: jnp.dot(a, b, preferred_element_type=F32)
    acc = x_ref[...]
    for i, y_ref in enumerate((ya_ref, yb_ref, yc_ref, yd_ref)):
        if precise:
            yh, yl = _split(y_ref[...])
            acc = acc + (dot(yh, w_refs[0][i]) + (dot(yh, w_refs[1][i]) + dot(yl, w_refs[0][i])))
        else:
            acc = acc + dot(y_ref[...], w_refs[0][i])
    o_ref[...] = acc


def _out_proj(x, ys, ws):
    n = x.shape[0]
    tm = 512
    y_spec = pl.BlockSpec((tm, GROUP_W), lambda i: (i, 0))
    x_spec = pl.BlockSpec((tm, D_MODEL), lambda i: (i, 0))
    return pl.pallas_call(
        functools.partial(_out_kernel, precise=len(ws) == 2),
        grid=(n // tm,),
        in_specs=[x_spec, y_spec, y_spec, y_spec, y_spec]
                 + [pl.BlockSpec(w.shape, lambda i: (0, 0, 0)) for w in ws],
        out_specs=x_spec,
        out_shape=jax.ShapeDtypeStruct((n, D_MODEL), F32),
        compiler_params=_cp(("parallel",)),
        name="out_proj",
    )(x, *ys, *ws)


def _router_kernel(x_ref, g_ref, w_ref, b_ref, h_ref, gate_ref):
    h = _rmsnorm(x_ref[...], g_ref[...])
    h_ref[...] = h.astype(BF16)
    logits = _dot_hi(h, w_ref[...]) + b_ref[...]
    tm = logits.shape[0]
    lane_i = lax.broadcasted_iota(jnp.int32, (tm, ROUTER_PAD), 1)
    lane = lane_i.astype(F32)
    big = float(ROUTER_PAD)
    rmax = lambda a: jnp.max(a, axis=1, keepdims=True)
    rsum = lambda a: jnp.sum(a, axis=1, keepdims=True)
    first = lambda m: jnp.min(jnp.where(m, lane, big), axis=1, keepdims=True)

    is_c = jnp.logical_and(lane_i >= N_EXPERTS, lane_i < N_EXPERTS + N_GROUPS)
    lc = jnp.where(is_c, logits, -jnp.inf)
    ec = jnp.exp(lc - rmax(lc))
    pc_all = ec / rsum(ec)
    pc = rmax(pc_all)
    gi = first(jnp.logical_and(is_c, pc_all == pc)) - float(N_EXPERTS)

    is_f = (lane_i // EXPERTS_PER_GROUP).astype(F32) == gi
    lf = jnp.where(is_f, logits, -jnp.inf)
    ef = jnp.exp(lf - rmax(lf))
    pf = jnp.where(is_f, ef / rsum(ef), -1.0)
    t1 = rmax(pf)
    i1 = first(pf == t1)
    pf2 = jnp.where(lane == i1, -1.0, pf)
    t2 = rmax(pf2)
    i2 = first(pf2 == t2)
    den = t1 + t2
    gate_ref[...] = jnp.where(lane == i1, pc * (t1 / den), jnp.where(lane == i2, pc * (t2 / den), 0.0))


def _router(x, g, wr, br):
    n = x.shape[0]
    tm = 512
    return pl.pallas_call(
        _router_kernel,
        grid=(n // tm,),
        in_specs=[pl.BlockSpec((tm, D_MODEL), lambda i: (i, 0)),
                  pl.BlockSpec((1, D_MODEL), lambda i: (0, 0)),
                  pl.BlockSpec((D_MODEL, ROUTER_PAD), lambda i: (0, 0)),
                  pl.BlockSpec((1, ROUTER_PAD), lambda i: (0, 0))],
        out_specs=[pl.BlockSpec((tm, D_MODEL), lambda i: (i, 0)),
                   pl.BlockSpec((tm, ROUTER_PAD), lambda i: (i, 0))],
        out_shape=[jax.ShapeDtypeStruct((n, D_MODEL), BF16),
                   jax.ShapeDtypeStruct((n, ROUTER_PAD), F32)],
        compiler_params=_cp(("parallel",)),
        name="moe_router",
    )(x, g, wr, br)


def _moe_kernel(x_ref, h_ref, gate_ref, w1_ref, w3_ref, w2_ref, nf_ref, o_ref, *, final):
    e = pl.program_id(1)

    @pl.when(e == 0)
    def _():
        o_ref[...] = x_ref[...]

    h = h_ref[...]
    a = jnp.dot(h, w1_ref[...], preferred_element_type=F32)
    b = jnp.dot(h, w3_ref[...], preferred_element_type=F32)
    he = (a * _sigmoid(a) * b).astype(BF16)
    ye = jnp.dot(he, w2_ref[...], preferred_element_type=F32)
    gate = gate_ref[...]
    lane = lax.broadcasted_iota(jnp.int32, gate.shape, 1)
    gcol = jnp.sum(jnp.where(lane == e, gate, 0.0), axis=1, keepdims=True)
    o_ref[...] += gcol * ye

    if final:
        @pl.when(e == N_EXPERTS - 1)
        def _():
            o_ref[...] = _rmsnorm(o_ref[...], nf_ref[...])


def _moe(x, h, gate, w1, w3, w2, nf, final):
    n = x.shape[0]
    tm = 512
    return pl.pallas_call(
        functools.partial(_moe_kernel, final=final),
        grid=(n // tm, N_EXPERTS),
        in_specs=[pl.BlockSpec((tm, D_MODEL), lambda i, e: (i, 0)),
                  pl.BlockSpec((tm, D_MODEL), lambda i, e: (i, 0)),
                  pl.BlockSpec((tm, ROUTER_PAD), lambda i, e: (i, 0)),
                  pl.BlockSpec((None, D_MODEL, D_EXPERT), lambda i, e: (e, 0, 0)),
                  pl.BlockSpec((None, D_MODEL, D_EXPERT), lambda i, e: (e, 0, 0)),
                  pl.BlockSpec((None, D_EXPERT, D_MODEL), lambda i, e: (e, 0, 0)),
                  pl.BlockSpec((1, D_MODEL), lambda i, e: (0, 0))],
        out_specs=pl.BlockSpec((tm, D_MODEL), lambda i, e: (i, 0)),
        out_shape=jax.ShapeDtypeStruct((n, D_MODEL), F32),
        compiler_params=_cp(("parallel", "arbitrary")),
        name="moe_experts",
    )(x, h, gate, w1, w3, w2, nf)


def _prep_layer(l, p):
    w_in = p["w_in"][l]
    o_rwkv, o_ret, o_fox = GROUP_W, GROUP_W + RWKV_COLS, GROUP_W + RWKV_COLS + 4 * GROUP_W
    ff = jnp.pad(w_in[:, o_fox + 4 * GROUP_W:], ((0, 0), (0, FF_PAD - N_HEADS)))
    w_in_r = jnp.concatenate([w_in[:, o_rwkv:o_ret], w_in[:, o_ret:o_fox],
                              w_in[:, o_fox:o_fox + 4 * GROUP_W], w_in[:, :GROUP_W], ff], axis=1)
    pw = p["pool_w"][l]
    pg = GROUP_W // len(POOL_WINDOWS)
    w_bd = jnp.zeros((GROUP_W, GROUP_W), F32)
    for gi in range(len(POOL_WINDOWS)):
        w_bd = w_bd.at[gi * pg:(gi + 1) * pg, gi * pg:(gi + 1) * pg].set(pw[gi])
    row = lambda a: a.reshape(1, -1)
    rwkv = (row(p["rwkv_mu"][l]), row(p["rwkv_w0"][l]), p["rwkv_w2"][l], row(p["rwkv_a0"][l]),
            p["rwkv_a2"][l], p["rwkv_g2"][l], row(p["rwkv_kk"][l]), row(p["rwkv_ka"][l]),
            row(p["rwkv_rk"][l]), row(p["rwkv_lnx"][l]))
    wf = jnp.transpose(p["moe_wf"][l], (1, 0, 2)).reshape(D_MODEL, N_EXPERTS)
    wr = jnp.pad(jnp.concatenate([wf, p["moe_wc"][l]], axis=1),
                 ((0, 0), (0, ROUTER_PAD - N_EXPERTS - N_GROUPS)))
    br = jnp.pad(jnp.concatenate([p["moe_bf"][l].reshape(-1), p["moe_bc"][l]]),
                 (0, ROUTER_PAD - N_EXPERTS - N_GROUPS)).reshape(1, ROUTER_PAD)
    w_out4 = p["w_out"][l].reshape(4, GROUP_W, D_MODEL)
    precise = l < p["w_in"].shape[0] - 1
    split = lambda w: _split_bits(w) if precise else (w.astype(BF16),)
    return dict(
        precise=precise, n1=row(p["norm1_g"][l]), w_in=split(w_in_r), pool_w=w_bd,
        pool_scale=row(p["pool_scale"][l]),
        rwkv=rwkv, ret_gn=row(p["ret_gn"][l]), fox_bf=p["fox_bf"][l],
        w_out=split(w_out4), n2=row(p["norm2_g"][l]),
        wr=wr, br=br, w1=p["moe_w1"][l].astype(BF16), w3=p["moe_w3"][l].astype(BF16),
        w2=p["moe_w2"][l].astype(BF16))


def _tables(t, pos0):
    half = HEAD_DIM // 2
    inv = ROPE_BASE ** (-jnp.arange(half, dtype=F32) / half)
    ang = (pos0 + jnp.arange(t)).astype(F32)[:, None] * inv[None, :]
    cos, sin = jnp.cos(ang), jnp.sin(ang)
    cos_t = jnp.tile(jnp.concatenate([cos, cos], axis=1), (1, N_HEADS))
    sin_t = jnp.tile(jnp.concatenate([-sin, sin], axis=1), (1, N_HEADS))
    L = CHUNK
    log_g = jnp.log(jnp.array(RET_GAMMA, F32))
    idx = jnp.arange(L, dtype=F32)
    dmask = jnp.exp(log_g[:, None, None] * jnp.abs(idx[:, None] - idx[None, :]))
    lanes = lambda a: jnp.repeat(a, HEAD_DIM, axis=1)
    kdec = lanes(jnp.exp(log_g[None, :] * (L - 1.0 - idx)[:, None]))
    qdec = lanes(jnp.exp(log_g[None, :] * (idx + 1.0)[:, None]))
    cdec = jnp.broadcast_to(jnp.exp(log_g * L)[:, None, None], (N_HEADS, HEAD_DIM, HEAD_DIM))
    hid = jnp.arange(GROUP_W) // HEAD_DIM
    seg = jnp.where(hid[:, None] == hid[None, :], 1.0 / HEAD_DIM, 0.0).astype(F32)
    tril = jnp.tril(jnp.ones((L, L), F32))
    triu = jnp.triu(jnp.ones((128, 128), F32))
    return dict(cos=cos_t, sin=sin_t, dmask=dmask, kdec=kdec, qdec=qdec, cdec=cdec, seg=seg,
                tril=tril, triu=triu)


def _trunk(x, pos0, states, layers, norm_f):
    b, t, _ = x.shape
    n = b * t
    tb = _tables(t, pos0)
    xf = x.reshape(n, D_MODEL)
    nf = norm_f.reshape(1, D_MODEL)
    outs = [[] for _ in range(7)]
    for l, lp in enumerate(layers):
        pool_buf, shift, wkv, ret, k_past, v_past, lf_past = states[l]
        precise = lp["precise"]
        z_rwkv, z_ret, z_fox, z_pool, z_ff = _in_proj(xf, lp["n1"], lp["w_in"])
        buf16 = jnp.pad(pool_buf, ((0, 0), (POOL_PAD - POOL_BUF, 0), (0, 0)))
        y_a, pool_new = _pool_mixer(z_pool.reshape(b, t, GROUP_W), buf16, lp["pool_w"], lp["pool_scale"], pos0,
                                    precise)
        y_b, wkv_new, shift_new = _rwkv_mixer(z_rwkv.reshape(b, t, RWKV_COLS), shift, wkv, lp["rwkv"],
                                              tb["seg"], tb["tril"], precise)
        y_c, ret_new = _ret_mixer(z_ret.reshape(b, t, 4 * GROUP_W), ret, tb["cos"], tb["sin"], tb["dmask"],
                                  tb["kdec"], tb["qdec"], tb["cdec"], lp["ret_gn"], tb["seg"], precise)
        zf_t = jnp.transpose(z_ff[:, :N_HEADS].reshape(b, t, N_HEADS), (0, 2, 1)).reshape(b * N_HEADS, t)
        bias_rows = jnp.tile(lp["fox_bf"], b).reshape(b * N_HEADS, 1)
        zfox3 = z_fox.reshape(b, t, 4 * GROUP_W)
        if k_past is None:
            lf_t, c_new = _fox_gate(zf_t, bias_rows, tb["triu"], None)
            cn3 = c_new.reshape(b, N_HEADS, t)
            y_d = _fox_attn(zfox3, jnp.transpose(cn3, (0, 2, 1)), cn3, precise)
        else:
            p = k_past.shape[1]
            lfp_t = jnp.transpose(lf_past, (0, 2, 1)).reshape(b * N_HEADS, p)
            lf_t, c_new, c_past = _fox_gate(zf_t, bias_rows, tb["triu"], lfp_t)
            cn3 = c_new.reshape(b, N_HEADS, t)
            y_d = _fox_attn(zfox3, jnp.transpose(cn3, (0, 2, 1)), cn3, precise,
                            k_past.reshape(b, p, GROUP_W), v_past.reshape(b, p, GROUP_W),
                            c_past.reshape(b, N_HEADS, p))
        x1 = _out_proj(xf, [y.reshape(n, GROUP_W) for y in (y_a, y_b, y_c, y_d)], lp["w_out"])
        h, gate = _router(x1, lp["n2"], lp["wr"], lp["br"])
        xf = _moe(x1, h, gate, lp["w1"], lp["w3"], lp["w2"], nf, final=(l == len(layers) - 1))
        k_new = zfox3[:, :, GROUP_W:2 * GROUP_W].reshape(b, t, N_HEADS, HEAD_DIM)
        v_new = zfox3[:, :, 2 * GROUP_W:3 * GROUP_W].reshape(b, t, N_HEADS, HEAD_DIM)
        lf_new = jnp.transpose(lf_t.reshape(b, N_HEADS, t), (0, 2, 1))
        for lst, s in zip(outs, (pool_new, shift_new, wkv_new, ret_new, k_new, v_new, lf_new)):
            lst.append(s)
    return xf.reshape(b, t, D_MODEL), [jnp.stack(lst) for lst in outs]


def kernel(x_prompt, x_sample, state_pool, state_shift, state_wkv, state_ret, cache_fox_k, cache_fox_v, cache_fox_logf, norm1_g, w_in, pool_w, pool_scale, rwkv_mu, rwkv_w0, rwkv_w2, rwkv_a0, rwkv_a2, rwkv_g2, rwkv_kk, rwkv_ka, rwkv_rk, rwkv_lnx, ret_gn, fox_bf, w_out, norm2_g, moe_wc, moe_bc, moe_wf, moe_bf, moe_w1, moe_w3, moe_w2, norm_f):
    p = dict(norm1_g=norm1_g, w_in=w_in, pool_w=pool_w, pool_scale=pool_scale, rwkv_mu=rwkv_mu,
             rwkv_w0=rwkv_w0, rwkv_w2=rwkv_w2, rwkv_a0=rwkv_a0, rwkv_a2=rwkv_a2, rwkv_g2=rwkv_g2,
             rwkv_kk=rwkv_kk, rwkv_ka=rwkv_ka, rwkv_rk=rwkv_rk, rwkv_lnx=rwkv_lnx, ret_gn=ret_gn,
             fox_bf=fox_bf, w_out=w_out, norm2_g=norm2_g, moe_wc=moe_wc, moe_bc=moe_bc, moe_wf=moe_wf,
             moe_bf=moe_bf, moe_w1=moe_w1, moe_w3=moe_w3, moe_w2=moe_w2)
    depth = w_in.shape[0]
    layers = [_prep_layer(l, p) for l in range(depth)]
    b = x_prompt.shape[0]
    dt = x_prompt.dtype
    prompt_init = [(jnp.zeros((b, POOL_BUF, GROUP_W), dt), jnp.zeros((b, 1, RWKV_COLS), dt),
                    jnp.zeros((b, N_HEADS, HEAD_DIM, HEAD_DIM), dt),
                    jnp.zeros((b, N_HEADS, HEAD_DIM, HEAD_DIM), dt), None, None, None)
                   for _ in range(depth)]
    sample_init = [(state_pool[l], state_shift[l], state_wkv[l], state_ret[l], cache_fox_k[l],
                    cache_fox_v[l], cache_fox_logf[l]) for l in range(depth)]
    past = cache_fox_k.shape[2]
    y_prompt, new_p = _trunk(x_prompt, 0, prompt_init, layers, norm_f)
    y_sample, new_s = _trunk(x_sample, past, sample_init, layers, norm_f)
    return (y_prompt, y_sample, *new_p, *new_s)
```

```python
import functools

import jax
import jax.numpy as jnp
from jax import lax
from jax.experimental import pallas as pl
from jax.experimental.pallas import tpu as pltpu

F32 = jnp.float32
BF16 = jnp.bfloat16
HIGHEST = lax.Precision.HIGHEST

D_MODEL = 1024
DEPTH = 2
CHUNK = 64
GROUP_W = 256
HEAD_DIM = 64
N_HEADS = 4
POOL_WINDOWS = (2, 4, 8, 16)
POOL_BUF = 15
POOL_PAD = 16
RWKV_COLS = 1024
N_IN = 3332
FF_PAD = 128
RET_GAMMA = tuple(1.0 - 2.0 ** (-5 - h) for h in range(N_HEADS))
ROPE_BASE = 10000.0
N_GROUPS = 4
EXPERTS_PER_GROUP = 4
N_EXPERTS = 16
D_EXPERT = 512
RMS_EPS = 1e-6
RWKV_GN_EPS = 64e-5
RET_GN_EPS = 1e-5
SUB = 16
ROUTER_PAD = 128
VMEM_LIMIT = 48 * 1024 * 1024


def _cp(sem):
    return pltpu.CompilerParams(dimension_semantics=sem, vmem_limit_bytes=VMEM_LIMIT)


def _dot(a, b):
    return jnp.dot(a.astype(BF16), b.astype(BF16), preferred_element_type=F32)


def _dot_nt(a, b):
    return lax.dot_general(a.astype(BF16), b.astype(BF16), (((1,), (1,)), ((), ())),
                           preferred_element_type=F32)


def _dot_tn(a, b):
    return lax.dot_general(a.astype(BF16), b.astype(BF16), (((0,), (0,)), ((), ())),
                           preferred_element_type=F32)


def _split(a):
    hi = a.astype(BF16)
    return hi, (a - hi.astype(F32)).astype(BF16)


def _split_bits(w):
    bits = lax.bitcast_convert_type(w, jnp.uint32) & jnp.uint32(0xFFFF0000)
    hi = lax.bitcast_convert_type(bits, F32)
    return hi.astype(BF16), (w - hi).astype(BF16)


def _dg3(a, b, dims):
    ah, al = _split(a)
    bh, bl = _split(b)
    dg = lambda x, y: lax.dot_general(x, y, (dims, ((), ())), preferred_element_type=F32)
    return dg(ah, bh) + (dg(ah, bl) + dg(al, bh))


class _MM:
    def __init__(self, precise):
        if precise:
            self.nn = lambda a, b: _dg3(a, b, ((1,), (0,)))
            self.nt = lambda a, b: _dg3(a, b, ((1,), (1,)))
            self.tn = lambda a, b: _dg3(a, b, ((0,), (0,)))
        else:
            self.nn, self.nt, self.tn = _dot, _dot_nt, _dot_tn


def _pieces(a, n):
    out = []
    for i in range(n):
        p = a.astype(BF16)
        out.append(p)
        if i + 1 < n:
            a = a - p.astype(F32)
    return out


def _dot_pieces(a, b, n, lhs_exact):
    dot = lambda x, y: jnp.dot(x, y, preferred_element_type=F32)
    terms = [dot(a, p) for p in _pieces(b, n)] if lhs_exact else [dot(p, b) for p in _pieces(a, n)]
    acc = terms[-1]
    for t in terms[-2::-1]:
        acc = acc + t
    return acc


def _dot_hi(a, b):
    return jnp.dot(a, b, precision=HIGHEST, preferred_element_type=F32)


def _dot_nt_hi(a, b):
    return lax.dot_general(a, b, (((1,), (1,)), ((), ())), precision=HIGHEST,
                           preferred_element_type=F32)


def _sigmoid(x):
    return 1.0 / (1.0 + jnp.exp(-x))


def _softplus(x):
    return jnp.maximum(x, 0.0) + jnp.log(1.0 + jnp.exp(-jnp.abs(x)))


def _rmsnorm(x, g):
    return x * lax.rsqrt(jnp.mean(x * x, axis=-1, keepdims=True) + RMS_EPS) * g


_IN_SPLITS = ((0, 1024), (1024, 2048), (2048, 3072), (3072, 3328), (3328, 3456))


def _in_kernel(x_ref, g_ref, *refs, precise):
    nw = 2 if precise else 1
    w_refs, o_refs = refs[:nw], refs[nw:]
    h = _rmsnorm(x_ref[...], g_ref[...])
    tm = h.shape[0]
    dot = lambda a, b: jnp.dot(a, b, preferred_element_type=F32)
    if precise:
        hh, hl = _split(h)
        lhs = jnp.concatenate([hh, hl], axis=0)
    else:
        lhs = h.astype(BF16)
    for o_ref, (c0, c1) in zip(o_refs, _IN_SPLITS):
        acc = dot(lhs, w_refs[0][:, c0:c1])
        if precise:
            acc = acc[:tm] + (acc[tm:] + dot(hh, w_refs[1][:, c0:c1]))
        o_ref[...] = acc


def _in_proj(x, g, ws):
    n = x.shape[0]
    tm = 512
    ncol = ws[0].shape[1]
    widths = [c1 - c0 for c0, c1 in _IN_SPLITS]
    return pl.pallas_call(
        functools.partial(_in_kernel, precise=len(ws) == 2),
        grid=(n // tm,),
        in_specs=[pl.BlockSpec((tm, D_MODEL), lambda i: (i, 0)),
                  pl.BlockSpec((1, D_MODEL), lambda i: (0, 0))]
                 + [pl.BlockSpec((D_MODEL, ncol), lambda i: (0, 0), pipeline_mode=pl.Buffered(1)) for _ in ws],
        out_specs=[pl.BlockSpec((tm, wd), lambda i: (i, 0)) for wd in widths],
        out_shape=[jax.ShapeDtypeStruct((n, wd), F32) for wd in widths],
        compiler_params=_cp(("parallel",)),
        name="in_proj",
    )(x, g, *ws)


def _pool_kernel(u_ref, buf_ref, w_ref, sc_ref, y_ref, nb_ref, ext_ref, *, tt, pos0, precise):
    mm = _MM(precise)
    j = pl.program_id(1)

    @pl.when(j == 0)
    def _():
        ext_ref[0:POOL_PAD, :] = buf_ref[...]

    u = u_ref[...]
    ext_ref[POOL_PAD:POOL_PAD + tt, :] = u
    acc = u
    sums = []
    for k in range(1, POOL_PAD):
        acc = acc + ext_ref[POOL_PAD - k:POOL_PAD - k + tt, :]
        if k + 1 in POOL_WINDOWS:
            sums.append(acc)
    lane = lax.broadcasted_iota(jnp.int32, (tt, GROUP_W), 1)
    grp = lane // (GROUP_W // len(POOL_WINDOWS))
    pos = (pos0 + j * tt + lax.broadcasted_iota(jnp.int32, (tt, GROUP_W), 0)).astype(F32)
    mean = jnp.zeros((tt, GROUP_W), F32)
    for gi, w in enumerate(POOL_WINDOWS):
        cnt = jnp.minimum(float(w), pos + 1.0)
        mean = jnp.where(grp == gi, sums[gi] / cnt, mean)
    d = mean - u
    y_ref[...] = (mm.nn(d, w_ref[...]) * sc_ref[...]).astype(y_ref.dtype)
    nb_ref[...] = ext_ref[tt + 1:tt + POOL_PAD, :]
    ext_ref[0:POOL_PAD, :] = ext_ref[tt:tt + POOL_PAD, :]


def _pool_mixer(u, buf16, w_bd, scale, pos0, precise):
    b, t, _ = u.shape
    tt = min(t, 512)
    return pl.pallas_call(
        functools.partial(_pool_kernel, tt=tt, pos0=pos0, precise=precise),
        grid=(b, t // tt),
        in_specs=[pl.BlockSpec((None, tt, GROUP_W), lambda i, j: (i, j, 0)),
                  pl.BlockSpec((None, POOL_PAD, GROUP_W), lambda i, j: (i, 0, 0)),
                  pl.BlockSpec((GROUP_W, GROUP_W), lambda i, j: (0, 0)),
                  pl.BlockSpec((1, GROUP_W), lambda i, j: (0, 0))],
        out_specs=[pl.BlockSpec((None, tt, GROUP_W), lambda i, j: (i, j, 0)),
                   pl.BlockSpec((None, POOL_BUF, GROUP_W), lambda i, j: (i, 0, 0))],
        out_shape=[jax.ShapeDtypeStruct((b, t, GROUP_W), F32 if precise else BF16),
                   jax.ShapeDtypeStruct((b, POOL_BUF, GROUP_W), F32)],
        scratch_shapes=[pltpu.VMEM((POOL_PAD + tt, GROUP_W), F32)],
        compiler_params=_cp(("parallel", "arbitrary")),
        name="pool_mixer",
    )(u, buf16, w_bd, scale)


def _rwkv_kernel(z_ref, sh_ref, s0_ref, mu_ref, w0_ref, w2_ref, a0_ref, a2_ref, g2_ref, kk_ref,
                 ka_ref, rk_ref, lnx_ref, seg_ref, tril_ref,
                 y_ref, so_ref, sho_ref, zext_ref, s_ref, ybuf_ref, pre_ref, *, precise, nb, nc):
    mm = _MM(precise)
    lora = _dot_hi if precise else _dot
    j = pl.program_id(1)
    L = CHUNK
    rows = nc * L
    n = nb * rows

    @pl.when(j == 0)
    def _():
        zext_ref[:, 7:8, :] = sh_ref[...]
        s_ref[...] = s0_ref[...]

    z3 = z_ref[...]
    zext_ref[:, 8:8 + rows, :] = z3
    prev3 = zext_ref[:, 7:7 + rows, :]
    zs = (z3 + (prev3 - z3) * mu_ref[...]).reshape(n, RWKV_COLS)
    c = GROUP_W
    r, k, v = zs[:, 0:c], zs[:, c:2 * c], zs[:, 2 * c:3 * c]
    wl, al, gl = zs[:, 768:832], zs[:, 832:896], zs[:, 896:1024]
    w = -_softplus(-(w0_ref[...] + lora(jnp.tanh(wl), w2_ref[...]))) - 0.5
    lw = -jnp.exp(w)
    a = _sigmoid(a0_ref[...] + lora(al, a2_ref[...]))
    g = lora(_sigmoid(gl), g2_ref[...])
    seg = seg_ref[...]
    segsum = lambda x: _dot_pieces(x, seg, 2 if precise else 1, lhs_exact=False)
    kk = k * kk_ref[...]
    kk = kk / jnp.maximum(jnp.sqrt(segsum(kk * kk) * float(HEAD_DIM)), 1e-12)
    k2 = k * (1.0 + (a - 1.0) * ka_ref[...])
    bonus = segsum(r * k2 * rk_ref[...]) * float(HEAD_DIM) * v
    b = kk * a

    ri = lax.broadcasted_iota(jnp.int32, (L, L), 0)
    ci = lax.broadcasted_iota(jnp.int32, (L, L), 1)
    strict, incl, eye = ri > ci, ri >= ci, ri == ci
    blk = (ri // SUB) == (ci // SUB)
    eye_f = jnp.where(eye, 1.0, 0.0)
    tril = tril_ref[...]

    nch = nb * nc
    chains = [(ch, h) for ch in range(nch) for h in range(N_HEADS)]
    each = lambda f, *lists: [f(*vals) for vals in zip(*lists)]
    at, rt, bt, kt, bh, kh, dend, vv = [], [], [], [], [], [], [], []
    for ch in range(nch):
        rs = slice(ch * L, (ch + 1) * L)
        lwc = lw[rs]
        cum = _dot_pieces(tril, lwc, 3 if precise else 2, lhs_exact=True)
        pend = cum[L - 1:L, :]
        e_neg = jnp.exp(-cum)
        e_end = jnp.exp(pend - cum)
        at_c, rt_c = -kk[rs] * jnp.exp(cum - lwc), r[rs] * jnp.exp(cum)
        bt_c, kt_c, bh_c, kh_c = b[rs] * e_neg, k2[rs] * e_neg, b[rs] * e_end, k2[rs] * e_end
        dend_c, v_c = jnp.exp(pend), v[rs]
        for h in range(N_HEADS):
            hs = slice(h * HEAD_DIM, (h + 1) * HEAD_DIM)
            for lst, val in ((at, at_c), (rt, rt_c), (bt, bt_c), (kt, kt_c), (bh, bh_c), (kh, kh_c),
                             (dend, dend_c), (vv, v_c)):
                lst.append(val[:, hs])
    gm = each(lambda a_, r_, b_, k_: mm.nt(jnp.concatenate([a_, r_], axis=0),
                                           jnp.concatenate([b_, k_], axis=0)), at, rt, bt, kt)
    gkv = each(lambda g_, v_: mm.nn(jnp.concatenate([jnp.where(strict, g_[:L, L:], 0.0),
                                                     jnp.where(incl, g_[L:, L:], 0.0)], axis=0), v_),
               gm, vv)
    a_ab = each(lambda g_: jnp.where(strict, g_[:L, :L], 0.0), gm)
    g_b = each(lambda g_: jnp.where(incl, g_[L:, :L], 0.0), gm)
    dp = each(lambda a_: jnp.where(blk, a_, 0.0), a_ab)
    aoff = each(lambda a_: jnp.where(blk, 0.0, a_), a_ab)
    md = each(lambda d_: eye_f + d_, dp)
    for _ in range(3):
        dp = each(lambda d_: mm.nn(d_, d_), dp)
        md = each(lambda m_, d_: m_ + mm.nn(m_, d_), md, dp)
    nm = each(mm.nn, md, aoff)
    nm2 = each(lambda n_: mm.nn(n_, n_), nm)
    x = each(lambda m_, a_, g_: mm.nn(m_, jnp.concatenate([a_, g_[:L]], axis=1)), md, at, gkv)
    x = each(lambda n_, x_: x_ + mm.nn(n_, x_), nm2, x)
    x = each(lambda n_, x_: x_ + mm.nn(n_, x_), nm, x)
    gx = each(mm.nn, g_b, x)
    xtb = each(mm.tn, x, bh)
    vtk = each(mm.tn, vv, kh)
    for i, (ch, h) in enumerate(chains):
        pre_ref[ch, h, 0] = rt[i] + gx[i][:, :HEAD_DIM]
        pre_ref[ch, h, 1] = gx[i][:, HEAD_DIM:] + gkv[i][L:]
        pre_ref[ch, h, 2] = jnp.where(eye, jnp.broadcast_to(dend[i], (L, L)), 0.0) + xtb[i][:HEAD_DIM]
        pre_ref[ch, h, 3] = xtb[i][HEAD_DIM:] + vtk[i]

    seqs = [(bi, h) for bi in range(nb) for h in range(N_HEADS)]
    s = [s_ref[bi, h] for bi, h in seqs]
    for ci_ in range(nc):
        for i, (bi, h) in enumerate(seqs):
            ch = bi * nc + ci_
            ybuf_ref[ch * L:(ch + 1) * L, h * HEAD_DIM:(h + 1) * HEAD_DIM] = (
                _dot_nt_hi(pre_ref[ch, h, 0], s[i]) + pre_ref[ch, h, 1])
        s = [_dot_hi(s[i], pre_ref[bi * nc + ci_, h, 2]) + pre_ref[bi * nc + ci_, h, 3]
             for i, (bi, h) in enumerate(seqs)]
    for i, (bi, h) in enumerate(seqs):
        s_ref[bi, h] = s[i]

    yb = ybuf_ref[...]
    mu = segsum(yb)
    yc = yb - mu
    y = yc * lax.rsqrt(segsum(yc * yc) + RWKV_GN_EPS) * lnx_ref[...]
    y_ref[...] = ((y + bonus) * g).astype(y_ref.dtype).reshape(nb, rows, GROUP_W)
    so_ref[...] = s_ref[...]
    sho_ref[...] = z3[:, rows - 1:rows, :]
    zext_ref[:, 7:8, :] = z3[:, rows - 1:rows, :]


def _rwkv_mixer(z, shift, s0, prm, seg, tril, precise):
    b, t, _ = z.shape
    L = CHUNK
    nc = min(t // L, 4)
    nb = 1 if nc > 1 else min(b, 4)
    rows = nc * L
    full = lambda a: pl.BlockSpec(a.shape, lambda i, j: (0,) * a.ndim)
    st_spec = pl.BlockSpec((nb, N_HEADS, HEAD_DIM, HEAD_DIM), lambda i, j: (i, 0, 0, 0))
    sh_spec = pl.BlockSpec((nb, 1, RWKV_COLS), lambda i, j: (i, 0, 0))
    consts = list(prm) + [seg.astype(BF16), tril.astype(BF16)]
    return pl.pallas_call(
        functools.partial(_rwkv_kernel, precise=precise, nb=nb, nc=nc),
        grid=(b // nb, t // rows),
        in_specs=[pl.BlockSpec((nb, rows, RWKV_COLS), lambda i, j: (i, j, 0)), sh_spec, st_spec]
                 + [full(a) for a in consts],
        out_specs=[pl.BlockSpec((nb, rows, GROUP_W), lambda i, j: (i, j, 0)), st_spec, sh_spec],
        out_shape=[jax.ShapeDtypeStruct((b, t, GROUP_W), F32 if precise else BF16),
                   jax.ShapeDtypeStruct((b, N_HEADS, HEAD_DIM, HEAD_DIM), F32),
                   jax.ShapeDtypeStruct((b, 1, RWKV_COLS), F32)],
        scratch_shapes=[pltpu.VMEM((nb, 8 + rows, RWKV_COLS), F32),
                        pltpu.VMEM((nb, N_HEADS, HEAD_DIM, HEAD_DIM), F32),
                        pltpu.VMEM((nb * rows, GROUP_W), F32),
                        pltpu.VMEM((nb * nc, N_HEADS, 4, HEAD_DIM, HEAD_DIM), F32)],
        compiler_params=_cp(("parallel", "arbitrary")),
        name="rwkv_mixer",
    )(z, shift, s0, *consts)


def _ret_kernel(z_ref, cos_ref, sin_ref, dm_ref, kdec_ref, qdec_ref, cdec_ref, gn_ref, seg_ref, s0_ref,
                y_ref, so_ref, s_ref, obuf_ref, *, precise, nb, nc):
    mm = _MM(precise)
    j = pl.program_id(1)
    L = CHUNK
    rows = nc * L

    @pl.when(j == 0)
    def _():
        s_ref[...] = s0_ref[...]

    c = GROUP_W
    lane = lax.broadcasted_iota(jnp.int32, (rows, c), 1)
    first_half = (lane % HEAD_DIM) < (HEAD_DIM // 2)
    cs, sn = cos_ref[...], sin_ref[...]
    kdec, qdec = kdec_ref[...], qdec_ref[...]

    def rope(x):
        partner = jnp.where(first_half, pltpu.roll(x, c - HEAD_DIM // 2, 1), pltpu.roll(x, HEAD_DIM // 2, 1))
        return x * cs + partner * sn

    each = lambda f, *lists: [f(*vals) for vals in zip(*lists)]
    keys = [(bi, ci, h) for bi in range(nb) for ci in range(nc) for h in range(N_HEADS)]
    qs, ks, vs, kds, qds, g_all = [], [], [], [], [], []
    for bi in range(nb):
        q = rope(z_ref[bi, :, 0:c])
        k = rope(z_ref[bi, :, c:2 * c]) * HEAD_DIM ** -0.5
        v = z_ref[bi, :, 2 * c:3 * c]
        g_all.append(z_ref[bi, :, 3 * c:4 * c])
        for ci in range(nc):
            rs = slice(ci * L, (ci + 1) * L)
            kd, qd = k[rs] * kdec, q[rs] * qdec
            for h in range(N_HEADS):
                hs = slice(h * HEAD_DIM, (h + 1) * HEAD_DIM)
                for lst, val in ((qs, q[rs]), (ks, k[rs]), (vs, v[rs]), (kds, kd), (qds, qd)):
                    lst.append(val[:, hs])
    att = [mm.nt(q_, k_) * dm_ref[h] for q_, k_, (_, _, h) in zip(qs, ks, keys)]
    o_in = each(mm.nn, att, vs)
    kv = each(mm.tn, kds, vs)
    s_prev = {}
    for bi in range(nb):
        for h in range(N_HEADS):
            s = s_ref[bi, h]
            for ci in range(nc):
                s_prev[(bi, ci, h)] = s
                s = s * cdec_ref[h] + kv[keys.index((bi, ci, h))]
            s_ref[bi, h] = s
    o_x = [mm.nn(qd_, s_prev[key]) for qd_, key in zip(qds, keys)]
    for i, (bi, ci, h) in enumerate(keys):
        r0 = bi * rows + ci * L
        obuf_ref[r0:r0 + L, h * HEAD_DIM:(h + 1) * HEAD_DIM] = o_in[i] + o_x[i]
    seg = seg_ref[...]
    segsum = lambda x: _dot_pieces(x, seg, 2 if precise else 1, lhs_exact=False)
    ob = obuf_ref[...]
    oc = ob - segsum(ob)
    o = oc * lax.rsqrt(segsum(oc * oc) + RET_GN_EPS) * gn_ref[...]
    g = jnp.concatenate(g_all, axis=0) if nb > 1 else g_all[0]
    y_ref[...] = (g * _sigmoid(g) * o).astype(y_ref.dtype).reshape(nb, rows, c)
    so_ref[...] = s_ref[...]


def _ret_mixer(z, s0, cos, sin, dmask, kdec, qdec, cdec, gn, seg, precise):
    b, t, _ = z.shape
    L = CHUNK
    nc = min(t // L, 4)
    nb = 1 if nc > 1 else min(b, 4)
    rows = nc * L
    full = lambda a: pl.BlockSpec(a.shape, lambda i, j: (0,) * a.ndim)
    st_spec = pl.BlockSpec((nb, N_HEADS, HEAD_DIM, HEAD_DIM), lambda i, j: (i, 0, 0, 0))
    tab_spec = pl.BlockSpec((rows, GROUP_W), lambda i, j: (j, 0))
    seg = seg.astype(BF16)
    return pl.pallas_call(
        functools.partial(_ret_kernel, precise=precise, nb=nb, nc=nc),
        grid=(b // nb, t // rows),
        in_specs=[pl.BlockSpec((nb, rows, 4 * GROUP_W), lambda i, j: (i, j, 0)), tab_spec, tab_spec,
                  full(dmask), full(kdec), full(qdec), full(cdec), full(gn), full(seg), st_spec],
        out_specs=[pl.BlockSpec((nb, rows, GROUP_W), lambda i, j: (i, j, 0)), st_spec],
        out_shape=[jax.ShapeDtypeStruct((b, t, GROUP_W), F32 if precise else BF16),
                   jax.ShapeDtypeStruct((b, N_HEADS, HEAD_DIM, HEAD_DIM), F32)],
        scratch_shapes=[pltpu.VMEM((nb, N_HEADS, HEAD_DIM, HEAD_DIM), F32),
                        pltpu.VMEM((nb * rows, GROUP_W), F32)],
        compiler_params=_cp(("parallel", "arbitrary")),
        name="ret_mixer",
    )(z, cos, sin, dmask, kdec, qdec, cdec, gn, seg, s0)


def _cumsum_lanes(x_ref, o_ref, tri, carry, width):
    blk = 128
    for c0 in range(0, width, blk):
        wd = min(blk, width - c0)
        cs = _dot_hi(x_ref[:, c0:c0 + wd], tri[:wd, :wd]) + carry
        o_ref[:, c0:c0 + wd] = cs
        carry = cs[:, wd - 1:wd]
    return carry


def _fox_gate_kernel(*refs, past, t):
    if past:
        zf_ref, b_ref, tri_ref, lfp_ref, lf_ref, cn_ref, cp_ref = refs
    else:
        zf_ref, b_ref, tri_ref, lf_ref, cn_ref = refs
    tri = tri_ref[...]
    x = zf_ref[...] + b_ref[...]
    lf_ref[...] = jnp.minimum(x, 0.0) - jnp.log(1.0 + jnp.exp(-jnp.abs(x)))
    carry = jnp.zeros((zf_ref.shape[0], 1), F32)
    if past:
        carry = _cumsum_lanes(lfp_ref, cp_ref, tri, carry, past)
    _cumsum_lanes(lf_ref, cn_ref, tri, carry, t)


def _fox_gate(zf_t, bias_rows, tri, lf_past_t):
    rows, t = zf_t.shape
    rt = min(rows, 32)
    past = 0 if lf_past_t is None else lf_past_t.shape[1]
    row_spec = lambda wd: pl.BlockSpec((rt, wd), lambda i: (i, 0))
    in_specs = [row_spec(t), row_spec(1), pl.BlockSpec(tri.shape, lambda i: (0, 0))]
    out_specs = [row_spec(t), row_spec(t)]
    out_shape = [jax.ShapeDtypeStruct((rows, t), F32), jax.ShapeDtypeStruct((rows, t), F32)]
    args = [zf_t, bias_rows, tri]
    if past:
        in_specs.append(row_spec(past))
        out_specs.append(row_spec(past))
        out_shape.append(jax.ShapeDtypeStruct((rows, past), F32))
        args.append(lf_past_t)
    return pl.pallas_call(
        functools.partial(_fox_gate_kernel, past=past, t=t),
        grid=(rows // rt,),
        in_specs=in_specs, out_specs=out_specs, out_shape=out_shape,
        compiler_params=_cp(("parallel",)),
        name="fox_gate",
    )(*args)


def _fox_attn_kernel(*refs, nkp, tq, tkn, precise):
    if nkp:
        (q_ref, kn_ref, vn_ref, og_ref, cq_ref, ckn_ref, kp_ref, vp_ref, ckp_ref,
         y_ref, m_ref, l_ref, acc_ref) = refs
    else:
        q_ref, kn_ref, vn_ref, og_ref, cq_ref, ckn_ref, y_ref, m_ref, l_ref, acc_ref = refs
    mm = _MM(precise)
    i, j = pl.program_id(1), pl.program_id(2)
    nk = pl.num_programs(2)

    @pl.when(j == 0)
    def _():
        m_ref[...] = jnp.full(m_ref.shape, -jnp.inf, F32)
        l_ref[...] = jnp.zeros(l_ref.shape, F32)
        acc_ref[...] = jnp.zeros(acc_ref.shape, F32)

    heads = range(N_HEADS)

    def per_head_lanes(cols):
        lane_head = lax.broadcasted_iota(jnp.int32, (tq, GROUP_W), 1) // HEAD_DIM
        out = jnp.broadcast_to(cols[N_HEADS - 1], (tq, GROUP_W))
        for h in range(N_HEADS - 2, -1, -1):
            out = jnp.where(lane_head == h, cols[h], out)
        return out

    def block(k_ref, v_ref, ck_ref, diagonal):
        q = q_ref[...] * HEAD_DIM ** -0.5
        k = k_ref[...]
        v = v_ref[...]
        tk = k.shape[0]
        cq = cq_ref[...]
        ck = ck_ref[...]
        lane_head = lax.broadcasted_iota(jnp.int32, (tk, GROUP_W), 1) // HEAD_DIM
        kbd = jnp.concatenate([jnp.where(lane_head == h, k, 0.0) for h in heads], axis=0)
        vbd = jnp.concatenate([jnp.where(lane_head == h, v, 0.0) for h in heads], axis=0)
        s_all = mm.nt(q, kbd)
        s = [s_all[:, h * tk:(h + 1) * tk] + cq[:, h:h + 1] - ck[h:h + 1, :] for h in heads]
        if diagonal:
            keep = (lax.broadcasted_iota(jnp.int32, (tq, tk), 1)
                    <= lax.broadcasted_iota(jnp.int32, (tq, tk), 0))
            s = [jnp.where(keep, s_h, -jnp.inf) for s_h in s]
        m_prev = [m_ref[h] for h in heads]
        m_new = [jnp.maximum(m_prev[h], jnp.max(s[h], axis=1, keepdims=True)) for h in heads]
        alpha = [jnp.exp(m_prev[h] - m_new[h]) for h in heads]
        p = [jnp.exp(s[h] - m_new[h]) for h in heads]
        pv = mm.nn(jnp.concatenate(p, axis=1), vbd)
        for h in heads:
            l_ref[h] = alpha[h] * l_ref[h] + jnp.sum(p[h], axis=1, keepdims=True)
            m_ref[h] = m_new[h]
        acc_ref[...] = per_head_lanes(alpha) * acc_ref[...] + pv

    if nkp:
        @pl.when(j < nkp)
        def _():
            block(kp_ref, vp_ref, ckp_ref, False)

    jn = j - nkp

    @pl.when(jnp.logical_and(jn >= 0, jn < i))
    def _():
        block(kn_ref, vn_ref, ckn_ref, False)

    @pl.when(jn == i)
    def _():
        block(kn_ref, vn_ref, ckn_ref, True)

    @pl.when(j == nk - 1)
    def _():
        o = acc_ref[...] / per_head_lanes([l_ref[h] for h in heads])
        y_ref[...] = (_sigmoid(og_ref[...]) * o).astype(y_ref.dtype)


def _fox_attn(zfox, cq, ck_new, precise, k_past=None, v_past=None, ck_past=None):
    b, t, _ = zfox.shape
    tq = tkn = min(t, 512)
    nq = t // tq
    nkp = 0
    tkp = 512
    if k_past is not None:
        nkp = k_past.shape[1] // tkp
    jn = lambda i, j: jnp.minimum(jnp.maximum(j - nkp, 0), i)
    jp = lambda j: jnp.minimum(j, nkp - 1)
    in_specs = [pl.BlockSpec((None, tq, GROUP_W), lambda bb, i, j: (bb, i, 0)),
                pl.BlockSpec((None, tkn, GROUP_W), lambda bb, i, j: (bb, jn(i, j), 1)),
                pl.BlockSpec((None, tkn, GROUP_W), lambda bb, i, j: (bb, jn(i, j), 2)),
                pl.BlockSpec((None, tq, GROUP_W), lambda bb, i, j: (bb, i, 3)),
                pl.BlockSpec((None, tq, N_HEADS), lambda bb, i, j: (bb, i, 0)),
                pl.BlockSpec((None, N_HEADS, tkn), lambda bb, i, j: (bb, 0, jn(i, j)))]
    args = [zfox, zfox, zfox, zfox, cq, ck_new]
    if nkp:
        in_specs += [pl.BlockSpec((None, tkp, GROUP_W), lambda bb, i, j: (bb, jp(j), 0)),
                     pl.BlockSpec((None, tkp, GROUP_W), lambda bb, i, j: (bb, jp(j), 0)),
                     pl.BlockSpec((None, N_HEADS, tkp), lambda bb, i, j: (bb, 0, jp(j)))]
        args += [k_past, v_past, ck_past]
    return pl.pallas_call(
        functools.partial(_fox_attn_kernel, nkp=nkp, tq=tq, tkn=tkn, precise=precise),
        grid=(b, nq, nkp + nq),
        in_specs=in_specs,
        out_specs=pl.BlockSpec((None, tq, GROUP_W), lambda bb, i, j: (bb, i, 0)),
        out_shape=jax.ShapeDtypeStruct((b, t, GROUP_W), F32 if precise else BF16),
        scratch_shapes=[pltpu.VMEM((N_HEADS, tq, 1), F32), pltpu.VMEM((N_HEADS, tq, 1), F32),
                        pltpu.VMEM((tq, GROUP_W), F32)],
        compiler_params=_cp(("parallel", "parallel", "arbitrary")),
        name="fox_attn",
    )(*args)


def _out_kernel(x_ref, ya_ref, yb_ref, yc_ref, yd_ref, *refs, precise):
    w_refs, o_ref = refs[:-1], refs[-1]
    dot = lambda a, b: jnp.dot(a, b, preferred_element_type=F32)
    acc = x_ref[...]
    for i, y_ref in enumerate((ya_ref, yb_ref, yc_ref, yd_ref)):
        if precise:
            yh, yl = _split(y_ref[...])
            acc = acc + (dot(yh, w_refs[0][i]) + (dot(yh, w_refs[1][i]) + dot(yl, w_refs[0][i])))
        else:
            acc = acc + dot(y_ref[...], w_refs[0][i])
    o_ref[...] = acc


def _out_proj(x, ys, ws):
    n = x.shape[0]
    tm = 512
    y_spec = pl.BlockSpec((tm, GROUP_W), lambda i: (i, 0))
    x_spec = pl.BlockSpec((tm, D_MODEL), lambda i: (i, 0))
    return pl.pallas_call(
        functools.partial(_out_kernel, precise=len(ws) == 2),
        grid=(n // tm,),
        in_specs=[x_spec, y_spec, y_spec, y_spec, y_spec]
                 + [pl.BlockSpec(w.shape, lambda i: (0, 0, 0)) for w in ws],
        out_specs=x_spec,
        out_shape=jax.ShapeDtypeStruct((n, D_MODEL), F32),
        compiler_params=_cp(("parallel",)),
        name="out_proj",
    )(x, *ys, *ws)


def _router_kernel(x_ref, g_ref, w_ref, b_ref, h_ref, gate_ref):
    h = _rmsnorm(x_ref[...], g_ref[...])
    h_ref[...] = h.astype(BF16)
    logits = _dot_hi(h, w_ref[...]) + b_ref[...]
    tm = logits.shape[0]
    lane_i = lax.broadcasted_iota(jnp.int32, (tm, ROUTER_PAD), 1)
    lane = lane_i.astype(F32)
    big = float(ROUTER_PAD)
    rmax = lambda a: jnp.max(a, axis=1, keepdims=True)
    rsum = lambda a: jnp.sum(a, axis=1, keepdims=True)
    first = lambda m: jnp.min(jnp.where(m, lane, big), axis=1, keepdims=True)

    is_c = jnp.logical_and(lane_i >= N_EXPERTS, lane_i < N_EXPERTS + N_GROUPS)
    lc = jnp.where(is_c, logits, -jnp.inf)
    ec = jnp.exp(lc - rmax(lc))
    pc_all = ec / rsum(ec)
    pc = rmax(pc_all)
    gi = first(jnp.logical_and(is_c, pc_all == pc)) - float(N_EXPERTS)

    is_f = (lane_i // EXPERTS_PER_GROUP).astype(F32) == gi
    lf = jnp.where(is_f, logits, -jnp.inf)
    ef = jnp.exp(lf - rmax(lf))
    pf = jnp.where(is_f, ef / rsum(ef), -1.0)
    t1 = rmax(pf)
    i1 = first(pf == t1)
    pf2 = jnp.where(lane == i1, -1.0, pf)
    t2 = rmax(pf2)
    i2 = first(pf2 == t2)
    den = t1 + t2
    gate_ref[...] = jnp.where(lane == i1, pc * (t1 / den), jnp.where(lane == i2, pc * (t2 / den), 0.0))


def _router(x, g, wr, br):
    n = x.shape[0]
    tm = 512
    return pl.pallas_call(
        _router_kernel,
        grid=(n // tm,),
        in_specs=[pl.BlockSpec((tm, D_MODEL), lambda i: (i, 0)),
                  pl.BlockSpec((1, D_MODEL), lambda i: (0, 0)),
                  pl.BlockSpec((D_MODEL, ROUTER_PAD), lambda i: (0, 0)),
                  pl.BlockSpec((1, ROUTER_PAD), lambda i: (0, 0))],
        out_specs=[pl.BlockSpec((tm, D_MODEL), lambda i: (i, 0)),
                   pl.BlockSpec((tm, ROUTER_PAD), lambda i: (i, 0))],
        out_shape=[jax.ShapeDtypeStruct((n, D_MODEL), BF16),
                   jax.ShapeDtypeStruct((n, ROUTER_PAD), F32)],
        compiler_params=_cp(("parallel",)),
        name="moe_router",
    )(x, g, wr, br)


def _moe_kernel(x_ref, h_ref, gate_ref, w1_ref, w3_ref, w2_ref, nf_ref, o_ref, *, final):
    e = pl.program_id(1)

    @pl.when(e == 0)
    def _():
        o_ref[...] = x_ref[...]

    h = h_ref[...]
    a = jnp.dot(h, w1_ref[...], preferred_element_type=F32)
    b = jnp.dot(h, w3_ref[...], preferred_element_type=F32)
    he = (a * _sigmoid(a) * b).astype(BF16)
    ye = jnp.dot(he, w2_ref[...], preferred_element_type=F32)
    gate = gate_ref[...]
    lane = lax.broadcasted_iota(jnp.int32, gate.shape, 1)
    gcol = jnp.sum(jnp.where(lane == e, gate, 0.0), axis=1, keepdims=True)
    o_ref[...] += gcol * ye

    if final:
        @pl.when(e == N_EXPERTS - 1)
        def _():
            o_ref[...] = _rmsnorm(o_ref[...], nf_ref[...])


def _moe(x, h, gate, w1, w3, w2, nf, final):
    n = x.shape[0]
    tm = min(n, 1024)
    return pl.pallas_call(
        functools.partial(_moe_kernel, final=final),
        grid=(n // tm, N_EXPERTS),
        in_specs=[pl.BlockSpec((tm, D_MODEL), lambda i, e: (i, 0)),
                  pl.BlockSpec((tm, D_MODEL), lambda i, e: (i, 0)),
                  pl.BlockSpec((tm, ROUTER_PAD), lambda i, e: (i, 0)),
                  pl.BlockSpec((None, D_MODEL, D_EXPERT), lambda i, e: (e, 0, 0)),
                  pl.BlockSpec((None, D_MODEL, D_EXPERT), lambda i, e: (e, 0, 0)),
                  pl.BlockSpec((None, D_EXPERT, D_MODEL), lambda i, e: (e, 0, 0)),
                  pl.BlockSpec((1, D_MODEL), lambda i, e: (0, 0))],
        out_specs=pl.BlockSpec((tm, D_MODEL), lambda i, e: (i, 0)),
        out_shape=jax.ShapeDtypeStruct((n, D_MODEL), F32),
        compiler_params=_cp(("parallel", "arbitrary")),
        name="moe_experts",
    )(x, h, gate, w1, w3, w2, nf)


def _prep_layer(l, p):
    w_in = p["w_in"][l]
    o_rwkv, o_ret, o_fox = GROUP_W, GROUP_W + RWKV_COLS, GROUP_W + RWKV_COLS + 4 * GROUP_W
    ff = jnp.pad(w_in[:, o_fox + 4 * GROUP_W:], ((0, 0), (0, FF_PAD - N_HEADS)))
    w_in_r = jnp.concatenate([w_in[:, o_rwkv:o_ret], w_in[:, o_ret:o_fox],
                              w_in[:, o_fox:o_fox + 4 * GROUP_W], w_in[:, :GROUP_W], ff], axis=1)
    pw = p["pool_w"][l]
    pg = GROUP_W // len(POOL_WINDOWS)
    w_bd = jnp.zeros((GROUP_W, GROUP_W), F32)
    for gi in range(len(POOL_WINDOWS)):
        w_bd = w_bd.at[gi * pg:(gi + 1) * pg, gi * pg:(gi + 1) * pg].set(pw[gi])
    row = lambda a: a.reshape(1, -1)
    rwkv = (row(p["rwkv_mu"][l]), row(p["rwkv_w0"][l]), p["rwkv_w2"][l], row(p["rwkv_a0"][l]),
            p["rwkv_a2"][l], p["rwkv_g2"][l], row(p["rwkv_kk"][l]), row(p["rwkv_ka"][l]),
            row(p["rwkv_rk"][l]), row(p["rwkv_lnx"][l]))
    wf = jnp.transpose(p["moe_wf"][l], (1, 0, 2)).reshape(D_MODEL, N_EXPERTS)
    wr = jnp.pad(jnp.concatenate([wf, p["moe_wc"][l]], axis=1),
                 ((0, 0), (0, ROUTER_PAD - N_EXPERTS - N_GROUPS)))
    br = jnp.pad(jnp.concatenate([p["moe_bf"][l].reshape(-1), p["moe_bc"][l]]),
                 (0, ROUTER_PAD - N_EXPERTS - N_GROUPS)).reshape(1, ROUTER_PAD)
    w_out4 = p["w_out"][l].reshape(4, GROUP_W, D_MODEL)
    precise = l < p["w_in"].shape[0] - 1
    split = lambda w: _split_bits(w) if precise else (w.astype(BF16),)
    return dict(
        precise=precise, n1=row(p["norm1_g"][l]), w_in=split(w_in_r), pool_w=w_bd,
        pool_scale=row(p["pool_scale"][l]),
        rwkv=rwkv, ret_gn=row(p["ret_gn"][l]), fox_bf=p["fox_bf"][l],
        w_out=split(w_out4), n2=row(p["norm2_g"][l]),
        wr=wr, br=br, w1=p["moe_w1"][l].astype(BF16), w3=p["moe_w3"][l].astype(BF16),
        w2=p["moe_w2"][l].astype(BF16))


def _tables(t, pos0):
    half = HEAD_DIM // 2
    inv = ROPE_BASE ** (-jnp.arange(half, dtype=F32) / half)
    ang = (pos0 + jnp.arange(t)).astype(F32)[:, None] * inv[None, :]
    cos, sin = jnp.cos(ang), jnp.sin(ang)
    cos_t = jnp.tile(jnp.concatenate([cos, cos], axis=1), (1, N_HEADS))
    sin_t = jnp.tile(jnp.concatenate([-sin, sin], axis=1), (1, N_HEADS))
    L = CHUNK
    log_g = jnp.log(jnp.array(RET_GAMMA, F32))
    idx = jnp.arange(L, dtype=F32)
    dmask = jnp.exp(log_g[:, None, None] * jnp.abs(idx[:, None] - idx[None, :]))
    lanes = lambda a: jnp.repeat(a, HEAD_DIM, axis=1)
    kdec = lanes(jnp.exp(log_g[None, :] * (L - 1.0 - idx)[:, None]))
    qdec = lanes(jnp.exp(log_g[None, :] * (idx + 1.0)[:, None]))
    cdec = jnp.broadcast_to(jnp.exp(log_g * L)[:, None, None], (N_HEADS, HEAD_DIM, HEAD_DIM))
    hid = jnp.arange(GROUP_W) // HEAD_DIM
    seg = jnp.where(hid[:, None] == hid[None, :], 1.0 / HEAD_DIM, 0.0).astype(F32)
    tril = jnp.tril(jnp.ones((L, L), F32))
    triu = jnp.triu(jnp.ones((128, 128), F32))
    return dict(cos=cos_t, sin=sin_t, dmask=dmask, kdec=kdec, qdec=qdec, cdec=cdec, seg=seg,
                tril=tril, triu=triu)


def _trunk(x, pos0, states, layers, norm_f):
    b, t, _ = x.shape
    n = b * t
    tb = _tables(t, pos0)
    xf = x.reshape(n, D_MODEL)
    nf = norm_f.reshape(1, D_MODEL)
    outs = [[] for _ in range(7)]
    for l, lp in enumerate(layers):
        pool_buf, shift, wkv, ret, k_past, v_past, lf_past = states[l]
        precise = lp["precise"]
        z_rwkv, z_ret, z_fox, z_pool, z_ff = _in_proj(xf, lp["n1"], lp["w_in"])
        buf16 = jnp.pad(pool_buf, ((0, 0), (POOL_PAD - POOL_BUF, 0), (0, 0)))
        y_a, pool_new = _pool_mixer(z_pool.reshape(b, t, GROUP_W), buf16, lp["pool_w"], lp["pool_scale"], pos0,
                                    precise)
        y_b, wkv_new, shift_new = _rwkv_mixer(z_rwkv.reshape(b, t, RWKV_COLS), shift, wkv, lp["rwkv"],
                                              tb["seg"], tb["tril"], precise)
        y_c, ret_new = _ret_mixer(z_ret.reshape(b, t, 4 * GROUP_W), ret, tb["cos"], tb["sin"], tb["dmask"],
                                  tb["kdec"], tb["qdec"], tb["cdec"], lp["ret_gn"], tb["seg"], precise)
        zf_t = jnp.transpose(z_ff[:, :N_HEADS].reshape(b, t, N_HEADS), (0, 2, 1)).reshape(b * N_HEADS, t)
        bias_rows = jnp.tile(lp["fox_bf"], b).reshape(b * N_HEADS, 1)
        zfox3 = z_fox.reshape(b, t, 4 * GROUP_W)
        if k_past is None:
            lf_t, c_new = _fox_gate(zf_t, bias_rows, tb["triu"], None)
            cn3 = c_new.reshape(b, N_HEADS, t)
            y_d = _fox_attn(zfox3, jnp.transpose(cn3, (0, 2, 1)), cn3, precise)
        else:
            p = k_past.shape[1]
            lfp_t = jnp.transpose(lf_past, (0, 2, 1)).reshape(b * N_HEADS, p)
            lf_t, c_new, c_past = _fox_gate(zf_t, bias_rows, tb["triu"], lfp_t)
            cn3 = c_new.reshape(b, N_HEADS, t)
            y_d = _fox_attn(zfox3, jnp.transpose(cn3, (0, 2, 1)), cn3, precise,
                            k_past.reshape(b, p, GROUP_W), v_past.reshape(b, p, GROUP_W),
                            c_past.reshape(b, N_HEADS, p))
        x1 = _out_proj(xf, [y.reshape(n, GROUP_W) for y in (y_a, y_b, y_c, y_d)], lp["w_out"])
        h, gate = _router(x1, lp["n2"], lp["wr"], lp["br"])
        xf = _moe(x1, h, gate, lp["w1"], lp["w3"], lp["w2"], nf, final=(l == len(layers) - 1))
        k_new = zfox3[:, :, GROUP_W:2 * GROUP_W].reshape(b, t, N_HEADS, HEAD_DIM)
        v_new = zfox3[:, :, 2 * GROUP_W:3 * GROUP_W].reshape(b, t, N_HEADS, HEAD_DIM)
        lf_new = jnp.transpose(lf_t.reshape(b, N_HEADS, t), (0, 2, 1))
        for lst, s in zip(outs, (pool_new, shift_new, wkv_new, ret_new, k_new, v_new, lf_new)):
            lst.append(s)
    return xf.reshape(b, t, D_MODEL), [jnp.stack(lst) for lst in outs]


def kernel(x_prompt, x_sample, state_pool, state_shift, state_wkv, state_ret, cache_fox_k, cache_fox_v, cache_fox_logf, norm1_g, w_in, pool_w, pool_scale, rwkv_mu, rwkv_w0, rwkv_w2, rwkv_a0, rwkv_a2, rwkv_g2, rwkv_kk, rwkv_ka, rwkv_rk, rwkv_lnx, ret_gn, fox_bf, w_out, norm2_g, moe_wc, moe_bc, moe_wf, moe_bf, moe_w1, moe_w3, moe_w2, norm_f):
    p = dict(norm1_g=norm1_g, w_in=w_in, pool_w=pool_w, pool_scale=pool_scale, rwkv_mu=rwkv_mu,
             rwkv_w0=rwkv_w0, rwkv_w2=rwkv_w2, rwkv_a0=rwkv_a0, rwkv_a2=rwkv_a2, rwkv_g2=rwkv_g2,
             rwkv_kk=rwkv_kk, rwkv_ka=rwkv_ka, rwkv_rk=rwkv_rk, rwkv_lnx=rwkv_lnx, ret_gn=ret_gn,
             fox_bf=fox_bf, w_out=w_out, norm2_g=norm2_g, moe_wc=moe_wc, moe_bc=moe_bc, moe_wf=moe_wf,
             moe_bf=moe_bf, moe_w1=moe_w1, moe_w3=moe_w3, moe_w2=moe_w2)
    depth = w_in.shape[0]
    layers = [_prep_layer(l, p) for l in range(depth)]
    b = x_prompt.shape[0]
    dt = x_prompt.dtype
    prompt_init = [(jnp.zeros((b, POOL_BUF, GROUP_W), dt), jnp.zeros((b, 1, RWKV_COLS), dt),
                    jnp.zeros((b, N_HEADS, HEAD_DIM, HEAD_DIM), dt),
                    jnp.zeros((b, N_HEADS, HEAD_DIM, HEAD_DIM), dt), None, None, None)
                   for _ in range(depth)]
    sample_init = [(state_pool[l], state_shift[l], state_wkv[l], state_ret[l], cache_fox_k[l],
                    cache_fox_v[l], cache_fox_logf[l]) for l in range(depth)]
    past = cache_fox_k.shape[2]
    y_prompt, new_p = _trunk(x_prompt, 0, prompt_init, layers, norm_f)
    y_sample, new_s = _trunk(x_sample, past, sample_init, layers, norm_f)
    return (y_prompt, y_sample, *new_p, *new_s)
```

```python
import functools

import jax
import jax.numpy as jnp
from jax import lax
from jax.experimental import pallas as pl
from jax.experimental.pallas import tpu as pltpu

F32 = jnp.float32
BF16 = jnp.bfloat16
HIGHEST = lax.Precision.HIGHEST

D_MODEL = 1024
DEPTH = 2
CHUNK = 64
GROUP_W = 256
HEAD_DIM = 64
N_HEADS = 4
POOL_WINDOWS = (2, 4, 8, 16)
POOL_BUF = 15
POOL_PAD = 16
RWKV_COLS = 1024
N_IN = 3332
FF_PAD = 128
RET_GAMMA = tuple(1.0 - 2.0 ** (-5 - h) for h in range(N_HEADS))
ROPE_BASE = 10000.0
N_GROUPS = 4
EXPERTS_PER_GROUP = 4
N_EXPERTS = 16
D_EXPERT = 512
RMS_EPS = 1e-6
RWKV_GN_EPS = 64e-5
RET_GN_EPS = 1e-5
SUB = 16
ROUTER_PAD = 128
VMEM_LIMIT = 48 * 1024 * 1024


def _cp(sem):
    return pltpu.CompilerParams(dimension_semantics=sem, vmem_limit_bytes=VMEM_LIMIT)


def _dot(a, b):
    return jnp.dot(a.astype(BF16), b.astype(BF16), preferred_element_type=F32)


def _dot_nt(a, b):
    return lax.dot_general(a.astype(BF16), b.astype(BF16), (((1,), (1,)), ((), ())),
                           preferred_element_type=F32)


def _dot_tn(a, b):
    return lax.dot_general(a.astype(BF16), b.astype(BF16), (((0,), (0,)), ((), ())),
                           preferred_element_type=F32)


def _split(a):
    hi = a.astype(BF16)
    return hi, (a - hi.astype(F32)).astype(BF16)


def _split_bits(w):
    bits = lax.bitcast_convert_type(w, jnp.uint32) & jnp.uint32(0xFFFF0000)
    hi = lax.bitcast_convert_type(bits, F32)
    return hi.astype(BF16), (w - hi).astype(BF16)


def _dg3(a, b, dims):
    ah, al = _split(a)
    bh, bl = _split(b)
    dg = lambda x, y: lax.dot_general(x, y, (dims, ((), ())), preferred_element_type=F32)
    return dg(ah, bh) + (dg(ah, bl) + dg(al, bh))


class _MM:
    def __init__(self, precise):
        if precise:
            self.nn = lambda a, b: _dg3(a, b, ((1,), (0,)))
            self.nt = lambda a, b: _dg3(a, b, ((1,), (1,)))
            self.tn = lambda a, b: _dg3(a, b, ((0,), (0,)))
        else:
            self.nn, self.nt, self.tn = _dot, _dot_nt, _dot_tn


def _pieces(a, n):
    out = []
    for i in range(n):
        p = a.astype(BF16)
        out.append(p)
        if i + 1 < n:
            a = a - p.astype(F32)
    return out


def _dot_pieces(a, b, n, lhs_exact):
    dot = lambda x, y: jnp.dot(x, y, preferred_element_type=F32)
    terms = [dot(a, p) for p in _pieces(b, n)] if lhs_exact else [dot(p, b) for p in _pieces(a, n)]
    acc = terms[-1]
    for t in terms[-2::-1]:
        acc = acc + t
    return acc


def _dot_hi(a, b):
    return jnp.dot(a, b, precision=HIGHEST, preferred_element_type=F32)


def _dot_nt_hi(a, b):
    return lax.dot_general(a, b, (((1,), (1,)), ((), ())), precision=HIGHEST,
                           preferred_element_type=F32)


def _sigmoid(x):
    return 1.0 / (1.0 + jnp.exp(-x))


def _softplus(x):
    return jnp.maximum(x, 0.0) + jnp.log(1.0 + jnp.exp(-jnp.abs(x)))


def _rmsnorm(x, g):
    return x * lax.rsqrt(jnp.mean(x * x, axis=-1, keepdims=True) + RMS_EPS) * g


_IN_SPLITS = ((0, 1024), (1024, 2048), (2048, 3072), (3072, 3328), (3328, 3456))


def _in_kernel(x_ref, g_ref, *refs, precise):
    nw = 2 if precise else 1
    w_refs, o_refs = refs[:nw], refs[nw:]
    h = _rmsnorm(x_ref[...], g_ref[...])
    tm = h.shape[0]
    dot = lambda a, b: jnp.dot(a, b, preferred_element_type=F32)
    if precise:
        hh, hl = _split(h)
        lhs = jnp.concatenate([hh, hl], axis=0)
    else:
        lhs = h.astype(BF16)
    for o_ref, (c0, c1) in zip(o_refs, _IN_SPLITS):
        acc = dot(lhs, w_refs[0][:, c0:c1])
        if precise:
            acc = acc[:tm] + (acc[tm:] + dot(hh, w_refs[1][:, c0:c1]))
        o_ref[...] = acc


def _in_proj(x, g, ws):
    n = x.shape[0]
    tm = 512
    ncol = ws[0].shape[1]
    widths = [c1 - c0 for c0, c1 in _IN_SPLITS]
    return pl.pallas_call(
        functools.partial(_in_kernel, precise=len(ws) == 2),
        grid=(n // tm,),
        in_specs=[pl.BlockSpec((tm, D_MODEL), lambda i: (i, 0)),
                  pl.BlockSpec((1, D_MODEL), lambda i: (0, 0))]
                 + [pl.BlockSpec((D_MODEL, ncol), lambda i: (0, 0), pipeline_mode=pl.Buffered(1)) for _ in ws],
        out_specs=[pl.BlockSpec((tm, wd), lambda i: (i, 0)) for wd in widths],
        out_shape=[jax.ShapeDtypeStruct((n, wd), F32) for wd in widths],
        compiler_params=_cp(("parallel",)),
        name="in_proj",
    )(x, g, *ws)


def _pool_kernel(u_ref, buf_ref, w_ref, sc_ref, y_ref, nb_ref, ext_ref, *, tt, pos0, precise):
    mm = _MM(precise)
    j = pl.program_id(1)

    @pl.when(j == 0)
    def _():
        ext_ref[0:POOL_PAD, :] = buf_ref[...]

    u = u_ref[...]
    ext_ref[POOL_PAD:POOL_PAD + tt, :] = u
    acc = u
    sums = []
    for k in range(1, POOL_PAD):
        acc = acc + ext_ref[POOL_PAD - k:POOL_PAD - k + tt, :]
        if k + 1 in POOL_WINDOWS:
            sums.append(acc)
    lane = lax.broadcasted_iota(jnp.int32, (tt, GROUP_W), 1)
    grp = lane // (GROUP_W // len(POOL_WINDOWS))
    pos = (pos0 + j * tt + lax.broadcasted_iota(jnp.int32, (tt, GROUP_W), 0)).astype(F32)
    mean = jnp.zeros((tt, GROUP_W), F32)
    for gi, w in enumerate(POOL_WINDOWS):
        cnt = jnp.minimum(float(w), pos + 1.0)
        mean = jnp.where(grp == gi, sums[gi] / cnt, mean)
    d = mean - u
    y_ref[...] = (mm.nn(d, w_ref[...]) * sc_ref[...]).astype(y_ref.dtype)
    nb_ref[...] = ext_ref[tt + 1:tt + POOL_PAD, :]
    ext_ref[0:POOL_PAD, :] = ext_ref[tt:tt + POOL_PAD, :]


def _pool_mixer(u, buf16, w_bd, scale, pos0, precise):
    b, t, _ = u.shape
    tt = min(t, 512)
    return pl.pallas_call(
        functools.partial(_pool_kernel, tt=tt, pos0=pos0, precise=precise),
        grid=(b, t // tt),
        in_specs=[pl.BlockSpec((None, tt, GROUP_W), lambda i, j: (i, j, 0)),
                  pl.BlockSpec((None, POOL_PAD, GROUP_W), lambda i, j: (i, 0, 0)),
                  pl.BlockSpec((GROUP_W, GROUP_W), lambda i, j: (0, 0)),
                  pl.BlockSpec((1, GROUP_W), lambda i, j: (0, 0))],
        out_specs=[pl.BlockSpec((None, tt, GROUP_W), lambda i, j: (i, j, 0)),
                   pl.BlockSpec((None, POOL_BUF, GROUP_W), lambda i, j: (i, 0, 0))],
        out_shape=[jax.ShapeDtypeStruct((b, t, GROUP_W), F32 if precise else BF16),
                   jax.ShapeDtypeStruct((b, POOL_BUF, GROUP_W), F32)],
        scratch_shapes=[pltpu.VMEM((POOL_PAD + tt, GROUP_W), F32)],
        compiler_params=_cp(("parallel", "arbitrary")),
        name="pool_mixer",
    )(u, buf16, w_bd, scale)


def _rwkv_kernel(z_ref, sh_ref, s0_ref, mu_ref, w0_ref, w2_ref, a0_ref, a2_ref, g2_ref, kk_ref,
                 ka_ref, rk_ref, lnx_ref, seg_ref, tril_ref,
                 y_ref, so_ref, sho_ref, zext_ref, s_ref, ybuf_ref, pre_ref, *, precise, nb, nc):
    mm = _MM(precise)
    lora = _dot_hi if precise else _dot
    j = pl.program_id(1)
    L = CHUNK
    rows = nc * L
    n = nb * rows

    @pl.when(j == 0)
    def _():
        zext_ref[:, 7:8, :] = sh_ref[...]
        s_ref[...] = s0_ref[...]

    z3 = z_ref[...]
    zext_ref[:, 8:8 + rows, :] = z3
    prev3 = zext_ref[:, 7:7 + rows, :]
    zs = (z3 + (prev3 - z3) * mu_ref[...]).reshape(n, RWKV_COLS)
    c = GROUP_W
    r, k, v = zs[:, 0:c], zs[:, c:2 * c], zs[:, 2 * c:3 * c]
    wl, al, gl = zs[:, 768:832], zs[:, 832:896], zs[:, 896:1024]
    w = -_softplus(-(w0_ref[...] + lora(jnp.tanh(wl), w2_ref[...]))) - 0.5
    lw = -jnp.exp(w)
    a = _sigmoid(a0_ref[...] + lora(al, a2_ref[...]))
    g = lora(_sigmoid(gl), g2_ref[...])
    seg = seg_ref[...]
    segsum = lambda x: _dot_pieces(x, seg, 2 if precise else 1, lhs_exact=False)
    kk = k * kk_ref[...]
    kk = kk / jnp.maximum(jnp.sqrt(segsum(kk * kk) * float(HEAD_DIM)), 1e-12)
    k2 = k * (1.0 + (a - 1.0) * ka_ref[...])
    bonus = segsum(r * k2 * rk_ref[...]) * float(HEAD_DIM) * v
    b = kk * a

    ri = lax.broadcasted_iota(jnp.int32, (L, L), 0)
    ci = lax.broadcasted_iota(jnp.int32, (L, L), 1)
    strict, incl, eye = ri > ci, ri >= ci, ri == ci
    blk = (ri // SUB) == (ci // SUB)
    eye_f = jnp.where(eye, 1.0, 0.0)
    tril = tril_ref[...]

    nch = nb * nc
    chains = [(ch, h) for ch in range(nch) for h in range(N_HEADS)]
    each = lambda f, *lists: [f(*vals) for vals in zip(*lists)]
    at, rt, bt, kt, bh, kh, dend, vv = [], [], [], [], [], [], [], []
    for ch in range(nch):
        rs = slice(ch * L, (ch + 1) * L)
        lwc = lw[rs]
        cum = _dot_pieces(tril, lwc, 3 if precise else 2, lhs_exact=True)
        pend = cum[L - 1:L, :]
        e_neg = jnp.exp(-cum)
        e_end = jnp.exp(pend - cum)
        at_c, rt_c = -kk[rs] * jnp.exp(cum - lwc), r[rs] * jnp.exp(cum)
        bt_c, kt_c, bh_c, kh_c = b[rs] * e_neg, k2[rs] * e_neg, b[rs] * e_end, k2[rs] * e_end
        dend_c, v_c = jnp.exp(pend), v[rs]
        for h in range(N_HEADS):
            hs = slice(h * HEAD_DIM, (h + 1) * HEAD_DIM)
            for lst, val in ((at, at_c), (rt, rt_c), (bt, bt_c), (kt, kt_c), (bh, bh_c), (kh, kh_c),
                             (dend, dend_c), (vv, v_c)):
                lst.append(val[:, hs])
    gm = each(lambda a_, r_, b_, k_: mm.nt(jnp.concatenate([a_, r_], axis=0),
                                           jnp.concatenate([b_, k_], axis=0)), at, rt, bt, kt)
    gkv = each(lambda g_, v_: mm.nn(jnp.concatenate([jnp.where(strict, g_[:L, L:], 0.0),
                                                     jnp.where(incl, g_[L:, L:], 0.0)], axis=0), v_),
               gm, vv)
    a_ab = each(lambda g_: jnp.where(strict, g_[:L, :L], 0.0), gm)
    g_b = each(lambda g_: jnp.where(incl, g_[L:, :L], 0.0), gm)
    dp = each(lambda a_: jnp.where(blk, a_, 0.0), a_ab)
    aoff = each(lambda a_: jnp.where(blk, 0.0, a_), a_ab)
    md = each(lambda d_: eye_f + d_, dp)
    for _ in range(3):
        dp = each(lambda d_: mm.nn(d_, d_), dp)
        md = each(lambda m_, d_: m_ + mm.nn(m_, d_), md, dp)
    nm = each(mm.nn, md, aoff)
    nm2 = each(lambda n_: mm.nn(n_, n_), nm)
    x = each(lambda m_, a_, g_: mm.nn(m_, jnp.concatenate([a_, g_[:L]], axis=1)), md, at, gkv)
    x = each(lambda n_, x_: x_ + mm.nn(n_, x_), nm2, x)
    x = each(lambda n_, x_: x_ + mm.nn(n_, x_), nm, x)
    gx = each(mm.nn, g_b, x)
    xtb = each(mm.tn, x, bh)
    vtk = each(mm.tn, vv, kh)
    for i, (ch, h) in enumerate(chains):
        pre_ref[ch, h, 0] = rt[i] + gx[i][:, :HEAD_DIM]
        pre_ref[ch, h, 1] = gx[i][:, HEAD_DIM:] + gkv[i][L:]
        pre_ref[ch, h, 2] = jnp.where(eye, jnp.broadcast_to(dend[i], (L, L)), 0.0) + xtb[i][:HEAD_DIM]
        pre_ref[ch, h, 3] = xtb[i][HEAD_DIM:] + vtk[i]

    seqs = [(bi, h) for bi in range(nb) for h in range(N_HEADS)]
    s = [s_ref[bi, h] for bi, h in seqs]
    for ci_ in range(nc):
        for i, (bi, h) in enumerate(seqs):
            ch = bi * nc + ci_
            ybuf_ref[ch * L:(ch + 1) * L, h * HEAD_DIM:(h + 1) * HEAD_DIM] = (
                _dot_nt_hi(pre_ref[ch, h, 0], s[i]) + pre_ref[ch, h, 1])
        s = [_dot_hi(s[i], pre_ref[bi * nc + ci_, h, 2]) + pre_ref[bi * nc + ci_, h, 3]
             for i, (bi, h) in enumerate(seqs)]
    for i, (bi, h) in enumerate(seqs):
        s_ref[bi, h] = s[i]

    yb = ybuf_ref[...]
    mu = segsum(yb)
    yc = yb - mu
    y = yc * lax.rsqrt(segsum(yc * yc) + RWKV_GN_EPS) * lnx_ref[...]
    y_ref[...] = ((y + bonus) * g).astype(y_ref.dtype).reshape(nb, rows, GROUP_W)
    so_ref[...] = s_ref[...]
    sho_ref[...] = z3[:, rows - 1:rows, :]
    zext_ref[:, 7:8, :] = z3[:, rows - 1:rows, :]


def _rwkv_mixer(z, shift, s0, prm, seg, tril, precise):
    b, t, _ = z.shape
    L = CHUNK
    nc = min(t // L, 4)
    nb = 1 if nc > 1 else min(b, 4)
    rows = nc * L
    full = lambda a: pl.BlockSpec(a.shape, lambda i, j: (0,) * a.ndim)
    st_spec = pl.BlockSpec((nb, N_HEADS, HEAD_DIM, HEAD_DIM), lambda i, j: (i, 0, 0, 0))
    sh_spec = pl.BlockSpec((nb, 1, RWKV_COLS), lambda i, j: (i, 0, 0))
    consts = list(prm) + [seg.astype(BF16), tril.astype(BF16)]
    return pl.pallas_call(
        functools.partial(_rwkv_kernel, precise=precise, nb=nb, nc=nc),
        grid=(b // nb, t // rows),
        in_specs=[pl.BlockSpec((nb, rows, RWKV_COLS), lambda i, j: (i, j, 0)), sh_spec, st_spec]
                 + [full(a) for a in consts],
        out_specs=[pl.BlockSpec((nb, rows, GROUP_W), lambda i, j: (i, j, 0)), st_spec, sh_spec],
        out_shape=[jax.ShapeDtypeStruct((b, t, GROUP_W), F32 if precise else BF16),
                   jax.ShapeDtypeStruct((b, N_HEADS, HEAD_DIM, HEAD_DIM), F32),
                   jax.ShapeDtypeStruct((b, 1, RWKV_COLS), F32)],
        scratch_shapes=[pltpu.VMEM((nb, 8 + rows, RWKV_COLS), F32),
                        pltpu.VMEM((nb, N_HEADS, HEAD_DIM, HEAD_DIM), F32),
                        pltpu.VMEM((nb * rows, GROUP_W), F32),
                        pltpu.VMEM((nb * nc, N_HEADS, 4, HEAD_DIM, HEAD_DIM), F32)],
        compiler_params=_cp(("parallel", "arbitrary")),
        name="rwkv_mixer",
    )(z, shift, s0, *consts)


def _ret_kernel(z_ref, cos_ref, sin_ref, dm_ref, kdec_ref, qdec_ref, cdec_ref, gn_ref, seg_ref, s0_ref,
                y_ref, so_ref, s_ref, obuf_ref, *, precise, nb, nc):
    mm = _MM(precise)
    j = pl.program_id(1)
    L = CHUNK
    rows = nc * L

    @pl.when(j == 0)
    def _():
        s_ref[...] = s0_ref[...]

    c = GROUP_W
    lane = lax.broadcasted_iota(jnp.int32, (rows, c), 1)
    first_half = (lane % HEAD_DIM) < (HEAD_DIM // 2)
    cs, sn = cos_ref[...], sin_ref[...]
    kdec, qdec = kdec_ref[...], qdec_ref[...]

    def rope(x):
        partner = jnp.where(first_half, pltpu.roll(x, c - HEAD_DIM // 2, 1), pltpu.roll(x, HEAD_DIM // 2, 1))
        return x * cs + partner * sn

    each = lambda f, *lists: [f(*vals) for vals in zip(*lists)]
    keys = [(bi, ci, h) for bi in range(nb) for ci in range(nc) for h in range(N_HEADS)]
    qs, ks, vs, kds, qds, g_all = [], [], [], [], [], []
    for bi in range(nb):
        q = rope(z_ref[bi, :, 0:c])
        k = rope(z_ref[bi, :, c:2 * c]) * HEAD_DIM ** -0.5
        v = z_ref[bi, :, 2 * c:3 * c]
        g_all.append(z_ref[bi, :, 3 * c:4 * c])
        for ci in range(nc):
            rs = slice(ci * L, (ci + 1) * L)
            kd, qd = k[rs] * kdec, q[rs] * qdec
            for h in range(N_HEADS):
                hs = slice(h * HEAD_DIM, (h + 1) * HEAD_DIM)
                for lst, val in ((qs, q[rs]), (ks, k[rs]), (vs, v[rs]), (kds, kd), (qds, qd)):
                    lst.append(val[:, hs])
    att = [mm.nt(q_, k_) * dm_ref[h] for q_, k_, (_, _, h) in zip(qs, ks, keys)]
    o_in = each(mm.nn, att, vs)
    kv = each(mm.tn, kds, vs)
    s_prev = {}
    for bi in range(nb):
        for h in range(N_HEADS):
            s = s_ref[bi, h]
            for ci in range(nc):
                s_prev[(bi, ci, h)] = s
                s = s * cdec_ref[h] + kv[keys.index((bi, ci, h))]
            s_ref[bi, h] = s
    o_x = [mm.nn(qd_, s_prev[key]) for qd_, key in zip(qds, keys)]
    for i, (bi, ci, h) in enumerate(keys):
        r0 = bi * rows + ci * L
        obuf_ref[r0:r0 + L, h * HEAD_DIM:(h + 1) * HEAD_DIM] = o_in[i] + o_x[i]
    seg = seg_ref[...]
    segsum = lambda x: _dot_pieces(x, seg, 2 if precise else 1, lhs_exact=False)
    ob = obuf_ref[...]
    oc = ob - segsum(ob)
    o = oc * lax.rsqrt(segsum(oc * oc) + RET_GN_EPS) * gn_ref[...]
    g = jnp.concatenate(g_all, axis=0) if nb > 1 else g_all[0]
    y_ref[...] = (g * _sigmoid(g) * o).astype(y_ref.dtype).reshape(nb, rows, c)
    so_ref[...] = s_ref[...]


def _ret_mixer(z, s0, cos, sin, dmask, kdec, qdec, cdec, gn, seg, precise):
    b, t, _ = z.shape
    L = CHUNK
    nc = min(t // L, 4)
    nb = 1 if nc > 1 else min(b, 4)
    rows = nc * L
    full = lambda a: pl.BlockSpec(a.shape, lambda i, j: (0,) * a.ndim)
    st_spec = pl.BlockSpec((nb, N_HEADS, HEAD_DIM, HEAD_DIM), lambda i, j: (i, 0, 0, 0))
    tab_spec = pl.BlockSpec((rows, GROUP_W), lambda i, j: (j, 0))
    seg = seg.astype(BF16)
    return pl.pallas_call(
        functools.partial(_ret_kernel, precise=precise, nb=nb, nc=nc),
        grid=(b // nb, t // rows),
        in_specs=[pl.BlockSpec((nb, rows, 4 * GROUP_W), lambda i, j: (i, j, 0)), tab_spec, tab_spec,
                  full(dmask), full(kdec), full(qdec), full(cdec), full(gn), full(seg), st_spec],
        out_specs=[pl.BlockSpec((nb, rows, GROUP_W), lambda i, j: (i, j, 0)), st_spec],
        out_shape=[jax.ShapeDtypeStruct((b, t, GROUP_W), F32 if precise else BF16),
                   jax.ShapeDtypeStruct((b, N_HEADS, HEAD_DIM, HEAD_DIM), F32)],
        scratch_shapes=[pltpu.VMEM((nb, N_HEADS, HEAD_DIM, HEAD_DIM), F32),
                        pltpu.VMEM((nb * rows, GROUP_W), F32)],
        compiler_params=_cp(("parallel", "arbitrary")),
        name="ret_mixer",
    )(z, cos, sin, dmask, kdec, qdec, cdec, gn, seg, s0)


def _cumsum_lanes(x_ref, o_ref, tri, carry, width):
    blk = 128
    for c0 in range(0, width, blk):
        wd = min(blk, width - c0)
        cs = _dot_hi(x_ref[:, c0:c0 + wd], tri[:wd, :wd]) + carry
        o_ref[:, c0:c0 + wd] = cs
        carry = cs[:, wd - 1:wd]
    return carry


def _fox_gate_kernel(*refs, past, t):
    if past:
        zf_ref, b_ref, tri_ref, lfp_ref, lf_ref, cn_ref, cp_ref = refs
    else:
        zf_ref, b_ref, tri_ref, lf_ref, cn_ref = refs
    tri = tri_ref[...]
    x = zf_ref[...] + b_ref[...]
    lf_ref[...] = jnp.minimum(x, 0.0) - jnp.log(1.0 + jnp.exp(-jnp.abs(x)))
    carry = jnp.zeros((zf_ref.shape[0], 1), F32)
    if past:
        carry = _cumsum_lanes(lfp_ref, cp_ref, tri, carry, past)
    _cumsum_lanes(lf_ref, cn_ref, tri, carry, t)


def _fox_gate(zf_t, bias_rows, tri, lf_past_t):
    rows, t = zf_t.shape
    rt = min(rows, 32)
    past = 0 if lf_past_t is None else lf_past_t.shape[1]
    row_spec = lambda wd: pl.BlockSpec((rt, wd), lambda i: (i, 0))
    in_specs = [row_spec(t), row_spec(1), pl.BlockSpec(tri.shape, lambda i: (0, 0))]
    out_specs = [row_spec(t), row_spec(t)]
    out_shape = [jax.ShapeDtypeStruct((rows, t), F32), jax.ShapeDtypeStruct((rows, t), F32)]
    args = [zf_t, bias_rows, tri]
    if past:
        in_specs.append(row_spec(past))
        out_specs.append(row_spec(past))
        out_shape.append(jax.ShapeDtypeStruct((rows, past), F32))
        args.append(lf_past_t)
    return pl.pallas_call(
        functools.partial(_fox_gate_kernel, past=past, t=t),
        grid=(rows // rt,),
        in_specs=in_specs, out_specs=out_specs, out_shape=out_shape,
        compiler_params=_cp(("parallel",)),
        name="fox_gate",
    )(*args)


def _fox_attn_kernel(q_ref, kn_ref, vn_ref, og_ref, cq_ref, ckn_ref, y_ref, m_ref, l_ref, acc_ref, *,
                     tq, precise):
    mm = _MM(precise)
    i, j = pl.program_id(1), pl.program_id(2)
    nk = pl.num_programs(2)

    @pl.when(j == 0)
    def _():
        m_ref[...] = jnp.full(m_ref.shape, -jnp.inf, F32)
        l_ref[...] = jnp.zeros(l_ref.shape, F32)
        acc_ref[...] = jnp.zeros(acc_ref.shape, F32)

    heads = range(N_HEADS)

    def per_head_lanes(cols):
        lane_head = lax.broadcasted_iota(jnp.int32, (tq, GROUP_W), 1) // HEAD_DIM
        out = jnp.broadcast_to(cols[N_HEADS - 1], (tq, GROUP_W))
        for h in range(N_HEADS - 2, -1, -1):
            out = jnp.where(lane_head == h, cols[h], out)
        return out

    def block(k_ref, v_ref, ck_ref, diagonal):
        q = q_ref[...] * HEAD_DIM ** -0.5
        k = k_ref[...]
        v = v_ref[...]
        tk = k.shape[0]
        cq = cq_ref[...]
        ck = ck_ref[...]
        lane_head = lax.broadcasted_iota(jnp.int32, (tk, GROUP_W), 1) // HEAD_DIM
        kbd = jnp.concatenate([jnp.where(lane_head == h, k, 0.0) for h in heads], axis=0)
        vbd = jnp.concatenate([jnp.where(lane_head == h, v, 0.0) for h in heads], axis=0)
        s_all = mm.nt(q, kbd)
        s = [s_all[:, h * tk:(h + 1) * tk] + cq[:, h:h + 1] - ck[h:h + 1, :] for h in heads]
        if diagonal:
            keep = (lax.broadcasted_iota(jnp.int32, (tq, tk), 1)
                    <= lax.broadcasted_iota(jnp.int32, (tq, tk), 0))
            s = [jnp.where(keep, s_h, -jnp.inf) for s_h in s]
        m_prev = [m_ref[h] for h in heads]
        m_new = [jnp.maximum(m_prev[h], jnp.max(s[h], axis=1, keepdims=True)) for h in heads]
        alpha = [jnp.exp(m_prev[h] - m_new[h]) for h in heads]
        p = [jnp.exp(s[h] - m_new[h]) for h in heads]
        pv = mm.nn(jnp.concatenate(p, axis=1), vbd)
        for h in heads:
            l_ref[h] = alpha[h] * l_ref[h] + jnp.sum(p[h], axis=1, keepdims=True)
            m_ref[h] = m_new[h]
        acc_ref[...] = per_head_lanes(alpha) * acc_ref[...] + pv

    @pl.when(j < i)
    def _():
        block(kn_ref, vn_ref, ckn_ref, False)

    @pl.when(j == i)
    def _():
        block(kn_ref, vn_ref, ckn_ref, True)

    @pl.when(j == nk - 1)
    def _():
        o = acc_ref[...] / per_head_lanes([l_ref[h] for h in heads])
        y_ref[...] = (_sigmoid(og_ref[...]) * o).astype(y_ref.dtype)


def _fox_attn(zfox, cq, ck_new, precise):
    b, t, _ = zfox.shape
    tq = min(t, 512)
    nq = t // tq
    jc = lambda i, j: jnp.minimum(j, i)
    in_specs = [pl.BlockSpec((None, tq, GROUP_W), lambda bb, i, j: (bb, i, 0)),
                pl.BlockSpec((None, tq, GROUP_W), lambda bb, i, j: (bb, jc(i, j), 1)),
                pl.BlockSpec((None, tq, GROUP_W), lambda bb, i, j: (bb, jc(i, j), 2)),
                pl.BlockSpec((None, tq, GROUP_W), lambda bb, i, j: (bb, i, 3)),
                pl.BlockSpec((None, tq, N_HEADS), lambda bb, i, j: (bb, i, 0)),
                pl.BlockSpec((None, N_HEADS, tq), lambda bb, i, j: (bb, 0, jc(i, j)))]
    return pl.pallas_call(
        functools.partial(_fox_attn_kernel, tq=tq, precise=precise),
        grid=(b, nq, nq),
        in_specs=in_specs,
        out_specs=pl.BlockSpec((None, tq, GROUP_W), lambda bb, i, j: (bb, i, 0)),
        out_shape=jax.ShapeDtypeStruct((b, t, GROUP_W), F32 if precise else BF16),
        scratch_shapes=[pltpu.VMEM((N_HEADS, tq, 1), F32), pltpu.VMEM((N_HEADS, tq, 1), F32),
                        pltpu.VMEM((tq, GROUP_W), F32)],
        compiler_params=_cp(("parallel", "parallel", "arbitrary")),
        name="fox_attn",
    )(zfox, zfox, zfox, zfox, cq, ck_new)


def _fox_past_kernel(zf_ref, cq_ref, ckn_ref, kt_ref, vt_ref, ckp_ref, y_ref, m_ref, l_ref, acc_ref, *,
                     nkp, nb, precise):
    mm = _MM(precise)
    j = pl.program_id(1)
    t = zf_ref.shape[1]
    c = GROUP_W
    chains = [(bi, h) for bi in range(nb) for h in range(N_HEADS)]
    hs = lambda h: slice(h * HEAD_DIM, (h + 1) * HEAD_DIM)

    @pl.when(j == 0)
    def _():
        m_ref[...] = jnp.full(m_ref.shape, -jnp.inf, F32)
        l_ref[...] = jnp.zeros(l_ref.shape, F32)
        acc_ref[...] = jnp.zeros(acc_ref.shape, F32)

    def online_update(s, pv_of):
        m_prev = [m_ref[i] for i in range(len(chains))]
        m_new = [jnp.maximum(mp, jnp.max(s_i, axis=1, keepdims=True)) for mp, s_i in zip(m_prev, s)]
        alpha = [jnp.exp(mp - mn) for mp, mn in zip(m_prev, m_new)]
        p = [jnp.exp(s_i - mn) for s_i, mn in zip(s, m_new)]
        pv = pv_of(p)
        for i in range(len(chains)):
            l_ref[i] = alpha[i] * l_ref[i] + jnp.sum(p[i], axis=1, keepdims=True)
            acc_ref[i] = alpha[i] * acc_ref[i] + pv[i]
            m_ref[i] = m_new[i]

    def queries():
        return [zf_ref[bi, :, hs(h)] * HEAD_DIM ** -0.5 for bi, h in chains]

    @pl.when(j < nkp)
    def _():
        q = queries()
        s = [mm.nn(q[i], kt_ref[bi, h]) + cq_ref[bi, :, h:h + 1] - ckp_ref[bi, h:h + 1, :]
             for i, (bi, h) in enumerate(chains)]
        online_update(s, lambda p: [mm.nt(p[i], vt_ref[bi, h]) for i, (bi, h) in enumerate(chains)])

    @pl.when(j == nkp)
    def _():
        q = queries()
        keep = lax.broadcasted_iota(jnp.int32, (t, t), 1) <= lax.broadcasted_iota(jnp.int32, (t, t), 0)
        s = [jnp.where(keep, mm.nt(q[i], zf_ref[bi, :, c + h * HEAD_DIM:c + (h + 1) * HEAD_DIM])
                       + cq_ref[bi, :, h:h + 1] - ckn_ref[bi, h:h + 1, :], -jnp.inf)
             for i, (bi, h) in enumerate(chains)]
        online_update(s, lambda p: [mm.nn(p[i], zf_ref[bi, :, 2 * c + h * HEAD_DIM:2 * c + (h + 1) * HEAD_DIM])
                                    for i, (bi, h) in enumerate(chains)])
        for i, (bi, h) in enumerate(chains):
            og = zf_ref[bi, :, 3 * c + h * HEAD_DIM:3 * c + (h + 1) * HEAD_DIM]
            y_ref[bi, :, hs(h)] = (_sigmoid(og) * (acc_ref[i] / l_ref[i])).astype(y_ref.dtype)


def _fox_attn_past(zfox, cq, ck_new, kt_all, vt_all, ck_past, layer, precise):
    b, t, _ = zfox.shape
    p = kt_all.shape[-1]
    tkp = 512
    nkp = p // tkp
    nb = min(b, 4)
    jp = lambda j: jnp.minimum(j, nkp - 1)
    cache_spec = pl.BlockSpec((None, nb, N_HEADS, HEAD_DIM, tkp), lambda i, j: (layer, i, 0, 0, jp(j)))
    return pl.pallas_call(
        functools.partial(_fox_past_kernel, nkp=nkp, nb=nb, precise=precise),
        grid=(b // nb, nkp + 1),
        in_specs=[pl.BlockSpec((nb, t, 4 * GROUP_W), lambda i, j: (i, 0, 0)),
                  pl.BlockSpec((nb, t, N_HEADS), lambda i, j: (i, 0, 0)),
                  pl.BlockSpec((nb, N_HEADS, t), lambda i, j: (i, 0, 0)),
                  cache_spec, cache_spec,
                  pl.BlockSpec((nb, N_HEADS, tkp), lambda i, j: (i, 0, jp(j)))],
        out_specs=pl.BlockSpec((nb, t, GROUP_W), lambda i, j: (i, 0, 0)),
        out_shape=jax.ShapeDtypeStruct((b, t, GROUP_W), F32 if precise else BF16),
        scratch_shapes=[pltpu.VMEM((nb * N_HEADS, t, 1), F32), pltpu.VMEM((nb * N_HEADS, t, 1), F32),
                        pltpu.VMEM((nb * N_HEADS, t, HEAD_DIM), F32)],
        compiler_params=_cp(("parallel", "arbitrary")),
        name="fox_attn_past",
    )(zfox, cq, ck_new, kt_all, vt_all, ck_past)


def _out_kernel(x_ref, ya_ref, yb_ref, yc_ref, yd_ref, *refs, precise):
    w_refs, o_ref = refs[:-1], refs[-1]
    dot = lambda a, b: jnp.dot(a, b, preferred_element_type=F32)
    acc = x_ref[...]
    for i, y_ref in enumerate((ya_ref, yb_ref, yc_ref, yd_ref)):
        if precise:
            yh, yl = _split(y_ref[...])
            acc = acc + (dot(yh, w_refs[0][i]) + (dot(yh, w_refs[1][i]) + dot(yl, w_refs[0][i])))
        else:
            acc = acc + dot(y_ref[...], w_refs[0][i])
    o_ref[...] = acc


def _out_proj(x, ys, ws):
    n = x.shape[0]
    tm = 512
    y_spec = pl.BlockSpec((tm, GROUP_W), lambda i: (i, 0))
    x_spec = pl.BlockSpec((tm, D_MODEL), lambda i: (i, 0))
    return pl.pallas_call(
        functools.partial(_out_kernel, precise=len(ws) == 2),
        grid=(n // tm,),
        in_specs=[x_spec, y_spec, y_spec, y_spec, y_spec]
                 + [pl.BlockSpec(w.shape, lambda i: (0, 0, 0)) for w in ws],
        out_specs=x_spec,
        out_shape=jax.ShapeDtypeStruct((n, D_MODEL), F32),
        compiler_params=_cp(("parallel",)),
        name="out_proj",
    )(x, *ys, *ws)


def _router_kernel(x_ref, g_ref, w_ref, b_ref, h_ref, gate_ref):
    h = _rmsnorm(x_ref[...], g_ref[...])
    h_ref[...] = h.astype(BF16)
    logits = _dot_hi(h, w_ref[...]) + b_ref[...]
    tm = logits.shape[0]
    lane_i = lax.broadcasted_iota(jnp.int32, (tm, ROUTER_PAD), 1)
    lane = lane_i.astype(F32)
    big = float(ROUTER_PAD)
    rmax = lambda a: jnp.max(a, axis=1, keepdims=True)
    rsum = lambda a: jnp.sum(a, axis=1, keepdims=True)
    first = lambda m: jnp.min(jnp.where(m, lane, big), axis=1, keepdims=True)

    is_c = jnp.logical_and(lane_i >= N_EXPERTS, lane_i < N_EXPERTS + N_GROUPS)
    lc = jnp.where(is_c, logits, -jnp.inf)
    ec = jnp.exp(lc - rmax(lc))
    pc_all = ec / rsum(ec)
    pc = rmax(pc_all)
    gi = first(jnp.logical_and(is_c, pc_all == pc)) - float(N_EXPERTS)

    is_f = (lane_i // EXPERTS_PER_GROUP).astype(F32) == gi
    lf = jnp.where(is_f, logits, -jnp.inf)
    ef = jnp.exp(lf - rmax(lf))
    pf = jnp.where(is_f, ef / rsum(ef), -1.0)
    t1 = rmax(pf)
    i1 = first(pf == t1)
    pf2 = jnp.where(lane == i1, -1.0, pf)
    t2 = rmax(pf2)
    i2 = first(pf2 == t2)
    den = t1 + t2
    gate_ref[...] = jnp.where(lane == i1, pc * (t1 / den), jnp.where(lane == i2, pc * (t2 / den), 0.0))


def _router(x, g, wr, br):
    n = x.shape[0]
    tm = 512
    return pl.pallas_call(
        _router_kernel,
        grid=(n // tm,),
        in_specs=[pl.BlockSpec((tm, D_MODEL), lambda i: (i, 0)),
                  pl.BlockSpec((1, D_MODEL), lambda i: (0, 0)),
                  pl.BlockSpec((D_MODEL, ROUTER_PAD), lambda i: (0, 0)),
                  pl.BlockSpec((1, ROUTER_PAD), lambda i: (0, 0))],
        out_specs=[pl.BlockSpec((tm, D_MODEL), lambda i: (i, 0)),
                   pl.BlockSpec((tm, ROUTER_PAD), lambda i: (i, 0))],
        out_shape=[jax.ShapeDtypeStruct((n, D_MODEL), BF16),
                   jax.ShapeDtypeStruct((n, ROUTER_PAD), F32)],
        compiler_params=_cp(("parallel",)),
        name="moe_router",
    )(x, g, wr, br)


def _moe_kernel(x_ref, h_ref, gate_ref, w1_ref, w3_ref, w2_ref, nf_ref, o_ref, *, final):
    e = pl.program_id(1)

    @pl.when(e == 0)
    def _():
        o_ref[...] = x_ref[...]

    h = h_ref[...]
    a = jnp.dot(h, w1_ref[...], preferred_element_type=F32)
    b = jnp.dot(h, w3_ref[...], preferred_element_type=F32)
    he = (a * _sigmoid(a) * b).astype(BF16)
    ye = jnp.dot(he, w2_ref[...], preferred_element_type=F32)
    gate = gate_ref[...]
    lane = lax.broadcasted_iota(jnp.int32, gate.shape, 1)
    gcol = jnp.sum(jnp.where(lane == e, gate, 0.0), axis=1, keepdims=True)
    o_ref[...] += gcol * ye

    if final:
        @pl.when(e == N_EXPERTS - 1)
        def _():
            o_ref[...] = _rmsnorm(o_ref[...], nf_ref[...])


def _moe(x, h, gate, w1, w3, w2, nf, final):
    n = x.shape[0]
    tm = min(n, 1024)
    return pl.pallas_call(
        functools.partial(_moe_kernel, final=final),
        grid=(n // tm, N_EXPERTS),
        in_specs=[pl.BlockSpec((tm, D_MODEL), lambda i, e: (i, 0)),
                  pl.BlockSpec((tm, D_MODEL), lambda i, e: (i, 0)),
                  pl.BlockSpec((tm, ROUTER_PAD), lambda i, e: (i, 0)),
                  pl.BlockSpec((None, D_MODEL, D_EXPERT), lambda i, e: (e, 0, 0)),
                  pl.BlockSpec((None, D_MODEL, D_EXPERT), lambda i, e: (e, 0, 0)),
                  pl.BlockSpec((None, D_EXPERT, D_MODEL), lambda i, e: (e, 0, 0)),
                  pl.BlockSpec((1, D_MODEL), lambda i, e: (0, 0))],
        out_specs=pl.BlockSpec((tm, D_MODEL), lambda i, e: (i, 0)),
        out_shape=jax.ShapeDtypeStruct((n, D_MODEL), F32),
        compiler_params=_cp(("parallel", "arbitrary")),
        name="moe_experts",
    )(x, h, gate, w1, w3, w2, nf)


def _prep_layer(l, p):
    w_in = p["w_in"][l]
    o_rwkv, o_ret, o_fox = GROUP_W, GROUP_W + RWKV_COLS, GROUP_W + RWKV_COLS + 4 * GROUP_W
    ff = jnp.pad(w_in[:, o_fox + 4 * GROUP_W:], ((0, 0), (0, FF_PAD - N_HEADS)))
    w_in_r = jnp.concatenate([w_in[:, o_rwkv:o_ret], w_in[:, o_ret:o_fox],
                              w_in[:, o_fox:o_fox + 4 * GROUP_W], w_in[:, :GROUP_W], ff], axis=1)
    pw = p["pool_w"][l]
    pg = GROUP_W // len(POOL_WINDOWS)
    w_bd = jnp.zeros((GROUP_W, GROUP_W), F32)
    for gi in range(len(POOL_WINDOWS)):
        w_bd = w_bd.at[gi * pg:(gi + 1) * pg, gi * pg:(gi + 1) * pg].set(pw[gi])
    row = lambda a: a.reshape(1, -1)
    rwkv = (row(p["rwkv_mu"][l]), row(p["rwkv_w0"][l]), p["rwkv_w2"][l], row(p["rwkv_a0"][l]),
            p["rwkv_a2"][l], p["rwkv_g2"][l], row(p["rwkv_kk"][l]), row(p["rwkv_ka"][l]),
            row(p["rwkv_rk"][l]), row(p["rwkv_lnx"][l]))
    wf = jnp.transpose(p["moe_wf"][l], (1, 0, 2)).reshape(D_MODEL, N_EXPERTS)
    wr = jnp.pad(jnp.concatenate([wf, p["moe_wc"][l]], axis=1),
                 ((0, 0), (0, ROUTER_PAD - N_EXPERTS - N_GROUPS)))
    br = jnp.pad(jnp.concatenate([p["moe_bf"][l].reshape(-1), p["moe_bc"][l]]),
                 (0, ROUTER_PAD - N_EXPERTS - N_GROUPS)).reshape(1, ROUTER_PAD)
    w_out4 = p["w_out"][l].reshape(4, GROUP_W, D_MODEL)
    precise = l < p["w_in"].shape[0] - 1
    split = lambda w: _split_bits(w) if precise else (w.astype(BF16),)
    return dict(
        precise=precise, n1=row(p["norm1_g"][l]), w_in=split(w_in_r), pool_w=w_bd,
        pool_scale=row(p["pool_scale"][l]),
        rwkv=rwkv, ret_gn=row(p["ret_gn"][l]), fox_bf=p["fox_bf"][l],
        w_out=split(w_out4), n2=row(p["norm2_g"][l]),
        wr=wr, br=br, w1=p["moe_w1"][l].astype(BF16), w3=p["moe_w3"][l].astype(BF16),
        w2=p["moe_w2"][l].astype(BF16))


def _tables(t, pos0):
    half = HEAD_DIM // 2
    inv = ROPE_BASE ** (-jnp.arange(half, dtype=F32) / half)
    ang = (pos0 + jnp.arange(t)).astype(F32)[:, None] * inv[None, :]
    cos, sin = jnp.cos(ang), jnp.sin(ang)
    cos_t = jnp.tile(jnp.concatenate([cos, cos], axis=1), (1, N_HEADS))
    sin_t = jnp.tile(jnp.concatenate([-sin, sin], axis=1), (1, N_HEADS))
    L = CHUNK
    log_g = jnp.log(jnp.array(RET_GAMMA, F32))
    idx = jnp.arange(L, dtype=F32)
    dmask = jnp.exp(log_g[:, None, None] * jnp.abs(idx[:, None] - idx[None, :]))
    lanes = lambda a: jnp.repeat(a, HEAD_DIM, axis=1)
    kdec = lanes(jnp.exp(log_g[None, :] * (L - 1.0 - idx)[:, None]))
    qdec = lanes(jnp.exp(log_g[None, :] * (idx + 1.0)[:, None]))
    cdec = jnp.broadcast_to(jnp.exp(log_g * L)[:, None, None], (N_HEADS, HEAD_DIM, HEAD_DIM))
    hid = jnp.arange(GROUP_W) // HEAD_DIM
    seg = jnp.where(hid[:, None] == hid[None, :], 1.0 / HEAD_DIM, 0.0).astype(F32)
    tril = jnp.tril(jnp.ones((L, L), F32))
    triu = jnp.triu(jnp.ones((128, 128), F32))
    return dict(cos=cos_t, sin=sin_t, dmask=dmask, kdec=kdec, qdec=qdec, cdec=cdec, seg=seg,
                tril=tril, triu=triu)


def _trunk(x, pos0, states, layers, norm_f, cache=None):
    b, t, _ = x.shape
    n = b * t
    tb = _tables(t, pos0)
    xf = x.reshape(n, D_MODEL)
    nf = norm_f.reshape(1, D_MODEL)
    outs = [[] for _ in range(7)]
    for l, lp in enumerate(layers):
        pool_buf, shift, wkv, ret, lf_past = states[l]
        precise = lp["precise"]
        z_rwkv, z_ret, z_fox, z_pool, z_ff = _in_proj(xf, lp["n1"], lp["w_in"])
        buf16 = jnp.pad(pool_buf, ((0, 0), (POOL_PAD - POOL_BUF, 0), (0, 0)))
        y_a, pool_new = _pool_mixer(z_pool.reshape(b, t, GROUP_W), buf16, lp["pool_w"], lp["pool_scale"], pos0,
                                    precise)
        y_b, wkv_new, shift_new = _rwkv_mixer(z_rwkv.reshape(b, t, RWKV_COLS), shift, wkv, lp["rwkv"],
                                              tb["seg"], tb["tril"], precise)
        y_c, ret_new = _ret_mixer(z_ret.reshape(b, t, 4 * GROUP_W), ret, tb["cos"], tb["sin"], tb["dmask"],
                                  tb["kdec"], tb["qdec"], tb["cdec"], lp["ret_gn"], tb["seg"], precise)
        zf_t = jnp.transpose(z_ff[:, :N_HEADS].reshape(b, t, N_HEADS), (0, 2, 1)).reshape(b * N_HEADS, t)
        bias_rows = jnp.tile(lp["fox_bf"], b).reshape(b * N_HEADS, 1)
        zfox3 = z_fox.reshape(b, t, 4 * GROUP_W)
        if cache is None:
            lf_t, c_new = _fox_gate(zf_t, bias_rows, tb["triu"], None)
            cn3 = c_new.reshape(b, N_HEADS, t)
            y_d = _fox_attn(zfox3, jnp.transpose(cn3, (0, 2, 1)), cn3, precise)
        else:
            p = lf_past.shape[1]
            lfp_t = jnp.transpose(lf_past, (0, 2, 1)).reshape(b * N_HEADS, p)
            lf_t, c_new, c_past = _fox_gate(zf_t, bias_rows, tb["triu"], lfp_t)
            cn3 = c_new.reshape(b, N_HEADS, t)
            y_d = _fox_attn_past(zfox3, jnp.transpose(cn3, (0, 2, 1)), cn3, cache[0], cache[1],
                                 c_past.reshape(b, N_HEADS, p), l, precise)
        x1 = _out_proj(xf, [y.reshape(n, GROUP_W) for y in (y_a, y_b, y_c, y_d)], lp["w_out"])
        h, gate = _router(x1, lp["n2"], lp["wr"], lp["br"])
        xf = _moe(x1, h, gate, lp["w1"], lp["w3"], lp["w2"], nf, final=(l == len(layers) - 1))
        k_new = zfox3[:, :, GROUP_W:2 * GROUP_W].reshape(b, t, N_HEADS, HEAD_DIM)
        v_new = zfox3[:, :, 2 * GROUP_W:3 * GROUP_W].reshape(b, t, N_HEADS, HEAD_DIM)
        lf_new = jnp.transpose(lf_t.reshape(b, N_HEADS, t), (0, 2, 1))
        for lst, s in zip(outs, (pool_new, shift_new, wkv_new, ret_new, k_new, v_new, lf_new)):
            lst.append(s)
    return xf.reshape(b, t, D_MODEL), [jnp.stack(lst) for lst in outs]


def kernel(x_prompt, x_sample, state_pool, state_shift, state_wkv, state_ret, cache_fox_k, cache_fox_v, cache_fox_logf, norm1_g, w_in, pool_w, pool_scale, rwkv_mu, rwkv_w0, rwkv_w2, rwkv_a0, rwkv_a2, rwkv_g2, rwkv_kk, rwkv_ka, rwkv_rk, rwkv_lnx, ret_gn, fox_bf, w_out, norm2_g, moe_wc, moe_bc, moe_wf, moe_bf, moe_w1, moe_w3, moe_w2, norm_f):
    p = dict(norm1_g=norm1_g, w_in=w_in, pool_w=pool_w, pool_scale=pool_scale, rwkv_mu=rwkv_mu,
             rwkv_w0=rwkv_w0, rwkv_w2=rwkv_w2, rwkv_a0=rwkv_a0, rwkv_a2=rwkv_a2, rwkv_g2=rwkv_g2,
             rwkv_kk=rwkv_kk, rwkv_ka=rwkv_ka, rwkv_rk=rwkv_rk, rwkv_lnx=rwkv_lnx, ret_gn=ret_gn,
             fox_bf=fox_bf, w_out=w_out, norm2_g=norm2_g, moe_wc=moe_wc, moe_bc=moe_bc, moe_wf=moe_wf,
             moe_bf=moe_bf, moe_w1=moe_w1, moe_w3=moe_w3, moe_w2=moe_w2)
    depth = w_in.shape[0]
    layers = [_prep_layer(l, p) for l in range(depth)]
    b = x_prompt.shape[0]
    dt = x_prompt.dtype
    prompt_init = [(jnp.zeros((b, POOL_BUF, GROUP_W), dt), jnp.zeros((b, 1, RWKV_COLS), dt),
                    jnp.zeros((b, N_HEADS, HEAD_DIM, HEAD_DIM), dt),
                    jnp.zeros((b, N_HEADS, HEAD_DIM, HEAD_DIM), dt), None)
                   for _ in range(depth)]
    sample_init = [(state_pool[l], state_shift[l], state_wkv[l], state_ret[l], cache_fox_logf[l])
                   for l in range(depth)]
    past = cache_fox_k.shape[2]
    cache = (jnp.transpose(cache_fox_k, (0, 1, 3, 4, 2)), jnp.transpose(cache_fox_v, (0, 1, 3, 4, 2)))
    y_prompt, new_p = _trunk(x_prompt, 0, prompt_init, layers, norm_f)
    y_sample, new_s = _trunk(x_sample, past, sample_init, layers, norm_f, cache)
    return (y_prompt, y_sample, *new_p, *new_s)
```

```python
import functools

import jax
import jax.numpy as jnp
from jax import lax
from jax.experimental import pallas as pl
from jax.experimental.pallas import tpu as pltpu

F32 = jnp.float32
BF16 = jnp.bfloat16
HIGHEST = lax.Precision.HIGHEST

D_MODEL = 1024
DEPTH = 2
CHUNK = 64
GROUP_W = 256
HEAD_DIM = 64
N_HEADS = 4
POOL_WINDOWS = (2, 4, 8, 16)
POOL_BUF = 15
POOL_PAD = 16
RWKV_COLS = 1024
N_IN = 3332
FF_PAD = 128
RET_GAMMA = tuple(1.0 - 2.0 ** (-5 - h) for h in range(N_HEADS))
ROPE_BASE = 10000.0
N_GROUPS = 4
EXPERTS_PER_GROUP = 4
N_EXPERTS = 16
D_EXPERT = 512
RMS_EPS = 1e-6
RWKV_GN_EPS = 64e-5
RET_GN_EPS = 1e-5
SUB = 16
ROUTER_PAD = 128
MOE_TILE = 1024
MOE_PAD = 128
MOE_WIN = 384
MOE_ROWS = MOE_TILE + N_GROUPS * MOE_PAD + (MOE_WIN - MOE_PAD)
VMEM_LIMIT = 48 * 1024 * 1024


def _cp(sem):
    return pltpu.CompilerParams(dimension_semantics=sem, vmem_limit_bytes=VMEM_LIMIT)


def _dot(a, b):
    return jnp.dot(a.astype(BF16), b.astype(BF16), preferred_element_type=F32)


def _dot_nt(a, b):
    return lax.dot_general(a.astype(BF16), b.astype(BF16), (((1,), (1,)), ((), ())),
                           preferred_element_type=F32)


def _dot_tn(a, b):
    return lax.dot_general(a.astype(BF16), b.astype(BF16), (((0,), (0,)), ((), ())),
                           preferred_element_type=F32)


def _split(a):
    hi = a.astype(BF16)
    return hi, (a - hi.astype(F32)).astype(BF16)


def _split_bits(w):
    bits = lax.bitcast_convert_type(w, jnp.uint32) & jnp.uint32(0xFFFF0000)
    hi = lax.bitcast_convert_type(bits, F32)
    return hi.astype(BF16), (w - hi).astype(BF16)


def _dg3(a, b, dims):
    ah, al = _split(a)
    bh, bl = _split(b)
    dg = lambda x, y: lax.dot_general(x, y, (dims, ((), ())), preferred_element_type=F32)
    return dg(ah, bh) + (dg(ah, bl) + dg(al, bh))


class _MM:
    def __init__(self, precise):
        if precise:
            self.nn = lambda a, b: _dg3(a, b, ((1,), (0,)))
            self.nt = lambda a, b: _dg3(a, b, ((1,), (1,)))
            self.tn = lambda a, b: _dg3(a, b, ((0,), (0,)))
        else:
            self.nn, self.nt, self.tn = _dot, _dot_nt, _dot_tn


def _pieces(a, n):
    out = []
    for i in range(n):
        p = a.astype(BF16)
        out.append(p)
        if i + 1 < n:
            a = a - p.astype(F32)
    return out


def _dot_pieces(a, b, n, lhs_exact):
    dot = lambda x, y: jnp.dot(x, y, preferred_element_type=F32)
    terms = [dot(a, p) for p in _pieces(b, n)] if lhs_exact else [dot(p, b) for p in _pieces(a, n)]
    acc = terms[-1]
    for t in terms[-2::-1]:
        acc = acc + t
    return acc


def _dot_hi(a, b):
    return jnp.dot(a, b, precision=HIGHEST, preferred_element_type=F32)


def _dot_nt_hi(a, b):
    return lax.dot_general(a, b, (((1,), (1,)), ((), ())), precision=HIGHEST,
                           preferred_element_type=F32)


def _sigmoid(x):
    return 1.0 / (1.0 + jnp.exp(-x))


def _softplus(x):
    return jnp.maximum(x, 0.0) + jnp.log(1.0 + jnp.exp(-jnp.abs(x)))


def _rmsnorm(x, g):
    return x * lax.rsqrt(jnp.mean(x * x, axis=-1, keepdims=True) + RMS_EPS) * g


_IN_SPLITS = ((0, 1024), (1024, 2048), (2048, 3072), (3072, 3328), (3328, 3456))


def _in_kernel(x_ref, g_ref, *refs, precise):
    nw = 2 if precise else 1
    w_refs, o_refs = refs[:nw], refs[nw:]
    h = _rmsnorm(x_ref[...], g_ref[...])
    tm = h.shape[0]
    dot = lambda a, b: jnp.dot(a, b, preferred_element_type=F32)
    if precise:
        hh, hl = _split(h)
        lhs = jnp.concatenate([hh, hl], axis=0)
    else:
        lhs = h.astype(BF16)
    for o_ref, (c0, c1) in zip(o_refs, _IN_SPLITS):
        acc = dot(lhs, w_refs[0][:, c0:c1])
        if precise:
            acc = acc[:tm] + (acc[tm:] + dot(hh, w_refs[1][:, c0:c1]))
        o_ref[...] = acc


def _in_proj(x, g, ws):
    n = x.shape[0]
    tm = 512
    ncol = ws[0].shape[1]
    widths = [c1 - c0 for c0, c1 in _IN_SPLITS]
    return pl.pallas_call(
        functools.partial(_in_kernel, precise=len(ws) == 2),
        grid=(n // tm,),
        in_specs=[pl.BlockSpec((tm, D_MODEL), lambda i: (i, 0)),
                  pl.BlockSpec((1, D_MODEL), lambda i: (0, 0))]
                 + [pl.BlockSpec((D_MODEL, ncol), lambda i: (0, 0), pipeline_mode=pl.Buffered(1)) for _ in ws],
        out_specs=[pl.BlockSpec((tm, wd), lambda i: (i, 0)) for wd in widths],
        out_shape=[jax.ShapeDtypeStruct((n, wd), F32) for wd in widths],
        compiler_params=_cp(("parallel",)),
        name="in_proj",
    )(x, g, *ws)


def _pool_kernel(u_ref, buf_ref, w_ref, sc_ref, y_ref, nb_ref, ext_ref, *, tt, pos0, precise):
    mm = _MM(precise)
    j = pl.program_id(1)

    @pl.when(j == 0)
    def _():
        ext_ref[0:POOL_PAD, :] = buf_ref[...]

    u = u_ref[...]
    ext_ref[POOL_PAD:POOL_PAD + tt, :] = u
    acc = u
    sums = []
    for k in range(1, POOL_PAD):
        acc = acc + ext_ref[POOL_PAD - k:POOL_PAD - k + tt, :]
        if k + 1 in POOL_WINDOWS:
            sums.append(acc)
    lane = lax.broadcasted_iota(jnp.int32, (tt, GROUP_W), 1)
    grp = lane // (GROUP_W // len(POOL_WINDOWS))
    pos = (pos0 + j * tt + lax.broadcasted_iota(jnp.int32, (tt, GROUP_W), 0)).astype(F32)
    mean = jnp.zeros((tt, GROUP_W), F32)
    for gi, w in enumerate(POOL_WINDOWS):
        cnt = jnp.minimum(float(w), pos + 1.0)
        mean = jnp.where(grp == gi, sums[gi] / cnt, mean)
    d = mean - u
    y_ref[...] = (mm.nn(d, w_ref[...]) * sc_ref[...]).astype(y_ref.dtype)
    nb_ref[...] = ext_ref[tt + 1:tt + POOL_PAD, :]
    ext_ref[0:POOL_PAD, :] = ext_ref[tt:tt + POOL_PAD, :]


def _pool_mixer(u, buf16, w_bd, scale, pos0, precise):
    b, t, _ = u.shape
    tt = min(t, 512)
    return pl.pallas_call(
        functools.partial(_pool_kernel, tt=tt, pos0=pos0, precise=precise),
        grid=(b, t // tt),
        in_specs=[pl.BlockSpec((None, tt, GROUP_W), lambda i, j: (i, j, 0)),
                  pl.BlockSpec((None, POOL_PAD, GROUP_W), lambda i, j: (i, 0, 0)),
                  pl.BlockSpec((GROUP_W, GROUP_W), lambda i, j: (0, 0)),
                  pl.BlockSpec((1, GROUP_W), lambda i, j: (0, 0))],
        out_specs=[pl.BlockSpec((None, tt, GROUP_W), lambda i, j: (i, j, 0)),
                   pl.BlockSpec((None, POOL_BUF, GROUP_W), lambda i, j: (i, 0, 0))],
        out_shape=[jax.ShapeDtypeStruct((b, t, GROUP_W), F32 if precise else BF16),
                   jax.ShapeDtypeStruct((b, POOL_BUF, GROUP_W), F32)],
        scratch_shapes=[pltpu.VMEM((POOL_PAD + tt, GROUP_W), F32)],
        compiler_params=_cp(("parallel", "arbitrary")),
        name="pool_mixer",
    )(u, buf16, w_bd, scale)


def _rwkv_kernel(z_ref, sh_ref, s0_ref, mu_ref, w0_ref, w2_ref, a0_ref, a2_ref, g2_ref, kk_ref,
                 ka_ref, rk_ref, lnx_ref, seg_ref, tril_ref,
                 y_ref, so_ref, sho_ref, zext_ref, s_ref, ybuf_ref, pre_ref, *, precise, nb, nc):
    mm = _MM(precise)
    lora = _dot_hi if precise else _dot
    j = pl.program_id(1)
    L = CHUNK
    rows = nc * L
    n = nb * rows

    @pl.when(j == 0)
    def _():
        zext_ref[:, 7:8, :] = sh_ref[...]
        s_ref[...] = s0_ref[...]

    z3 = z_ref[...]
    zext_ref[:, 8:8 + rows, :] = z3
    prev3 = zext_ref[:, 7:7 + rows, :]
    zs = (z3 + (prev3 - z3) * mu_ref[...]).reshape(n, RWKV_COLS)
    c = GROUP_W
    r, k, v = zs[:, 0:c], zs[:, c:2 * c], zs[:, 2 * c:3 * c]
    wl, al, gl = zs[:, 768:832], zs[:, 832:896], zs[:, 896:1024]
    w = -_softplus(-(w0_ref[...] + lora(jnp.tanh(wl), w2_ref[...]))) - 0.5
    lw = -jnp.exp(w)
    a = _sigmoid(a0_ref[...] + lora(al, a2_ref[...]))
    g = lora(_sigmoid(gl), g2_ref[...])
    seg = seg_ref[...]
    segsum = lambda x: _dot_pieces(x, seg, 2 if precise else 1, lhs_exact=False)
    kk = k * kk_ref[...]
    kk = kk / jnp.maximum(jnp.sqrt(segsum(kk * kk) * float(HEAD_DIM)), 1e-12)
    k2 = k * (1.0 + (a - 1.0) * ka_ref[...])
    bonus = segsum(r * k2 * rk_ref[...]) * float(HEAD_DIM) * v
    b = kk * a

    ri = lax.broadcasted_iota(jnp.int32, (L, L), 0)
    ci = lax.broadcasted_iota(jnp.int32, (L, L), 1)
    strict, incl, eye = ri > ci, ri >= ci, ri == ci
    blk = (ri // SUB) == (ci // SUB)
    eye_f = jnp.where(eye, 1.0, 0.0)
    tril = tril_ref[...]

    nch = nb * nc
    chains = [(ch, h) for ch in range(nch) for h in range(N_HEADS)]
    each = lambda f, *lists: [f(*vals) for vals in zip(*lists)]
    at, rt, bt, kt, bh, kh, dend, vv = [], [], [], [], [], [], [], []
    for ch in range(nch):
        rs = slice(ch * L, (ch + 1) * L)
        lwc = lw[rs]
        cum = _dot_pieces(tril, lwc, 3 if precise else 2, lhs_exact=True)
        pend = cum[L - 1:L, :]
        e_neg = jnp.exp(-cum)
        e_end = jnp.exp(pend - cum)
        at_c, rt_c = -kk[rs] * jnp.exp(cum - lwc), r[rs] * jnp.exp(cum)
        bt_c, kt_c, bh_c, kh_c = b[rs] * e_neg, k2[rs] * e_neg, b[rs] * e_end, k2[rs] * e_end
        dend_c, v_c = jnp.exp(pend), v[rs]
        for h in range(N_HEADS):
            hs = slice(h * HEAD_DIM, (h + 1) * HEAD_DIM)
            for lst, val in ((at, at_c), (rt, rt_c), (bt, bt_c), (kt, kt_c), (bh, bh_c), (kh, kh_c),
                             (dend, dend_c), (vv, v_c)):
                lst.append(val[:, hs])
    gm = each(lambda a_, r_, b_, k_: mm.nt(jnp.concatenate([a_, r_], axis=0),
                                           jnp.concatenate([b_, k_], axis=0)), at, rt, bt, kt)
    gkv = each(lambda g_, v_: mm.nn(jnp.concatenate([jnp.where(strict, g_[:L, L:], 0.0),
                                                     jnp.where(incl, g_[L:, L:], 0.0)], axis=0), v_),
               gm, vv)
    a_ab = each(lambda g_: jnp.where(strict, g_[:L, :L], 0.0), gm)
    g_b = each(lambda g_: jnp.where(incl, g_[L:, :L], 0.0), gm)
    dp = each(lambda a_: jnp.where(blk, a_, 0.0), a_ab)
    aoff = each(lambda a_: jnp.where(blk, 0.0, a_), a_ab)
    md = each(lambda d_: eye_f + d_, dp)
    for _ in range(3):
        dp = each(lambda d_: mm.nn(d_, d_), dp)
        md = each(lambda m_, d_: m_ + mm.nn(m_, d_), md, dp)
    nm = each(mm.nn, md, aoff)
    nm2 = each(lambda n_: mm.nn(n_, n_), nm)
    x = each(lambda m_, a_, g_: mm.nn(m_, jnp.concatenate([a_, g_[:L]], axis=1)), md, at, gkv)
    x = each(lambda n_, x_: x_ + mm.nn(n_, x_), nm2, x)
    x = each(lambda n_, x_: x_ + mm.nn(n_, x_), nm, x)
    gx = each(mm.nn, g_b, x)
    xtb = each(mm.tn, x, bh)
    vtk = each(mm.tn, vv, kh)
    for i, (ch, h) in enumerate(chains):
        pre_ref[ch, h, 0] = rt[i] + gx[i][:, :HEAD_DIM]
        pre_ref[ch, h, 1] = gx[i][:, HEAD_DIM:] + gkv[i][L:]
        pre_ref[ch, h, 2] = jnp.where(eye, jnp.broadcast_to(dend[i], (L, L)), 0.0) + xtb[i][:HEAD_DIM]
        pre_ref[ch, h, 3] = xtb[i][HEAD_DIM:] + vtk[i]

    seqs = [(bi, h) for bi in range(nb) for h in range(N_HEADS)]
    s = [s_ref[bi, h] for bi, h in seqs]
    for ci_ in range(nc):
        for i, (bi, h) in enumerate(seqs):
            ch = bi * nc + ci_
            ybuf_ref[ch * L:(ch + 1) * L, h * HEAD_DIM:(h + 1) * HEAD_DIM] = (
                _dot_nt_hi(pre_ref[ch, h, 0], s[i]) + pre_ref[ch, h, 1])
        s = [_dot_hi(s[i], pre_ref[bi * nc + ci_, h, 2]) + pre_ref[bi * nc + ci_, h, 3]
             for i, (bi, h) in enumerate(seqs)]
    for i, (bi, h) in enumerate(seqs):
        s_ref[bi, h] = s[i]

    yb = ybuf_ref[...]
    mu = segsum(yb)
    yc = yb - mu
    y = yc * lax.rsqrt(segsum(yc * yc) + RWKV_GN_EPS) * lnx_ref[...]
    y_ref[...] = ((y + bonus) * g).astype(y_ref.dtype).reshape(nb, rows, GROUP_W)
    so_ref[...] = s_ref[...]
    sho_ref[...] = z3[:, rows - 1:rows, :]
    zext_ref[:, 7:8, :] = z3[:, rows - 1:rows, :]


def _rwkv_mixer(z, shift, s0, prm, seg, tril, precise):
    b, t, _ = z.shape
    L = CHUNK
    nc = min(t // L, 4)
    nb = 1 if nc > 1 else min(b, 4)
    rows = nc * L
    full = lambda a: pl.BlockSpec(a.shape, lambda i, j: (0,) * a.ndim)
    st_spec = pl.BlockSpec((nb, N_HEADS, HEAD_DIM, HEAD_DIM), lambda i, j: (i, 0, 0, 0))
    sh_spec = pl.BlockSpec((nb, 1, RWKV_COLS), lambda i, j: (i, 0, 0))
    consts = list(prm) + [seg.astype(BF16), tril.astype(BF16)]
    return pl.pallas_call(
        functools.partial(_rwkv_kernel, precise=precise, nb=nb, nc=nc),
        grid=(b // nb, t // rows),
        in_specs=[pl.BlockSpec((nb, rows, RWKV_COLS), lambda i, j: (i, j, 0)), sh_spec, st_spec]
                 + [full(a) for a in consts],
        out_specs=[pl.BlockSpec((nb, rows, GROUP_W), lambda i, j: (i, j, 0)), st_spec, sh_spec],
        out_shape=[jax.ShapeDtypeStruct((b, t, GROUP_W), F32 if precise else BF16),
                   jax.ShapeDtypeStruct((b, N_HEADS, HEAD_DIM, HEAD_DIM), F32),
                   jax.ShapeDtypeStruct((b, 1, RWKV_COLS), F32)],
        scratch_shapes=[pltpu.VMEM((nb, 8 + rows, RWKV_COLS), F32),
                        pltpu.VMEM((nb, N_HEADS, HEAD_DIM, HEAD_DIM), F32),
                        pltpu.VMEM((nb * rows, GROUP_W), F32),
                        pltpu.VMEM((nb * nc, N_HEADS, 4, HEAD_DIM, HEAD_DIM), F32)],
        compiler_params=_cp(("parallel", "arbitrary")),
        name="rwkv_mixer",
    )(z, shift, s0, *consts)


def _ret_kernel(z_ref, cos_ref, sin_ref, dm_ref, kdec_ref, qdec_ref, cdec_ref, gn_ref, seg_ref, s0_ref,
                y_ref, so_ref, s_ref, obuf_ref, *, precise, nb, nc):
    mm = _MM(precise)
    j = pl.program_id(1)
    L = CHUNK
    rows = nc * L

    @pl.when(j == 0)
    def _():
        s_ref[...] = s0_ref[...]

    c = GROUP_W
    lane = lax.broadcasted_iota(jnp.int32, (rows, c), 1)
    first_half = (lane % HEAD_DIM) < (HEAD_DIM // 2)
    cs, sn = cos_ref[...], sin_ref[...]
    kdec, qdec = kdec_ref[...], qdec_ref[...]

    def rope(x):
        partner = jnp.where(first_half, pltpu.roll(x, c - HEAD_DIM // 2, 1), pltpu.roll(x, HEAD_DIM // 2, 1))
        return x * cs + partner * sn

    each = lambda f, *lists: [f(*vals) for vals in zip(*lists)]
    keys = [(bi, ci, h) for bi in range(nb) for ci in range(nc) for h in range(N_HEADS)]
    qs, ks, vs, kds, qds, g_all = [], [], [], [], [], []
    for bi in range(nb):
        q = rope(z_ref[bi, :, 0:c])
        k = rope(z_ref[bi, :, c:2 * c]) * HEAD_DIM ** -0.5
        v = z_ref[bi, :, 2 * c:3 * c]
        g_all.append(z_ref[bi, :, 3 * c:4 * c])
        for ci in range(nc):
            rs = slice(ci * L, (ci + 1) * L)
            kd, qd = k[rs] * kdec, q[rs] * qdec
            for h in range(N_HEADS):
                hs = slice(h * HEAD_DIM, (h + 1) * HEAD_DIM)
                for lst, val in ((qs, q[rs]), (ks, k[rs]), (vs, v[rs]), (kds, kd), (qds, qd)):
                    lst.append(val[:, hs])
    att = [mm.nt(q_, k_) * dm_ref[h] for q_, k_, (_, _, h) in zip(qs, ks, keys)]
    o_in = each(mm.nn, att, vs)
    kv = each(mm.tn, kds, vs)
    s_prev = {}
    for bi in range(nb):
        for h in range(N_HEADS):
            s = s_ref[bi, h]
            for ci in range(nc):
                s_prev[(bi, ci, h)] = s
                s = s * cdec_ref[h] + kv[keys.index((bi, ci, h))]
            s_ref[bi, h] = s
    o_x = [mm.nn(qd_, s_prev[key]) for qd_, key in zip(qds, keys)]
    for i, (bi, ci, h) in enumerate(keys):
        r0 = bi * rows + ci * L
        obuf_ref[r0:r0 + L, h * HEAD_DIM:(h + 1) * HEAD_DIM] = o_in[i] + o_x[i]
    seg = seg_ref[...]
    segsum = lambda x: _dot_pieces(x, seg, 2 if precise else 1, lhs_exact=False)
    ob = obuf_ref[...]
    oc = ob - segsum(ob)
    o = oc * lax.rsqrt(segsum(oc * oc) + RET_GN_EPS) * gn_ref[...]
    g = jnp.concatenate(g_all, axis=0) if nb > 1 else g_all[0]
    y_ref[...] = (g * _sigmoid(g) * o).astype(y_ref.dtype).reshape(nb, rows, c)
    so_ref[...] = s_ref[...]


def _ret_mixer(z, s0, cos, sin, dmask, kdec, qdec, cdec, gn, seg, precise):
    b, t, _ = z.shape
    L = CHUNK
    nc = min(t // L, 4)
    nb = 1 if nc > 1 else min(b, 4)
    rows = nc * L
    full = lambda a: pl.BlockSpec(a.shape, lambda i, j: (0,) * a.ndim)
    st_spec = pl.BlockSpec((nb, N_HEADS, HEAD_DIM, HEAD_DIM), lambda i, j: (i, 0, 0, 0))
    tab_spec = pl.BlockSpec((rows, GROUP_W), lambda i, j: (j, 0))
    seg = seg.astype(BF16)
    return pl.pallas_call(
        functools.partial(_ret_kernel, precise=precise, nb=nb, nc=nc),
        grid=(b // nb, t // rows),
        in_specs=[pl.BlockSpec((nb, rows, 4 * GROUP_W), lambda i, j: (i, j, 0)), tab_spec, tab_spec,
                  full(dmask), full(kdec), full(qdec), full(cdec), full(gn), full(seg), st_spec],
        out_specs=[pl.BlockSpec((nb, rows, GROUP_W), lambda i, j: (i, j, 0)), st_spec],
        out_shape=[jax.ShapeDtypeStruct((b, t, GROUP_W), F32 if precise else BF16),
                   jax.ShapeDtypeStruct((b, N_HEADS, HEAD_DIM, HEAD_DIM), F32)],
        scratch_shapes=[pltpu.VMEM((nb, N_HEADS, HEAD_DIM, HEAD_DIM), F32),
                        pltpu.VMEM((nb * rows, GROUP_W), F32)],
        compiler_params=_cp(("parallel", "arbitrary")),
        name="ret_mixer",
    )(z, cos, sin, dmask, kdec, qdec, cdec, gn, seg, s0)


def _cumsum_lanes(x_ref, o_ref, tri, carry, width):
    blk = 128
    for c0 in range(0, width, blk):
        wd = min(blk, width - c0)
        cs = _dot_hi(x_ref[:, c0:c0 + wd], tri[:wd, :wd]) + carry
        o_ref[:, c0:c0 + wd] = cs
        carry = cs[:, wd - 1:wd]
    return carry


def _fox_gate_kernel(*refs, past, t):
    if past:
        zf_ref, b_ref, tri_ref, lfp_ref, lf_ref, cn_ref, cp_ref = refs
    else:
        zf_ref, b_ref, tri_ref, lf_ref, cn_ref = refs
    tri = tri_ref[...]
    x = zf_ref[...] + b_ref[...]
    lf_ref[...] = jnp.minimum(x, 0.0) - jnp.log(1.0 + jnp.exp(-jnp.abs(x)))
    carry = jnp.zeros((zf_ref.shape[0], 1), F32)
    if past:
        carry = _cumsum_lanes(lfp_ref, cp_ref, tri, carry, past)
    _cumsum_lanes(lf_ref, cn_ref, tri, carry, t)


def _fox_gate(zf_t, bias_rows, tri, lf_past_t):
    rows, t = zf_t.shape
    rt = min(rows, 32)
    past = 0 if lf_past_t is None else lf_past_t.shape[1]
    row_spec = lambda wd: pl.BlockSpec((rt, wd), lambda i: (i, 0))
    in_specs = [row_spec(t), row_spec(1), pl.BlockSpec(tri.shape, lambda i: (0, 0))]
    out_specs = [row_spec(t), row_spec(t)]
    out_shape = [jax.ShapeDtypeStruct((rows, t), F32), jax.ShapeDtypeStruct((rows, t), F32)]
    args = [zf_t, bias_rows, tri]
    if past:
        in_specs.append(row_spec(past))
        out_specs.append(row_spec(past))
        out_shape.append(jax.ShapeDtypeStruct((rows, past), F32))
        args.append(lf_past_t)
    return pl.pallas_call(
        functools.partial(_fox_gate_kernel, past=past, t=t),
        grid=(rows // rt,),
        in_specs=in_specs, out_specs=out_specs, out_shape=out_shape,
        compiler_params=_cp(("parallel",)),
        name="fox_gate",
    )(*args)


def _fox_attn_kernel(q_ref, kn_ref, vn_ref, og_ref, cq_ref, ckn_ref, y_ref, m_ref, l_ref, acc_ref, *,
                     tq, precise):
    mm = _MM(precise)
    i, j = pl.program_id(1), pl.program_id(2)
    nk = pl.num_programs(2)

    @pl.when(j == 0)
    def _():
        m_ref[...] = jnp.full(m_ref.shape, -jnp.inf, F32)
        l_ref[...] = jnp.zeros(l_ref.shape, F32)
        acc_ref[...] = jnp.zeros(acc_ref.shape, F32)

    heads = range(N_HEADS)

    def per_head_lanes(cols):
        lane_head = lax.broadcasted_iota(jnp.int32, (tq, GROUP_W), 1) // HEAD_DIM
        out = jnp.broadcast_to(cols[N_HEADS - 1], (tq, GROUP_W))
        for h in range(N_HEADS - 2, -1, -1):
            out = jnp.where(lane_head == h, cols[h], out)
        return out

    def block(k_ref, v_ref, ck_ref, diagonal):
        q = q_ref[...] * HEAD_DIM ** -0.5
        k = k_ref[...]
        v = v_ref[...]
        tk = k.shape[0]
        cq = cq_ref[...]
        ck = ck_ref[...]
        lane_head = lax.broadcasted_iota(jnp.int32, (tk, GROUP_W), 1) // HEAD_DIM
        kbd = jnp.concatenate([jnp.where(lane_head == h, k, 0.0) for h in heads], axis=0)
        vbd = jnp.concatenate([jnp.where(lane_head == h, v, 0.0) for h in heads], axis=0)
        s_all = mm.nt(q, kbd)
        s = [s_all[:, h * tk:(h + 1) * tk] + cq[:, h:h + 1] - ck[h:h + 1, :] for h in heads]
        if diagonal:
            keep = (lax.broadcasted_iota(jnp.int32, (tq, tk), 1)
                    <= lax.broadcasted_iota(jnp.int32, (tq, tk), 0))
            s = [jnp.where(keep, s_h, -jnp.inf) for s_h in s]
        m_prev = [m_ref[h] for h in heads]
        m_new = [jnp.maximum(m_prev[h], jnp.max(s[h], axis=1, keepdims=True)) for h in heads]
        alpha = [jnp.exp(m_prev[h] - m_new[h]) for h in heads]
        p = [jnp.exp(s[h] - m_new[h]) for h in heads]
        pv = mm.nn(jnp.concatenate(p, axis=1), vbd)
        for h in heads:
            l_ref[h] = alpha[h] * l_ref[h] + jnp.sum(p[h], axis=1, keepdims=True)
            m_ref[h] = m_new[h]
        acc_ref[...] = per_head_lanes(alpha) * acc_ref[...] + pv

    @pl.when(j < i)
    def _():
        block(kn_ref, vn_ref, ckn_ref, False)

    @pl.when(j == i)
    def _():
        block(kn_ref, vn_ref, ckn_ref, True)

    @pl.when(j == nk - 1)
    def _():
        o = acc_ref[...] / per_head_lanes([l_ref[h] for h in heads])
        y_ref[...] = (_sigmoid(og_ref[...]) * o).astype(y_ref.dtype)


def _fox_attn(zfox, cq, ck_new, precise):
    b, t, _ = zfox.shape
    tq = min(t, 512)
    nq = t // tq
    jc = lambda i, j: jnp.minimum(j, i)
    in_specs = [pl.BlockSpec((None, tq, GROUP_W), lambda bb, i, j: (bb, i, 0)),
                pl.BlockSpec((None, tq, GROUP_W), lambda bb, i, j: (bb, jc(i, j), 1)),
                pl.BlockSpec((None, tq, GROUP_W), lambda bb, i, j: (bb, jc(i, j), 2)),
                pl.BlockSpec((None, tq, GROUP_W), lambda bb, i, j: (bb, i, 3)),
                pl.BlockSpec((None, tq, N_HEADS), lambda bb, i, j: (bb, i, 0)),
                pl.BlockSpec((None, N_HEADS, tq), lambda bb, i, j: (bb, 0, jc(i, j)))]
    return pl.pallas_call(
        functools.partial(_fox_attn_kernel, tq=tq, precise=precise),
        grid=(b, nq, nq),
        in_specs=in_specs,
        out_specs=pl.BlockSpec((None, tq, GROUP_W), lambda bb, i, j: (bb, i, 0)),
        out_shape=jax.ShapeDtypeStruct((b, t, GROUP_W), F32 if precise else BF16),
        scratch_shapes=[pltpu.VMEM((N_HEADS, tq, 1), F32), pltpu.VMEM((N_HEADS, tq, 1), F32),
                        pltpu.VMEM((tq, GROUP_W), F32)],
        compiler_params=_cp(("parallel", "parallel", "arbitrary")),
        name="fox_attn",
    )(zfox, zfox, zfox, zfox, cq, ck_new)


def _fox_past_kernel(zf_ref, cq_ref, ckn_ref, kt_ref, vt_ref, ckp_ref, y_ref, m_ref, l_ref, acc_ref, *,
                     nkp, nb, precise):
    mm = _MM(precise)
    j = pl.program_id(1)
    t = zf_ref.shape[1]
    c = GROUP_W
    chains = [(bi, h) for bi in range(nb) for h in range(N_HEADS)]
    hs = lambda h: slice(h * HEAD_DIM, (h + 1) * HEAD_DIM)

    @pl.when(j == 0)
    def _():
        m_ref[...] = jnp.full(m_ref.shape, -jnp.inf, F32)
        l_ref[...] = jnp.zeros(l_ref.shape, F32)
        acc_ref[...] = jnp.zeros(acc_ref.shape, F32)

    def online_update(s, pv_of):
        m_prev = [m_ref[i] for i in range(len(chains))]
        m_new = [jnp.maximum(mp, jnp.max(s_i, axis=1, keepdims=True)) for mp, s_i in zip(m_prev, s)]
        alpha = [jnp.exp(mp - mn) for mp, mn in zip(m_prev, m_new)]
        p = [jnp.exp(s_i - mn) for s_i, mn in zip(s, m_new)]
        pv = pv_of(p)
        for i in range(len(chains)):
            l_ref[i] = alpha[i] * l_ref[i] + jnp.sum(p[i], axis=1, keepdims=True)
            acc_ref[i] = alpha[i] * acc_ref[i] + pv[i]
            m_ref[i] = m_new[i]

    def queries():
        return [zf_ref[bi, :, hs(h)] * HEAD_DIM ** -0.5 for bi, h in chains]

    @pl.when(j < nkp)
    def _():
        q = queries()
        s = [mm.nn(q[i], kt_ref[bi, h]) + cq_ref[bi, :, h:h + 1] - ckp_ref[bi, h:h + 1, :]
             for i, (bi, h) in enumerate(chains)]
        online_update(s, lambda p: [mm.nt(p[i], vt_ref[bi, h]) for i, (bi, h) in enumerate(chains)])

    @pl.when(j == nkp)
    def _():
        q = queries()
        keep = lax.broadcasted_iota(jnp.int32, (t, t), 1) <= lax.broadcasted_iota(jnp.int32, (t, t), 0)
        s = [jnp.where(keep, mm.nt(q[i], zf_ref[bi, :, c + h * HEAD_DIM:c + (h + 1) * HEAD_DIM])
                       + cq_ref[bi, :, h:h + 1] - ckn_ref[bi, h:h + 1, :], -jnp.inf)
             for i, (bi, h) in enumerate(chains)]
        online_update(s, lambda p: [mm.nn(p[i], zf_ref[bi, :, 2 * c + h * HEAD_DIM:2 * c + (h + 1) * HEAD_DIM])
                                    for i, (bi, h) in enumerate(chains)])
        for i, (bi, h) in enumerate(chains):
            og = zf_ref[bi, :, 3 * c + h * HEAD_DIM:3 * c + (h + 1) * HEAD_DIM]
            y_ref[bi, :, hs(h)] = (_sigmoid(og) * (acc_ref[i] / l_ref[i])).astype(y_ref.dtype)


def _fox_attn_past(zfox, cq, ck_new, kt_all, vt_all, ck_past, layer, precise):
    b, t, _ = zfox.shape
    p = kt_all.shape[-1]
    tkp = 512
    nkp = p // tkp
    nb = min(b, 4)
    jp = lambda j: jnp.minimum(j, nkp - 1)
    cache_spec = pl.BlockSpec((None, nb, N_HEADS, HEAD_DIM, tkp), lambda i, j: (layer, i, 0, 0, jp(j)))
    return pl.pallas_call(
        functools.partial(_fox_past_kernel, nkp=nkp, nb=nb, precise=precise),
        grid=(b // nb, nkp + 1),
        in_specs=[pl.BlockSpec((nb, t, 4 * GROUP_W), lambda i, j: (i, 0, 0)),
                  pl.BlockSpec((nb, t, N_HEADS), lambda i, j: (i, 0, 0)),
                  pl.BlockSpec((nb, N_HEADS, t), lambda i, j: (i, 0, 0)),
                  cache_spec, cache_spec,
                  pl.BlockSpec((nb, N_HEADS, tkp), lambda i, j: (i, 0, jp(j)))],
        out_specs=pl.BlockSpec((nb, t, GROUP_W), lambda i, j: (i, 0, 0)),
        out_shape=jax.ShapeDtypeStruct((b, t, GROUP_W), F32 if precise else BF16),
        scratch_shapes=[pltpu.VMEM((nb * N_HEADS, t, 1), F32), pltpu.VMEM((nb * N_HEADS, t, 1), F32),
                        pltpu.VMEM((nb * N_HEADS, t, HEAD_DIM), F32)],
        compiler_params=_cp(("parallel", "arbitrary")),
        name="fox_attn_past",
    )(zfox, cq, ck_new, kt_all, vt_all, ck_past)


def _out_kernel(x_ref, ya_ref, yb_ref, yc_ref, yd_ref, *refs, precise):
    w_refs, o_ref = refs[:-1], refs[-1]
    dot = lambda a, b: jnp.dot(a, b, preferred_element_type=F32)
    acc = x_ref[...]
    for i, y_ref in enumerate((ya_ref, yb_ref, yc_ref, yd_ref)):
        if precise:
            yh, yl = _split(y_ref[...])
            acc = acc + (dot(yh, w_refs[0][i]) + (dot(yh, w_refs[1][i]) + dot(yl, w_refs[0][i])))
        else:
            acc = acc + dot(y_ref[...], w_refs[0][i])
    o_ref[...] = acc


def _out_proj(x, ys, ws):
    n = x.shape[0]
    tm = 512
    y_spec = pl.BlockSpec((tm, GROUP_W), lambda i: (i, 0))
    x_spec = pl.BlockSpec((tm, D_MODEL), lambda i: (i, 0))
    return pl.pallas_call(
        functools.partial(_out_kernel, precise=len(ws) == 2),
        grid=(n // tm,),
        in_specs=[x_spec, y_spec, y_spec, y_spec, y_spec]
                 + [pl.BlockSpec(w.shape, lambda i: (0, 0, 0)) for w in ws],
        out_specs=x_spec,
        out_shape=jax.ShapeDtypeStruct((n, D_MODEL), F32),
        compiler_params=_cp(("parallel",)),
        name="out_proj",
    )(x, *ys, *ws)


def _router_kernel(x_ref, g_ref, w_ref, b_ref, tril_ref, xs_ref, gs_ref, dest_ref, meta_ref):
    h = _rmsnorm(x_ref[...], g_ref[...])
    logits = _dot_hi(h, w_ref[...]) + b_ref[...]
    tm = logits.shape[0]
    lane_i = lax.broadcasted_iota(jnp.int32, (tm, ROUTER_PAD), 1)
    lane = lane_i.astype(F32)
    big = float(ROUTER_PAD)
    rmax = lambda a: jnp.max(a, axis=1, keepdims=True)
    rsum = lambda a: jnp.sum(a, axis=1, keepdims=True)
    first = lambda m: jnp.min(jnp.where(m, lane, big), axis=1, keepdims=True)

    is_c = jnp.logical_and(lane_i >= N_EXPERTS, lane_i < N_EXPERTS + N_GROUPS)
    lc = jnp.where(is_c, logits, -jnp.inf)
    ec = jnp.exp(lc - rmax(lc))
    pc_all = ec / rsum(ec)
    pc = rmax(pc_all)
    gi = first(jnp.logical_and(is_c, pc_all == pc)) - float(N_EXPERTS)

    is_f = (lane_i // EXPERTS_PER_GROUP).astype(F32) == gi
    lf = jnp.where(is_f, logits, -jnp.inf)
    ef = jnp.exp(lf - rmax(lf))
    pf = jnp.where(is_f, ef / rsum(ef), -1.0)
    t1 = rmax(pf)
    i1 = first(pf == t1)
    pf2 = jnp.where(lane == i1, -1.0, pf)
    t2 = rmax(pf2)
    i2 = first(pf2 == t2)
    den = t1 + t2
    gate = jnp.where(lane == i1, pc * (t1 / den), jnp.where(lane == i2, pc * (t2 / den), 0.0))

    onehot = jnp.where(lane == gi, 1.0, 0.0)
    rank = jnp.dot(tril_ref[...], onehot.astype(BF16), preferred_element_type=F32)
    cnt = jnp.sum(onehot, axis=0, keepdims=True)
    padded = jnp.floor((cnt + (MOE_PAD - 1.0)) * (1.0 / MOE_PAD)) * MOE_PAD
    lane1 = lax.broadcasted_iota(jnp.int32, (1, ROUTER_PAD), 1)
    off = jnp.zeros((1, ROUTER_PAD), F32)
    meta = jnp.zeros((1, ROUTER_PAD), F32)
    start = jnp.zeros((1, 1), F32)
    for g in range(N_GROUPS):
        size_g = jnp.sum(jnp.where(lane1 == g, padded, 0.0), axis=1, keepdims=True)
        off = off + jnp.where(lane1 == g, start, 0.0)
        meta = meta + jnp.where(lane1 == g, start, 0.0) + jnp.where(lane1 == N_GROUPS + g, size_g, 0.0)
        start = start + size_g
    dest = rsum(onehot * (off + rank))
    rows = lax.broadcasted_iota(jnp.int32, (tm, MOE_ROWS), 1).astype(F32)
    perm_t = jnp.where(rows == dest, 1.0, 0.0).astype(BF16)
    rhs = jnp.concatenate([h.astype(BF16)] + _pieces(gate, 3), axis=1)
    srt = lax.dot_general(perm_t, rhs, (((0,), (0,)), ((), ())), preferred_element_type=F32)
    xs_ref[...] = srt[:, :D_MODEL].astype(BF16)
    gs_ref[...] = (srt[:, D_MODEL:D_MODEL + ROUTER_PAD] + srt[:, D_MODEL + ROUTER_PAD:D_MODEL + 2 * ROUTER_PAD]
                   + srt[:, D_MODEL + 2 * ROUTER_PAD:])
    dest_ref[...] = dest
    meta_ref[...] = meta


def _router(x, g, wr, br, tril):
    n = x.shape[0]
    tm = MOE_TILE
    nt = n // tm
    return pl.pallas_call(
        _router_kernel,
        grid=(nt,),
        in_specs=[pl.BlockSpec((tm, D_MODEL), lambda i: (i, 0)),
                  pl.BlockSpec((1, D_MODEL), lambda i: (0, 0)),
                  pl.BlockSpec((D_MODEL, ROUTER_PAD), lambda i: (0, 0)),
                  pl.BlockSpec((1, ROUTER_PAD), lambda i: (0, 0)),
                  pl.BlockSpec((tm, tm), lambda i: (0, 0))],
        out_specs=[pl.BlockSpec((MOE_ROWS, D_MODEL), lambda i: (i, 0)),
                   pl.BlockSpec((MOE_ROWS, ROUTER_PAD), lambda i: (i, 0)),
                   pl.BlockSpec((tm, 1), lambda i: (i, 0)),
                   pl.BlockSpec((None, 1, ROUTER_PAD), lambda i: (i, 0, 0))],
        out_shape=[jax.ShapeDtypeStruct((nt * MOE_ROWS, D_MODEL), BF16),
                   jax.ShapeDtypeStruct((nt * MOE_ROWS, ROUTER_PAD), F32),
                   jax.ShapeDtypeStruct((n, 1), F32),
                   jax.ShapeDtypeStruct((nt, 1, ROUTER_PAD), F32)],
        compiler_params=_cp(("parallel",)),
        name="moe_router",
    )(x, g, wr, br, tril)


def _moe_kernel(meta_ref, xs_ref, gs_ref, w1_ref, w3_ref, w2_ref, o_ref):
    i, e = pl.program_id(0), pl.program_id(1)

    @pl.when(e == 0)
    def _():
        o_ref[...] = jnp.zeros(o_ref.shape, F32)

    g = e // EXPERTS_PER_GROUP
    off = meta_ref[i, g]
    size = meta_ref[i, N_GROUPS + g]
    for w in range(MOE_TILE // MOE_WIN + 1):
        @pl.when(w * MOE_WIN < size)
        def _():
            rows = pl.ds(pl.multiple_of(off + w * MOE_WIN, MOE_PAD), MOE_WIN)
            xw = xs_ref[rows, :]
            a = jnp.dot(xw, w1_ref[...], preferred_element_type=F32)
            b = jnp.dot(xw, w3_ref[...], preferred_element_type=F32)
            he = (a * _sigmoid(a) * b).astype(BF16)
            ye = jnp.dot(he, w2_ref[...], preferred_element_type=F32)
            gw = gs_ref[rows, :]
            lane = lax.broadcasted_iota(jnp.int32, gw.shape, 1)
            gcol = jnp.sum(jnp.where(lane == e, gw, 0.0), axis=1, keepdims=True)
            o_ref[rows, :] += gcol * ye


def _moe(meta, xs, gs, w1, w3, w2):
    nt = meta.shape[0]
    return pl.pallas_call(
        _moe_kernel,
        grid_spec=pltpu.PrefetchScalarGridSpec(
            num_scalar_prefetch=1,
            grid=(nt, N_EXPERTS),
            in_specs=[pl.BlockSpec((MOE_ROWS, D_MODEL), lambda i, e, m: (i, 0)),
                      pl.BlockSpec((MOE_ROWS, ROUTER_PAD), lambda i, e, m: (i, 0)),
                      pl.BlockSpec((None, D_MODEL, D_EXPERT), lambda i, e, m: (e, 0, 0)),
                      pl.BlockSpec((None, D_MODEL, D_EXPERT), lambda i, e, m: (e, 0, 0)),
                      pl.BlockSpec((None, D_EXPERT, D_MODEL), lambda i, e, m: (e, 0, 0))],
            out_specs=pl.BlockSpec((MOE_ROWS, D_MODEL), lambda i, e, m: (i, 0))),
        out_shape=jax.ShapeDtypeStruct((nt * MOE_ROWS, D_MODEL), F32),
        compiler_params=_cp(("parallel", "arbitrary")),
        name="moe_experts",
    )(meta, xs, gs, w1, w3, w2)


def _combine_kernel(x_ref, os_ref, dest_ref, nf_ref, o_ref, *, final):
    tm = x_ref.shape[0]
    rows = lax.broadcasted_iota(jnp.int32, (tm, MOE_ROWS), 1).astype(F32)
    perm_t = jnp.where(rows == dest_ref[...], 1.0, 0.0).astype(BF16)
    y = x_ref[...] + _dot_pieces(perm_t, os_ref[...], 2, lhs_exact=True)
    o_ref[...] = _rmsnorm(y, nf_ref[...]) if final else y


def _moe_combine(x, os, dest, nf, final):
    n = x.shape[0]
    tm = MOE_TILE
    return pl.pallas_call(
        functools.partial(_combine_kernel, final=final),
        grid=(n // tm,),
        in_specs=[pl.BlockSpec((tm, D_MODEL), lambda i: (i, 0)),
                  pl.BlockSpec((MOE_ROWS, D_MODEL), lambda i: (i, 0)),
                  pl.BlockSpec((tm, 1), lambda i: (i, 0)),
                  pl.BlockSpec((1, D_MODEL), lambda i: (0, 0))],
        out_specs=pl.BlockSpec((tm, D_MODEL), lambda i: (i, 0)),
        out_shape=jax.ShapeDtypeStruct((n, D_MODEL), F32),
        compiler_params=_cp(("parallel",)),
        name="moe_combine",
    )(x, os, dest, nf)


def _prep_layer(l, p):
    w_in = p["w_in"][l]
    o_rwkv, o_ret, o_fox = GROUP_W, GROUP_W + RWKV_COLS, GROUP_W + RWKV_COLS + 4 * GROUP_W
    ff = jnp.pad(w_in[:, o_fox + 4 * GROUP_W:], ((0, 0), (0, FF_PAD - N_HEADS)))
    w_in_r = jnp.concatenate([w_in[:, o_rwkv:o_ret], w_in[:, o_ret:o_fox],
                              w_in[:, o_fox:o_fox + 4 * GROUP_W], w_in[:, :GROUP_W], ff], axis=1)
    pw = p["pool_w"][l]
    pg = GROUP_W // len(POOL_WINDOWS)
    w_bd = jnp.zeros((GROUP_W, GROUP_W), F32)
    for gi in range(len(POOL_WINDOWS)):
        w_bd = w_bd.at[gi * pg:(gi + 1) * pg, gi * pg:(gi + 1) * pg].set(pw[gi])
    row = lambda a: a.reshape(1, -1)
    rwkv = (row(p["rwkv_mu"][l]), row(p["rwkv_w0"][l]), p["rwkv_w2"][l], row(p["rwkv_a0"][l]),
            p["rwkv_a2"][l], p["rwkv_g2"][l], row(p["rwkv_kk"][l]), row(p["rwkv_ka"][l]),
            row(p["rwkv_rk"][l]), row(p["rwkv_lnx"][l]))
    wf = jnp.transpose(p["moe_wf"][l], (1, 0, 2)).reshape(D_MODEL, N_EXPERTS)
    wr = jnp.pad(jnp.concatenate([wf, p["moe_wc"][l]], axis=1),
                 ((0, 0), (0, ROUTER_PAD - N_EXPERTS - N_GROUPS)))
    br = jnp.pad(jnp.concatenate([p["moe_bf"][l].reshape(-1), p["moe_bc"][l]]),
                 (0, ROUTER_PAD - N_EXPERTS - N_GROUPS)).reshape(1, ROUTER_PAD)
    w_out4 = p["w_out"][l].reshape(4, GROUP_W, D_MODEL)
    precise = l < p["w_in"].shape[0] - 1
    split = lambda w: _split_bits(w) if precise else (w.astype(BF16),)
    return dict(
        precise=precise, n1=row(p["norm1_g"][l]), w_in=split(w_in_r), pool_w=w_bd,
        pool_scale=row(p["pool_scale"][l]),
        rwkv=rwkv, ret_gn=row(p["ret_gn"][l]), fox_bf=p["fox_bf"][l],
        w_out=split(w_out4), n2=row(p["norm2_g"][l]),
        wr=wr, br=br, w1=p["moe_w1"][l].astype(BF16), w3=p["moe_w3"][l].astype(BF16),
        w2=p["moe_w2"][l].astype(BF16))


def _tables(t, pos0):
    half = HEAD_DIM // 2
    inv = ROPE_BASE ** (-jnp.arange(half, dtype=F32) / half)
    ang = (pos0 + jnp.arange(t)).astype(F32)[:, None] * inv[None, :]
    cos, sin = jnp.cos(ang), jnp.sin(ang)
    cos_t = jnp.tile(jnp.concatenate([cos, cos], axis=1), (1, N_HEADS))
    sin_t = jnp.tile(jnp.concatenate([-sin, sin], axis=1), (1, N_HEADS))
    L = CHUNK
    log_g = jnp.log(jnp.array(RET_GAMMA, F32))
    idx = jnp.arange(L, dtype=F32)
    dmask = jnp.exp(log_g[:, None, None] * jnp.abs(idx[:, None] - idx[None, :]))
    lanes = lambda a: jnp.repeat(a, HEAD_DIM, axis=1)
    kdec = lanes(jnp.exp(log_g[None, :] * (L - 1.0 - idx)[:, None]))
    qdec = lanes(jnp.exp(log_g[None, :] * (idx + 1.0)[:, None]))
    cdec = jnp.broadcast_to(jnp.exp(log_g * L)[:, None, None], (N_HEADS, HEAD_DIM, HEAD_DIM))
    hid = jnp.arange(GROUP_W) // HEAD_DIM
    seg = jnp.where(hid[:, None] == hid[None, :], 1.0 / HEAD_DIM, 0.0).astype(F32)
    tril = jnp.tril(jnp.ones((L, L), F32))
    triu = jnp.triu(jnp.ones((128, 128), F32))
    tril_moe = jnp.tril(jnp.ones((MOE_TILE, MOE_TILE), F32), -1).astype(BF16)
    return dict(cos=cos_t, sin=sin_t, dmask=dmask, kdec=kdec, qdec=qdec, cdec=cdec, seg=seg,
                tril=tril, triu=triu, tril_moe=tril_moe)


def _trunk(x, pos0, states, layers, norm_f, cache=None):
    b, t, _ = x.shape
    n = b * t
    tb = _tables(t, pos0)
    xf = x.reshape(n, D_MODEL)
    nf = norm_f.reshape(1, D_MODEL)
    outs = [[] for _ in range(7)]
    for l, lp in enumerate(layers):
        pool_buf, shift, wkv, ret, lf_past = states[l]
        precise = lp["precise"]
        z_rwkv, z_ret, z_fox, z_pool, z_ff = _in_proj(xf, lp["n1"], lp["w_in"])
        buf16 = jnp.pad(pool_buf, ((0, 0), (POOL_PAD - POOL_BUF, 0), (0, 0)))
        y_a, pool_new = _pool_mixer(z_pool.reshape(b, t, GROUP_W), buf16, lp["pool_w"], lp["pool_scale"], pos0,
                                    precise)
        y_b, wkv_new, shift_new = _rwkv_mixer(z_rwkv.reshape(b, t, RWKV_COLS), shift, wkv, lp["rwkv"],
                                              tb["seg"], tb["tril"], precise)
        y_c, ret_new = _ret_mixer(z_ret.reshape(b, t, 4 * GROUP_W), ret, tb["cos"], tb["sin"], tb["dmask"],
                                  tb["kdec"], tb["qdec"], tb["cdec"], lp["ret_gn"], tb["seg"], precise)
        zf_t = jnp.transpose(z_ff[:, :N_HEADS].reshape(b, t, N_HEADS), (0, 2, 1)).reshape(b * N_HEADS, t)
        bias_rows = jnp.tile(lp["fox_bf"], b).reshape(b * N_HEADS, 1)
        zfox3 = z_fox.reshape(b, t, 4 * GROUP_W)
        if cache is None:
            lf_t, c_new = _fox_gate(zf_t, bias_rows, tb["triu"], None)
            cn3 = c_new.reshape(b, N_HEADS, t)
            y_d = _fox_attn(zfox3, jnp.transpose(cn3, (0, 2, 1)), cn3, precise)
        else:
            p = lf_past.shape[1]
            lfp_t = jnp.transpose(lf_past, (0, 2, 1)).reshape(b * N_HEADS, p)
            lf_t, c_new, c_past = _fox_gate(zf_t, bias_rows, tb["triu"], lfp_t)
            cn3 = c_new.reshape(b, N_HEADS, t)
            y_d = _fox_attn_past(zfox3, jnp.transpose(cn3, (0, 2, 1)), cn3, cache[0], cache[1],
                                 c_past.reshape(b, N_HEADS, p), l, precise)
        x1 = _out_proj(xf, [y.reshape(n, GROUP_W) for y in (y_a, y_b, y_c, y_d)], lp["w_out"])
        xs, gs, dest, meta = _router(x1, lp["n2"], lp["wr"], lp["br"], tb["tril_moe"])
        meta = meta[:, 0, :2 * N_GROUPS].astype(jnp.int32)
        ys = _moe(meta, xs, gs, lp["w1"], lp["w3"], lp["w2"])
        xf = _moe_combine(x1, ys, dest, nf, final=(l == len(layers) - 1))
        k_new = zfox3[:, :, GROUP_W:2 * GROUP_W].reshape(b, t, N_HEADS, HEAD_DIM)
        v_new = zfox3[:, :, 2 * GROUP_W:3 * GROUP_W].reshape(b, t, N_HEADS, HEAD_DIM)
        lf_new = jnp.transpose(lf_t.reshape(b, N_HEADS, t), (0, 2, 1))
        for lst, s in zip(outs, (pool_new, shift_new, wkv_new, ret_new, k_new, v_new, lf_new)):
            lst.append(s)
    return xf.reshape(b, t, D_MODEL), [jnp.stack(lst) for lst in outs]


def kernel(x_prompt, x_sample, state_pool, state_shift, state_wkv, state_ret, cache_fox_k, cache_fox_v, cache_fox_logf, norm1_g, w_in, pool_w, pool_scale, rwkv_mu, rwkv_w0, rwkv_w2, rwkv_a0, rwkv_a2, rwkv_g2, rwkv_kk, rwkv_ka, rwkv_rk, rwkv_lnx, ret_gn, fox_bf, w_out, norm2_g, moe_wc, moe_bc, moe_wf, moe_bf, moe_w1, moe_w3, moe_w2, norm_f):
    p = dict(norm1_g=norm1_g, w_in=w_in, pool_w=pool_w, pool_scale=pool_scale, rwkv_mu=rwkv_mu,
             rwkv_w0=rwkv_w0, rwkv_w2=rwkv_w2, rwkv_a0=rwkv_a0, rwkv_a2=rwkv_a2, rwkv_g2=rwkv_g2,
             rwkv_kk=rwkv_kk, rwkv_ka=rwkv_ka, rwkv_rk=rwkv_rk, rwkv_lnx=rwkv_lnx, ret_gn=ret_gn,
             fox_bf=fox_bf, w_out=w_out, norm2_g=norm2_g, moe_wc=moe_wc, moe_bc=moe_bc, moe_wf=moe_wf,
             moe_bf=moe_bf, moe_w1=moe_w1, moe_w3=moe_w3, moe_w2=moe_w2)
    depth = w_in.shape[0]
    layers = [_prep_layer(l, p) for l in range(depth)]
    b = x_prompt.shape[0]
    dt = x_prompt.dtype
    prompt_init = [(jnp.zeros((b, POOL_BUF, GROUP_W), dt), jnp.zeros((b, 1, RWKV_COLS), dt),
                    jnp.zeros((b, N_HEADS, HEAD_DIM, HEAD_DIM), dt),
                    jnp.zeros((b, N_HEADS, HEAD_DIM, HEAD_DIM), dt), None)
                   for _ in range(depth)]
    sample_init = [(state_pool[l], state_shift[l], state_wkv[l], state_ret[l], cache_fox_logf[l])
                   for l in range(depth)]
    past = cache_fox_k.shape[2]
    cache = (jnp.transpose(cache_fox_k, (0, 1, 3, 4, 2)), jnp.transpose(cache_fox_v, (0, 1, 3, 4, 2)))
    y_prompt, new_p = _trunk(x_prompt, 0, prompt_init, layers, norm_f)
    y_sample, new_s = _trunk(x_sample, past, sample_init, layers, norm_f, cache)
    return (y_prompt, y_sample, *new_p, *new_s)
```

```python
import functools

import jax
import jax.numpy as jnp
from jax import lax
from jax.experimental import pallas as pl
from jax.experimental.pallas import tpu as pltpu

F32 = jnp.float32
BF16 = jnp.bfloat16
HIGHEST = lax.Precision.HIGHEST

D_MODEL = 1024
DEPTH = 2
CHUNK = 64
GROUP_W = 256
HEAD_DIM = 64
N_HEADS = 4
POOL_WINDOWS = (2, 4, 8, 16)
POOL_BUF = 15
POOL_PAD = 16
RWKV_COLS = 1024
N_IN = 3332
FF_PAD = 128
RET_GAMMA = tuple(1.0 - 2.0 ** (-5 - h) for h in range(N_HEADS))
ROPE_BASE = 10000.0
N_GROUPS = 4
EXPERTS_PER_GROUP = 4
N_EXPERTS = 16
D_EXPERT = 512
RMS_EPS = 1e-6
RWKV_GN_EPS = 64e-5
RET_GN_EPS = 1e-5
SUB = 16
ROUTER_PAD = 128
MOE_TILE = 1024
MOE_PAD = 64
MOE_WIN = 320
MOE_ROWS = MOE_TILE + N_GROUPS * MOE_PAD + (MOE_WIN - MOE_PAD)
VMEM_LIMIT = 48 * 1024 * 1024


def _cp(sem):
    return pltpu.CompilerParams(dimension_semantics=sem, vmem_limit_bytes=VMEM_LIMIT)


def _dot(a, b):
    return jnp.dot(a.astype(BF16), b.astype(BF16), preferred_element_type=F32)


def _dot_nt(a, b):
    return lax.dot_general(a.astype(BF16), b.astype(BF16), (((1,), (1,)), ((), ())),
                           preferred_element_type=F32)


def _dot_tn(a, b):
    return lax.dot_general(a.astype(BF16), b.astype(BF16), (((0,), (0,)), ((), ())),
                           preferred_element_type=F32)


def _split(a):
    hi = a.astype(BF16)
    return hi, (a - hi.astype(F32)).astype(BF16)


def _split_bits(w):
    bits = lax.bitcast_convert_type(w, jnp.uint32) & jnp.uint32(0xFFFF0000)
    hi = lax.bitcast_convert_type(bits, F32)
    return hi.astype(BF16), (w - hi).astype(BF16)


def _dg3(a, b, dims):
    ah, al = _split(a)
    bh, bl = _split(b)
    dg = lambda x, y: lax.dot_general(x, y, (dims, ((), ())), preferred_element_type=F32)
    return dg(ah, bh) + (dg(ah, bl) + dg(al, bh))


class _MM:
    def __init__(self, precise):
        if precise:
            self.nn = lambda a, b: _dg3(a, b, ((1,), (0,)))
            self.nt = lambda a, b: _dg3(a, b, ((1,), (1,)))
            self.tn = lambda a, b: _dg3(a, b, ((0,), (0,)))
        else:
            self.nn, self.nt, self.tn = _dot, _dot_nt, _dot_tn


def _pieces(a, n):
    out = []
    for i in range(n):
        p = a.astype(BF16)
        out.append(p)
        if i + 1 < n:
            a = a - p.astype(F32)
    return out


def _dot_pieces(a, b, n, lhs_exact):
    dot = lambda x, y: jnp.dot(x, y, preferred_element_type=F32)
    terms = [dot(a, p) for p in _pieces(b, n)] if lhs_exact else [dot(p, b) for p in _pieces(a, n)]
    acc = terms[-1]
    for t in terms[-2::-1]:
        acc = acc + t
    return acc


def _dot_hi(a, b):
    return jnp.dot(a, b, precision=HIGHEST, preferred_element_type=F32)


def _dot_nt_hi(a, b):
    return lax.dot_general(a, b, (((1,), (1,)), ((), ())), precision=HIGHEST,
                           preferred_element_type=F32)


def _sigmoid(x):
    return 1.0 / (1.0 + jnp.exp(-x))


def _softplus(x):
    return jnp.maximum(x, 0.0) + jnp.log(1.0 + jnp.exp(-jnp.abs(x)))


def _rmsnorm(x, g):
    return x * lax.rsqrt(jnp.mean(x * x, axis=-1, keepdims=True) + RMS_EPS) * g


_IN_SPLITS = ((0, 1024), (1024, 2048), (2048, 3072), (3072, 3328), (3328, 3456))


def _in_kernel(x_ref, g_ref, *refs, precise):
    nw = 2 if precise else 1
    w_refs, o_refs = refs[:nw], refs[nw:]
    h = _rmsnorm(x_ref[...], g_ref[...])
    tm = h.shape[0]
    dot = lambda a, b: jnp.dot(a, b, preferred_element_type=F32)
    if precise:
        hh, hl = _split(h)
        lhs = jnp.concatenate([hh, hl], axis=0)
    else:
        lhs = h.astype(BF16)
    for o_ref, (c0, c1) in zip(o_refs, _IN_SPLITS):
        acc = dot(lhs, w_refs[0][:, c0:c1])
        if precise:
            acc = acc[:tm] + (acc[tm:] + dot(hh, w_refs[1][:, c0:c1]))
        o_ref[...] = acc


def _in_proj(x, g, ws):
    n = x.shape[0]
    tm = 512
    ncol = ws[0].shape[1]
    widths = [c1 - c0 for c0, c1 in _IN_SPLITS]
    return pl.pallas_call(
        functools.partial(_in_kernel, precise=len(ws) == 2),
        grid=(n // tm,),
        in_specs=[pl.BlockSpec((tm, D_MODEL), lambda i: (i, 0)),
                  pl.BlockSpec((1, D_MODEL), lambda i: (0, 0))]
                 + [pl.BlockSpec((D_MODEL, ncol), lambda i: (0, 0), pipeline_mode=pl.Buffered(1)) for _ in ws],
        out_specs=[pl.BlockSpec((tm, wd), lambda i: (i, 0)) for wd in widths],
        out_shape=[jax.ShapeDtypeStruct((n, wd), F32) for wd in widths],
        compiler_params=_cp(("parallel",)),
        name="in_proj",
    )(x, g, *ws)


def _pool_kernel(u_ref, buf_ref, w_ref, sc_ref, y_ref, nb_ref, ext_ref, *, tt, pos0, precise):
    mm = _MM(precise)
    j = pl.program_id(1)

    @pl.when(j == 0)
    def _():
        ext_ref[0:POOL_PAD, :] = buf_ref[...]

    u = u_ref[...]
    ext_ref[POOL_PAD:POOL_PAD + tt, :] = u
    acc = u
    sums = []
    for k in range(1, POOL_PAD):
        acc = acc + ext_ref[POOL_PAD - k:POOL_PAD - k + tt, :]
        if k + 1 in POOL_WINDOWS:
            sums.append(acc)
    lane = lax.broadcasted_iota(jnp.int32, (tt, GROUP_W), 1)
    grp = lane // (GROUP_W // len(POOL_WINDOWS))
    pos = (pos0 + j * tt + lax.broadcasted_iota(jnp.int32, (tt, GROUP_W), 0)).astype(F32)
    mean = jnp.zeros((tt, GROUP_W), F32)
    for gi, w in enumerate(POOL_WINDOWS):
        cnt = jnp.minimum(float(w), pos + 1.0)
        mean = jnp.where(grp == gi, sums[gi] / cnt, mean)
    d = mean - u
    y_ref[...] = (mm.nn(d, w_ref[...]) * sc_ref[...]).astype(y_ref.dtype)
    nb_ref[...] = ext_ref[tt + 1:tt + POOL_PAD, :]
    ext_ref[0:POOL_PAD, :] = ext_ref[tt:tt + POOL_PAD, :]


def _pool_mixer(u, buf16, w_bd, scale, pos0, precise):
    b, t, _ = u.shape
    tt = min(t, 512)
    return pl.pallas_call(
        functools.partial(_pool_kernel, tt=tt, pos0=pos0, precise=precise),
        grid=(b, t // tt),
        in_specs=[pl.BlockSpec((None, tt, GROUP_W), lambda i, j: (i, j, 0)),
                  pl.BlockSpec((None, POOL_PAD, GROUP_W), lambda i, j: (i, 0, 0)),
                  pl.BlockSpec((GROUP_W, GROUP_W), lambda i, j: (0, 0)),
                  pl.BlockSpec((1, GROUP_W), lambda i, j: (0, 0))],
        out_specs=[pl.BlockSpec((None, tt, GROUP_W), lambda i, j: (i, j, 0)),
                   pl.BlockSpec((None, POOL_BUF, GROUP_W), lambda i, j: (i, 0, 0))],
        out_shape=[jax.ShapeDtypeStruct((b, t, GROUP_W), F32 if precise else BF16),
                   jax.ShapeDtypeStruct((b, POOL_BUF, GROUP_W), F32)],
        scratch_shapes=[pltpu.VMEM((POOL_PAD + tt, GROUP_W), F32)],
        compiler_params=_cp(("parallel", "arbitrary")),
        name="pool_mixer",
    )(u, buf16, w_bd, scale)


def _rwkv_kernel(z_ref, sh_ref, s0_ref, mu_ref, w0_ref, w2_ref, a0_ref, a2_ref, g2_ref, kk_ref,
                 ka_ref, rk_ref, lnx_ref, seg_ref, tril_ref,
                 y_ref, so_ref, sho_ref, zext_ref, s_ref, ybuf_ref, pre_ref, *, precise, nb, nc):
    mm = _MM(precise)
    lora = _dot_hi if precise else _dot
    j = pl.program_id(1)
    L = CHUNK
    rows = nc * L
    n = nb * rows

    @pl.when(j == 0)
    def _():
        zext_ref[:, 7:8, :] = sh_ref[...]
        s_ref[...] = s0_ref[...]

    z3 = z_ref[...]
    zext_ref[:, 8:8 + rows, :] = z3
    prev3 = zext_ref[:, 7:7 + rows, :]
    zs = (z3 + (prev3 - z3) * mu_ref[...]).reshape(n, RWKV_COLS)
    c = GROUP_W
    r, k, v = zs[:, 0:c], zs[:, c:2 * c], zs[:, 2 * c:3 * c]
    wl, al, gl = zs[:, 768:832], zs[:, 832:896], zs[:, 896:1024]
    w = -_softplus(-(w0_ref[...] + lora(jnp.tanh(wl), w2_ref[...]))) - 0.5
    lw = -jnp.exp(w)
    a = _sigmoid(a0_ref[...] + lora(al, a2_ref[...]))
    g = lora(_sigmoid(gl), g2_ref[...])
    seg = seg_ref[...]
    segsum = lambda x: _dot_pieces(x, seg, 2 if precise else 1, lhs_exact=False)
    kk = k * kk_ref[...]
    kk = kk / jnp.maximum(jnp.sqrt(segsum(kk * kk) * float(HEAD_DIM)), 1e-12)
    k2 = k * (1.0 + (a - 1.0) * ka_ref[...])
    bonus = segsum(r * k2 * rk_ref[...]) * float(HEAD_DIM) * v
    b = kk * a

    ri = lax.broadcasted_iota(jnp.int32, (L, L), 0)
    ci = lax.broadcasted_iota(jnp.int32, (L, L), 1)
    strict, incl, eye = ri > ci, ri >= ci, ri == ci
    blk = (ri // SUB) == (ci // SUB)
    eye_f = jnp.where(eye, 1.0, 0.0)
    tril = tril_ref[...]

    nch = nb * nc
    chains = [(ch, h) for ch in range(nch) for h in range(N_HEADS)]
    each = lambda f, *lists: [f(*vals) for vals in zip(*lists)]
    at, rt, bt, kt, bh, kh, dend, vv = [], [], [], [], [], [], [], []
    for ch in range(nch):
        rs = slice(ch * L, (ch + 1) * L)
        lwc = lw[rs]
        cum = _dot_pieces(tril, lwc, 3 if precise else 2, lhs_exact=True)
        pend = cum[L - 1:L, :]
        e_neg = jnp.exp(-cum)
        e_end = jnp.exp(pend - cum)
        at_c, rt_c = -kk[rs] * jnp.exp(cum - lwc), r[rs] * jnp.exp(cum)
        bt_c, kt_c, bh_c, kh_c = b[rs] * e_neg, k2[rs] * e_neg, b[rs] * e_end, k2[rs] * e_end
        dend_c, v_c = jnp.exp(pend), v[rs]
        for h in range(N_HEADS):
            hs = slice(h * HEAD_DIM, (h + 1) * HEAD_DIM)
            for lst, val in ((at, at_c), (rt, rt_c), (bt, bt_c), (kt, kt_c), (bh, bh_c), (kh, kh_c),
                             (dend, dend_c), (vv, v_c)):
                lst.append(val[:, hs])
    gm = each(lambda a_, r_, b_, k_: mm.nt(jnp.concatenate([a_, r_], axis=0),
                                           jnp.concatenate([b_, k_], axis=0)), at, rt, bt, kt)
    gkv = each(lambda g_, v_: mm.nn(jnp.concatenate([jnp.where(strict, g_[:L, L:], 0.0),
                                                     jnp.where(incl, g_[L:, L:], 0.0)], axis=0), v_),
               gm, vv)
    a_ab = each(lambda g_: jnp.where(strict, g_[:L, :L], 0.0), gm)
    g_b = each(lambda g_: jnp.where(incl, g_[L:, :L], 0.0), gm)
    dp = each(lambda a_: jnp.where(blk, a_, 0.0), a_ab)
    aoff = each(lambda a_: jnp.where(blk, 0.0, a_), a_ab)
    md = each(lambda d_: eye_f + d_, dp)
    for _ in range(3):
        dp = each(lambda d_: mm.nn(d_, d_), dp)
        md = each(lambda m_, d_: m_ + mm.nn(m_, d_), md, dp)
    nm = each(mm.nn, md, aoff)
    nm2 = each(lambda n_: mm.nn(n_, n_), nm)
    x = each(lambda m_, a_, g_: mm.nn(m_, jnp.concatenate([a_, g_[:L]], axis=1)), md, at, gkv)
    x = each(lambda n_, x_: x_ + mm.nn(n_, x_), nm2, x)
    x = each(lambda n_, x_: x_ + mm.nn(n_, x_), nm, x)
    gx = each(mm.nn, g_b, x)
    xtb = each(mm.tn, x, bh)
    vtk = each(mm.tn, vv, kh)
    for i, (ch, h) in enumerate(chains):
        pre_ref[ch, h, 0] = rt[i] + gx[i][:, :HEAD_DIM]
        pre_ref[ch, h, 1] = gx[i][:, HEAD_DIM:] + gkv[i][L:]
        pre_ref[ch, h, 2] = jnp.where(eye, jnp.broadcast_to(dend[i], (L, L)), 0.0) + xtb[i][:HEAD_DIM]
        pre_ref[ch, h, 3] = xtb[i][HEAD_DIM:] + vtk[i]

    seqs = [(bi, h) for bi in range(nb) for h in range(N_HEADS)]
    s = [s_ref[bi, h] for bi, h in seqs]
    for ci_ in range(nc):
        for i, (bi, h) in enumerate(seqs):
            ch = bi * nc + ci_
            ybuf_ref[ch * L:(ch + 1) * L, h * HEAD_DIM:(h + 1) * HEAD_DIM] = (
                _dot_nt_hi(pre_ref[ch, h, 0], s[i]) + pre_ref[ch, h, 1])
        s = [_dot_hi(s[i], pre_ref[bi * nc + ci_, h, 2]) + pre_ref[bi * nc + ci_, h, 3]
             for i, (bi, h) in enumerate(seqs)]
    for i, (bi, h) in enumerate(seqs):
        s_ref[bi, h] = s[i]

    yb = ybuf_ref[...]
    mu = segsum(yb)
    yc = yb - mu
    y = yc * lax.rsqrt(segsum(yc * yc) + RWKV_GN_EPS) * lnx_ref[...]
    y_ref[...] = ((y + bonus) * g).astype(y_ref.dtype).reshape(nb, rows, GROUP_W)
    so_ref[...] = s_ref[...]
    sho_ref[...] = z3[:, rows - 1:rows, :]
    zext_ref[:, 7:8, :] = z3[:, rows - 1:rows, :]


def _rwkv_mixer(z, shift, s0, prm, seg, tril, precise):
    b, t, _ = z.shape
    L = CHUNK
    nc = min(t // L, 4)
    nb = 1 if nc > 1 else min(b, 4)
    rows = nc * L
    full = lambda a: pl.BlockSpec(a.shape, lambda i, j: (0,) * a.ndim)
    st_spec = pl.BlockSpec((nb, N_HEADS, HEAD_DIM, HEAD_DIM), lambda i, j: (i, 0, 0, 0))
    sh_spec = pl.BlockSpec((nb, 1, RWKV_COLS), lambda i, j: (i, 0, 0))
    consts = list(prm) + [seg.astype(BF16), tril.astype(BF16)]
    return pl.pallas_call(
        functools.partial(_rwkv_kernel, precise=precise, nb=nb, nc=nc),
        grid=(b // nb, t // rows),
        in_specs=[pl.BlockSpec((nb, rows, RWKV_COLS), lambda i, j: (i, j, 0)), sh_spec, st_spec]
                 + [full(a) for a in consts],
        out_specs=[pl.BlockSpec((nb, rows, GROUP_W), lambda i, j: (i, j, 0)), st_spec, sh_spec],
        out_shape=[jax.ShapeDtypeStruct((b, t, GROUP_W), F32 if precise else BF16),
                   jax.ShapeDtypeStruct((b, N_HEADS, HEAD_DIM, HEAD_DIM), F32),
                   jax.ShapeDtypeStruct((b, 1, RWKV_COLS), F32)],
        scratch_shapes=[pltpu.VMEM((nb, 8 + rows, RWKV_COLS), F32),
                        pltpu.VMEM((nb, N_HEADS, HEAD_DIM, HEAD_DIM), F32),
                        pltpu.VMEM((nb * rows, GROUP_W), F32),
                        pltpu.VMEM((nb * nc, N_HEADS, 4, HEAD_DIM, HEAD_DIM), F32)],
        compiler_params=_cp(("parallel", "arbitrary")),
        name="rwkv_mixer",
    )(z, shift, s0, *consts)


def _ret_kernel(z_ref, cos_ref, sin_ref, dm_ref, kdec_ref, qdec_ref, cdec_ref, gn_ref, seg_ref, s0_ref,
                y_ref, so_ref, s_ref, obuf_ref, *, precise, nb, nc):
    mm = _MM(precise)
    j = pl.program_id(1)
    L = CHUNK
    rows = nc * L

    @pl.when(j == 0)
    def _():
        s_ref[...] = s0_ref[...]

    c = GROUP_W
    lane = lax.broadcasted_iota(jnp.int32, (rows, c), 1)
    first_half = (lane % HEAD_DIM) < (HEAD_DIM // 2)
    cs, sn = cos_ref[...], sin_ref[...]
    kdec, qdec = kdec_ref[...], qdec_ref[...]

    def rope(x):
        partner = jnp.where(first_half, pltpu.roll(x, c - HEAD_DIM // 2, 1), pltpu.roll(x, HEAD_DIM // 2, 1))
        return x * cs + partner * sn

    each = lambda f, *lists: [f(*vals) for vals in zip(*lists)]
    keys = [(bi, ci, h) for bi in range(nb) for ci in range(nc) for h in range(N_HEADS)]
    qs, ks, vs, kds, qds, g_all = [], [], [], [], [], []
    for bi in range(nb):
        q = rope(z_ref[bi, :, 0:c])
        k = rope(z_ref[bi, :, c:2 * c]) * HEAD_DIM ** -0.5
        v = z_ref[bi, :, 2 * c:3 * c]
        g_all.append(z_ref[bi, :, 3 * c:4 * c])
        for ci in range(nc):
            rs = slice(ci * L, (ci + 1) * L)
            kd, qd = k[rs] * kdec, q[rs] * qdec
            for h in range(N_HEADS):
                hs = slice(h * HEAD_DIM, (h + 1) * HEAD_DIM)
                for lst, val in ((qs, q[rs]), (ks, k[rs]), (vs, v[rs]), (kds, kd), (qds, qd)):
                    lst.append(val[:, hs])
    att = [mm.nt(q_, k_) * dm_ref[h] for q_, k_, (_, _, h) in zip(qs, ks, keys)]
    o_in = each(mm.nn, att, vs)
    kv = each(mm.tn, kds, vs)
    s_prev = {}
    for bi in range(nb):
        for h in range(N_HEADS):
            s = s_ref[bi, h]
            for ci in range(nc):
                s_prev[(bi, ci, h)] = s
                s = s * cdec_ref[h] + kv[keys.index((bi, ci, h))]
            s_ref[bi, h] = s
    o_x = [mm.nn(qd_, s_prev[key]) for qd_, key in zip(qds, keys)]
    for i, (bi, ci, h) in enumerate(keys):
        r0 = bi * rows + ci * L
        obuf_ref[r0:r0 + L, h * HEAD_DIM:(h + 1) * HEAD_DIM] = o_in[i] + o_x[i]
    seg = seg_ref[...]
    segsum = lambda x: _dot_pieces(x, seg, 2 if precise else 1, lhs_exact=False)
    ob = obuf_ref[...]
    oc = ob - segsum(ob)
    o = oc * lax.rsqrt(segsum(oc * oc) + RET_GN_EPS) * gn_ref[...]
    g = jnp.concatenate(g_all, axis=0) if nb > 1 else g_all[0]
    y_ref[...] = (g * _sigmoid(g) * o).astype(y_ref.dtype).reshape(nb, rows, c)
    so_ref[...] = s_ref[...]


def _ret_mixer(z, s0, cos, sin, dmask, kdec, qdec, cdec, gn, seg, precise):
    b, t, _ = z.shape
    L = CHUNK
    nc = min(t // L, 4)
    nb = 1 if nc > 1 else min(b, 4)
    rows = nc * L
    full = lambda a: pl.BlockSpec(a.shape, lambda i, j: (0,) * a.ndim)
    st_spec = pl.BlockSpec((nb, N_HEADS, HEAD_DIM, HEAD_DIM), lambda i, j: (i, 0, 0, 0))
    tab_spec = pl.BlockSpec((rows, GROUP_W), lambda i, j: (j, 0))
    seg = seg.astype(BF16)
    return pl.pallas_call(
        functools.partial(_ret_kernel, precise=precise, nb=nb, nc=nc),
        grid=(b // nb, t // rows),
        in_specs=[pl.BlockSpec((nb, rows, 4 * GROUP_W), lambda i, j: (i, j, 0)), tab_spec, tab_spec,
                  full(dmask), full(kdec), full(qdec), full(cdec), full(gn), full(seg), st_spec],
        out_specs=[pl.BlockSpec((nb, rows, GROUP_W), lambda i, j: (i, j, 0)), st_spec],
        out_shape=[jax.ShapeDtypeStruct((b, t, GROUP_W), F32 if precise else BF16),
                   jax.ShapeDtypeStruct((b, N_HEADS, HEAD_DIM, HEAD_DIM), F32)],
        scratch_shapes=[pltpu.VMEM((nb, N_HEADS, HEAD_DIM, HEAD_DIM), F32),
                        pltpu.VMEM((nb * rows, GROUP_W), F32)],
        compiler_params=_cp(("parallel", "arbitrary")),
        name="ret_mixer",
    )(z, cos, sin, dmask, kdec, qdec, cdec, gn, seg, s0)


def _cumsum_lanes(x_ref, o_ref, tri, carry, width):
    blk = 128
    for c0 in range(0, width, blk):
        wd = min(blk, width - c0)
        cs = _dot_hi(x_ref[:, c0:c0 + wd], tri[:wd, :wd]) + carry
        o_ref[:, c0:c0 + wd] = cs
        carry = cs[:, wd - 1:wd]
    return carry


def _fox_gate_kernel(*refs, past, t):
    if past:
        zf_ref, b_ref, tri_ref, lfp_ref, lf_ref, cn_ref, cp_ref = refs
    else:
        zf_ref, b_ref, tri_ref, lf_ref, cn_ref = refs
    tri = tri_ref[...]
    x = zf_ref[...] + b_ref[...]
    lf_ref[...] = jnp.minimum(x, 0.0) - jnp.log(1.0 + jnp.exp(-jnp.abs(x)))
    carry = jnp.zeros((zf_ref.shape[0], 1), F32)
    if past:
        carry = _cumsum_lanes(lfp_ref, cp_ref, tri, carry, past)
    _cumsum_lanes(lf_ref, cn_ref, tri, carry, t)


def _fox_gate(zf_t, bias_rows, tri, lf_past_t):
    rows, t = zf_t.shape
    rt = min(rows, 32)
    past = 0 if lf_past_t is None else lf_past_t.shape[1]
    row_spec = lambda wd: pl.BlockSpec((rt, wd), lambda i: (i, 0))
    in_specs = [row_spec(t), row_spec(1), pl.BlockSpec(tri.shape, lambda i: (0, 0))]
    out_specs = [row_spec(t), row_spec(t)]
    out_shape = [jax.ShapeDtypeStruct((rows, t), F32), jax.ShapeDtypeStruct((rows, t), F32)]
    args = [zf_t, bias_rows, tri]
    if past:
        in_specs.append(row_spec(past))
        out_specs.append(row_spec(past))
        out_shape.append(jax.ShapeDtypeStruct((rows, past), F32))
        args.append(lf_past_t)
    return pl.pallas_call(
        functools.partial(_fox_gate_kernel, past=past, t=t),
        grid=(rows // rt,),
        in_specs=in_specs, out_specs=out_specs, out_shape=out_shape,
        compiler_params=_cp(("parallel",)),
        name="fox_gate",
    )(*args)


def _fox_attn_kernel(q_ref, kn_ref, vn_ref, og_ref, cq_ref, ckn_ref, y_ref, m_ref, l_ref, acc_ref, *,
                     tq, precise):
    mm = _MM(precise)
    i, j = pl.program_id(1), pl.program_id(2)
    nk = pl.num_programs(2)

    @pl.when(j == 0)
    def _():
        m_ref[...] = jnp.full(m_ref.shape, -jnp.inf, F32)
        l_ref[...] = jnp.zeros(l_ref.shape, F32)
        acc_ref[...] = jnp.zeros(acc_ref.shape, F32)

    heads = range(N_HEADS)

    def per_head_lanes(cols):
        lane_head = lax.broadcasted_iota(jnp.int32, (tq, GROUP_W), 1) // HEAD_DIM
        out = jnp.broadcast_to(cols[N_HEADS - 1], (tq, GROUP_W))
        for h in range(N_HEADS - 2, -1, -1):
            out = jnp.where(lane_head == h, cols[h], out)
        return out

    def block(k_ref, v_ref, ck_ref, diagonal):
        q = q_ref[...] * HEAD_DIM ** -0.5
        k = k_ref[...]
        v = v_ref[...]
        tk = k.shape[0]
        cq = cq_ref[...]
        ck = ck_ref[...]
        lane_head = lax.broadcasted_iota(jnp.int32, (tk, GROUP_W), 1) // HEAD_DIM
        kbd = jnp.concatenate([jnp.where(lane_head == h, k, 0.0) for h in heads], axis=0)
        vbd = jnp.concatenate([jnp.where(lane_head == h, v, 0.0) for h in heads], axis=0)
        s_all = mm.nt(q, kbd)
        s = [s_all[:, h * tk:(h + 1) * tk] + cq[:, h:h + 1] - ck[h:h + 1, :] for h in heads]
        if diagonal:
            keep = (lax.broadcasted_iota(jnp.int32, (tq, tk), 1)
                    <= lax.broadcasted_iota(jnp.int32, (tq, tk), 0))
            s = [jnp.where(keep, s_h, -jnp.inf) for s_h in s]
        m_prev = [m_ref[h] for h in heads]
        m_new = [jnp.maximum(m_prev[h], jnp.max(s[h], axis=1, keepdims=True)) for h in heads]
        alpha = [jnp.exp(m_prev[h] - m_new[h]) for h in heads]
        p = [jnp.exp(s[h] - m_new[h]) for h in heads]
        pv = mm.nn(jnp.concatenate(p, axis=1), vbd)
        for h in heads:
            l_ref[h] = alpha[h] * l_ref[h] + jnp.sum(p[h], axis=1, keepdims=True)
            m_ref[h] = m_new[h]
        acc_ref[...] = per_head_lanes(alpha) * acc_ref[...] + pv

    @pl.when(j < i)
    def _():
        block(kn_ref, vn_ref, ckn_ref, False)

    @pl.when(j == i)
    def _():
        block(kn_ref, vn_ref, ckn_ref, True)

    @pl.when(j == nk - 1)
    def _():
        o = acc_ref[...] / per_head_lanes([l_ref[h] for h in heads])
        y_ref[...] = (_sigmoid(og_ref[...]) * o).astype(y_ref.dtype)


def _fox_attn(zfox, cq, ck_new, precise):
    b, t, _ = zfox.shape
    tq = min(t, 512)
    nq = t // tq
    jc = lambda i, j: jnp.minimum(j, i)
    in_specs = [pl.BlockSpec((None, tq, GROUP_W), lambda bb, i, j: (bb, i, 0)),
                pl.BlockSpec((None, tq, GROUP_W), lambda bb, i, j: (bb, jc(i, j), 1)),
                pl.BlockSpec((None, tq, GROUP_W), lambda bb, i, j: (bb, jc(i, j), 2)),
                pl.BlockSpec((None, tq, GROUP_W), lambda bb, i, j: (bb, i, 3)),
                pl.BlockSpec((None, tq, N_HEADS), lambda bb, i, j: (bb, i, 0)),
                pl.BlockSpec((None, N_HEADS, tq), lambda bb, i, j: (bb, 0, jc(i, j)))]
    return pl.pallas_call(
        functools.partial(_fox_attn_kernel, tq=tq, precise=precise),
        grid=(b, nq, nq),
        in_specs=in_specs,
        out_specs=pl.BlockSpec((None, tq, GROUP_W), lambda bb, i, j: (bb, i, 0)),
        out_shape=jax.ShapeDtypeStruct((b, t, GROUP_W), F32 if precise else BF16),
        scratch_shapes=[pltpu.VMEM((N_HEADS, tq, 1), F32), pltpu.VMEM((N_HEADS, tq, 1), F32),
                        pltpu.VMEM((tq, GROUP_W), F32)],
        compiler_params=_cp(("parallel", "parallel", "arbitrary")),
        name="fox_attn",
    )(zfox, zfox, zfox, zfox, cq, ck_new)


def _fox_past_kernel(zf_ref, cq_ref, ckn_ref, kt_ref, vt_ref, ckp_ref, y_ref, m_ref, l_ref, acc_ref, *,
                     nkp, nb, precise):
    mm = _MM(precise)
    j = pl.program_id(1)
    t = zf_ref.shape[1]
    c = GROUP_W
    chains = [(bi, h) for bi in range(nb) for h in range(N_HEADS)]
    hs = lambda h: slice(h * HEAD_DIM, (h + 1) * HEAD_DIM)

    @pl.when(j == 0)
    def _():
        m_ref[...] = jnp.full(m_ref.shape, -jnp.inf, F32)
        l_ref[...] = jnp.zeros(l_ref.shape, F32)
        acc_ref[...] = jnp.zeros(acc_ref.shape, F32)

    def online_update(s, pv_of):
        m_prev = [m_ref[i] for i in range(len(chains))]
        m_new = [jnp.maximum(mp, jnp.max(s_i, axis=1, keepdims=True)) for mp, s_i in zip(m_prev, s)]
        alpha = [jnp.exp(mp - mn) for mp, mn in zip(m_prev, m_new)]
        p = [jnp.exp(s_i - mn) for s_i, mn in zip(s, m_new)]
        pv = pv_of(p)
        for i in range(len(chains)):
            l_ref[i] = alpha[i] * l_ref[i] + jnp.sum(p[i], axis=1, keepdims=True)
            acc_ref[i] = alpha[i] * acc_ref[i] + pv[i]
            m_ref[i] = m_new[i]

    def queries():
        return [zf_ref[bi, :, hs(h)] * HEAD_DIM ** -0.5 for bi, h in chains]

    @pl.when(j < nkp)
    def _():
        q = queries()
        s = [mm.nn(q[i], kt_ref[bi, h]) + cq_ref[bi, :, h:h + 1] - ckp_ref[bi, h:h + 1, :]
             for i, (bi, h) in enumerate(chains)]
        online_update(s, lambda p: [mm.nt(p[i], vt_ref[bi, h]) for i, (bi, h) in enumerate(chains)])

    @pl.when(j == nkp)
    def _():
        q = queries()
        keep = lax.broadcasted_iota(jnp.int32, (t, t), 1) <= lax.broadcasted_iota(jnp.int32, (t, t), 0)
        s = [jnp.where(keep, mm.nt(q[i], zf_ref[bi, :, c + h * HEAD_DIM:c + (h + 1) * HEAD_DIM])
                       + cq_ref[bi, :, h:h + 1] - ckn_ref[bi, h:h + 1, :], -jnp.inf)
             for i, (bi, h) in enumerate(chains)]
        online_update(s, lambda p: [mm.nn(p[i], zf_ref[bi, :, 2 * c + h * HEAD_DIM:2 * c + (h + 1) * HEAD_DIM])
                                    for i, (bi, h) in enumerate(chains)])
        for i, (bi, h) in enumerate(chains):
            og = zf_ref[bi, :, 3 * c + h * HEAD_DIM:3 * c + (h + 1) * HEAD_DIM]
            y_ref[bi, :, hs(h)] = (_sigmoid(og) * (acc_ref[i] / l_ref[i])).astype(y_ref.dtype)


def _fox_attn_past(zfox, cq, ck_new, kt_all, vt_all, ck_past, layer, precise):
    b, t, _ = zfox.shape
    p = kt_all.shape[-1]
    tkp = 512
    nkp = p // tkp
    nb = min(b, 4)
    jp = lambda j: jnp.minimum(j, nkp - 1)
    cache_spec = pl.BlockSpec((None, nb, N_HEADS, HEAD_DIM, tkp), lambda i, j: (layer, i, 0, 0, jp(j)))
    return pl.pallas_call(
        functools.partial(_fox_past_kernel, nkp=nkp, nb=nb, precise=precise),
        grid=(b // nb, nkp + 1),
        in_specs=[pl.BlockSpec((nb, t, 4 * GROUP_W), lambda i, j: (i, 0, 0)),
                  pl.BlockSpec((nb, t, N_HEADS), lambda i, j: (i, 0, 0)),
                  pl.BlockSpec((nb, N_HEADS, t), lambda i, j: (i, 0, 0)),
                  cache_spec, cache_spec,
                  pl.BlockSpec((nb, N_HEADS, tkp), lambda i, j: (i, 0, jp(j)))],
        out_specs=pl.BlockSpec((nb, t, GROUP_W), lambda i, j: (i, 0, 0)),
        out_shape=jax.ShapeDtypeStruct((b, t, GROUP_W), F32 if precise else BF16),
        scratch_shapes=[pltpu.VMEM((nb * N_HEADS, t, 1), F32), pltpu.VMEM((nb * N_HEADS, t, 1), F32),
                        pltpu.VMEM((nb * N_HEADS, t, HEAD_DIM), F32)],
        compiler_params=_cp(("parallel", "arbitrary")),
        name="fox_attn_past",
    )(zfox, cq, ck_new, kt_all, vt_all, ck_past)


def _out_kernel(x_ref, ya_ref, yb_ref, yc_ref, yd_ref, *refs, precise):
    w_refs, o_ref = refs[:-1], refs[-1]
    dot = lambda a, b: jnp.dot(a, b, preferred_element_type=F32)
    acc = x_ref[...]
    for i, y_ref in enumerate((ya_ref, yb_ref, yc_ref, yd_ref)):
        if precise:
            yh, yl = _split(y_ref[...])
            acc = acc + (dot(yh, w_refs[0][i]) + (dot(yh, w_refs[1][i]) + dot(yl, w_refs[0][i])))
        else:
            acc = acc + dot(y_ref[...], w_refs[0][i])
    o_ref[...] = acc


def _out_proj(x, ys, ws):
    n = x.shape[0]
    tm = 512
    y_spec = pl.BlockSpec((tm, GROUP_W), lambda i: (i, 0))
    x_spec = pl.BlockSpec((tm, D_MODEL), lambda i: (i, 0))
    return pl.pallas_call(
        functools.partial(_out_kernel, precise=len(ws) == 2),
        grid=(n // tm,),
        in_specs=[x_spec, y_spec, y_spec, y_spec, y_spec]
                 + [pl.BlockSpec(w.shape, lambda i: (0, 0, 0)) for w in ws],
        out_specs=x_spec,
        out_shape=jax.ShapeDtypeStruct((n, D_MODEL), F32),
        compiler_params=_cp(("parallel",)),
        name="out_proj",
    )(x, *ys, *ws)


def _router_kernel(x_ref, g_ref, w_ref, b_ref, tril_ref, xs_ref, gs_ref, dest_ref, meta_ref):
    h = _rmsnorm(x_ref[...], g_ref[...])
    logits = _dot_hi(h, w_ref[...]) + b_ref[...]
    tm = logits.shape[0]
    lane_i = lax.broadcasted_iota(jnp.int32, (tm, ROUTER_PAD), 1)
    lane = lane_i.astype(F32)
    big = float(ROUTER_PAD)
    rmax = lambda a: jnp.max(a, axis=1, keepdims=True)
    rsum = lambda a: jnp.sum(a, axis=1, keepdims=True)
    first = lambda m: jnp.min(jnp.where(m, lane, big), axis=1, keepdims=True)

    is_c = jnp.logical_and(lane_i >= N_EXPERTS, lane_i < N_EXPERTS + N_GROUPS)
    lc = jnp.where(is_c, logits, -jnp.inf)
    ec = jnp.exp(lc - rmax(lc))
    pc_all = ec / rsum(ec)
    pc = rmax(pc_all)
    gi = first(jnp.logical_and(is_c, pc_all == pc)) - float(N_EXPERTS)

    is_f = (lane_i // EXPERTS_PER_GROUP).astype(F32) == gi
    lf = jnp.where(is_f, logits, -jnp.inf)
    ef = jnp.exp(lf - rmax(lf))
    pf = jnp.where(is_f, ef / rsum(ef), -1.0)
    t1 = rmax(pf)
    i1 = first(pf == t1)
    pf2 = jnp.where(lane == i1, -1.0, pf)
    t2 = rmax(pf2)
    i2 = first(pf2 == t2)
    den = t1 + t2
    gate = jnp.where(lane == i1, pc * (t1 / den), jnp.where(lane == i2, pc * (t2 / den), 0.0))

    onehot = jnp.where(lane == gi, 1.0, 0.0)
    rank = jnp.dot(tril_ref[...], onehot.astype(BF16), preferred_element_type=F32)
    cnt = jnp.sum(onehot, axis=0, keepdims=True)
    padded = jnp.floor((cnt + (MOE_PAD - 1.0)) * (1.0 / MOE_PAD)) * MOE_PAD
    lane1 = lax.broadcasted_iota(jnp.int32, (1, ROUTER_PAD), 1)
    off = jnp.zeros((1, ROUTER_PAD), F32)
    meta = jnp.zeros((1, ROUTER_PAD), F32)
    start = jnp.zeros((1, 1), F32)
    for g in range(N_GROUPS):
        size_g = jnp.sum(jnp.where(lane1 == g, padded, 0.0), axis=1, keepdims=True)
        off = off + jnp.where(lane1 == g, start, 0.0)
        meta = meta + jnp.where(lane1 == g, start, 0.0) + jnp.where(lane1 == N_GROUPS + g, size_g, 0.0)
        start = start + size_g
    dest = rsum(onehot * (off + rank))
    rows = lax.broadcasted_iota(jnp.int32, (tm, MOE_ROWS), 1).astype(F32)
    perm_t = jnp.where(rows == dest, 1.0, 0.0).astype(BF16)
    rhs = jnp.concatenate([h.astype(BF16)] + _pieces(gate, 3), axis=1)
    srt = lax.dot_general(perm_t, rhs, (((0,), (0,)), ((), ())), preferred_element_type=F32)
    xs_ref[...] = srt[:, :D_MODEL].astype(BF16)
    gs_ref[...] = (srt[:, D_MODEL:D_MODEL + ROUTER_PAD] + srt[:, D_MODEL + ROUTER_PAD:D_MODEL + 2 * ROUTER_PAD]
                   + srt[:, D_MODEL + 2 * ROUTER_PAD:])
    dest_ref[...] = dest
    meta_ref[...] = meta


def _router(x, g, wr, br, tril):
    n = x.shape[0]
    tm = MOE_TILE
    nt = n // tm
    return pl.pallas_call(
        _router_kernel,
        grid=(nt,),
        in_specs=[pl.BlockSpec((tm, D_MODEL), lambda i: (i, 0)),
                  pl.BlockSpec((1, D_MODEL), lambda i: (0, 0)),
                  pl.BlockSpec((D_MODEL, ROUTER_PAD), lambda i: (0, 0)),
                  pl.BlockSpec((1, ROUTER_PAD), lambda i: (0, 0)),
                  pl.BlockSpec((tm, tm), lambda i: (0, 0))],
        out_specs=[pl.BlockSpec((MOE_ROWS, D_MODEL), lambda i: (i, 0)),
                   pl.BlockSpec((MOE_ROWS, ROUTER_PAD), lambda i: (i, 0)),
                   pl.BlockSpec((tm, 1), lambda i: (i, 0)),
                   pl.BlockSpec((None, 1, ROUTER_PAD), lambda i: (i, 0, 0))],
        out_shape=[jax.ShapeDtypeStruct((nt * MOE_ROWS, D_MODEL), BF16),
                   jax.ShapeDtypeStruct((nt * MOE_ROWS, ROUTER_PAD), F32),
                   jax.ShapeDtypeStruct((n, 1), F32),
                   jax.ShapeDtypeStruct((nt, 1, ROUTER_PAD), F32)],
        compiler_params=_cp(("parallel",)),
        name="moe_router",
    )(x, g, wr, br, tril)


def _moe_kernel(meta_ref, xs_ref, gs_ref, w1_ref, w3_ref, w2_ref, o_ref):
    i, e = pl.program_id(0), pl.program_id(1)

    @pl.when(e == 0)
    def _():
        o_ref[...] = jnp.zeros(o_ref.shape, F32)

    g = e // EXPERTS_PER_GROUP
    off = meta_ref[i, g]
    size = meta_ref[i, N_GROUPS + g]
    for w in range(MOE_TILE // MOE_WIN + 1):
        @pl.when(w * MOE_WIN < size)
        def _():
            rows = pl.ds(pl.multiple_of(off + w * MOE_WIN, MOE_PAD), MOE_WIN)
            xw = xs_ref[rows, :]
            a = jnp.dot(xw, w1_ref[...], preferred_element_type=F32)
            b = jnp.dot(xw, w3_ref[...], preferred_element_type=F32)
            he = (a * _sigmoid(a) * b).astype(BF16)
            ye = jnp.dot(he, w2_ref[...], preferred_element_type=F32)
            gw = gs_ref[rows, :]
            lane = lax.broadcasted_iota(jnp.int32, gw.shape, 1)
            gcol = jnp.sum(jnp.where(lane == e, gw, 0.0), axis=1, keepdims=True)
            o_ref[rows, :] += gcol * ye


def _moe(meta, xs, gs, w1, w3, w2, layer):
    nt = meta.shape[0]
    return pl.pallas_call(
        _moe_kernel,
        grid_spec=pltpu.PrefetchScalarGridSpec(
            num_scalar_prefetch=1,
            grid=(nt, N_EXPERTS),
            in_specs=[pl.BlockSpec((MOE_ROWS, D_MODEL), lambda i, e, m: (i, 0)),
                      pl.BlockSpec((MOE_ROWS, ROUTER_PAD), lambda i, e, m: (i, 0)),
                      pl.BlockSpec((None, None, D_MODEL, D_EXPERT), lambda i, e, m: (layer, e, 0, 0)),
                      pl.BlockSpec((None, None, D_MODEL, D_EXPERT), lambda i, e, m: (layer, e, 0, 0)),
                      pl.BlockSpec((None, None, D_EXPERT, D_MODEL), lambda i, e, m: (layer, e, 0, 0))],
            out_specs=pl.BlockSpec((MOE_ROWS, D_MODEL), lambda i, e, m: (i, 0))),
        out_shape=jax.ShapeDtypeStruct((nt * MOE_ROWS, D_MODEL), F32),
        compiler_params=_cp(("parallel", "arbitrary")),
        name="moe_experts",
    )(meta, xs, gs, w1, w3, w2)


def _combine_kernel(x_ref, os_ref, dest_ref, nf_ref, o_ref, *, final):
    tm = x_ref.shape[0]
    rows = lax.broadcasted_iota(jnp.int32, (tm, MOE_ROWS), 1).astype(F32)
    perm_t = jnp.where(rows == dest_ref[...], 1.0, 0.0).astype(BF16)
    y = x_ref[...] + _dot_pieces(perm_t, os_ref[...], 2, lhs_exact=True)
    o_ref[...] = _rmsnorm(y, nf_ref[...]) if final else y


def _moe_combine(x, os, dest, nf, final):
    n = x.shape[0]
    tm = MOE_TILE
    return pl.pallas_call(
        functools.partial(_combine_kernel, final=final),
        grid=(n // tm,),
        in_specs=[pl.BlockSpec((tm, D_MODEL), lambda i: (i, 0)),
                  pl.BlockSpec((MOE_ROWS, D_MODEL), lambda i: (i, 0)),
                  pl.BlockSpec((tm, 1), lambda i: (i, 0)),
                  pl.BlockSpec((1, D_MODEL), lambda i: (0, 0))],
        out_specs=pl.BlockSpec((tm, D_MODEL), lambda i: (i, 0)),
        out_shape=jax.ShapeDtypeStruct((n, D_MODEL), F32),
        compiler_params=_cp(("parallel",)),
        name="moe_combine",
    )(x, os, dest, nf)


def _prep_layer(l, p):
    w_in = p["w_in"][l]
    o_rwkv, o_ret, o_fox = GROUP_W, GROUP_W + RWKV_COLS, GROUP_W + RWKV_COLS + 4 * GROUP_W
    ff = jnp.pad(w_in[:, o_fox + 4 * GROUP_W:], ((0, 0), (0, FF_PAD - N_HEADS)))
    w_in_r = jnp.concatenate([w_in[:, o_rwkv:o_ret], w_in[:, o_ret:o_fox],
                              w_in[:, o_fox:o_fox + 4 * GROUP_W], w_in[:, :GROUP_W], ff], axis=1)
    pw = p["pool_w"][l]
    pg = GROUP_W // len(POOL_WINDOWS)
    w_bd = jnp.zeros((GROUP_W, GROUP_W), F32)
    for gi in range(len(POOL_WINDOWS)):
        w_bd = w_bd.at[gi * pg:(gi + 1) * pg, gi * pg:(gi + 1) * pg].set(pw[gi])
    row = lambda a: a.reshape(1, -1)
    rwkv = (row(p["rwkv_mu"][l]), row(p["rwkv_w0"][l]), p["rwkv_w2"][l], row(p["rwkv_a0"][l]),
            p["rwkv_a2"][l], p["rwkv_g2"][l], row(p["rwkv_kk"][l]), row(p["rwkv_ka"][l]),
            row(p["rwkv_rk"][l]), row(p["rwkv_lnx"][l]))
    wf = jnp.transpose(p["moe_wf"][l], (1, 0, 2)).reshape(D_MODEL, N_EXPERTS)
    wr = jnp.pad(jnp.concatenate([wf, p["moe_wc"][l]], axis=1),
                 ((0, 0), (0, ROUTER_PAD - N_EXPERTS - N_GROUPS)))
    br = jnp.pad(jnp.concatenate([p["moe_bf"][l].reshape(-1), p["moe_bc"][l]]),
                 (0, ROUTER_PAD - N_EXPERTS - N_GROUPS)).reshape(1, ROUTER_PAD)
    w_out4 = p["w_out"][l].reshape(4, GROUP_W, D_MODEL)
    precise = l < p["w_in"].shape[0] - 1
    split = lambda w: _split_bits(w) if precise else (w.astype(BF16),)
    return dict(
        precise=precise, n1=row(p["norm1_g"][l]), w_in=split(w_in_r), pool_w=w_bd,
        pool_scale=row(p["pool_scale"][l]),
        rwkv=rwkv, ret_gn=row(p["ret_gn"][l]), fox_bf=p["fox_bf"][l],
        w_out=split(w_out4), n2=row(p["norm2_g"][l]),
        wr=wr, br=br, layer=l,
        w1=p["moe_w1"].astype(BF16), w3=p["moe_w3"].astype(BF16), w2=p["moe_w2"].astype(BF16))


def _tables(t, pos0):
    half = HEAD_DIM // 2
    inv = ROPE_BASE ** (-jnp.arange(half, dtype=F32) / half)
    ang = (pos0 + jnp.arange(t)).astype(F32)[:, None] * inv[None, :]
    cos, sin = jnp.cos(ang), jnp.sin(ang)
    cos_t = jnp.tile(jnp.concatenate([cos, cos], axis=1), (1, N_HEADS))
    sin_t = jnp.tile(jnp.concatenate([-sin, sin], axis=1), (1, N_HEADS))
    L = CHUNK
    log_g = jnp.log(jnp.array(RET_GAMMA, F32))
    idx = jnp.arange(L, dtype=F32)
    dmask = jnp.exp(log_g[:, None, None] * jnp.abs(idx[:, None] - idx[None, :]))
    lanes = lambda a: jnp.repeat(a, HEAD_DIM, axis=1)
    kdec = lanes(jnp.exp(log_g[None, :] * (L - 1.0 - idx)[:, None]))
    qdec = lanes(jnp.exp(log_g[None, :] * (idx + 1.0)[:, None]))
    cdec = jnp.broadcast_to(jnp.exp(log_g * L)[:, None, None], (N_HEADS, HEAD_DIM, HEAD_DIM))
    hid = jnp.arange(GROUP_W) // HEAD_DIM
    seg = jnp.where(hid[:, None] == hid[None, :], 1.0 / HEAD_DIM, 0.0).astype(F32)
    tril = jnp.tril(jnp.ones((L, L), F32))
    triu = jnp.triu(jnp.ones((128, 128), F32))
    tril_moe = jnp.tril(jnp.ones((MOE_TILE, MOE_TILE), F32), -1).astype(BF16)
    return dict(cos=cos_t, sin=sin_t, dmask=dmask, kdec=kdec, qdec=qdec, cdec=cdec, seg=seg,
                tril=tril, triu=triu, tril_moe=tril_moe)


def _trunk(x, pos0, states, layers, norm_f, cache=None):
    b, t, _ = x.shape
    n = b * t
    tb = _tables(t, pos0)
    xf = x.reshape(n, D_MODEL)
    nf = norm_f.reshape(1, D_MODEL)
    outs = [[] for _ in range(7)]
    for l, lp in enumerate(layers):
        pool_buf, shift, wkv, ret, lf_past = states[l]
        precise = lp["precise"]
        z_rwkv, z_ret, z_fox, z_pool, z_ff = _in_proj(xf, lp["n1"], lp["w_in"])
        buf16 = jnp.pad(pool_buf, ((0, 0), (POOL_PAD - POOL_BUF, 0), (0, 0)))
        y_a, pool_new = _pool_mixer(z_pool.reshape(b, t, GROUP_W), buf16, lp["pool_w"], lp["pool_scale"], pos0,
                                    precise)
        y_b, wkv_new, shift_new = _rwkv_mixer(z_rwkv.reshape(b, t, RWKV_COLS), shift, wkv, lp["rwkv"],
                                              tb["seg"], tb["tril"], precise)
        y_c, ret_new = _ret_mixer(z_ret.reshape(b, t, 4 * GROUP_W), ret, tb["cos"], tb["sin"], tb["dmask"],
                                  tb["kdec"], tb["qdec"], tb["cdec"], lp["ret_gn"], tb["seg"], precise)
        zf_t = jnp.transpose(z_ff[:, :N_HEADS].reshape(b, t, N_HEADS), (0, 2, 1)).reshape(b * N_HEADS, t)
        bias_rows = jnp.tile(lp["fox_bf"], b).reshape(b * N_HEADS, 1)
        zfox3 = z_fox.reshape(b, t, 4 * GROUP_W)
        if cache is None:
            lf_t, c_new = _fox_gate(zf_t, bias_rows, tb["triu"], None)
            cn3 = c_new.reshape(b, N_HEADS, t)
            y_d = _fox_attn(zfox3, jnp.transpose(cn3, (0, 2, 1)), cn3, precise)
        else:
            p = lf_past.shape[1]
            lfp_t = jnp.transpose(lf_past, (0, 2, 1)).reshape(b * N_HEADS, p)
            lf_t, c_new, c_past = _fox_gate(zf_t, bias_rows, tb["triu"], lfp_t)
            cn3 = c_new.reshape(b, N_HEADS, t)
            y_d = _fox_attn_past(zfox3, jnp.transpose(cn3, (0, 2, 1)), cn3, cache[0], cache[1],
                                 c_past.reshape(b, N_HEADS, p), l, precise)
        x1 = _out_proj(xf, [y.reshape(n, GROUP_W) for y in (y_a, y_b, y_c, y_d)], lp["w_out"])
        xs, gs, dest, meta = _router(x1, lp["n2"], lp["wr"], lp["br"], tb["tril_moe"])
        meta = meta[:, 0, :2 * N_GROUPS].astype(jnp.int32)
        ys = _moe(meta, xs, gs, lp["w1"], lp["w3"], lp["w2"], lp["layer"])
        xf = _moe_combine(x1, ys, dest, nf, final=(l == len(layers) - 1))
        k_new = zfox3[:, :, GROUP_W:2 * GROUP_W].reshape(b, t, N_HEADS, HEAD_DIM)
        v_new = zfox3[:, :, 2 * GROUP_W:3 * GROUP_W].reshape(b, t, N_HEADS, HEAD_DIM)
        lf_new = jnp.transpose(lf_t.reshape(b, N_HEADS, t), (0, 2, 1))
        for lst, s in zip(outs, (pool_new, shift_new, wkv_new, ret_new, k_new, v_new, lf_new)):
            lst.append(s)
    return xf.reshape(b, t, D_MODEL), [jnp.stack(lst) for lst in outs]


def kernel(x_prompt, x_sample, state_pool, state_shift, state_wkv, state_ret, cache_fox_k, cache_fox_v, cache_fox_logf, norm1_g, w_in, pool_w, pool_scale, rwkv_mu, rwkv_w0, rwkv_w2, rwkv_a0, rwkv_a2, rwkv_g2, rwkv_kk, rwkv_ka, rwkv_rk, rwkv_lnx, ret_gn, fox_bf, w_out, norm2_g, moe_wc, moe_bc, moe_wf, moe_bf, moe_w1, moe_w3, moe_w2, norm_f):
    p = dict(norm1_g=norm1_g, w_in=w_in, pool_w=pool_w, pool_scale=pool_scale, rwkv_mu=rwkv_mu,
             rwkv_w0=rwkv_w0, rwkv_w2=rwkv_w2, rwkv_a0=rwkv_a0, rwkv_a2=rwkv_a2, rwkv_g2=rwkv_g2,
             rwkv_kk=rwkv_kk, rwkv_ka=rwkv_ka, rwkv_rk=rwkv_rk, rwkv_lnx=rwkv_lnx, ret_gn=ret_gn,
             fox_bf=fox_bf, w_out=w_out, norm2_g=norm2_g, moe_wc=moe_wc, moe_bc=moe_bc, moe_wf=moe_wf,
             moe_bf=moe_bf, moe_w1=moe_w1, moe_w3=moe_w3, moe_w2=moe_w2)
    depth = w_in.shape[0]
    layers = [_prep_layer(l, p) for l in range(depth)]
    b = x_prompt.shape[0]
    dt = x_prompt.dtype
    prompt_init = [(jnp.zeros((b, POOL_BUF, GROUP_W), dt), jnp.zeros((b, 1, RWKV_COLS), dt),
                    jnp.zeros((b, N_HEADS, HEAD_DIM, HEAD_DIM), dt),
                    jnp.zeros((b, N_HEADS, HEAD_DIM, HEAD_DIM), dt), None)
                   for _ in range(depth)]
    sample_init = [(state_pool[l], state_shift[l], state_wkv[l], state_ret[l], cache_fox_logf[l])
                   for l in range(depth)]
    past = cache_fox_k.shape[2]
    cache = (jnp.transpose(cache_fox_k, (0, 1, 3, 4, 2)), jnp.transpose(cache_fox_v, (0, 1, 3, 4, 2)))
    y_prompt, new_p = _trunk(x_prompt, 0, prompt_init, layers, norm_f)
    y_sample, new_s = _trunk(x_sample, past, sample_init, layers, norm_f, cache)
    return (y_prompt, y_sample, *new_p, *new_s)
```

```python
import functools

import jax
import jax.numpy as jnp
from jax import lax
from jax.experimental import pallas as pl
from jax.experimental.pallas import tpu as pltpu

F32 = jnp.float32
BF16 = jnp.bfloat16
HIGHEST = lax.Precision.HIGHEST

D_MODEL = 1024
DEPTH = 2
CHUNK = 64
GROUP_W = 256
HEAD_DIM = 64
N_HEADS = 4
POOL_WINDOWS = (2, 4, 8, 16)
POOL_BUF = 15
POOL_PAD = 16
RWKV_COLS = 1024
N_IN = 3332
FF_PAD = 128
RET_GAMMA = tuple(1.0 - 2.0 ** (-5 - h) for h in range(N_HEADS))
ROPE_BASE = 10000.0
N_GROUPS = 4
EXPERTS_PER_GROUP = 4
N_EXPERTS = 16
D_EXPERT = 512
LOG2E = 1.4426950408889634
RMS_EPS = 1e-6
RWKV_GN_EPS = 64e-5
RET_GN_EPS = 1e-5
SUB = 16
ROUTER_PAD = 128
MOE_TILE = 1024
MOE_PAD = 64
MOE_WIN = 320
MOE_ROWS = MOE_TILE + N_GROUPS * MOE_PAD + (MOE_WIN - MOE_PAD)
MOE_TILES_PER_STEP = 2
VMEM_LIMIT = 48 * 1024 * 1024
MOE_VMEM_LIMIT = 56 * 1024 * 1024


def _cp(sem):
    return pltpu.CompilerParams(dimension_semantics=sem, vmem_limit_bytes=VMEM_LIMIT)


def _dot(a, b):
    return jnp.dot(a.astype(BF16), b.astype(BF16), preferred_element_type=F32)


def _dot_nt(a, b):
    return lax.dot_general(a.astype(BF16), b.astype(BF16), (((1,), (1,)), ((), ())),
                           preferred_element_type=F32)


def _dot_tn(a, b):
    return lax.dot_general(a.astype(BF16), b.astype(BF16), (((0,), (0,)), ((), ())),
                           preferred_element_type=F32)


def _split(a):
    hi = a.astype(BF16)
    return hi, (a - hi.astype(F32)).astype(BF16)


def _split_bits(w):
    bits = lax.bitcast_convert_type(w, jnp.uint32) & jnp.uint32(0xFFFF0000)
    hi = lax.bitcast_convert_type(bits, F32)
    return hi.astype(BF16), (w - hi).astype(BF16)


def _dg3(a, b, dims):
    ah, al = _split(a)
    bh, bl = _split(b)
    dg = lambda x, y: lax.dot_general(x, y, (dims, ((), ())), preferred_element_type=F32)
    return dg(ah, bh) + (dg(ah, bl) + dg(al, bh))


class _MM:
    def __init__(self, precise):
        if precise:
            self.nn = lambda a, b: _dg3(a, b, ((1,), (0,)))
            self.nt = lambda a, b: _dg3(a, b, ((1,), (1,)))
            self.tn = lambda a, b: _dg3(a, b, ((0,), (0,)))
        else:
            self.nn, self.nt, self.tn = _dot, _dot_nt, _dot_tn


def _pieces(a, n):
    out = []
    for i in range(n):
        p = a.astype(BF16)
        out.append(p)
        if i + 1 < n:
            a = a - p.astype(F32)
    return out


def _dot_pieces(a, b, n, lhs_exact):
    dot = lambda x, y: jnp.dot(x, y, preferred_element_type=F32)
    terms = [dot(a, p) for p in _pieces(b, n)] if lhs_exact else [dot(p, b) for p in _pieces(a, n)]
    acc = terms[-1]
    for t in terms[-2::-1]:
        acc = acc + t
    return acc


def _dot_hi(a, b):
    return jnp.dot(a, b, precision=HIGHEST, preferred_element_type=F32)


def _dot_nt_hi(a, b):
    return lax.dot_general(a, b, (((1,), (1,)), ((), ())), precision=HIGHEST,
                           preferred_element_type=F32)


def _sigmoid(x):
    return 1.0 / (1.0 + jnp.exp(-x))


def _softplus(x):
    return jnp.maximum(x, 0.0) + jnp.log(1.0 + jnp.exp(-jnp.abs(x)))


def _rmsnorm(x, g):
    return x * lax.rsqrt(jnp.mean(x * x, axis=-1, keepdims=True) + RMS_EPS) * g


_IN_SPLITS = ((0, 1024), (1024, 2048), (2048, 3072), (3072, 3328), (3328, 3456))


def _in_kernel(x_ref, g_ref, *refs, precise):
    nw = 2 if precise else 1
    w_refs, o_refs = refs[:nw], refs[nw:]
    h = _rmsnorm(x_ref[...], g_ref[...])
    tm = h.shape[0]
    dot = lambda a, b: jnp.dot(a, b, preferred_element_type=F32)
    if precise:
        hh, hl = _split(h)
        lhs = jnp.concatenate([hh, hl], axis=0)
    else:
        lhs = h.astype(BF16)
    for o_ref, (c0, c1) in zip(o_refs, _IN_SPLITS):
        acc = dot(lhs, w_refs[0][:, c0:c1])
        if precise:
            acc = acc[:tm] + (acc[tm:] + dot(hh, w_refs[1][:, c0:c1]))
        o_ref[...] = acc


def _in_proj(x, g, ws):
    n = x.shape[0]
    tm = 512
    ncol = ws[0].shape[1]
    widths = [c1 - c0 for c0, c1 in _IN_SPLITS]
    return pl.pallas_call(
        functools.partial(_in_kernel, precise=len(ws) == 2),
        grid=(n // tm,),
        in_specs=[pl.BlockSpec((tm, D_MODEL), lambda i: (i, 0)),
                  pl.BlockSpec((1, D_MODEL), lambda i: (0, 0))]
                 + [pl.BlockSpec((D_MODEL, ncol), lambda i: (0, 0), pipeline_mode=pl.Buffered(1)) for _ in ws],
        out_specs=[pl.BlockSpec((tm, wd), lambda i: (i, 0)) for wd in widths],
        out_shape=[jax.ShapeDtypeStruct((n, wd), F32) for wd in widths],
        compiler_params=_cp(("parallel",)),
        name="in_proj",
    )(x, g, *ws)


def _pool_kernel(u_ref, buf_ref, w_ref, sc_ref, y_ref, nb_ref, ext_ref, *, tt, pos0, precise):
    mm = _MM(precise)
    j = pl.program_id(1)

    @pl.when(j == 0)
    def _():
        ext_ref[0:POOL_PAD, :] = buf_ref[...]

    u = u_ref[...]
    ext_ref[POOL_PAD:POOL_PAD + tt, :] = u
    acc = u
    sums = []
    for k in range(1, POOL_PAD):
        acc = acc + ext_ref[POOL_PAD - k:POOL_PAD - k + tt, :]
        if k + 1 in POOL_WINDOWS:
            sums.append(acc)
    lane = lax.broadcasted_iota(jnp.int32, (tt, GROUP_W), 1)
    grp = lane // (GROUP_W // len(POOL_WINDOWS))
    pos = (pos0 + j * tt + lax.broadcasted_iota(jnp.int32, (tt, GROUP_W), 0)).astype(F32)
    mean = jnp.zeros((tt, GROUP_W), F32)
    for gi, w in enumerate(POOL_WINDOWS):
        cnt = jnp.minimum(float(w), pos + 1.0)
        mean = jnp.where(grp == gi, sums[gi] / cnt, mean)
    d = mean - u
    y_ref[...] = (mm.nn(d, w_ref[...]) * sc_ref[...]).astype(y_ref.dtype)
    nb_ref[...] = ext_ref[tt + 1:tt + POOL_PAD, :]
    ext_ref[0:POOL_PAD, :] = ext_ref[tt:tt + POOL_PAD, :]


def _pool_mixer(u, buf16, w_bd, scale, pos0, precise):
    b, t, _ = u.shape
    tt = min(t, 512)
    return pl.pallas_call(
        functools.partial(_pool_kernel, tt=tt, pos0=pos0, precise=precise),
        grid=(b, t // tt),
        in_specs=[pl.BlockSpec((None, tt, GROUP_W), lambda i, j: (i, j, 0)),
                  pl.BlockSpec((None, POOL_PAD, GROUP_W), lambda i, j: (i, 0, 0)),
                  pl.BlockSpec((GROUP_W, GROUP_W), lambda i, j: (0, 0)),
                  pl.BlockSpec((1, GROUP_W), lambda i, j: (0, 0))],
        out_specs=[pl.BlockSpec((None, tt, GROUP_W), lambda i, j: (i, j, 0)),
                   pl.BlockSpec((None, POOL_BUF, GROUP_W), lambda i, j: (i, 0, 0))],
        out_shape=[jax.ShapeDtypeStruct((b, t, GROUP_W), F32 if precise else BF16),
                   jax.ShapeDtypeStruct((b, POOL_BUF, GROUP_W), F32)],
        scratch_shapes=[pltpu.VMEM((POOL_PAD + tt, GROUP_W), F32)],
        compiler_params=_cp(("parallel", "arbitrary")),
        name="pool_mixer",
    )(u, buf16, w_bd, scale)


def _rwkv_kernel(z_ref, sh_ref, s0_ref, mu_ref, w0_ref, w2_ref, a0_ref, a2_ref, g2_ref, kk_ref,
                 ka_ref, rk_ref, lnx_ref, seg_ref, tril_ref,
                 y_ref, so_ref, sho_ref, zext_ref, s_ref, ybuf_ref, pre_ref, *, precise, nb, nc):
    mm = _MM(precise)
    lora = _dot_hi if precise else _dot
    j = pl.program_id(1)
    L = CHUNK
    rows = nc * L
    n = nb * rows

    @pl.when(j == 0)
    def _():
        zext_ref[:, 7:8, :] = sh_ref[...]
        s_ref[...] = s0_ref[...]

    z3 = z_ref[...]
    zext_ref[:, 8:8 + rows, :] = z3
    prev3 = zext_ref[:, 7:7 + rows, :]
    zs = (z3 + (prev3 - z3) * mu_ref[...]).reshape(n, RWKV_COLS)
    c = GROUP_W
    r, k, v = zs[:, 0:c], zs[:, c:2 * c], zs[:, 2 * c:3 * c]
    wl, al, gl = zs[:, 768:832], zs[:, 832:896], zs[:, 896:1024]
    w = -_softplus(-(w0_ref[...] + lora(jnp.tanh(wl), w2_ref[...]))) - 0.5
    lw = -jnp.exp(w)
    a = _sigmoid(a0_ref[...] + lora(al, a2_ref[...]))
    g = lora(_sigmoid(gl), g2_ref[...])
    seg = seg_ref[...]
    segsum = lambda x: _dot_pieces(x, seg, 2 if precise else 1, lhs_exact=False)
    kk = k * kk_ref[...]
    kk = kk / jnp.maximum(jnp.sqrt(segsum(kk * kk) * float(HEAD_DIM)), 1e-12)
    k2 = k * (1.0 + (a - 1.0) * ka_ref[...])
    bonus = segsum(r * k2 * rk_ref[...]) * float(HEAD_DIM) * v
    b = kk * a

    ri = lax.broadcasted_iota(jnp.int32, (L, L), 0)
    ci = lax.broadcasted_iota(jnp.int32, (L, L), 1)
    strict, incl, eye = ri > ci, ri >= ci, ri == ci
    blk = (ri // SUB) == (ci // SUB)
    eye_f = jnp.where(eye, 1.0, 0.0)
    tril = tril_ref[...]

    nch = nb * nc
    chains = [(ch, h) for ch in range(nch) for h in range(N_HEADS)]
    each = lambda f, *lists: [f(*vals) for vals in zip(*lists)]
    at, rt, bt, kt, bh, kh, dend, vv = [], [], [], [], [], [], [], []
    for ch in range(nch):
        rs = slice(ch * L, (ch + 1) * L)
        lwc = lw[rs]
        cum = _dot_pieces(tril, lwc, 3 if precise else 2, lhs_exact=True)
        pend = cum[L - 1:L, :]
        e_neg = jnp.exp(-cum)
        e_end = jnp.exp(pend - cum)
        at_c, rt_c = -kk[rs] * jnp.exp(cum - lwc), r[rs] * jnp.exp(cum)
        bt_c, kt_c, bh_c, kh_c = b[rs] * e_neg, k2[rs] * e_neg, b[rs] * e_end, k2[rs] * e_end
        dend_c, v_c = jnp.exp(pend), v[rs]
        for h in range(N_HEADS):
            hs = slice(h * HEAD_DIM, (h + 1) * HEAD_DIM)
            for lst, val in ((at, at_c), (rt, rt_c), (bt, bt_c), (kt, kt_c), (bh, bh_c), (kh, kh_c),
                             (dend, dend_c), (vv, v_c)):
                lst.append(val[:, hs])
    gm = each(lambda a_, r_, b_, k_: mm.nt(jnp.concatenate([a_, r_], axis=0),
                                           jnp.concatenate([b_, k_], axis=0)), at, rt, bt, kt)
    gkv = each(lambda g_, v_: mm.nn(jnp.concatenate([jnp.where(strict, g_[:L, L:], 0.0),
                                                     jnp.where(incl, g_[L:, L:], 0.0)], axis=0), v_),
               gm, vv)
    a_ab = each(lambda g_: jnp.where(strict, g_[:L, :L], 0.0), gm)
    g_b = each(lambda g_: jnp.where(incl, g_[L:, :L], 0.0), gm)
    dp = each(lambda a_: jnp.where(blk, a_, 0.0), a_ab)
    aoff = each(lambda a_: jnp.where(blk, 0.0, a_), a_ab)
    md = each(lambda d_: eye_f + d_, dp)
    for _ in range(3):
        dp = each(lambda d_: mm.nn(d_, d_), dp)
        md = each(lambda m_, d_: m_ + mm.nn(m_, d_), md, dp)
    nm = each(mm.nn, md, aoff)
    nm2 = each(lambda n_: mm.nn(n_, n_), nm)
    x = each(lambda m_, a_, g_: mm.nn(m_, jnp.concatenate([a_, g_[:L]], axis=1)), md, at, gkv)
    x = each(lambda n_, x_: x_ + mm.nn(n_, x_), nm2, x)
    x = each(lambda n_, x_: x_ + mm.nn(n_, x_), nm, x)
    gx = each(mm.nn, g_b, x)
    xtb = each(mm.tn, x, bh)
    vtk = each(mm.tn, vv, kh)
    for i, (ch, h) in enumerate(chains):
        pre_ref[ch, h, 0] = rt[i] + gx[i][:, :HEAD_DIM]
        pre_ref[ch, h, 1] = gx[i][:, HEAD_DIM:] + gkv[i][L:]
        pre_ref[ch, h, 2] = jnp.where(eye, jnp.broadcast_to(dend[i], (L, L)), 0.0) + xtb[i][:HEAD_DIM]
        pre_ref[ch, h, 3] = xtb[i][HEAD_DIM:] + vtk[i]

    seqs = [(bi, h) for bi in range(nb) for h in range(N_HEADS)]
    s = [s_ref[bi, h] for bi, h in seqs]
    for ci_ in range(nc):
        for i, (bi, h) in enumerate(seqs):
            ch = bi * nc + ci_
            ybuf_ref[ch * L:(ch + 1) * L, h * HEAD_DIM:(h + 1) * HEAD_DIM] = (
                _dot_nt_hi(pre_ref[ch, h, 0], s[i]) + pre_ref[ch, h, 1])
        s = [_dot_hi(s[i], pre_ref[bi * nc + ci_, h, 2]) + pre_ref[bi * nc + ci_, h, 3]
             for i, (bi, h) in enumerate(seqs)]
    for i, (bi, h) in enumerate(seqs):
        s_ref[bi, h] = s[i]

    yb = ybuf_ref[...]
    mu = segsum(yb)
    yc = yb - mu
    y = yc * lax.rsqrt(segsum(yc * yc) + RWKV_GN_EPS) * lnx_ref[...]
    y_ref[...] = ((y + bonus) * g).astype(y_ref.dtype).reshape(nb, rows, GROUP_W)
    so_ref[...] = s_ref[...]
    sho_ref[...] = z3[:, rows - 1:rows, :]
    zext_ref[:, 7:8, :] = z3[:, rows - 1:rows, :]


def _rwkv_mixer(z, shift, s0, prm, seg, tril, precise):
    b, t, _ = z.shape
    L = CHUNK
    nc = min(t // L, 4)
    nb = 1 if nc > 1 else min(b, 4)
    rows = nc * L
    full = lambda a: pl.BlockSpec(a.shape, lambda i, j: (0,) * a.ndim)
    st_spec = pl.BlockSpec((nb, N_HEADS, HEAD_DIM, HEAD_DIM), lambda i, j: (i, 0, 0, 0))
    sh_spec = pl.BlockSpec((nb, 1, RWKV_COLS), lambda i, j: (i, 0, 0))
    consts = list(prm) + [seg.astype(BF16), tril.astype(BF16)]
    return pl.pallas_call(
        functools.partial(_rwkv_kernel, precise=precise, nb=nb, nc=nc),
        grid=(b // nb, t // rows),
        in_specs=[pl.BlockSpec((nb, rows, RWKV_COLS), lambda i, j: (i, j, 0)), sh_spec, st_spec]
                 + [full(a) for a in consts],
        out_specs=[pl.BlockSpec((nb, rows, GROUP_W), lambda i, j: (i, j, 0)), st_spec, sh_spec],
        out_shape=[jax.ShapeDtypeStruct((b, t, GROUP_W), F32 if precise else BF16),
                   jax.ShapeDtypeStruct((b, N_HEADS, HEAD_DIM, HEAD_DIM), F32),
                   jax.ShapeDtypeStruct((b, 1, RWKV_COLS), F32)],
        scratch_shapes=[pltpu.VMEM((nb, 8 + rows, RWKV_COLS), F32),
                        pltpu.VMEM((nb, N_HEADS, HEAD_DIM, HEAD_DIM), F32),
                        pltpu.VMEM((nb * rows, GROUP_W), F32),
                        pltpu.VMEM((nb * nc, N_HEADS, 4, HEAD_DIM, HEAD_DIM), F32)],
        compiler_params=_cp(("parallel", "arbitrary")),
        name="rwkv_mixer",
    )(z, shift, s0, *consts)


def _ret_kernel(z_ref, cos_ref, sin_ref, dm_ref, kdec_ref, qdec_ref, cdec_ref, gn_ref, seg_ref, s0_ref,
                y_ref, so_ref, s_ref, obuf_ref, *, precise, nb, nc):
    mm = _MM(precise)
    j = pl.program_id(1)
    L = CHUNK
    rows = nc * L

    @pl.when(j == 0)
    def _():
        s_ref[...] = s0_ref[...]

    c = GROUP_W
    lane = lax.broadcasted_iota(jnp.int32, (rows, c), 1)
    first_half = (lane % HEAD_DIM) < (HEAD_DIM // 2)
    cs, sn = cos_ref[...], sin_ref[...]
    kdec, qdec = kdec_ref[...], qdec_ref[...]

    def rope(x):
        partner = jnp.where(first_half, pltpu.roll(x, c - HEAD_DIM // 2, 1), pltpu.roll(x, HEAD_DIM // 2, 1))
        return x * cs + partner * sn

    each = lambda f, *lists: [f(*vals) for vals in zip(*lists)]
    keys = [(bi, ci, h) for bi in range(nb) for ci in range(nc) for h in range(N_HEADS)]
    qs, ks, vs, kds, qds, g_all = [], [], [], [], [], []
    for bi in range(nb):
        q = rope(z_ref[bi, :, 0:c])
        k = rope(z_ref[bi, :, c:2 * c]) * HEAD_DIM ** -0.5
        v = z_ref[bi, :, 2 * c:3 * c]
        g_all.append(z_ref[bi, :, 3 * c:4 * c])
        for ci in range(nc):
            rs = slice(ci * L, (ci + 1) * L)
            kd, qd = k[rs] * kdec, q[rs] * qdec
            for h in range(N_HEADS):
                hs = slice(h * HEAD_DIM, (h + 1) * HEAD_DIM)
                for lst, val in ((qs, q[rs]), (ks, k[rs]), (vs, v[rs]), (kds, kd), (qds, qd)):
                    lst.append(val[:, hs])
    att = [mm.nt(q_, k_) * dm_ref[h] for q_, k_, (_, _, h) in zip(qs, ks, keys)]
    o_in = each(mm.nn, att, vs)
    kv = each(mm.tn, kds, vs)
    s_prev = {}
    for bi in range(nb):
        for h in range(N_HEADS):
            s = s_ref[bi, h]
            for ci in range(nc):
                s_prev[(bi, ci, h)] = s
                s = s * cdec_ref[h] + kv[keys.index((bi, ci, h))]
            s_ref[bi, h] = s
    o_x = [mm.nn(qd_, s_prev[key]) for qd_, key in zip(qds, keys)]
    for i, (bi, ci, h) in enumerate(keys):
        r0 = bi * rows + ci * L
        obuf_ref[r0:r0 + L, h * HEAD_DIM:(h + 1) * HEAD_DIM] = o_in[i] + o_x[i]
    seg = seg_ref[...]
    segsum = lambda x: _dot_pieces(x, seg, 2 if precise else 1, lhs_exact=False)
    ob = obuf_ref[...]
    oc = ob - segsum(ob)
    o = oc * lax.rsqrt(segsum(oc * oc) + RET_GN_EPS) * gn_ref[...]
    g = jnp.concatenate(g_all, axis=0) if nb > 1 else g_all[0]
    y_ref[...] = (g * _sigmoid(g) * o).astype(y_ref.dtype).reshape(nb, rows, c)
    so_ref[...] = s_ref[...]


def _ret_mixer(z, s0, cos, sin, dmask, kdec, qdec, cdec, gn, seg, precise):
    b, t, _ = z.shape
    L = CHUNK
    nc = min(t // L, 4)
    nb = 1 if nc > 1 else min(b, 4)
    rows = nc * L
    full = lambda a: pl.BlockSpec(a.shape, lambda i, j: (0,) * a.ndim)
    st_spec = pl.BlockSpec((nb, N_HEADS, HEAD_DIM, HEAD_DIM), lambda i, j: (i, 0, 0, 0))
    tab_spec = pl.BlockSpec((rows, GROUP_W), lambda i, j: (j, 0))
    seg = seg.astype(BF16)
    return pl.pallas_call(
        functools.partial(_ret_kernel, precise=precise, nb=nb, nc=nc),
        grid=(b // nb, t // rows),
        in_specs=[pl.BlockSpec((nb, rows, 4 * GROUP_W), lambda i, j: (i, j, 0)), tab_spec, tab_spec,
                  full(dmask), full(kdec), full(qdec), full(cdec), full(gn), full(seg), st_spec],
        out_specs=[pl.BlockSpec((nb, rows, GROUP_W), lambda i, j: (i, j, 0)), st_spec],
        out_shape=[jax.ShapeDtypeStruct((b, t, GROUP_W), F32 if precise else BF16),
                   jax.ShapeDtypeStruct((b, N_HEADS, HEAD_DIM, HEAD_DIM), F32)],
        scratch_shapes=[pltpu.VMEM((nb, N_HEADS, HEAD_DIM, HEAD_DIM), F32),
                        pltpu.VMEM((nb * rows, GROUP_W), F32)],
        compiler_params=_cp(("parallel", "arbitrary")),
        name="ret_mixer",
    )(z, cos, sin, dmask, kdec, qdec, cdec, gn, seg, s0)


def _cumsum_lanes(x_ref, o_ref, tri, carry, width):
    blk = 128
    for c0 in range(0, width, blk):
        wd = min(blk, width - c0)
        cs = _dot_hi(x_ref[:, c0:c0 + wd], tri[:wd, :wd]) + carry
        o_ref[:, c0:c0 + wd] = cs
        carry = cs[:, wd - 1:wd]
    return carry


def _fox_gate_kernel(*refs, past, t):
    if past:
        zf_ref, b_ref, tri_ref, lfp_ref, lf_ref, cn_ref, cp_ref = refs
    else:
        zf_ref, b_ref, tri_ref, lf_ref, cn_ref = refs
    tri = tri_ref[...]
    x = zf_ref[...] + b_ref[...]
    lf_ref[...] = jnp.minimum(x, 0.0) - jnp.log(1.0 + jnp.exp(-jnp.abs(x)))
    carry = jnp.zeros((zf_ref.shape[0], 1), F32)
    if past:
        carry = _cumsum_lanes(lfp_ref, cp_ref, tri, carry, past)
    _cumsum_lanes(lf_ref, cn_ref, tri, carry, t)


def _fox_gate(zf_t, bias_rows, tri, lf_past_t):
    rows, t = zf_t.shape
    rt = min(rows, 32)
    past = 0 if lf_past_t is None else lf_past_t.shape[1]
    row_spec = lambda wd: pl.BlockSpec((rt, wd), lambda i: (i, 0))
    in_specs = [row_spec(t), row_spec(1), pl.BlockSpec(tri.shape, lambda i: (0, 0))]
    out_specs = [row_spec(t), row_spec(t)]
    out_shape = [jax.ShapeDtypeStruct((rows, t), F32), jax.ShapeDtypeStruct((rows, t), F32)]
    args = [zf_t, bias_rows, tri]
    if past:
        in_specs.append(row_spec(past))
        out_specs.append(row_spec(past))
        out_shape.append(jax.ShapeDtypeStruct((rows, past), F32))
        args.append(lf_past_t)
    return pl.pallas_call(
        functools.partial(_fox_gate_kernel, past=past, t=t),
        grid=(rows // rt,),
        in_specs=in_specs, out_specs=out_specs, out_shape=out_shape,
        compiler_params=_cp(("parallel",)),
        name="fox_gate",
    )(*args)


def _fox_attn_kernel(q_ref, kn_ref, vn_ref, og_ref, ckn_ref, y_ref, m_ref, l_ref, acc_ref, *,
                     tq, precise):
    mm = _MM(precise)
    i, j = pl.program_id(1), pl.program_id(2)
    nk = pl.num_programs(2)

    @pl.when(j == 0)
    def _():
        m_ref[...] = jnp.full(m_ref.shape, -jnp.inf, F32)
        l_ref[...] = jnp.zeros(l_ref.shape, F32)
        acc_ref[...] = jnp.zeros(acc_ref.shape, F32)

    heads = range(N_HEADS)

    def per_head_lanes(cols):
        lane_head = lax.broadcasted_iota(jnp.int32, (tq, GROUP_W), 1) // HEAD_DIM
        out = jnp.broadcast_to(cols[N_HEADS - 1], (tq, GROUP_W))
        for h in range(N_HEADS - 2, -1, -1):
            out = jnp.where(lane_head == h, cols[h], out)
        return out

    def block(k_ref, v_ref, ck_ref, diagonal):
        q = q_ref[...] * (HEAD_DIM ** -0.5 * LOG2E)
        k = k_ref[...]
        v = v_ref[...]
        tk = k.shape[0]
        ck = ck_ref[...] * LOG2E
        lane_head = lax.broadcasted_iota(jnp.int32, (tk, GROUP_W), 1) // HEAD_DIM
        kbd = jnp.concatenate([jnp.where(lane_head == h, k, 0.0) for h in heads], axis=0)
        vbd = jnp.concatenate([jnp.where(lane_head == h, v, 0.0) for h in heads], axis=0)
        s_all = mm.nt(q, kbd)
        s = [s_all[:, h * tk:(h + 1) * tk] - ck[h:h + 1, :] for h in heads]
        if diagonal:
            keep = (lax.broadcasted_iota(jnp.int32, (tq, tk), 1)
                    <= lax.broadcasted_iota(jnp.int32, (tq, tk), 0))
            s = [jnp.where(keep, s_h, -jnp.inf) for s_h in s]
        m_prev = [m_ref[h] for h in heads]
        m_new = [jnp.maximum(m_prev[h], jnp.max(s[h], axis=1, keepdims=True)) for h in heads]
        alpha = [jnp.exp2(m_prev[h] - m_new[h]) for h in heads]
        p = [jnp.exp2(s[h] - m_new[h]) for h in heads]
        pv = mm.nn(jnp.concatenate(p, axis=1), vbd)
        for h in heads:
            l_ref[h] = alpha[h] * l_ref[h] + jnp.sum(p[h], axis=1, keepdims=True)
            m_ref[h] = m_new[h]
        acc_ref[...] = per_head_lanes(alpha) * acc_ref[...] + pv

    @pl.when(j < i)
    def _():
        block(kn_ref, vn_ref, ckn_ref, False)

    @pl.when(j == i)
    def _():
        block(kn_ref, vn_ref, ckn_ref, True)

    @pl.when(j == nk - 1)
    def _():
        o = acc_ref[...] / per_head_lanes([l_ref[h] for h in heads])
        y_ref[...] = (_sigmoid(og_ref[...]) * o).astype(y_ref.dtype)


def _fox_attn(zfox, ck_new, precise):
    b, t, _ = zfox.shape
    tq = min(t, 512)
    nq = t // tq
    jc = lambda i, j: jnp.minimum(j, i)
    in_specs = [pl.BlockSpec((None, tq, GROUP_W), lambda bb, i, j: (bb, i, 0)),
                pl.BlockSpec((None, tq, GROUP_W), lambda bb, i, j: (bb, jc(i, j), 1)),
                pl.BlockSpec((None, tq, GROUP_W), lambda bb, i, j: (bb, jc(i, j), 2)),
                pl.BlockSpec((None, tq, GROUP_W), lambda bb, i, j: (bb, i, 3)),
                pl.BlockSpec((None, N_HEADS, tq), lambda bb, i, j: (bb, 0, jc(i, j)))]
    return pl.pallas_call(
        functools.partial(_fox_attn_kernel, tq=tq, precise=precise),
        grid=(b, nq, nq),
        in_specs=in_specs,
        out_specs=pl.BlockSpec((None, tq, GROUP_W), lambda bb, i, j: (bb, i, 0)),
        out_shape=jax.ShapeDtypeStruct((b, t, GROUP_W), F32 if precise else BF16),
        scratch_shapes=[pltpu.VMEM((N_HEADS, tq, 1), F32), pltpu.VMEM((N_HEADS, tq, 1), F32),
                        pltpu.VMEM((tq, GROUP_W), F32)],
        compiler_params=_cp(("parallel", "parallel", "arbitrary")),
        name="fox_attn",
    )(zfox, zfox, zfox, zfox, ck_new)


def _fox_past_kernel(zf_ref, ckn_ref, kt_ref, vt_ref, ckp_ref, y_ref, m_ref, l_ref, acc_ref, *,
                     nkp, nb, precise):
    mm = _MM(precise)
    j = pl.program_id(1)
    t = zf_ref.shape[1]
    c = GROUP_W
    chains = [(bi, h) for bi in range(nb) for h in range(N_HEADS)]
    hs = lambda h: slice(h * HEAD_DIM, (h + 1) * HEAD_DIM)

    @pl.when(j == 0)
    def _():
        m_ref[...] = jnp.full(m_ref.shape, -jnp.inf, F32)
        l_ref[...] = jnp.zeros(l_ref.shape, F32)
        acc_ref[...] = jnp.zeros(acc_ref.shape, F32)

    def online_update(s, pv_of):
        m_prev = [m_ref[i] for i in range(len(chains))]
        m_new = [jnp.maximum(mp, jnp.max(s_i, axis=1, keepdims=True)) for mp, s_i in zip(m_prev, s)]
        alpha = [jnp.exp2(mp - mn) for mp, mn in zip(m_prev, m_new)]
        p = [jnp.exp2(s_i - mn) for s_i, mn in zip(s, m_new)]
        pv = pv_of(p)
        for i in range(len(chains)):
            l_ref[i] = alpha[i] * l_ref[i] + jnp.sum(p[i], axis=1, keepdims=True)
            acc_ref[i] = alpha[i] * acc_ref[i] + pv[i]
            m_ref[i] = m_new[i]

    def queries():
        return [zf_ref[bi, :, hs(h)] * (HEAD_DIM ** -0.5 * LOG2E) for bi, h in chains]

    @pl.when(j < nkp)
    def _():
        q = queries()
        s = [mm.nn(q[i], kt_ref[bi, h]) - ckp_ref[bi, h:h + 1, :] * LOG2E
             for i, (bi, h) in enumerate(chains)]
        online_update(s, lambda p: [mm.nt(p[i], vt_ref[bi, h]) for i, (bi, h) in enumerate(chains)])

    @pl.when(j == nkp)
    def _():
        q = queries()
        keep = lax.broadcasted_iota(jnp.int32, (t, t), 1) <= lax.broadcasted_iota(jnp.int32, (t, t), 0)
        s = [jnp.where(keep, mm.nt(q[i], zf_ref[bi, :, c + h * HEAD_DIM:c + (h + 1) * HEAD_DIM])
                       - ckn_ref[bi, h:h + 1, :] * LOG2E, -jnp.inf)
             for i, (bi, h) in enumerate(chains)]
        online_update(s, lambda p: [mm.nn(p[i], zf_ref[bi, :, 2 * c + h * HEAD_DIM:2 * c + (h + 1) * HEAD_DIM])
                                    for i, (bi, h) in enumerate(chains)])
        for i, (bi, h) in enumerate(chains):
            og = zf_ref[bi, :, 3 * c + h * HEAD_DIM:3 * c + (h + 1) * HEAD_DIM]
            y_ref[bi, :, hs(h)] = (_sigmoid(og) * (acc_ref[i] / l_ref[i])).astype(y_ref.dtype)


def _fox_attn_past(zfox, ck_new, kt_all, vt_all, ck_past, layer, precise):
    b, t, _ = zfox.shape
    p = kt_all.shape[-1]
    tkp = 512
    nkp = p // tkp
    nb = min(b, 4)
    jp = lambda j: jnp.minimum(j, nkp - 1)
    cache_spec = pl.BlockSpec((None, nb, N_HEADS, HEAD_DIM, tkp), lambda i, j: (layer, i, 0, 0, jp(j)))
    return pl.pallas_call(
        functools.partial(_fox_past_kernel, nkp=nkp, nb=nb, precise=precise),
        grid=(b // nb, nkp + 1),
        in_specs=[pl.BlockSpec((nb, t, 4 * GROUP_W), lambda i, j: (i, 0, 0)),
                  pl.BlockSpec((nb, N_HEADS, t), lambda i, j: (i, 0, 0)),
                  cache_spec, cache_spec,
                  pl.BlockSpec((nb, N_HEADS, tkp), lambda i, j: (i, 0, jp(j)))],
        out_specs=pl.BlockSpec((nb, t, GROUP_W), lambda i, j: (i, 0, 0)),
        out_shape=jax.ShapeDtypeStruct((b, t, GROUP_W), F32 if precise else BF16),
        scratch_shapes=[pltpu.VMEM((nb * N_HEADS, t, 1), F32), pltpu.VMEM((nb * N_HEADS, t, 1), F32),
                        pltpu.VMEM((nb * N_HEADS, t, HEAD_DIM), F32)],
        compiler_params=_cp(("parallel", "arbitrary")),
        name="fox_attn_past",
    )(zfox, ck_new, kt_all, vt_all, ck_past)


def _out_kernel(x_ref, ya_ref, yb_ref, yc_ref, yd_ref, *refs, precise):
    w_refs, o_ref = refs[:-1], refs[-1]
    dot = lambda a, b: jnp.dot(a, b, preferred_element_type=F32)
    acc = x_ref[...]
    for i, y_ref in enumerate((ya_ref, yb_ref, yc_ref, yd_ref)):
        if precise:
            yh, yl = _split(y_ref[...])
            acc = acc + (dot(yh, w_refs[0][i]) + (dot(yh, w_refs[1][i]) + dot(yl, w_refs[0][i])))
        else:
            acc = acc + dot(y_ref[...], w_refs[0][i])
    o_ref[...] = acc


def _out_proj(x, ys, ws):
    n = x.shape[0]
    tm = 512
    y_spec = pl.BlockSpec((tm, GROUP_W), lambda i: (i, 0))
    x_spec = pl.BlockSpec((tm, D_MODEL), lambda i: (i, 0))
    return pl.pallas_call(
        functools.partial(_out_kernel, precise=len(ws) == 2),
        grid=(n // tm,),
        in_specs=[x_spec, y_spec, y_spec, y_spec, y_spec]
                 + [pl.BlockSpec(w.shape, lambda i: (0, 0, 0)) for w in ws],
        out_specs=x_spec,
        out_shape=jax.ShapeDtypeStruct((n, D_MODEL), F32),
        compiler_params=_cp(("parallel",)),
        name="out_proj",
    )(x, *ys, *ws)


def _router_kernel(x_ref, g_ref, w_ref, b_ref, tril_ref, xs_ref, gs_ref, dest_ref, meta_ref):
    h = _rmsnorm(x_ref[...], g_ref[...])
    logits = _dot_hi(h, w_ref[...]) + b_ref[...]
    tm = logits.shape[0]
    lane_i = lax.broadcasted_iota(jnp.int32, (tm, ROUTER_PAD), 1)
    lane = lane_i.astype(F32)
    big = float(ROUTER_PAD)
    rmax = lambda a: jnp.max(a, axis=1, keepdims=True)
    rsum = lambda a: jnp.sum(a, axis=1, keepdims=True)
    first = lambda m: jnp.min(jnp.where(m, lane, big), axis=1, keepdims=True)

    is_c = jnp.logical_and(lane_i >= N_EXPERTS, lane_i < N_EXPERTS + N_GROUPS)
    lc = jnp.where(is_c, logits, -jnp.inf)
    ec = jnp.exp(lc - rmax(lc))
    pc_all = ec / rsum(ec)
    pc = rmax(pc_all)
    gi = first(jnp.logical_and(is_c, pc_all == pc)) - float(N_EXPERTS)

    is_f = (lane_i // EXPERTS_PER_GROUP).astype(F32) == gi
    lf = jnp.where(is_f, logits, -jnp.inf)
    ef = jnp.exp(lf - rmax(lf))
    pf = jnp.where(is_f, ef / rsum(ef), -1.0)
    t1 = rmax(pf)
    i1 = first(pf == t1)
    pf2 = jnp.where(lane == i1, -1.0, pf)
    t2 = rmax(pf2)
    i2 = first(pf2 == t2)
    den = t1 + t2
    gate = jnp.where(lane == i1, pc * (t1 / den), jnp.where(lane == i2, pc * (t2 / den), 0.0))

    onehot = jnp.where(lane == gi, 1.0, 0.0)
    rank = jnp.dot(tril_ref[...], onehot.astype(BF16), preferred_element_type=F32)
    cnt = jnp.sum(onehot, axis=0, keepdims=True)
    padded = jnp.floor((cnt + (MOE_PAD - 1.0)) * (1.0 / MOE_PAD)) * MOE_PAD
    lane1 = lax.broadcasted_iota(jnp.int32, (1, ROUTER_PAD), 1)
    off = jnp.zeros((1, ROUTER_PAD), F32)
    meta = jnp.zeros((1, ROUTER_PAD), F32)
    start = jnp.zeros((1, 1), F32)
    for g in range(N_GROUPS):
        size_g = jnp.sum(jnp.where(lane1 == g, padded, 0.0), axis=1, keepdims=True)
        off = off + jnp.where(lane1 == g, start, 0.0)
        meta = meta + jnp.where(lane1 == g, start, 0.0) + jnp.where(lane1 == N_GROUPS + g, size_g, 0.0)
        start = start + size_g
    dest = rsum(onehot * (off + rank))
    rows = lax.broadcasted_iota(jnp.int32, (tm, MOE_ROWS), 1).astype(F32)
    perm_t = jnp.where(rows == dest, 1.0, 0.0).astype(BF16)
    rhs = jnp.concatenate([h.astype(BF16)] + _pieces(gate, 3), axis=1)
    srt = lax.dot_general(perm_t, rhs, (((0,), (0,)), ((), ())), preferred_element_type=F32)
    xs_ref[...] = srt[:, :D_MODEL].astype(BF16)
    gs_ref[...] = (srt[:, D_MODEL:D_MODEL + ROUTER_PAD] + srt[:, D_MODEL + ROUTER_PAD:D_MODEL + 2 * ROUTER_PAD]
                   + srt[:, D_MODEL + 2 * ROUTER_PAD:])
    dest_ref[...] = dest
    meta_ref[...] = meta


def _router(x, g, wr, br, tril):
    n = x.shape[0]
    tm = MOE_TILE
    nt = n // tm
    return pl.pallas_call(
        _router_kernel,
        grid=(nt,),
        in_specs=[pl.BlockSpec((tm, D_MODEL), lambda i: (i, 0)),
                  pl.BlockSpec((1, D_MODEL), lambda i: (0, 0)),
                  pl.BlockSpec((D_MODEL, ROUTER_PAD), lambda i: (0, 0)),
                  pl.BlockSpec((1, ROUTER_PAD), lambda i: (0, 0)),
                  pl.BlockSpec((tm, tm), lambda i: (0, 0))],
        out_specs=[pl.BlockSpec((MOE_ROWS, D_MODEL), lambda i: (i, 0)),
                   pl.BlockSpec((MOE_ROWS, ROUTER_PAD), lambda i: (i, 0)),
                   pl.BlockSpec((tm, 1), lambda i: (i, 0)),
                   pl.BlockSpec((None, 1, ROUTER_PAD), lambda i: (i, 0, 0))],
        out_shape=[jax.ShapeDtypeStruct((nt * MOE_ROWS, D_MODEL), BF16),
                   jax.ShapeDtypeStruct((nt * MOE_ROWS, ROUTER_PAD), F32),
                   jax.ShapeDtypeStruct((n, 1), F32),
                   jax.ShapeDtypeStruct((nt, 1, ROUTER_PAD), F32)],
        compiler_params=_cp(("parallel",)),
        name="moe_router",
    )(x, g, wr, br, tril)


def _moe_kernel(meta_ref, xs_ref, gs_ref, w1_ref, w3_ref, w2_ref, o_ref, *, tps):
    i, e = pl.program_id(0), pl.program_id(1)

    @pl.when(e == 0)
    def _():
        o_ref[...] = jnp.zeros(o_ref.shape, F32)

    g = e // EXPERTS_PER_GROUP
    for sub in range(tps):
        tile = i * tps + sub
        off = meta_ref[tile, g] + sub * MOE_ROWS
        size = meta_ref[tile, N_GROUPS + g]
        for w in range(MOE_TILE // MOE_WIN + 1):
            @pl.when(w * MOE_WIN < size)
            def _():
                rows = pl.ds(pl.multiple_of(off + w * MOE_WIN, MOE_PAD), MOE_WIN)
                xw = xs_ref[rows, :]
                a = jnp.dot(xw, w1_ref[...], preferred_element_type=F32)
                b = jnp.dot(xw, w3_ref[...], preferred_element_type=F32)
                he = (a * _sigmoid(a) * b).astype(BF16)
                ye = jnp.dot(he, w2_ref[...], preferred_element_type=F32)
                gw = gs_ref[rows, :]
                lane = lax.broadcasted_iota(jnp.int32, gw.shape, 1)
                gcol = jnp.sum(jnp.where(lane == e, gw, 0.0), axis=1, keepdims=True)
                o_ref[rows, :] += gcol * ye


def _moe(meta, xs, gs, w1, w3, w2, layer):
    nt = meta.shape[0]
    tps = MOE_TILES_PER_STEP if nt % MOE_TILES_PER_STEP == 0 else 1
    rows = tps * MOE_ROWS
    return pl.pallas_call(
        functools.partial(_moe_kernel, tps=tps),
        grid_spec=pltpu.PrefetchScalarGridSpec(
            num_scalar_prefetch=1,
            grid=(nt // tps, N_EXPERTS),
            in_specs=[pl.BlockSpec((rows, D_MODEL), lambda i, e, m: (i, 0)),
                      pl.BlockSpec((rows, ROUTER_PAD), lambda i, e, m: (i, 0)),
                      pl.BlockSpec((None, None, D_MODEL, D_EXPERT), lambda i, e, m: (layer, e, 0, 0)),
                      pl.BlockSpec((None, None, D_MODEL, D_EXPERT), lambda i, e, m: (layer, e, 0, 0)),
                      pl.BlockSpec((None, None, D_EXPERT, D_MODEL), lambda i, e, m: (layer, e, 0, 0))],
            out_specs=pl.BlockSpec((rows, D_MODEL), lambda i, e, m: (i, 0))),
        out_shape=jax.ShapeDtypeStruct((nt * MOE_ROWS, D_MODEL), F32),
        compiler_params=pltpu.CompilerParams(dimension_semantics=("parallel", "arbitrary"),
                                             vmem_limit_bytes=MOE_VMEM_LIMIT),
        name="moe_experts",
    )(meta, xs, gs, w1, w3, w2)


def _combine_kernel(x_ref, os_ref, dest_ref, nf_ref, o_ref, *, final):
    tm = x_ref.shape[0]
    rows = lax.broadcasted_iota(jnp.int32, (tm, MOE_ROWS), 1).astype(F32)
    perm_t = jnp.where(rows == dest_ref[...], 1.0, 0.0).astype(BF16)
    y = x_ref[...] + _dot_pieces(perm_t, os_ref[...], 2, lhs_exact=True)
    o_ref[...] = _rmsnorm(y, nf_ref[...]) if final else y


def _moe_combine(x, os, dest, nf, final):
    n = x.shape[0]
    tm = MOE_TILE
    return pl.pallas_call(
        functools.partial(_combine_kernel, final=final),
        grid=(n // tm,),
        in_specs=[pl.BlockSpec((tm, D_MODEL), lambda i: (i, 0)),
                  pl.BlockSpec((MOE_ROWS, D_MODEL), lambda i: (i, 0)),
                  pl.BlockSpec((tm, 1), lambda i: (i, 0)),
                  pl.BlockSpec((1, D_MODEL), lambda i: (0, 0))],
        out_specs=pl.BlockSpec((tm, D_MODEL), lambda i: (i, 0)),
        out_shape=jax.ShapeDtypeStruct((n, D_MODEL), F32),
        compiler_params=_cp(("parallel",)),
        name="moe_combine",
    )(x, os, dest, nf)


def _prep_layer(l, p):
    w_in = p["w_in"][l]
    o_rwkv, o_ret, o_fox = GROUP_W, GROUP_W + RWKV_COLS, GROUP_W + RWKV_COLS + 4 * GROUP_W
    ff = jnp.pad(w_in[:, o_fox + 4 * GROUP_W:], ((0, 0), (0, FF_PAD - N_HEADS)))
    w_in_r = jnp.concatenate([w_in[:, o_rwkv:o_ret], w_in[:, o_ret:o_fox],
                              w_in[:, o_fox:o_fox + 4 * GROUP_W], w_in[:, :GROUP_W], ff], axis=1)
    pw = p["pool_w"][l]
    pg = GROUP_W // len(POOL_WINDOWS)
    w_bd = jnp.zeros((GROUP_W, GROUP_W), F32)
    for gi in range(len(POOL_WINDOWS)):
        w_bd = w_bd.at[gi * pg:(gi + 1) * pg, gi * pg:(gi + 1) * pg].set(pw[gi])
    row = lambda a: a.reshape(1, -1)
    rwkv = (row(p["rwkv_mu"][l]), row(p["rwkv_w0"][l]), p["rwkv_w2"][l], row(p["rwkv_a0"][l]),
            p["rwkv_a2"][l], p["rwkv_g2"][l], row(p["rwkv_kk"][l]), row(p["rwkv_ka"][l]),
            row(p["rwkv_rk"][l]), row(p["rwkv_lnx"][l]))
    wf = jnp.transpose(p["moe_wf"][l], (1, 0, 2)).reshape(D_MODEL, N_EXPERTS)
    wr = jnp.pad(jnp.concatenate([wf, p["moe_wc"][l]], axis=1),
                 ((0, 0), (0, ROUTER_PAD - N_EXPERTS - N_GROUPS)))
    br = jnp.pad(jnp.concatenate([p["moe_bf"][l].reshape(-1), p["moe_bc"][l]]),
                 (0, ROUTER_PAD - N_EXPERTS - N_GROUPS)).reshape(1, ROUTER_PAD)
    w_out4 = p["w_out"][l].reshape(4, GROUP_W, D_MODEL)
    precise = l < p["w_in"].shape[0] - 1
    split = lambda w: _split_bits(w) if precise else (w.astype(BF16),)
    return dict(
        precise=precise, n1=row(p["norm1_g"][l]), w_in=split(w_in_r), pool_w=w_bd,
        pool_scale=row(p["pool_scale"][l]),
        rwkv=rwkv, ret_gn=row(p["ret_gn"][l]), fox_bf=p["fox_bf"][l],
        w_out=split(w_out4), n2=row(p["norm2_g"][l]),
        wr=wr, br=br, layer=l,
        w1=p["moe_w1"].astype(BF16), w3=p["moe_w3"].astype(BF16), w2=p["moe_w2"].astype(BF16))


def _tables(t, pos0):
    half = HEAD_DIM // 2
    inv = ROPE_BASE ** (-jnp.arange(half, dtype=F32) / half)
    ang = (pos0 + jnp.arange(t)).astype(F32)[:, None] * inv[None, :]
    cos, sin = jnp.cos(ang), jnp.sin(ang)
    cos_t = jnp.tile(jnp.concatenate([cos, cos], axis=1), (1, N_HEADS))
    sin_t = jnp.tile(jnp.concatenate([-sin, sin], axis=1), (1, N_HEADS))
    L = CHUNK
    log_g = jnp.log(jnp.array(RET_GAMMA, F32))
    idx = jnp.arange(L, dtype=F32)
    dmask = jnp.exp(log_g[:, None, None] * jnp.abs(idx[:, None] - idx[None, :]))
    lanes = lambda a: jnp.repeat(a, HEAD_DIM, axis=1)
    kdec = lanes(jnp.exp(log_g[None, :] * (L - 1.0 - idx)[:, None]))
    qdec = lanes(jnp.exp(log_g[None, :] * (idx + 1.0)[:, None]))
    cdec = jnp.broadcast_to(jnp.exp(log_g * L)[:, None, None], (N_HEADS, HEAD_DIM, HEAD_DIM))
    hid = jnp.arange(GROUP_W) // HEAD_DIM
    seg = jnp.where(hid[:, None] == hid[None, :], 1.0 / HEAD_DIM, 0.0).astype(F32)
    tril = jnp.tril(jnp.ones((L, L), F32))
    triu = jnp.triu(jnp.ones((128, 128), F32))
    tril_moe = jnp.tril(jnp.ones((MOE_TILE, MOE_TILE), F32), -1).astype(BF16)
    return dict(cos=cos_t, sin=sin_t, dmask=dmask, kdec=kdec, qdec=qdec, cdec=cdec, seg=seg,
                tril=tril, triu=triu, tril_moe=tril_moe)


def _trunk(x, pos0, states, layers, norm_f, cache=None):
    b, t, _ = x.shape
    n = b * t
    tb = _tables(t, pos0)
    xf = x.reshape(n, D_MODEL)
    nf = norm_f.reshape(1, D_MODEL)
    outs = [[] for _ in range(7)]
    for l, lp in enumerate(layers):
        pool_buf, shift, wkv, ret, lf_past = states[l]
        precise = lp["precise"]
        z_rwkv, z_ret, z_fox, z_pool, z_ff = _in_proj(xf, lp["n1"], lp["w_in"])
        buf16 = jnp.pad(pool_buf, ((0, 0), (POOL_PAD - POOL_BUF, 0), (0, 0)))
        y_a, pool_new = _pool_mixer(z_pool.reshape(b, t, GROUP_W), buf16, lp["pool_w"], lp["pool_scale"], pos0,
                                    precise)
        y_b, wkv_new, shift_new = _rwkv_mixer(z_rwkv.reshape(b, t, RWKV_COLS), shift, wkv, lp["rwkv"],
                                              tb["seg"], tb["tril"], precise)
        y_c, ret_new = _ret_mixer(z_ret.reshape(b, t, 4 * GROUP_W), ret, tb["cos"], tb["sin"], tb["dmask"],
                                  tb["kdec"], tb["qdec"], tb["cdec"], lp["ret_gn"], tb["seg"], precise)
        zf_t = jnp.transpose(z_ff[:, :N_HEADS].reshape(b, t, N_HEADS), (0, 2, 1)).reshape(b * N_HEADS, t)
        bias_rows = jnp.tile(lp["fox_bf"], b).reshape(b * N_HEADS, 1)
        zfox3 = z_fox.reshape(b, t, 4 * GROUP_W)
        if cache is None:
            lf_t, c_new = _fox_gate(zf_t, bias_rows, tb["triu"], None)
            cn3 = c_new.reshape(b, N_HEADS, t)
            y_d = _fox_attn(zfox3, cn3, precise)
        else:
            p = lf_past.shape[1]
            lfp_t = jnp.transpose(lf_past, (0, 2, 1)).reshape(b * N_HEADS, p)
            lf_t, c_new, c_past = _fox_gate(zf_t, bias_rows, tb["triu"], lfp_t)
            cn3 = c_new.reshape(b, N_HEADS, t)
            y_d = _fox_attn_past(zfox3, cn3, cache[0], cache[1], c_past.reshape(b, N_HEADS, p), l, precise)
        x1 = _out_proj(xf, [y.reshape(n, GROUP_W) for y in (y_a, y_b, y_c, y_d)], lp["w_out"])
        xs, gs, dest, meta = _router(x1, lp["n2"], lp["wr"], lp["br"], tb["tril_moe"])
        meta = meta[:, 0, :2 * N_GROUPS].astype(jnp.int32)
        ys = _moe(meta, xs, gs, lp["w1"], lp["w3"], lp["w2"], lp["layer"])
        xf = _moe_combine(x1, ys, dest, nf, final=(l == len(layers) - 1))
        k_new = zfox3[:, :, GROUP_W:2 * GROUP_W].reshape(b, t, N_HEADS, HEAD_DIM)
        v_new = zfox3[:, :, 2 * GROUP_W:3 * GROUP_W].reshape(b, t, N_HEADS, HEAD_DIM)
        lf_new = jnp.transpose(lf_t.reshape(b, N_HEADS, t), (0, 2, 1))
        for lst, s in zip(outs, (pool_new, shift_new, wkv_new, ret_new, k_new, v_new, lf_new)):
            lst.append(s)
    return xf.reshape(b, t, D_MODEL), [jnp.stack(lst) for lst in outs]


def kernel(x_prompt, x_sample, state_pool, state_shift, state_wkv, state_ret, cache_fox_k, cache_fox_v, cache_fox_logf, norm1_g, w_in, pool_w, pool_scale, rwkv_mu, rwkv_w0, rwkv_w2, rwkv_a0, rwkv_a2, rwkv_g2, rwkv_kk, rwkv_ka, rwkv_rk, rwkv_lnx, ret_gn, fox_bf, w_out, norm2_g, moe_wc, moe_bc, moe_wf, moe_bf, moe_w1, moe_w3, moe_w2, norm_f):
    p = dict(norm1_g=norm1_g, w_in=w_in, pool_w=pool_w, pool_scale=pool_scale, rwkv_mu=rwkv_mu,
             rwkv_w0=rwkv_w0, rwkv_w2=rwkv_w2, rwkv_a0=rwkv_a0, rwkv_a2=rwkv_a2, rwkv_g2=rwkv_g2,
             rwkv_kk=rwkv_kk, rwkv_ka=rwkv_ka, rwkv_rk=rwkv_rk, rwkv_lnx=rwkv_lnx, ret_gn=ret_gn,
             fox_bf=fox_bf, w_out=w_out, norm2_g=norm2_g, moe_wc=moe_wc, moe_bc=moe_bc, moe_wf=moe_wf,
             moe_bf=moe_bf, moe_w1=moe_w1, moe_w3=moe_w3, moe_w2=moe_w2)
    depth = w_in.shape[0]
    layers = [_prep_layer(l, p) for l in range(depth)]
    b = x_prompt.shape[0]
    dt = x_prompt.dtype
    prompt_init = [(jnp.zeros((b, POOL_BUF, GROUP_W), dt), jnp.zeros((b, 1, RWKV_COLS), dt),
                    jnp.zeros((b, N_HEADS, HEAD_DIM, HEAD_DIM), dt),
                    jnp.zeros((b, N_HEADS, HEAD_DIM, HEAD_DIM), dt), None)
                   for _ in range(depth)]
    sample_init = [(state_pool[l], state_shift[l], state_wkv[l], state_ret[l], cache_fox_logf[l])
                   for l in range(depth)]
    past = cache_fox_k.shape[2]
    cache = (jnp.transpose(cache_fox_k, (0, 1, 3, 4, 2)), jnp.transpose(cache_fox_v, (0, 1, 3, 4, 2)))
    y_prompt, new_p = _trunk(x_prompt, 0, prompt_init, layers, norm_f)
    y_sample, new_s = _trunk(x_sample, past, sample_init, layers, norm_f, cache)
    return (y_prompt, y_sample, *new_p, *new_s)
```

```python
import functools

import jax
import jax.numpy as jnp
import numpy as np
from jax import lax
from jax.experimental import pallas as pl
from jax.experimental.pallas import tpu as pltpu

F32 = jnp.float32
BF16 = jnp.bfloat16
HIGHEST = lax.Precision.HIGHEST

D_MODEL = 1024
DEPTH = 2
CHUNK = 64
GROUP_W = 256
HEAD_DIM = 64
N_HEADS = 4
POOL_WINDOWS = (2, 4, 8, 16)
POOL_BUF = 15
POOL_PAD = 16
RWKV_COLS = 1024
N_IN = 3332
FF_PAD = 128
RET_GAMMA = tuple(1.0 - 2.0 ** (-5 - h) for h in range(N_HEADS))
ROPE_BASE = 10000.0
N_GROUPS = 4
EXPERTS_PER_GROUP = 4
N_EXPERTS = 16
D_EXPERT = 512
LOG2E = 1.4426950408889634
RMS_EPS = 1e-6
RWKV_GN_EPS = 64e-5
RET_GN_EPS = 1e-5
SUB = 16
ROUTER_PAD = 128
MOE_TILE = 1024
MOE_PAD = 64
MOE_WIN = 320
MOE_ROWS = MOE_TILE + N_GROUPS * MOE_PAD + (MOE_WIN - MOE_PAD)
MOE_TILES_PER_STEP = 2
VMEM_LIMIT = 48 * 1024 * 1024
MOE_VMEM_LIMIT = 56 * 1024 * 1024


def _cp(sem):
    return pltpu.CompilerParams(dimension_semantics=sem, vmem_limit_bytes=VMEM_LIMIT)


def _dot(a, b):
    return jnp.dot(a.astype(BF16), b.astype(BF16), preferred_element_type=F32)


def _dot_nt(a, b):
    return lax.dot_general(a.astype(BF16), b.astype(BF16), (((1,), (1,)), ((), ())),
                           preferred_element_type=F32)


def _dot_tn(a, b):
    return lax.dot_general(a.astype(BF16), b.astype(BF16), (((0,), (0,)), ((), ())),
                           preferred_element_type=F32)


def _split(a):
    hi = a.astype(BF16)
    return hi, (a - hi.astype(F32)).astype(BF16)


def _split_bits(w):
    bits = lax.bitcast_convert_type(w, jnp.uint32) & jnp.uint32(0xFFFF0000)
    hi = lax.bitcast_convert_type(bits, F32)
    return hi.astype(BF16), (w - hi).astype(BF16)


def _dg3(a, b, dims):
    ah, al = _split(a)
    bh, bl = _split(b)
    dg = lambda x, y: lax.dot_general(x, y, (dims, ((), ())), preferred_element_type=F32)
    return dg(ah, bh) + (dg(ah, bl) + dg(al, bh))


class _MM:
    def __init__(self, precise):
        if precise:
            self.nn = lambda a, b: _dg3(a, b, ((1,), (0,)))
            self.nt = lambda a, b: _dg3(a, b, ((1,), (1,)))
            self.tn = lambda a, b: _dg3(a, b, ((0,), (0,)))
        else:
            self.nn, self.nt, self.tn = _dot, _dot_nt, _dot_tn


def _pieces(a, n):
    out = []
    for i in range(n):
        p = a.astype(BF16)
        out.append(p)
        if i + 1 < n:
            a = a - p.astype(F32)
    return out


def _dot_pieces(a, b, n, lhs_exact):
    dot = lambda x, y: jnp.dot(x, y, preferred_element_type=F32)
    terms = [dot(a, p) for p in _pieces(b, n)] if lhs_exact else [dot(p, b) for p in _pieces(a, n)]
    acc = terms[-1]
    for t in terms[-2::-1]:
        acc = acc + t
    return acc


def _dot_hi(a, b):
    return jnp.dot(a, b, precision=HIGHEST, preferred_element_type=F32)


def _dot_nt_hi(a, b):
    return lax.dot_general(a, b, (((1,), (1,)), ((), ())), precision=HIGHEST,
                           preferred_element_type=F32)


def _sigmoid(x):
    return 1.0 / (1.0 + jnp.exp(-x))


def _softplus(x):
    return jnp.maximum(x, 0.0) + jnp.log(1.0 + jnp.exp(-jnp.abs(x)))


def _rmsnorm(x, g):
    return x * lax.rsqrt(jnp.mean(x * x, axis=-1, keepdims=True) + RMS_EPS) * g


_IN_SPLITS = ((0, 1024), (1024, 2048), (2048, 3072), (3072, 3328), (3328, 3456))


def _in_kernel(x_ref, g_ref, *refs, precise):
    nw = 2 if precise else 1
    w_refs, o_refs = refs[:nw], refs[nw:]
    h = _rmsnorm(x_ref[...], g_ref[...])
    tm = h.shape[0]
    dot = lambda a, b: jnp.dot(a, b, preferred_element_type=F32)
    if precise:
        hh, hl = _split(h)
        lhs = jnp.concatenate([hh, hl], axis=0)
    else:
        lhs = h.astype(BF16)
    for o_ref, (c0, c1) in zip(o_refs, _IN_SPLITS):
        acc = dot(lhs, w_refs[0][:, c0:c1])
        if precise:
            acc = acc[:tm] + (acc[tm:] + dot(hh, w_refs[1][:, c0:c1]))
        o_ref[...] = acc


def _in_proj(x, g, ws):
    n = x.shape[0]
    tm = 512
    ncol = ws[0].shape[1]
    widths = [c1 - c0 for c0, c1 in _IN_SPLITS]
    return pl.pallas_call(
        functools.partial(_in_kernel, precise=len(ws) == 2),
        grid=(n // tm,),
        in_specs=[pl.BlockSpec((tm, D_MODEL), lambda i: (i, 0)),
                  pl.BlockSpec((1, D_MODEL), lambda i: (0, 0))]
                 + [pl.BlockSpec((D_MODEL, ncol), lambda i: (0, 0), pipeline_mode=pl.Buffered(1)) for _ in ws],
        out_specs=[pl.BlockSpec((tm, wd), lambda i: (i, 0)) for wd in widths],
        out_shape=[jax.ShapeDtypeStruct((n, wd), F32) for wd in widths],
        compiler_params=_cp(("parallel",)),
        name="in_proj",
    )(x, g, *ws)


def _pool_kernel(u_ref, buf_ref, w_ref, sc_ref, y_ref, nb_ref, ext_ref, *, tt, pos0, precise):
    mm = _MM(precise)
    j = pl.program_id(1)

    @pl.when(j == 0)
    def _():
        ext_ref[0:POOL_PAD, :] = buf_ref[...]

    u = u_ref[...]
    ext_ref[POOL_PAD:POOL_PAD + tt, :] = u
    acc = u
    sums = []
    for k in range(1, POOL_PAD):
        acc = acc + ext_ref[POOL_PAD - k:POOL_PAD - k + tt, :]
        if k + 1 in POOL_WINDOWS:
            sums.append(acc)
    lane = lax.broadcasted_iota(jnp.int32, (tt, GROUP_W), 1)
    grp = lane // (GROUP_W // len(POOL_WINDOWS))
    pos = (pos0 + j * tt + lax.broadcasted_iota(jnp.int32, (tt, GROUP_W), 0)).astype(F32)
    mean = jnp.zeros((tt, GROUP_W), F32)
    for gi, w in enumerate(POOL_WINDOWS):
        cnt = jnp.minimum(float(w), pos + 1.0)
        mean = jnp.where(grp == gi, sums[gi] / cnt, mean)
    d = mean - u
    y_ref[...] = (mm.nn(d, w_ref[...]) * sc_ref[...]).astype(y_ref.dtype)
    nb_ref[...] = ext_ref[tt + 1:tt + POOL_PAD, :]
    ext_ref[0:POOL_PAD, :] = ext_ref[tt:tt + POOL_PAD, :]


def _pool_mixer(u, buf16, w_bd, scale, pos0, precise):
    b, t, _ = u.shape
    tt = min(t, 512)
    return pl.pallas_call(
        functools.partial(_pool_kernel, tt=tt, pos0=pos0, precise=precise),
        grid=(b, t // tt),
        in_specs=[pl.BlockSpec((None, tt, GROUP_W), lambda i, j: (i, j, 0)),
                  pl.BlockSpec((None, POOL_PAD, GROUP_W), lambda i, j: (i, 0, 0)),
                  pl.BlockSpec((GROUP_W, GROUP_W), lambda i, j: (0, 0)),
                  pl.BlockSpec((1, GROUP_W), lambda i, j: (0, 0))],
        out_specs=[pl.BlockSpec((None, tt, GROUP_W), lambda i, j: (i, j, 0)),
                   pl.BlockSpec((None, POOL_BUF, GROUP_W), lambda i, j: (i, 0, 0))],
        out_shape=[jax.ShapeDtypeStruct((b, t, GROUP_W), F32 if precise else BF16),
                   jax.ShapeDtypeStruct((b, POOL_BUF, GROUP_W), F32)],
        scratch_shapes=[pltpu.VMEM((POOL_PAD + tt, GROUP_W), F32)],
        compiler_params=_cp(("parallel", "arbitrary")),
        name="pool_mixer",
    )(u, buf16, w_bd, scale)


def _rwkv_kernel(z_ref, sh_ref, s0_ref, mu_ref, w0_ref, w2_ref, a0_ref, a2_ref, g2_ref, kk_ref,
                 ka_ref, rk_ref, lnx_ref, seg_ref, tril_ref,
                 y_ref, so_ref, sho_ref, zext_ref, s_ref, ybuf_ref, pre_ref, *, precise, nb, nc):
    mm = _MM(precise)
    lora = _dot_hi if precise else _dot
    j = pl.program_id(1)
    L = CHUNK
    rows = nc * L
    n = nb * rows

    @pl.when(j == 0)
    def _():
        zext_ref[:, 7:8, :] = sh_ref[...]
        s_ref[...] = s0_ref[...]

    z3 = z_ref[...]
    zext_ref[:, 8:8 + rows, :] = z3
    prev3 = zext_ref[:, 7:7 + rows, :]
    zs = (z3 + (prev3 - z3) * mu_ref[...]).reshape(n, RWKV_COLS)
    c = GROUP_W
    r, k, v = zs[:, 0:c], zs[:, c:2 * c], zs[:, 2 * c:3 * c]
    wl, al, gl = zs[:, 768:832], zs[:, 832:896], zs[:, 896:1024]
    w = -_softplus(-(w0_ref[...] + lora(jnp.tanh(wl), w2_ref[...]))) - 0.5
    lw = -jnp.exp(w)
    a = _sigmoid(a0_ref[...] + lora(al, a2_ref[...]))
    g = lora(_sigmoid(gl), g2_ref[...])
    seg = seg_ref[...]
    segsum = lambda x: _dot_pieces(x, seg, 2 if precise else 1, lhs_exact=False)
    kk = k * kk_ref[...]
    kk = kk / jnp.maximum(jnp.sqrt(segsum(kk * kk) * float(HEAD_DIM)), 1e-12)
    k2 = k * (1.0 + (a - 1.0) * ka_ref[...])
    bonus = segsum(r * k2 * rk_ref[...]) * float(HEAD_DIM) * v
    b = kk * a

    ri = lax.broadcasted_iota(jnp.int32, (L, L), 0)
    ci = lax.broadcasted_iota(jnp.int32, (L, L), 1)
    strict, incl, eye = ri > ci, ri >= ci, ri == ci
    blk = (ri // SUB) == (ci // SUB)
    eye_f = jnp.where(eye, 1.0, 0.0)
    tril = tril_ref[...]

    nch = nb * nc
    chains = [(ch, h) for ch in range(nch) for h in range(N_HEADS)]
    each = lambda f, *lists: [f(*vals) for vals in zip(*lists)]
    at, rt, bt, kt, bh, kh, dend, vv = [], [], [], [], [], [], [], []
    for ch in range(nch):
        rs = slice(ch * L, (ch + 1) * L)
        lwc = lw[rs]
        cum = _dot_pieces(tril, lwc, 3 if precise else 2, lhs_exact=True)
        pend = cum[L - 1:L, :]
        e_neg = jnp.exp(-cum)
        e_end = jnp.exp(pend - cum)
        at_c, rt_c = -kk[rs] * jnp.exp(cum - lwc), r[rs] * jnp.exp(cum)
        bt_c, kt_c, bh_c, kh_c = b[rs] * e_neg, k2[rs] * e_neg, b[rs] * e_end, k2[rs] * e_end
        dend_c, v_c = jnp.exp(pend), v[rs]
        for h in range(N_HEADS):
            hs = slice(h * HEAD_DIM, (h + 1) * HEAD_DIM)
            for lst, val in ((at, at_c), (rt, rt_c), (bt, bt_c), (kt, kt_c), (bh, bh_c), (kh, kh_c),
                             (dend, dend_c), (vv, v_c)):
                lst.append(val[:, hs])
    gm = each(lambda a_, r_, b_, k_: mm.nt(jnp.concatenate([a_, r_], axis=0),
                                           jnp.concatenate([b_, k_], axis=0)), at, rt, bt, kt)
    gkv = each(lambda g_, v_: mm.nn(jnp.concatenate([jnp.where(strict, g_[:L, L:], 0.0),
                                                     jnp.where(incl, g_[L:, L:], 0.0)], axis=0), v_),
               gm, vv)
    a_ab = each(lambda g_: jnp.where(strict, g_[:L, :L], 0.0), gm)
    g_b = each(lambda g_: jnp.where(incl, g_[L:, :L], 0.0), gm)
    dp = each(lambda a_: jnp.where(blk, a_, 0.0), a_ab)
    aoff = each(lambda a_: jnp.where(blk, 0.0, a_), a_ab)
    md = each(lambda d_: eye_f + d_, dp)
    for _ in range(3):
        dp = each(lambda d_: mm.nn(d_, d_), dp)
        md = each(lambda m_, d_: m_ + mm.nn(m_, d_), md, dp)
    nm = each(mm.nn, md, aoff)
    nm2 = each(lambda n_: mm.nn(n_, n_), nm)
    x = each(lambda m_, a_, g_: mm.nn(m_, jnp.concatenate([a_, g_[:L]], axis=1)), md, at, gkv)
    x = each(lambda n_, x_: x_ + mm.nn(n_, x_), nm2, x)
    x = each(lambda n_, x_: x_ + mm.nn(n_, x_), nm, x)
    gx = each(mm.nn, g_b, x)
    xtb = each(mm.tn, x, bh)
    vtk = each(mm.tn, vv, kh)
    for i, (ch, h) in enumerate(chains):
        pre_ref[ch, h, 0] = rt[i] + gx[i][:, :HEAD_DIM]
        pre_ref[ch, h, 1] = gx[i][:, HEAD_DIM:] + gkv[i][L:]
        pre_ref[ch, h, 2] = jnp.where(eye, jnp.broadcast_to(dend[i], (L, L)), 0.0) + xtb[i][:HEAD_DIM]
        pre_ref[ch, h, 3] = xtb[i][HEAD_DIM:] + vtk[i]

    st_nt = _dot_nt_hi if precise else (lambda a_, b_: _dg3(a_, b_, ((1,), (1,))))
    st_nn = _dot_hi if precise else (lambda a_, b_: _dg3(a_, b_, ((1,), (0,))))
    seqs = [(bi, h) for bi in range(nb) for h in range(N_HEADS)]
    s = [s_ref[bi, h] for bi, h in seqs]
    for ci_ in range(nc):
        for i, (bi, h) in enumerate(seqs):
            ch = bi * nc + ci_
            ybuf_ref[ch * L:(ch + 1) * L, h * HEAD_DIM:(h + 1) * HEAD_DIM] = (
                st_nt(pre_ref[ch, h, 0], s[i]) + pre_ref[ch, h, 1])
        s = [st_nn(s[i], pre_ref[bi * nc + ci_, h, 2]) + pre_ref[bi * nc + ci_, h, 3]
             for i, (bi, h) in enumerate(seqs)]
    for i, (bi, h) in enumerate(seqs):
        s_ref[bi, h] = s[i]

    yb = ybuf_ref[...]
    mu = segsum(yb)
    yc = yb - mu
    y = yc * lax.rsqrt(segsum(yc * yc) + RWKV_GN_EPS) * lnx_ref[...]
    y_ref[...] = ((y + bonus) * g).astype(y_ref.dtype).reshape(nb, rows, GROUP_W)
    so_ref[...] = s_ref[...]
    sho_ref[...] = z3[:, rows - 1:rows, :]
    zext_ref[:, 7:8, :] = z3[:, rows - 1:rows, :]


def _rwkv_mixer(z, shift, s0, prm, seg, tril, precise):
    b, t, _ = z.shape
    L = CHUNK
    nc = min(t // L, 4)
    nb = 1 if nc > 1 else min(b, 4)
    rows = nc * L
    full = lambda a: pl.BlockSpec(a.shape, lambda i, j: (0,) * a.ndim)
    st_spec = pl.BlockSpec((nb, N_HEADS, HEAD_DIM, HEAD_DIM), lambda i, j: (i, 0, 0, 0))
    sh_spec = pl.BlockSpec((nb, 1, RWKV_COLS), lambda i, j: (i, 0, 0))
    consts = list(prm) + [seg.astype(BF16), tril.astype(BF16)]
    return pl.pallas_call(
        functools.partial(_rwkv_kernel, precise=precise, nb=nb, nc=nc),
        grid=(b // nb, t // rows),
        in_specs=[pl.BlockSpec((nb, rows, RWKV_COLS), lambda i, j: (i, j, 0)), sh_spec, st_spec]
                 + [full(a) for a in consts],
        out_specs=[pl.BlockSpec((nb, rows, GROUP_W), lambda i, j: (i, j, 0)), st_spec, sh_spec],
        out_shape=[jax.ShapeDtypeStruct((b, t, GROUP_W), F32 if precise else BF16),
                   jax.ShapeDtypeStruct((b, N_HEADS, HEAD_DIM, HEAD_DIM), F32),
                   jax.ShapeDtypeStruct((b, 1, RWKV_COLS), F32)],
        scratch_shapes=[pltpu.VMEM((nb, 8 + rows, RWKV_COLS), F32),
                        pltpu.VMEM((nb, N_HEADS, HEAD_DIM, HEAD_DIM), F32),
                        pltpu.VMEM((nb * rows, GROUP_W), F32),
                        pltpu.VMEM((nb * nc, N_HEADS, 4, HEAD_DIM, HEAD_DIM), F32)],
        compiler_params=_cp(("parallel", "arbitrary")),
        name="rwkv_mixer",
    )(z, shift, s0, *consts)


def _ret_kernel(z_ref, cos_ref, sin_ref, dm_ref, kdec_ref, qdec_ref, cdec_ref, gn_ref, seg_ref, s0_ref,
                y_ref, so_ref, s_ref, obuf_ref, *, precise, nb, nc):
    mm = _MM(precise)
    j = pl.program_id(1)
    L = CHUNK
    rows = nc * L

    @pl.when(j == 0)
    def _():
        s_ref[...] = s0_ref[...]

    c = GROUP_W
    lane = lax.broadcasted_iota(jnp.int32, (rows, c), 1)
    first_half = (lane % HEAD_DIM) < (HEAD_DIM // 2)
    cs, sn = cos_ref[...], sin_ref[...]
    kdec, qdec = kdec_ref[...], qdec_ref[...]

    def rope(x):
        partner = jnp.where(first_half, pltpu.roll(x, c - HEAD_DIM // 2, 1), pltpu.roll(x, HEAD_DIM // 2, 1))
        return x * cs + partner * sn

    each = lambda f, *lists: [f(*vals) for vals in zip(*lists)]
    keys = [(bi, ci, h) for bi in range(nb) for ci in range(nc) for h in range(N_HEADS)]
    qs, ks, vs, kds, qds, g_all = [], [], [], [], [], []
    for bi in range(nb):
        q = rope(z_ref[bi, :, 0:c])
        k = rope(z_ref[bi, :, c:2 * c]) * HEAD_DIM ** -0.5
        v = z_ref[bi, :, 2 * c:3 * c]
        g_all.append(z_ref[bi, :, 3 * c:4 * c])
        for ci in range(nc):
            rs = slice(ci * L, (ci + 1) * L)
            kd, qd = k[rs] * kdec, q[rs] * qdec
            for h in range(N_HEADS):
                hs = slice(h * HEAD_DIM, (h + 1) * HEAD_DIM)
                for lst, val in ((qs, q[rs]), (ks, k[rs]), (vs, v[rs]), (kds, kd), (qds, qd)):
                    lst.append(val[:, hs])
    att = [mm.nt(q_, k_) * dm_ref[h] for q_, k_, (_, _, h) in zip(qs, ks, keys)]
    o_in = each(mm.nn, att, vs)
    kv = each(mm.tn, kds, vs)
    s_prev = {}
    for bi in range(nb):
        for h in range(N_HEADS):
            s = s_ref[bi, h]
            for ci in range(nc):
                s_prev[(bi, ci, h)] = s
                s = s * cdec_ref[h] + kv[keys.index((bi, ci, h))]
            s_ref[bi, h] = s
    o_x = [mm.nn(qd_, s_prev[key]) for qd_, key in zip(qds, keys)]
    for i, (bi, ci, h) in enumerate(keys):
        r0 = bi * rows + ci * L
        obuf_ref[r0:r0 + L, h * HEAD_DIM:(h + 1) * HEAD_DIM] = o_in[i] + o_x[i]
    seg = seg_ref[...]
    segsum = lambda x: _dot_pieces(x, seg, 2 if precise else 1, lhs_exact=False)
    ob = obuf_ref[...]
    oc = ob - segsum(ob)
    o = oc * lax.rsqrt(segsum(oc * oc) + RET_GN_EPS) * gn_ref[...]
    g = jnp.concatenate(g_all, axis=0) if nb > 1 else g_all[0]
    y_ref[...] = (g * _sigmoid(g) * o).astype(y_ref.dtype).reshape(nb, rows, c)
    so_ref[...] = s_ref[...]


def _ret_mixer(z, s0, cos, sin, dmask, kdec, qdec, cdec, gn, seg, precise):
    b, t, _ = z.shape
    L = CHUNK
    nc = min(t // L, 4)
    nb = 1 if nc > 1 else min(b, 4)
    rows = nc * L
    full = lambda a: pl.BlockSpec(a.shape, lambda i, j: (0,) * a.ndim)
    st_spec = pl.BlockSpec((nb, N_HEADS, HEAD_DIM, HEAD_DIM), lambda i, j: (i, 0, 0, 0))
    tab_spec = pl.BlockSpec((rows, GROUP_W), lambda i, j: (j, 0))
    seg = seg.astype(BF16)
    return pl.pallas_call(
        functools.partial(_ret_kernel, precise=precise, nb=nb, nc=nc),
        grid=(b // nb, t // rows),
        in_specs=[pl.BlockSpec((nb, rows, 4 * GROUP_W), lambda i, j: (i, j, 0)), tab_spec, tab_spec,
                  full(dmask), full(kdec), full(qdec), full(cdec), full(gn), full(seg), st_spec],
        out_specs=[pl.BlockSpec((nb, rows, GROUP_W), lambda i, j: (i, j, 0)), st_spec],
        out_shape=[jax.ShapeDtypeStruct((b, t, GROUP_W), F32 if precise else BF16),
                   jax.ShapeDtypeStruct((b, N_HEADS, HEAD_DIM, HEAD_DIM), F32)],
        scratch_shapes=[pltpu.VMEM((nb, N_HEADS, HEAD_DIM, HEAD_DIM), F32),
                        pltpu.VMEM((nb * rows, GROUP_W), F32)],
        compiler_params=_cp(("parallel", "arbitrary")),
        name="ret_mixer",
    )(z, cos, sin, dmask, kdec, qdec, cdec, gn, seg, s0)


def _cumsum_lanes(x_ref, o_ref, tri, carry, width):
    blk = 128
    for c0 in range(0, width, blk):
        wd = min(blk, width - c0)
        cs = _dot_hi(x_ref[:, c0:c0 + wd], tri[:wd, :wd]) + carry
        o_ref[:, c0:c0 + wd] = cs
        carry = cs[:, wd - 1:wd]
    return carry


def _fox_gate_kernel(*refs, past, t):
    if past:
        zf_ref, b_ref, tri_ref, lfp_ref, lf_ref, cn_ref, cp_ref = refs
    else:
        zf_ref, b_ref, tri_ref, lf_ref, cn_ref = refs
    tri = tri_ref[...]
    x = zf_ref[...] + b_ref[...]
    lf_ref[...] = jnp.minimum(x, 0.0) - jnp.log(1.0 + jnp.exp(-jnp.abs(x)))
    carry = jnp.zeros((zf_ref.shape[0], 1), F32)
    if past:
        carry = _cumsum_lanes(lfp_ref, cp_ref, tri, carry, past)
    _cumsum_lanes(lf_ref, cn_ref, tri, carry, t)


def _fox_gate(zf_t, bias_rows, tri, lf_past_t):
    rows, t = zf_t.shape
    rt = min(rows, 32)
    past = 0 if lf_past_t is None else lf_past_t.shape[1]
    row_spec = lambda wd: pl.BlockSpec((rt, wd), lambda i: (i, 0))
    in_specs = [row_spec(t), row_spec(1), pl.BlockSpec(tri.shape, lambda i: (0, 0))]
    out_specs = [row_spec(t), row_spec(t)]
    out_shape = [jax.ShapeDtypeStruct((rows, t), F32), jax.ShapeDtypeStruct((rows, t), F32)]
    args = [zf_t, bias_rows, tri]
    if past:
        in_specs.append(row_spec(past))
        out_specs.append(row_spec(past))
        out_shape.append(jax.ShapeDtypeStruct((rows, past), F32))
        args.append(lf_past_t)
    return pl.pallas_call(
        functools.partial(_fox_gate_kernel, past=past, t=t),
        grid=(rows // rt,),
        in_specs=in_specs, out_specs=out_specs, out_shape=out_shape,
        compiler_params=_cp(("parallel",)),
        name="fox_gate",
    )(*args)


def _fox_attn_kernel(q_ref, kn_ref, vn_ref, og_ref, ckn_ref, y_ref, m_ref, l_ref, acc_ref, *,
                     tq, precise):
    mm = _MM(precise)
    i, j = pl.program_id(1), pl.program_id(2)
    nk = pl.num_programs(2)

    @pl.when(j == 0)
    def _():
        m_ref[...] = jnp.full(m_ref.shape, -jnp.inf, F32)
        l_ref[...] = jnp.zeros(l_ref.shape, F32)
        acc_ref[...] = jnp.zeros(acc_ref.shape, F32)

    heads = range(N_HEADS)

    def per_head_lanes(cols):
        lane_head = lax.broadcasted_iota(jnp.int32, (tq, GROUP_W), 1) // HEAD_DIM
        out = jnp.broadcast_to(cols[N_HEADS - 1], (tq, GROUP_W))
        for h in range(N_HEADS - 2, -1, -1):
            out = jnp.where(lane_head == h, cols[h], out)
        return out

    def block(k_ref, v_ref, ck_ref, diagonal):
        q = q_ref[...] * (HEAD_DIM ** -0.5 * LOG2E)
        k = k_ref[...]
        v = v_ref[...]
        tk = k.shape[0]
        ck = ck_ref[...] * LOG2E
        lane_head = lax.broadcasted_iota(jnp.int32, (tk, GROUP_W), 1) // HEAD_DIM
        kbd = jnp.concatenate([jnp.where(lane_head == h, k, 0.0) for h in heads], axis=0)
        vbd = jnp.concatenate([jnp.where(lane_head == h, v, 0.0) for h in heads], axis=0)
        s_all = mm.nt(q, kbd)
        s = [s_all[:, h * tk:(h + 1) * tk] - ck[h:h + 1, :] for h in heads]
        if diagonal:
            keep = (lax.broadcasted_iota(jnp.int32, (tq, tk), 1)
                    <= lax.broadcasted_iota(jnp.int32, (tq, tk), 0))
            s = [jnp.where(keep, s_h, -jnp.inf) for s_h in s]
        m_prev = [m_ref[h] for h in heads]
        m_new = [jnp.maximum(m_prev[h], jnp.max(s[h], axis=1, keepdims=True)) for h in heads]
        alpha = [jnp.exp2(m_prev[h] - m_new[h]) for h in heads]
        p = [jnp.exp2(s[h] - m_new[h]) for h in heads]
        pv = mm.nn(jnp.concatenate(p, axis=1), vbd)
        for h in heads:
            l_ref[h] = alpha[h] * l_ref[h] + jnp.sum(p[h], axis=1, keepdims=True)
            m_ref[h] = m_new[h]
        acc_ref[...] = per_head_lanes(alpha) * acc_ref[...] + pv

    @pl.when(j < i)
    def _():
        block(kn_ref, vn_ref, ckn_ref, False)

    @pl.when(j == i)
    def _():
        block(kn_ref, vn_ref, ckn_ref, True)

    @pl.when(j == nk - 1)
    def _():
        o = acc_ref[...] / per_head_lanes([l_ref[h] for h in heads])
        y_ref[...] = (_sigmoid(og_ref[...]) * o).astype(y_ref.dtype)


def _fox_attn(zfox, ck_new, precise):
    b, t, _ = zfox.shape
    tq = min(t, 512)
    nq = t // tq
    jc = lambda i, j: jnp.minimum(j, i)
    in_specs = [pl.BlockSpec((None, tq, GROUP_W), lambda bb, i, j: (bb, i, 0)),
                pl.BlockSpec((None, tq, GROUP_W), lambda bb, i, j: (bb, jc(i, j), 1)),
                pl.BlockSpec((None, tq, GROUP_W), lambda bb, i, j: (bb, jc(i, j), 2)),
                pl.BlockSpec((None, tq, GROUP_W), lambda bb, i, j: (bb, i, 3)),
                pl.BlockSpec((None, N_HEADS, tq), lambda bb, i, j: (bb, 0, jc(i, j)))]
    return pl.pallas_call(
        functools.partial(_fox_attn_kernel, tq=tq, precise=precise),
        grid=(b, nq, nq),
        in_specs=in_specs,
        out_specs=pl.BlockSpec((None, tq, GROUP_W), lambda bb, i, j: (bb, i, 0)),
        out_shape=jax.ShapeDtypeStruct((b, t, GROUP_W), F32 if precise else BF16),
        scratch_shapes=[pltpu.VMEM((N_HEADS, tq, 1), F32), pltpu.VMEM((N_HEADS, tq, 1), F32),
                        pltpu.VMEM((tq, GROUP_W), F32)],
        compiler_params=_cp(("parallel", "parallel", "arbitrary")),
        name="fox_attn",
    )(zfox, zfox, zfox, zfox, ck_new)


def _fox_past_kernel(zf_ref, ckn_ref, kt_ref, vt_ref, ckp_ref, y_ref, m_ref, l_ref, acc_ref, *,
                     nkp, nb, precise):
    mm = _MM(precise)
    j = pl.program_id(1)
    t = zf_ref.shape[1]
    c = GROUP_W
    chains = [(bi, h) for bi in range(nb) for h in range(N_HEADS)]
    hs = lambda h: slice(h * HEAD_DIM, (h + 1) * HEAD_DIM)

    @pl.when(j == 0)
    def _():
        m_ref[...] = jnp.full(m_ref.shape, -jnp.inf, F32)
        l_ref[...] = jnp.zeros(l_ref.shape, F32)
        acc_ref[...] = jnp.zeros(acc_ref.shape, F32)

    def online_update(s, pv_of):
        m_prev = [m_ref[i] for i in range(len(chains))]
        m_new = [jnp.maximum(mp, jnp.max(s_i, axis=1, keepdims=True)) for mp, s_i in zip(m_prev, s)]
        alpha = [jnp.exp2(mp - mn) for mp, mn in zip(m_prev, m_new)]
        p = [jnp.exp2(s_i - mn) for s_i, mn in zip(s, m_new)]
        pv = pv_of(p)
        for i in range(len(chains)):
            l_ref[i] = alpha[i] * l_ref[i] + jnp.sum(p[i], axis=1, keepdims=True)
            acc_ref[i] = alpha[i] * acc_ref[i] + pv[i]
            m_ref[i] = m_new[i]

    def queries():
        return [zf_ref[bi, :, hs(h)] * (HEAD_DIM ** -0.5 * LOG2E) for bi, h in chains]

    @pl.when(j < nkp)
    def _():
        q = queries()
        s = [mm.nn(q[i], kt_ref[bi, h]) - ckp_ref[bi, h:h + 1, :] * LOG2E
             for i, (bi, h) in enumerate(chains)]
        online_update(s, lambda p: [mm.nt(p[i], vt_ref[bi, h]) for i, (bi, h) in enumerate(chains)])

    @pl.when(j == nkp)
    def _():
        q = queries()
        keep = lax.broadcasted_iota(jnp.int32, (t, t), 1) <= lax.broadcasted_iota(jnp.int32, (t, t), 0)
        s = [jnp.where(keep, mm.nt(q[i], zf_ref[bi, :, c + h * HEAD_DIM:c + (h + 1) * HEAD_DIM])
                       - ckn_ref[bi, h:h + 1, :] * LOG2E, -jnp.inf)
             for i, (bi, h) in enumerate(chains)]
        online_update(s, lambda p: [mm.nn(p[i], zf_ref[bi, :, 2 * c + h * HEAD_DIM:2 * c + (h + 1) * HEAD_DIM])
                                    for i, (bi, h) in enumerate(chains)])
        for i, (bi, h) in enumerate(chains):
            og = zf_ref[bi, :, 3 * c + h * HEAD_DIM:3 * c + (h + 1) * HEAD_DIM]
            y_ref[bi, :, hs(h)] = (_sigmoid(og) * (acc_ref[i] / l_ref[i])).astype(y_ref.dtype)


def _fox_attn_past(zfox, ck_new, kt_all, vt_all, ck_past, layer, precise):
    b, t, _ = zfox.shape
    p = kt_all.shape[-1]
    tkp = 512
    nkp = p // tkp
    nb = min(b, 4)
    jp = lambda j: jnp.minimum(j, nkp - 1)
    cache_spec = pl.BlockSpec((None, nb, N_HEADS, HEAD_DIM, tkp), lambda i, j: (layer, i, 0, 0, jp(j)))
    return pl.pallas_call(
        functools.partial(_fox_past_kernel, nkp=nkp, nb=nb, precise=precise),
        grid=(b // nb, nkp + 1),
        in_specs=[pl.BlockSpec((nb, t, 4 * GROUP_W), lambda i, j: (i, 0, 0)),
                  pl.BlockSpec((nb, N_HEADS, t), lambda i, j: (i, 0, 0)),
                  cache_spec, cache_spec,
                  pl.BlockSpec((nb, N_HEADS, tkp), lambda i, j: (i, 0, jp(j)))],
        out_specs=pl.BlockSpec((nb, t, GROUP_W), lambda i, j: (i, 0, 0)),
        out_shape=jax.ShapeDtypeStruct((b, t, GROUP_W), F32 if precise else BF16),
        scratch_shapes=[pltpu.VMEM((nb * N_HEADS, t, 1), F32), pltpu.VMEM((nb * N_HEADS, t, 1), F32),
                        pltpu.VMEM((nb * N_HEADS, t, HEAD_DIM), F32)],
        compiler_params=_cp(("parallel", "arbitrary")),
        name="fox_attn_past",
    )(zfox, ck_new, kt_all, vt_all, ck_past)


def _out_kernel(x_ref, ya_ref, yb_ref, yc_ref, yd_ref, *refs, precise):
    w_refs, o_ref = refs[:-1], refs[-1]
    dot = lambda a, b: jnp.dot(a, b, preferred_element_type=F32)
    acc = x_ref[...]
    for i, y_ref in enumerate((ya_ref, yb_ref, yc_ref, yd_ref)):
        if precise:
            yh, yl = _split(y_ref[...])
            acc = acc + (dot(yh, w_refs[0][i]) + (dot(yh, w_refs[1][i]) + dot(yl, w_refs[0][i])))
        else:
            acc = acc + dot(y_ref[...], w_refs[0][i])
    o_ref[...] = acc


def _out_proj(x, ys, ws):
    n = x.shape[0]
    tm = 1024
    y_spec = pl.BlockSpec((tm, GROUP_W), lambda i: (i, 0))
    x_spec = pl.BlockSpec((tm, D_MODEL), lambda i: (i, 0))
    return pl.pallas_call(
        functools.partial(_out_kernel, precise=len(ws) == 2),
        grid=(n // tm,),
        in_specs=[x_spec, y_spec, y_spec, y_spec, y_spec]
                 + [pl.BlockSpec(w.shape, lambda i: (0, 0, 0)) for w in ws],
        out_specs=x_spec,
        out_shape=jax.ShapeDtypeStruct((n, D_MODEL), F32),
        compiler_params=_cp(("parallel",)),
        name="out_proj",
    )(x, *ys, *ws)


def _router_kernel(x_ref, g_ref, w_ref, b_ref, tril_ref, xs_ref, gs_ref, dest_ref, meta_ref):
    h = _rmsnorm(x_ref[...], g_ref[...])
    logits = _dot_hi(h, w_ref[...]) + b_ref[...]
    tm = logits.shape[0]
    lane_i = lax.broadcasted_iota(jnp.int32, (tm, ROUTER_PAD), 1)
    lane = lane_i.astype(F32)
    big = float(ROUTER_PAD)
    rmax = lambda a: jnp.max(a, axis=1, keepdims=True)
    rsum = lambda a: jnp.sum(a, axis=1, keepdims=True)
    first = lambda m: jnp.min(jnp.where(m, lane, big), axis=1, keepdims=True)

    is_c = jnp.logical_and(lane_i >= N_EXPERTS, lane_i < N_EXPERTS + N_GROUPS)
    lc = jnp.where(is_c, logits, -jnp.inf)
    ec = jnp.exp(lc - rmax(lc))
    pc_all = ec / rsum(ec)
    pc = rmax(pc_all)
    gi = first(jnp.logical_and(is_c, pc_all == pc)) - float(N_EXPERTS)

    is_f = (lane_i // EXPERTS_PER_GROUP).astype(F32) == gi
    lf = jnp.where(is_f, logits, -jnp.inf)
    ef = jnp.exp(lf - rmax(lf))
    pf = jnp.where(is_f, ef / rsum(ef), -1.0)
    t1 = rmax(pf)
    i1 = first(pf == t1)
    pf2 = jnp.where(lane == i1, -1.0, pf)
    t2 = rmax(pf2)
    i2 = first(pf2 == t2)
    den = t1 + t2
    gate = jnp.where(lane == i1, pc * (t1 / den), jnp.where(lane == i2, pc * (t2 / den), 0.0))

    onehot = jnp.where(lane == gi, 1.0, 0.0)
    rank = jnp.dot(tril_ref[...], onehot.astype(BF16), preferred_element_type=F32)
    cnt = jnp.sum(onehot, axis=0, keepdims=True)
    padded = jnp.floor((cnt + (MOE_PAD - 1.0)) * (1.0 / MOE_PAD)) * MOE_PAD
    lane1 = lax.broadcasted_iota(jnp.int32, (1, ROUTER_PAD), 1)
    off = jnp.zeros((1, ROUTER_PAD), F32)
    meta = jnp.zeros((1, ROUTER_PAD), F32)
    start = jnp.zeros((1, 1), F32)
    for g in range(N_GROUPS):
        size_g = jnp.sum(jnp.where(lane1 == g, padded, 0.0), axis=1, keepdims=True)
        off = off + jnp.where(lane1 == g, start, 0.0)
        meta = meta + jnp.where(lane1 == g, start, 0.0) + jnp.where(lane1 == N_GROUPS + g, size_g, 0.0)
        start = start + size_g
    dest = rsum(onehot * (off + rank))
    rows = lax.broadcasted_iota(jnp.int32, (tm, MOE_ROWS), 1).astype(F32)
    perm_t = jnp.where(rows == dest, 1.0, 0.0).astype(BF16)
    rhs = jnp.concatenate([h.astype(BF16)] + _pieces(gate, 3), axis=1)
    srt = lax.dot_general(perm_t, rhs, (((0,), (0,)), ((), ())), preferred_element_type=F32)
    xs_ref[...] = srt[:, :D_MODEL].astype(BF16)
    gs_ref[...] = (srt[:, D_MODEL:D_MODEL + ROUTER_PAD] + srt[:, D_MODEL + ROUTER_PAD:D_MODEL + 2 * ROUTER_PAD]
                   + srt[:, D_MODEL + 2 * ROUTER_PAD:])
    dest_ref[...] = dest
    meta_ref[...] = meta


def _router(x, g, wr, br, tril):
    n = x.shape[0]
    tm = MOE_TILE
    nt = n // tm
    return pl.pallas_call(
        _router_kernel,
        grid=(nt,),
        in_specs=[pl.BlockSpec((tm, D_MODEL), lambda i: (i, 0)),
                  pl.BlockSpec((1, D_MODEL), lambda i: (0, 0)),
                  pl.BlockSpec((D_MODEL, ROUTER_PAD), lambda i: (0, 0)),
                  pl.BlockSpec((1, ROUTER_PAD), lambda i: (0, 0)),
                  pl.BlockSpec((tm, tm), lambda i: (0, 0))],
        out_specs=[pl.BlockSpec((MOE_ROWS, D_MODEL), lambda i: (i, 0)),
                   pl.BlockSpec((MOE_ROWS, ROUTER_PAD), lambda i: (i, 0)),
                   pl.BlockSpec((tm, 1), lambda i: (i, 0)),
                   pl.BlockSpec((None, 1, ROUTER_PAD), lambda i: (i, 0, 0))],
        out_shape=[jax.ShapeDtypeStruct((nt * MOE_ROWS, D_MODEL), BF16),
                   jax.ShapeDtypeStruct((nt * MOE_ROWS, ROUTER_PAD), F32),
                   jax.ShapeDtypeStruct((n, 1), F32),
                   jax.ShapeDtypeStruct((nt, 1, ROUTER_PAD), F32)],
        compiler_params=_cp(("parallel",)),
        name="moe_router",
    )(x, g, wr, br, tril)


def _moe_kernel(meta_ref, xs_ref, gs_ref, w1_ref, w3_ref, w2_ref, o_ref, *, tps):
    i, e = pl.program_id(0), pl.program_id(1)

    @pl.when(e == 0)
    def _():
        o_ref[...] = jnp.zeros(o_ref.shape, F32)

    g = e // EXPERTS_PER_GROUP
    for sub in range(tps):
        tile = i * tps + sub
        off = meta_ref[tile, g] + sub * MOE_ROWS
        size = meta_ref[tile, N_GROUPS + g]
        for w in range(MOE_TILE // MOE_WIN + 1):
            @pl.when(w * MOE_WIN < size)
            def _():
                rows = pl.ds(pl.multiple_of(off + w * MOE_WIN, MOE_PAD), MOE_WIN)
                xw = xs_ref[rows, :]
                a = jnp.dot(xw, w1_ref[...], preferred_element_type=F32)
                b = jnp.dot(xw, w3_ref[...], preferred_element_type=F32)
                he = (a * _sigmoid(a) * b).astype(BF16)
                ye = jnp.dot(he, w2_ref[...], preferred_element_type=F32)
                gw = gs_ref[rows, :]
                lane = lax.broadcasted_iota(jnp.int32, gw.shape, 1)
                gcol = jnp.sum(jnp.where(lane == e, gw, 0.0), axis=1, keepdims=True)
                o_ref[rows, :] += gcol * ye


def _moe(meta, xs, gs, w1, w3, w2, layer):
    nt = meta.shape[0]
    tps = MOE_TILES_PER_STEP if nt % MOE_TILES_PER_STEP == 0 else 1
    rows = tps * MOE_ROWS
    return pl.pallas_call(
        functools.partial(_moe_kernel, tps=tps),
        grid_spec=pltpu.PrefetchScalarGridSpec(
            num_scalar_prefetch=1,
            grid=(nt // tps, N_EXPERTS),
            in_specs=[pl.BlockSpec((rows, D_MODEL), lambda i, e, m: (i, 0)),
                      pl.BlockSpec((rows, ROUTER_PAD), lambda i, e, m: (i, 0)),
                      pl.BlockSpec((None, None, D_MODEL, D_EXPERT), lambda i, e, m: (layer, e, 0, 0)),
                      pl.BlockSpec((None, None, D_MODEL, D_EXPERT), lambda i, e, m: (layer, e, 0, 0)),
                      pl.BlockSpec((None, None, D_EXPERT, D_MODEL), lambda i, e, m: (layer, e, 0, 0))],
            out_specs=pl.BlockSpec((rows, D_MODEL), lambda i, e, m: (i, 0))),
        out_shape=jax.ShapeDtypeStruct((nt * MOE_ROWS, D_MODEL), F32),
        compiler_params=pltpu.CompilerParams(dimension_semantics=("parallel", "arbitrary"),
                                             vmem_limit_bytes=MOE_VMEM_LIMIT),
        name="moe_experts",
    )(meta, xs, gs, w1, w3, w2)


def _combine_kernel(x_ref, os_ref, dest_ref, nf_ref, o_ref, *, final):
    tm = x_ref.shape[0]
    rows = lax.broadcasted_iota(jnp.int32, (tm, MOE_ROWS), 1).astype(F32)
    perm_t = jnp.where(rows == dest_ref[...], 1.0, 0.0).astype(BF16)
    y = x_ref[...] + _dot_pieces(perm_t, os_ref[...], 2, lhs_exact=True)
    o_ref[...] = _rmsnorm(y, nf_ref[...]) if final else y


def _moe_combine(x, os, dest, nf, final):
    n = x.shape[0]
    tm = MOE_TILE
    return pl.pallas_call(
        functools.partial(_combine_kernel, final=final),
        grid=(n // tm,),
        in_specs=[pl.BlockSpec((tm, D_MODEL), lambda i: (i, 0)),
                  pl.BlockSpec((MOE_ROWS, D_MODEL), lambda i: (i, 0)),
                  pl.BlockSpec((tm, 1), lambda i: (i, 0)),
                  pl.BlockSpec((1, D_MODEL), lambda i: (0, 0))],
        out_specs=pl.BlockSpec((tm, D_MODEL), lambda i: (i, 0)),
        out_shape=jax.ShapeDtypeStruct((n, D_MODEL), F32),
        compiler_params=_cp(("parallel",)),
        name="moe_combine",
    )(x, os, dest, nf)


def _prep_layer(l, p):
    w_in = p["w_in"][l]
    o_rwkv, o_ret, o_fox = GROUP_W, GROUP_W + RWKV_COLS, GROUP_W + RWKV_COLS + 4 * GROUP_W
    ff = jnp.pad(w_in[:, o_fox + 4 * GROUP_W:], ((0, 0), (0, FF_PAD - N_HEADS)))
    w_in_r = jnp.concatenate([w_in[:, o_rwkv:o_ret], w_in[:, o_ret:o_fox],
                              w_in[:, o_fox:o_fox + 4 * GROUP_W], w_in[:, :GROUP_W], ff], axis=1)
    pw = p["pool_w"][l]
    pg = GROUP_W // len(POOL_WINDOWS)
    w_bd = jnp.zeros((GROUP_W, GROUP_W), F32)
    for gi in range(len(POOL_WINDOWS)):
        w_bd = w_bd.at[gi * pg:(gi + 1) * pg, gi * pg:(gi + 1) * pg].set(pw[gi])
    row = lambda a: a.reshape(1, -1)
    rwkv = (row(p["rwkv_mu"][l]), row(p["rwkv_w0"][l]), p["rwkv_w2"][l], row(p["rwkv_a0"][l]),
            p["rwkv_a2"][l], p["rwkv_g2"][l], row(p["rwkv_kk"][l]), row(p["rwkv_ka"][l]),
            row(p["rwkv_rk"][l]), row(p["rwkv_lnx"][l]))
    wf = jnp.transpose(p["moe_wf"][l], (1, 0, 2)).reshape(D_MODEL, N_EXPERTS)
    wr = jnp.pad(jnp.concatenate([wf, p["moe_wc"][l]], axis=1),
                 ((0, 0), (0, ROUTER_PAD - N_EXPERTS - N_GROUPS)))
    br = jnp.pad(jnp.concatenate([p["moe_bf"][l].reshape(-1), p["moe_bc"][l]]),
                 (0, ROUTER_PAD - N_EXPERTS - N_GROUPS)).reshape(1, ROUTER_PAD)
    w_out4 = p["w_out"][l].reshape(4, GROUP_W, D_MODEL)
    precise = l < p["w_in"].shape[0] - 1
    split = lambda w: _split_bits(w) if precise else (w.astype(BF16),)
    return dict(
        precise=precise, n1=row(p["norm1_g"][l]), w_in=split(w_in_r), pool_w=w_bd,
        pool_scale=row(p["pool_scale"][l]),
        rwkv=rwkv, ret_gn=row(p["ret_gn"][l]), fox_bf=p["fox_bf"][l],
        w_out=split(w_out4), n2=row(p["norm2_g"][l]),
        wr=wr, br=br, layer=l,
        w1=p["moe_w1"].astype(BF16), w3=p["moe_w3"].astype(BF16), w2=p["moe_w2"].astype(BF16))


def _tables(t, pos0):
    half = HEAD_DIM // 2
    inv = ROPE_BASE ** (-np.arange(half, dtype=np.float64) / half)
    ang = (pos0 + np.arange(t, dtype=np.float64))[:, None] * inv[None, :]
    cos, sin = np.cos(ang), np.sin(ang)
    cos_t = np.tile(np.concatenate([cos, cos], axis=1), (1, N_HEADS))
    sin_t = np.tile(np.concatenate([-sin, sin], axis=1), (1, N_HEADS))
    L = CHUNK
    log_g = np.log(np.array(RET_GAMMA, np.float64))
    idx = np.arange(L, dtype=np.float64)
    dmask = np.exp(log_g[:, None, None] * np.abs(idx[:, None] - idx[None, :]))
    lanes = lambda a: np.repeat(a, HEAD_DIM, axis=1)
    kdec = lanes(np.exp(log_g[None, :] * (L - 1.0 - idx)[:, None]))
    qdec = lanes(np.exp(log_g[None, :] * (idx + 1.0)[:, None]))
    cdec = np.broadcast_to(np.exp(log_g * L)[:, None, None], (N_HEADS, HEAD_DIM, HEAD_DIM))
    hid = np.arange(GROUP_W) // HEAD_DIM
    seg = np.where(hid[:, None] == hid[None, :], 1.0 / HEAD_DIM, 0.0)
    tril = np.tril(np.ones((L, L)))
    triu = np.triu(np.ones((128, 128)))
    tril_moe = np.tril(np.ones((MOE_TILE, MOE_TILE)), -1)
    f32 = lambda a: jnp.asarray(np.asarray(a, np.float32))
    return dict(cos=f32(cos_t), sin=f32(sin_t), dmask=f32(dmask), kdec=f32(kdec), qdec=f32(qdec),
                cdec=f32(cdec), seg=f32(seg), tril=f32(tril), triu=f32(triu),
                tril_moe=f32(tril_moe).astype(BF16))


def _trunk(x, pos0, states, layers, norm_f, cache=None):
    b, t, _ = x.shape
    n = b * t
    tb = _tables(t, pos0)
    xf = x.reshape(n, D_MODEL)
    nf = norm_f.reshape(1, D_MODEL)
    outs = [[] for _ in range(7)]
    for l, lp in enumerate(layers):
        pool_buf, shift, wkv, ret, lf_past = states[l]
        precise = lp["precise"]
        z_rwkv, z_ret, z_fox, z_pool, z_ff = _in_proj(xf, lp["n1"], lp["w_in"])
        buf16 = jnp.pad(pool_buf, ((0, 0), (POOL_PAD - POOL_BUF, 0), (0, 0)))
        y_a, pool_new = _pool_mixer(z_pool.reshape(b, t, GROUP_W), buf16, lp["pool_w"], lp["pool_scale"], pos0,
                                    precise)
        y_b, wkv_new, shift_new = _rwkv_mixer(z_rwkv.reshape(b, t, RWKV_COLS), shift, wkv, lp["rwkv"],
                                              tb["seg"], tb["tril"], precise)
        y_c, ret_new = _ret_mixer(z_ret.reshape(b, t, 4 * GROUP_W), ret, tb["cos"], tb["sin"], tb["dmask"],
                                  tb["kdec"], tb["qdec"], tb["cdec"], lp["ret_gn"], tb["seg"], precise)
        zf_t = jnp.transpose(z_ff[:, :N_HEADS].reshape(b, t, N_HEADS), (0, 2, 1)).reshape(b * N_HEADS, t)
        bias_rows = jnp.tile(lp["fox_bf"], b).reshape(b * N_HEADS, 1)
        zfox3 = z_fox.reshape(b, t, 4 * GROUP_W)
        if cache is None:
            lf_t, c_new = _fox_gate(zf_t, bias_rows, tb["triu"], None)
            cn3 = c_new.reshape(b, N_HEADS, t)
            y_d = _fox_attn(zfox3, cn3, precise)
        else:
            p = lf_past.shape[1]
            lfp_t = jnp.transpose(lf_past, (0, 2, 1)).reshape(b * N_HEADS, p)
            lf_t, c_new, c_past = _fox_gate(zf_t, bias_rows, tb["triu"], lfp_t)
            cn3 = c_new.reshape(b, N_HEADS, t)
            y_d = _fox_attn_past(zfox3, cn3, cache[0], cache[1], c_past.reshape(b, N_HEADS, p), l, precise)
        x1 = _out_proj(xf, [y.reshape(n, GROUP_W) for y in (y_a, y_b, y_c, y_d)], lp["w_out"])
        xs, gs, dest, meta = _router(x1, lp["n2"], lp["wr"], lp["br"], tb["tril_moe"])
        meta = meta[:, 0, :2 * N_GROUPS].astype(jnp.int32)
        ys = _moe(meta, xs, gs, lp["w1"], lp["w3"], lp["w2"], lp["layer"])
        xf = _moe_combine(x1, ys, dest, nf, final=(l == len(layers) - 1))
        k_new = zfox3[:, :, GROUP_W:2 * GROUP_W].reshape(b, t, N_HEADS, HEAD_DIM)
        v_new = zfox3[:, :, 2 * GROUP_W:3 * GROUP_W].reshape(b, t, N_HEADS, HEAD_DIM)
        lf_new = jnp.transpose(lf_t.reshape(b, N_HEADS, t), (0, 2, 1))
        for lst, s in zip(outs, (pool_new, shift_new, wkv_new, ret_new, k_new, v_new, lf_new)):
            lst.append(s)
    return xf.reshape(b, t, D_MODEL), [jnp.stack(lst) for lst in outs]


def kernel(x_prompt, x_sample, state_pool, state_shift, state_wkv, state_ret, cache_fox_k, cache_fox_v, cache_fox_logf, norm1_g, w_in, pool_w, pool_scale, rwkv_mu, rwkv_w0, rwkv_w2, rwkv_a0, rwkv_a2, rwkv_g2, rwkv_kk, rwkv_ka, rwkv_rk, rwkv_lnx, ret_gn, fox_bf, w_out, norm2_g, moe_wc, moe_bc, moe_wf, moe_bf, moe_w1, moe_w3, moe_w2, norm_f):
    p = dict(norm1_g=norm1_g, w_in=w_in, pool_w=pool_w, pool_scale=pool_scale, rwkv_mu=rwkv_mu,
             rwkv_w0=rwkv_w0, rwkv_w2=rwkv_w2, rwkv_a0=rwkv_a0, rwkv_a2=rwkv_a2, rwkv_g2=rwkv_g2,
             rwkv_kk=rwkv_kk, rwkv_ka=rwkv_ka, rwkv_rk=rwkv_rk, rwkv_lnx=rwkv_lnx, ret_gn=ret_gn,
             fox_bf=fox_bf, w_out=w_out, norm2_g=norm2_g, moe_wc=moe_wc, moe_bc=moe_bc, moe_wf=moe_wf,
             moe_bf=moe_bf, moe_w1=moe_w1, moe_w3=moe_w3, moe_w2=moe_w2)
    depth = w_in.shape[0]
    layers = [_prep_layer(l, p) for l in range(depth)]
    b = x_prompt.shape[0]
    dt = x_prompt.dtype
    prompt_init = [(jnp.zeros((b, POOL_BUF, GROUP_W), dt), jnp.zeros((b, 1, RWKV_COLS), dt),
                    jnp.zeros((b, N_HEADS, HEAD_DIM, HEAD_DIM), dt),
                    jnp.zeros((b, N_HEADS, HEAD_DIM, HEAD_DIM), dt), None)
                   for _ in range(depth)]
    sample_init = [(state_pool[l], state_shift[l], state_wkv[l], state_ret[l], cache_fox_logf[l])
                   for l in range(depth)]
    past = cache_fox_k.shape[2]
    cache = (jnp.transpose(cache_fox_k, (0, 1, 3, 4, 2)), jnp.transpose(cache_fox_v, (0, 1, 3, 4, 2)))
    y_prompt, new_p = _trunk(x_prompt, 0, prompt_init, layers, norm_f)
    y_sample, new_s = _trunk(x_sample, past, sample_init, layers, norm_f, cache)
    return (y_prompt, y_sample, *new_p, *new_s)
```

```python
import functools

import jax
import jax.numpy as jnp
import numpy as np
from jax import lax
from jax.experimental import pallas as pl
from jax.experimental.pallas import tpu as pltpu

F32 = jnp.float32
BF16 = jnp.bfloat16
HIGHEST = lax.Precision.HIGHEST

D_MODEL = 1024
DEPTH = 2
CHUNK = 64
GROUP_W = 256
HEAD_DIM = 64
N_HEADS = 4
POOL_WINDOWS = (2, 4, 8, 16)
POOL_BUF = 15
POOL_PAD = 16
RWKV_COLS = 1024
N_IN = 3332
FF_PAD = 128
RET_GAMMA = tuple(1.0 - 2.0 ** (-5 - h) for h in range(N_HEADS))
ROPE_BASE = 10000.0
N_GROUPS = 4
EXPERTS_PER_GROUP = 4
N_EXPERTS = 16
D_EXPERT = 512
LOG2E = 1.4426950408889634
RMS_EPS = 1e-6
RWKV_GN_EPS = 64e-5
RET_GN_EPS = 1e-5
SUB = 16
ROUTER_PAD = 128
MOE_TILE = 1024
MOE_PAD = 64
MOE_WIN = 320
MOE_ROWS = MOE_TILE + N_GROUPS * MOE_PAD + (MOE_WIN - MOE_PAD)
MOE_TILES_PER_STEP = 2
ATTN_TILE = 512
PRECISE_TAIL = 512
VMEM_LIMIT = 48 * 1024 * 1024
MOE_VMEM_LIMIT = 56 * 1024 * 1024


def _cp(sem):
    return pltpu.CompilerParams(dimension_semantics=sem, vmem_limit_bytes=VMEM_LIMIT)


def _dot(a, b):
    return jnp.dot(a.astype(BF16), b.astype(BF16), preferred_element_type=F32)


def _dot_nt(a, b):
    return lax.dot_general(a.astype(BF16), b.astype(BF16), (((1,), (1,)), ((), ())),
                           preferred_element_type=F32)


def _dot_tn(a, b):
    return lax.dot_general(a.astype(BF16), b.astype(BF16), (((0,), (0,)), ((), ())),
                           preferred_element_type=F32)


def _split(a):
    hi = a.astype(BF16)
    return hi, (a - hi.astype(F32)).astype(BF16)


def _split_bits(w):
    bits = lax.bitcast_convert_type(w, jnp.uint32) & jnp.uint32(0xFFFF0000)
    hi = lax.bitcast_convert_type(bits, F32)
    return hi.astype(BF16), (w - hi).astype(BF16)


def _dg3(a, b, dims):
    ah, al = _split(a)
    bh, bl = _split(b)
    dg = lambda x, y: lax.dot_general(x, y, (dims, ((), ())), preferred_element_type=F32)
    return dg(ah, bh) + (dg(ah, bl) + dg(al, bh))


class _MM:
    def __init__(self, precise):
        if precise:
            self.nn = lambda a, b: _dg3(a, b, ((1,), (0,)))
            self.nt = lambda a, b: _dg3(a, b, ((1,), (1,)))
            self.tn = lambda a, b: _dg3(a, b, ((0,), (0,)))
        else:
            self.nn, self.nt, self.tn = _dot, _dot_nt, _dot_tn


def _pieces(a, n):
    out = []
    for i in range(n):
        p = a.astype(BF16)
        out.append(p)
        if i + 1 < n:
            a = a - p.astype(F32)
    return out


def _dot_pieces(a, b, n, lhs_exact):
    dot = lambda x, y: jnp.dot(x, y, preferred_element_type=F32)
    terms = [dot(a, p) for p in _pieces(b, n)] if lhs_exact else [dot(p, b) for p in _pieces(a, n)]
    acc = terms[-1]
    for t in terms[-2::-1]:
        acc = acc + t
    return acc


def _dot_hi(a, b):
    return jnp.dot(a, b, precision=HIGHEST, preferred_element_type=F32)


def _dot_nt_hi(a, b):
    return lax.dot_general(a, b, (((1,), (1,)), ((), ())), precision=HIGHEST,
                           preferred_element_type=F32)


def _sigmoid(x):
    return 1.0 / (1.0 + jnp.exp(-x))


def _softplus(x):
    return jnp.maximum(x, 0.0) + jnp.log(1.0 + jnp.exp(-jnp.abs(x)))


def _rmsnorm(x, g):
    return x * lax.rsqrt(jnp.mean(x * x, axis=-1, keepdims=True) + RMS_EPS) * g


_IN_SPLITS = ((0, 1024), (1024, 2048), (2048, 3072), (3072, 3328), (3328, 3456))


def _in_kernel(x_ref, g_ref, *refs, precise):
    nw = 2 if precise else 1
    w_refs, o_refs = refs[:nw], refs[nw:]
    h = _rmsnorm(x_ref[...], g_ref[...])
    tm = h.shape[0]
    dot = lambda a, b: jnp.dot(a, b, preferred_element_type=F32)
    if precise:
        hh, hl = _split(h)
        lhs = jnp.concatenate([hh, hl], axis=0)
    else:
        lhs = h.astype(BF16)
    for o_ref, (c0, c1) in zip(o_refs, _IN_SPLITS):
        acc = dot(lhs, w_refs[0][:, c0:c1])
        if precise:
            acc = acc[:tm] + (acc[tm:] + dot(hh, w_refs[1][:, c0:c1]))
        o_ref[...] = acc


def _in_proj(x, g, ws):
    n = x.shape[0]
    tm = 512
    ncol = ws[0].shape[1]
    widths = [c1 - c0 for c0, c1 in _IN_SPLITS]
    return pl.pallas_call(
        functools.partial(_in_kernel, precise=len(ws) == 2),
        grid=(n // tm,),
        in_specs=[pl.BlockSpec((tm, D_MODEL), lambda i: (i, 0)),
                  pl.BlockSpec((1, D_MODEL), lambda i: (0, 0))]
                 + [pl.BlockSpec((D_MODEL, ncol), lambda i: (0, 0), pipeline_mode=pl.Buffered(1)) for _ in ws],
        out_specs=[pl.BlockSpec((tm, wd), lambda i: (i, 0)) for wd in widths],
        out_shape=[jax.ShapeDtypeStruct((n, wd), F32) for wd in widths],
        compiler_params=_cp(("parallel",)),
        name="in_proj",
    )(x, g, *ws)


def _pool_kernel(u_ref, buf_ref, w_ref, sc_ref, y_ref, nb_ref, ext_ref, *, tt, pos0, precise):
    mm = _MM(precise)
    j = pl.program_id(1)

    @pl.when(j == 0)
    def _():
        ext_ref[0:POOL_PAD, :] = buf_ref[...]

    u = u_ref[...]
    ext_ref[POOL_PAD:POOL_PAD + tt, :] = u
    acc = u
    sums = []
    for k in range(1, POOL_PAD):
        acc = acc + ext_ref[POOL_PAD - k:POOL_PAD - k + tt, :]
        if k + 1 in POOL_WINDOWS:
            sums.append(acc)
    lane = lax.broadcasted_iota(jnp.int32, (tt, GROUP_W), 1)
    grp = lane // (GROUP_W // len(POOL_WINDOWS))
    pos = (pos0 + j * tt + lax.broadcasted_iota(jnp.int32, (tt, GROUP_W), 0)).astype(F32)
    mean = jnp.zeros((tt, GROUP_W), F32)
    for gi, w in enumerate(POOL_WINDOWS):
        cnt = jnp.minimum(float(w), pos + 1.0)
        mean = jnp.where(grp == gi, sums[gi] / cnt, mean)
    d = mean - u
    y_ref[...] = (mm.nn(d, w_ref[...]) * sc_ref[...]).astype(y_ref.dtype)
    nb_ref[...] = ext_ref[tt + 1:tt + POOL_PAD, :]
    ext_ref[0:POOL_PAD, :] = ext_ref[tt:tt + POOL_PAD, :]


def _pool_mixer(u, buf16, w_bd, scale, pos0, precise):
    b, t, _ = u.shape
    tt = min(t, 512)
    return pl.pallas_call(
        functools.partial(_pool_kernel, tt=tt, pos0=pos0, precise=precise),
        grid=(b, t // tt),
        in_specs=[pl.BlockSpec((None, tt, GROUP_W), lambda i, j: (i, j, 0)),
                  pl.BlockSpec((None, POOL_PAD, GROUP_W), lambda i, j: (i, 0, 0)),
                  pl.BlockSpec((GROUP_W, GROUP_W), lambda i, j: (0, 0)),
                  pl.BlockSpec((1, GROUP_W), lambda i, j: (0, 0))],
        out_specs=[pl.BlockSpec((None, tt, GROUP_W), lambda i, j: (i, j, 0)),
                   pl.BlockSpec((None, POOL_BUF, GROUP_W), lambda i, j: (i, 0, 0))],
        out_shape=[jax.ShapeDtypeStruct((b, t, GROUP_W), F32 if precise else BF16),
                   jax.ShapeDtypeStruct((b, POOL_BUF, GROUP_W), F32)],
        scratch_shapes=[pltpu.VMEM((POOL_PAD + tt, GROUP_W), F32)],
        compiler_params=_cp(("parallel", "arbitrary")),
        name="pool_mixer",
    )(u, buf16, w_bd, scale)


def _rwkv_kernel(z_ref, sh_ref, s0_ref, mu_ref, w0_ref, w2_ref, a0_ref, a2_ref, g2_ref, kk_ref,
                 ka_ref, rk_ref, lnx_ref, seg_ref, tril_ref,
                 y_ref, so_ref, sho_ref, zext_ref, s_ref, ybuf_ref, pre_ref, *, precise, nb, nc):
    mm = _MM(precise)
    lora = _dot_hi if precise else _dot
    j = pl.program_id(1)
    L = CHUNK
    rows = nc * L
    n = nb * rows

    @pl.when(j == 0)
    def _():
        zext_ref[:, 7:8, :] = sh_ref[...]
        s_ref[...] = s0_ref[...]

    z3 = z_ref[...]
    zext_ref[:, 8:8 + rows, :] = z3
    prev3 = zext_ref[:, 7:7 + rows, :]
    zs = (z3 + (prev3 - z3) * mu_ref[...]).reshape(n, RWKV_COLS)
    c = GROUP_W
    r, k, v = zs[:, 0:c], zs[:, c:2 * c], zs[:, 2 * c:3 * c]
    wl, al, gl = zs[:, 768:832], zs[:, 832:896], zs[:, 896:1024]
    w = -_softplus(-(w0_ref[...] + lora(jnp.tanh(wl), w2_ref[...]))) - 0.5
    lw = -jnp.exp(w)
    a = _sigmoid(a0_ref[...] + lora(al, a2_ref[...]))
    g = lora(_sigmoid(gl), g2_ref[...])
    seg = seg_ref[...]
    segsum = lambda x: _dot_pieces(x, seg, 2 if precise else 1, lhs_exact=False)
    kk = k * kk_ref[...]
    kk = kk / jnp.maximum(jnp.sqrt(segsum(kk * kk) * float(HEAD_DIM)), 1e-12)
    k2 = k * (1.0 + (a - 1.0) * ka_ref[...])
    bonus = segsum(r * k2 * rk_ref[...]) * float(HEAD_DIM) * v
    b = kk * a

    ri = lax.broadcasted_iota(jnp.int32, (L, L), 0)
    ci = lax.broadcasted_iota(jnp.int32, (L, L), 1)
    strict, incl, eye = ri > ci, ri >= ci, ri == ci
    blk = (ri // SUB) == (ci // SUB)
    eye_f = jnp.where(eye, 1.0, 0.0)
    tril = tril_ref[...]

    nch = nb * nc
    chains = [(ch, h) for ch in range(nch) for h in range(N_HEADS)]
    each = lambda f, *lists: [f(*vals) for vals in zip(*lists)]
    at, rt, bt, kt, bh, kh, dend, vv = [], [], [], [], [], [], [], []
    for ch in range(nch):
        rs = slice(ch * L, (ch + 1) * L)
        lwc = lw[rs]
        cum = _dot_pieces(tril, lwc, 3 if precise else 2, lhs_exact=True)
        pend = cum[L - 1:L, :]
        e_neg = jnp.exp(-cum)
        e_end = jnp.exp(pend - cum)
        at_c, rt_c = -kk[rs] * jnp.exp(cum - lwc), r[rs] * jnp.exp(cum)
        bt_c, kt_c, bh_c, kh_c = b[rs] * e_neg, k2[rs] * e_neg, b[rs] * e_end, k2[rs] * e_end
        dend_c, v_c = jnp.exp(pend), v[rs]
        for h in range(N_HEADS):
            hs = slice(h * HEAD_DIM, (h + 1) * HEAD_DIM)
            for lst, val in ((at, at_c), (rt, rt_c), (bt, bt_c), (kt, kt_c), (bh, bh_c), (kh, kh_c),
                             (dend, dend_c), (vv, v_c)):
                lst.append(val[:, hs])
    gm = each(lambda a_, r_, b_, k_: mm.nt(jnp.concatenate([a_, r_], axis=0),
                                           jnp.concatenate([b_, k_], axis=0)), at, rt, bt, kt)
    gkv = each(lambda g_, v_: mm.nn(jnp.concatenate([jnp.where(strict, g_[:L, L:], 0.0),
                                                     jnp.where(incl, g_[L:, L:], 0.0)], axis=0), v_),
               gm, vv)
    a_ab = each(lambda g_: jnp.where(strict, g_[:L, :L], 0.0), gm)
    g_b = each(lambda g_: jnp.where(incl, g_[L:, :L], 0.0), gm)
    dp = each(lambda a_: jnp.where(blk, a_, 0.0), a_ab)
    aoff = each(lambda a_: jnp.where(blk, 0.0, a_), a_ab)
    md = each(lambda d_: eye_f + d_, dp)
    for _ in range(3):
        dp = each(lambda d_: mm.nn(d_, d_), dp)
        md = each(lambda m_, d_: m_ + mm.nn(m_, d_), md, dp)
    nm = each(mm.nn, md, aoff)
    nm2 = each(lambda n_: mm.nn(n_, n_), nm)
    x = each(lambda m_, a_, g_: mm.nn(m_, jnp.concatenate([a_, g_[:L]], axis=1)), md, at, gkv)
    x = each(lambda n_, x_: x_ + mm.nn(n_, x_), nm2, x)
    x = each(lambda n_, x_: x_ + mm.nn(n_, x_), nm, x)
    gx = each(mm.nn, g_b, x)
    xtb = each(mm.tn, x, bh)
    vtk = each(mm.tn, vv, kh)
    for i, (ch, h) in enumerate(chains):
        pre_ref[ch, h, 0] = rt[i] + gx[i][:, :HEAD_DIM]
        pre_ref[ch, h, 1] = gx[i][:, HEAD_DIM:] + gkv[i][L:]
        pre_ref[ch, h, 2] = jnp.where(eye, jnp.broadcast_to(dend[i], (L, L)), 0.0) + xtb[i][:HEAD_DIM]
        pre_ref[ch, h, 3] = xtb[i][HEAD_DIM:] + vtk[i]

    st_nt = _dot_nt_hi if precise else (lambda a_, b_: _dg3(a_, b_, ((1,), (1,))))
    st_nn = _dot_hi if precise else (lambda a_, b_: _dg3(a_, b_, ((1,), (0,))))
    seqs = [(bi, h) for bi in range(nb) for h in range(N_HEADS)]
    s = [s_ref[bi, h] for bi, h in seqs]
    for ci_ in range(nc):
        for i, (bi, h) in enumerate(seqs):
            ch = bi * nc + ci_
            ybuf_ref[ch * L:(ch + 1) * L, h * HEAD_DIM:(h + 1) * HEAD_DIM] = (
                st_nt(pre_ref[ch, h, 0], s[i]) + pre_ref[ch, h, 1])
        s = [st_nn(s[i], pre_ref[bi * nc + ci_, h, 2]) + pre_ref[bi * nc + ci_, h, 3]
             for i, (bi, h) in enumerate(seqs)]
    for i, (bi, h) in enumerate(seqs):
        s_ref[bi, h] = s[i]

    yb = ybuf_ref[...]
    mu = segsum(yb)
    yc = yb - mu
    y = yc * lax.rsqrt(segsum(yc * yc) + RWKV_GN_EPS) * lnx_ref[...]
    y_ref[...] = ((y + bonus) * g).astype(y_ref.dtype).reshape(nb, rows, GROUP_W)
    so_ref[...] = s_ref[...]
    sho_ref[...] = z3[:, rows - 1:rows, :]
    zext_ref[:, 7:8, :] = z3[:, rows - 1:rows, :]


def _rwkv_mixer(z, shift, s0, prm, seg, tril, precise):
    b, t, _ = z.shape
    L = CHUNK
    nc = min(t // L, 4)
    nb = 1 if nc > 1 else min(b, 4)
    rows = nc * L
    full = lambda a: pl.BlockSpec(a.shape, lambda i, j: (0,) * a.ndim)
    st_spec = pl.BlockSpec((nb, N_HEADS, HEAD_DIM, HEAD_DIM), lambda i, j: (i, 0, 0, 0))
    sh_spec = pl.BlockSpec((nb, 1, RWKV_COLS), lambda i, j: (i, 0, 0))
    consts = list(prm) + [seg.astype(BF16), tril.astype(BF16)]
    return pl.pallas_call(
        functools.partial(_rwkv_kernel, precise=precise, nb=nb, nc=nc),
        grid=(b // nb, t // rows),
        in_specs=[pl.BlockSpec((nb, rows, RWKV_COLS), lambda i, j: (i, j, 0)), sh_spec, st_spec]
                 + [full(a) for a in consts],
        out_specs=[pl.BlockSpec((nb, rows, GROUP_W), lambda i, j: (i, j, 0)), st_spec, sh_spec],
        out_shape=[jax.ShapeDtypeStruct((b, t, GROUP_W), F32 if precise else BF16),
                   jax.ShapeDtypeStruct((b, N_HEADS, HEAD_DIM, HEAD_DIM), F32),
                   jax.ShapeDtypeStruct((b, 1, RWKV_COLS), F32)],
        scratch_shapes=[pltpu.VMEM((nb, 8 + rows, RWKV_COLS), F32),
                        pltpu.VMEM((nb, N_HEADS, HEAD_DIM, HEAD_DIM), F32),
                        pltpu.VMEM((nb * rows, GROUP_W), F32),
                        pltpu.VMEM((nb * nc, N_HEADS, 4, HEAD_DIM, HEAD_DIM), F32)],
        compiler_params=_cp(("parallel", "arbitrary")),
        name="rwkv_mixer",
    )(z, shift, s0, *consts)


def _ret_kernel(z_ref, cos_ref, sin_ref, dm_ref, kdec_ref, qdec_ref, cdec_ref, gn_ref, seg_ref, s0_ref,
                y_ref, so_ref, s_ref, obuf_ref, *, precise, nb, nc):
    mm = _MM(precise)
    j = pl.program_id(1)
    L = CHUNK
    rows = nc * L

    @pl.when(j == 0)
    def _():
        s_ref[...] = s0_ref[...]

    c = GROUP_W
    lane = lax.broadcasted_iota(jnp.int32, (rows, c), 1)
    first_half = (lane % HEAD_DIM) < (HEAD_DIM // 2)
    cs, sn = cos_ref[...], sin_ref[...]
    kdec, qdec = kdec_ref[...], qdec_ref[...]

    def rope(x):
        partner = jnp.where(first_half, pltpu.roll(x, c - HEAD_DIM // 2, 1), pltpu.roll(x, HEAD_DIM // 2, 1))
        return x * cs + partner * sn

    each = lambda f, *lists: [f(*vals) for vals in zip(*lists)]
    keys = [(bi, ci, h) for bi in range(nb) for ci in range(nc) for h in range(N_HEADS)]
    qs, ks, vs, kds, qds, g_all = [], [], [], [], [], []
    for bi in range(nb):
        q = rope(z_ref[bi, :, 0:c])
        k = rope(z_ref[bi, :, c:2 * c]) * HEAD_DIM ** -0.5
        v = z_ref[bi, :, 2 * c:3 * c]
        g_all.append(z_ref[bi, :, 3 * c:4 * c])
        for ci in range(nc):
            rs = slice(ci * L, (ci + 1) * L)
            kd, qd = k[rs] * kdec, q[rs] * qdec
            for h in range(N_HEADS):
                hs = slice(h * HEAD_DIM, (h + 1) * HEAD_DIM)
                for lst, val in ((qs, q[rs]), (ks, k[rs]), (vs, v[rs]), (kds, kd), (qds, qd)):
                    lst.append(val[:, hs])
    att = [mm.nt(q_, k_) * dm_ref[h] for q_, k_, (_, _, h) in zip(qs, ks, keys)]
    o_in = each(mm.nn, att, vs)
    kv = each(mm.tn, kds, vs)
    s_prev = {}
    for bi in range(nb):
        for h in range(N_HEADS):
            s = s_ref[bi, h]
            for ci in range(nc):
                s_prev[(bi, ci, h)] = s
                s = s * cdec_ref[h] + kv[keys.index((bi, ci, h))]
            s_ref[bi, h] = s
    o_x = [mm.nn(qd_, s_prev[key]) for qd_, key in zip(qds, keys)]
    for i, (bi, ci, h) in enumerate(keys):
        r0 = bi * rows + ci * L
        obuf_ref[r0:r0 + L, h * HEAD_DIM:(h + 1) * HEAD_DIM] = o_in[i] + o_x[i]
    seg = seg_ref[...]
    segsum = lambda x: _dot_pieces(x, seg, 2 if precise else 1, lhs_exact=False)
    ob = obuf_ref[...]
    oc = ob - segsum(ob)
    o = oc * lax.rsqrt(segsum(oc * oc) + RET_GN_EPS) * gn_ref[...]
    g = jnp.concatenate(g_all, axis=0) if nb > 1 else g_all[0]
    y_ref[...] = (g * _sigmoid(g) * o).astype(y_ref.dtype).reshape(nb, rows, c)
    so_ref[...] = s_ref[...]


def _ret_mixer(z, s0, cos, sin, dmask, kdec, qdec, cdec, gn, seg, precise):
    b, t, _ = z.shape
    L = CHUNK
    nc = min(t // L, 4)
    nb = 1 if nc > 1 else min(b, 4)
    rows = nc * L
    full = lambda a: pl.BlockSpec(a.shape, lambda i, j: (0,) * a.ndim)
    st_spec = pl.BlockSpec((nb, N_HEADS, HEAD_DIM, HEAD_DIM), lambda i, j: (i, 0, 0, 0))
    tab_spec = pl.BlockSpec((rows, GROUP_W), lambda i, j: (j, 0))
    seg = seg.astype(BF16)
    return pl.pallas_call(
        functools.partial(_ret_kernel, precise=precise, nb=nb, nc=nc),
        grid=(b // nb, t // rows),
        in_specs=[pl.BlockSpec((nb, rows, 4 * GROUP_W), lambda i, j: (i, j, 0)), tab_spec, tab_spec,
                  full(dmask), full(kdec), full(qdec), full(cdec), full(gn), full(seg), st_spec],
        out_specs=[pl.BlockSpec((nb, rows, GROUP_W), lambda i, j: (i, j, 0)), st_spec],
        out_shape=[jax.ShapeDtypeStruct((b, t, GROUP_W), F32 if precise else BF16),
                   jax.ShapeDtypeStruct((b, N_HEADS, HEAD_DIM, HEAD_DIM), F32)],
        scratch_shapes=[pltpu.VMEM((nb, N_HEADS, HEAD_DIM, HEAD_DIM), F32),
                        pltpu.VMEM((nb * rows, GROUP_W), F32)],
        compiler_params=_cp(("parallel", "arbitrary")),
        name="ret_mixer",
    )(z, cos, sin, dmask, kdec, qdec, cdec, gn, seg, s0)


def _cumsum_lanes(x_ref, o_ref, tri, carry, width):
    blk = 128
    for c0 in range(0, width, blk):
        wd = min(blk, width - c0)
        cs = _dot_hi(x_ref[:, c0:c0 + wd], tri[:wd, :wd]) + carry
        o_ref[:, c0:c0 + wd] = cs
        carry = cs[:, wd - 1:wd]
    return carry


def _fox_gate_kernel(*refs, past, t):
    if past:
        zf_ref, b_ref, tri_ref, lfp_ref, lf_ref, cn_ref, cp_ref = refs
    else:
        zf_ref, b_ref, tri_ref, lf_ref, cn_ref = refs
    tri = tri_ref[...]
    x = zf_ref[...] + b_ref[...]
    lf_ref[...] = jnp.minimum(x, 0.0) - jnp.log(1.0 + jnp.exp(-jnp.abs(x)))
    carry = jnp.zeros((zf_ref.shape[0], 1), F32)
    if past:
        carry = _cumsum_lanes(lfp_ref, cp_ref, tri, carry, past)
    _cumsum_lanes(lf_ref, cn_ref, tri, carry, t)


def _fox_gate(zf_t, bias_rows, tri, lf_past_t):
    rows, t = zf_t.shape
    rt = min(rows, 32)
    past = 0 if lf_past_t is None else lf_past_t.shape[1]
    row_spec = lambda wd: pl.BlockSpec((rt, wd), lambda i: (i, 0))
    in_specs = [row_spec(t), row_spec(1), pl.BlockSpec(tri.shape, lambda i: (0, 0))]
    out_specs = [row_spec(t), row_spec(t)]
    out_shape = [jax.ShapeDtypeStruct((rows, t), F32), jax.ShapeDtypeStruct((rows, t), F32)]
    args = [zf_t, bias_rows, tri]
    if past:
        in_specs.append(row_spec(past))
        out_specs.append(row_spec(past))
        out_shape.append(jax.ShapeDtypeStruct((rows, past), F32))
        args.append(lf_past_t)
    return pl.pallas_call(
        functools.partial(_fox_gate_kernel, past=past, t=t),
        grid=(rows // rt,),
        in_specs=in_specs, out_specs=out_specs, out_shape=out_shape,
        compiler_params=_cp(("parallel",)),
        name="fox_gate",
    )(*args)


def _per_head_lanes(cols, tq):
    lane_head = lax.broadcasted_iota(jnp.int32, (tq, GROUP_W), 1) // HEAD_DIM
    out = jnp.broadcast_to(cols[N_HEADS - 1], (tq, GROUP_W))
    for h in range(N_HEADS - 2, -1, -1):
        out = jnp.where(lane_head == h, cols[h], out)
    return out


def _fox_attn_tile(q_ref, kn_ref, vn_ref, ckn_ref, m_ref, l_ref, acc_ref, *, i, j, tq, mm):
    heads = range(N_HEADS)
    per_head_lanes = lambda cols: _per_head_lanes(cols, tq)

    def block(k_ref, v_ref, ck_ref, diagonal, mm):
        q = q_ref[...] * (HEAD_DIM ** -0.5 * LOG2E)
        k = k_ref[...]
        v = v_ref[...]
        tk = k.shape[0]
        ck = ck_ref[...] * LOG2E
        lane_head = lax.broadcasted_iota(jnp.int32, (tk, GROUP_W), 1) // HEAD_DIM
        kbd = jnp.concatenate([jnp.where(lane_head == h, k, 0.0) for h in heads], axis=0)
        vbd = jnp.concatenate([jnp.where(lane_head == h, v, 0.0) for h in heads], axis=0)
        s_all = mm.nt(q, kbd)
        s = [s_all[:, h * tk:(h + 1) * tk] - ck[h:h + 1, :] for h in heads]
        if diagonal:
            keep = (lax.broadcasted_iota(jnp.int32, (tq, tk), 1)
                    <= lax.broadcasted_iota(jnp.int32, (tq, tk), 0))
            s = [jnp.where(keep, s_h, -jnp.inf) for s_h in s]
        m_prev = [m_ref[h] for h in heads]
        m_new = [jnp.maximum(m_prev[h], jnp.max(s[h], axis=1, keepdims=True)) for h in heads]
        alpha = [jnp.exp2(m_prev[h] - m_new[h]) for h in heads]
        p = [jnp.exp2(s[h] - m_new[h]) for h in heads]
        pv = mm.nn(jnp.concatenate(p, axis=1), vbd)
        for h in heads:
            l_ref[h] = alpha[h] * l_ref[h] + jnp.sum(p[h], axis=1, keepdims=True)
            m_ref[h] = m_new[h]
        acc_ref[...] = per_head_lanes(alpha) * acc_ref[...] + pv

    @pl.when(j < i)
    def _():
        block(kn_ref, vn_ref, ckn_ref, False, mm)

    @pl.when(j == i)
    def _():
        block(kn_ref, vn_ref, ckn_ref, True, mm)


def _fox_attn_kernel(q_ref, kn_ref, vn_ref, og_ref, ckn_ref, y_ref, m_ref, l_ref, acc_ref, *,
                     tq, precise_from):
    i, j = pl.program_id(1), pl.program_id(2)
    nk = pl.num_programs(2)
    refs = (q_ref, kn_ref, vn_ref, ckn_ref, m_ref, l_ref, acc_ref)

    @pl.when(j == 0)
    def _():
        m_ref[...] = jnp.full(m_ref.shape, -jnp.inf, F32)
        l_ref[...] = jnp.zeros(l_ref.shape, F32)
        acc_ref[...] = jnp.zeros(acc_ref.shape, F32)

    if precise_from is None:
        _fox_attn_tile(*refs, i=i, j=j, tq=tq, mm=_MM(False))
    elif precise_from <= 0:
        _fox_attn_tile(*refs, i=i, j=j, tq=tq, mm=_MM(True))
    else:
        @pl.when(i >= precise_from)
        def _():
            _fox_attn_tile(*refs, i=i, j=j, tq=tq, mm=_MM(True))

        @pl.when(i < precise_from)
        def _():
            _fox_attn_tile(*refs, i=i, j=j, tq=tq, mm=_MM(False))

    @pl.when(j == nk - 1)
    def _():
        o = acc_ref[...] / _per_head_lanes([l_ref[h] for h in range(N_HEADS)], tq)
        y_ref[...] = (_sigmoid(og_ref[...]) * o).astype(y_ref.dtype)


def _fox_attn(zfox, ck_new, precise, precise_from):
    b, t, _ = zfox.shape
    tq = min(t, ATTN_TILE)
    nq = t // tq
    jc = lambda i, j: jnp.minimum(j, i)
    in_specs = [pl.BlockSpec((None, tq, GROUP_W), lambda bb, i, j: (bb, i, 0)),
                pl.BlockSpec((None, tq, GROUP_W), lambda bb, i, j: (bb, jc(i, j), 1)),
                pl.BlockSpec((None, tq, GROUP_W), lambda bb, i, j: (bb, jc(i, j), 2)),
                pl.BlockSpec((None, tq, GROUP_W), lambda bb, i, j: (bb, i, 3)),
                pl.BlockSpec((None, N_HEADS, tq), lambda bb, i, j: (bb, 0, jc(i, j)))]
    return pl.pallas_call(
        functools.partial(_fox_attn_kernel, tq=tq, precise_from=precise_from),
        grid=(b, nq, nq),
        in_specs=in_specs,
        out_specs=pl.BlockSpec((None, tq, GROUP_W), lambda bb, i, j: (bb, i, 0)),
        out_shape=jax.ShapeDtypeStruct((b, t, GROUP_W), F32 if precise else BF16),
        scratch_shapes=[pltpu.VMEM((N_HEADS, tq, 1), F32), pltpu.VMEM((N_HEADS, tq, 1), F32),
                        pltpu.VMEM((tq, GROUP_W), F32)],
        compiler_params=_cp(("parallel", "parallel", "arbitrary")),
        name="fox_attn",
    )(zfox, zfox, zfox, zfox, ck_new)


def _fox_past_kernel(zf_ref, ckn_ref, kt_ref, vt_ref, ckp_ref, y_ref, m_ref, l_ref, acc_ref, *,
                     nkp, nb, precise):
    mm = _MM(precise)
    j = pl.program_id(1)
    t = zf_ref.shape[1]
    c = GROUP_W
    chains = [(bi, h) for bi in range(nb) for h in range(N_HEADS)]
    hs = lambda h: slice(h * HEAD_DIM, (h + 1) * HEAD_DIM)

    @pl.when(j == 0)
    def _():
        m_ref[...] = jnp.full(m_ref.shape, -jnp.inf, F32)
        l_ref[...] = jnp.zeros(l_ref.shape, F32)
        acc_ref[...] = jnp.zeros(acc_ref.shape, F32)

    def online_update(s, pv_of):
        m_prev = [m_ref[i] for i in range(len(chains))]
        m_new = [jnp.maximum(mp, jnp.max(s_i, axis=1, keepdims=True)) for mp, s_i in zip(m_prev, s)]
        alpha = [jnp.exp2(mp - mn) for mp, mn in zip(m_prev, m_new)]
        p = [jnp.exp2(s_i - mn) for s_i, mn in zip(s, m_new)]
        pv = pv_of(p)
        for i in range(len(chains)):
            l_ref[i] = alpha[i] * l_ref[i] + jnp.sum(p[i], axis=1, keepdims=True)
            acc_ref[i] = alpha[i] * acc_ref[i] + pv[i]
            m_ref[i] = m_new[i]

    def queries():
        return [zf_ref[bi, :, hs(h)] * (HEAD_DIM ** -0.5 * LOG2E) for bi, h in chains]

    @pl.when(j < nkp)
    def _():
        q = queries()
        s = [mm.nn(q[i], kt_ref[bi, h]) - ckp_ref[bi, h:h + 1, :] * LOG2E
             for i, (bi, h) in enumerate(chains)]
        online_update(s, lambda p: [mm.nt(p[i], vt_ref[bi, h]) for i, (bi, h) in enumerate(chains)])

    @pl.when(j == nkp)
    def _():
        q = queries()
        keep = lax.broadcasted_iota(jnp.int32, (t, t), 1) <= lax.broadcasted_iota(jnp.int32, (t, t), 0)
        s = [jnp.where(keep, mm.nt(q[i], zf_ref[bi, :, c + h * HEAD_DIM:c + (h + 1) * HEAD_DIM])
                       - ckn_ref[bi, h:h + 1, :] * LOG2E, -jnp.inf)
             for i, (bi, h) in enumerate(chains)]
        online_update(s, lambda p: [mm.nn(p[i], zf_ref[bi, :, 2 * c + h * HEAD_DIM:2 * c + (h + 1) * HEAD_DIM])
                                    for i, (bi, h) in enumerate(chains)])
        for i, (bi, h) in enumerate(chains):
            og = zf_ref[bi, :, 3 * c + h * HEAD_DIM:3 * c + (h + 1) * HEAD_DIM]
            y_ref[bi, :, hs(h)] = (_sigmoid(og) * (acc_ref[i] / l_ref[i])).astype(y_ref.dtype)


def _fox_attn_past(zfox, ck_new, kt_all, vt_all, ck_past, layer, precise):
    b, t, _ = zfox.shape
    p = kt_all.shape[-1]
    tkp = 512
    nkp = p // tkp
    nb = min(b, 4)
    jp = lambda j: jnp.minimum(j, nkp - 1)
    cache_spec = pl.BlockSpec((None, nb, N_HEADS, HEAD_DIM, tkp), lambda i, j: (layer, i, 0, 0, jp(j)))
    return pl.pallas_call(
        functools.partial(_fox_past_kernel, nkp=nkp, nb=nb, precise=precise),
        grid=(b // nb, nkp + 1),
        in_specs=[pl.BlockSpec((nb, t, 4 * GROUP_W), lambda i, j: (i, 0, 0)),
                  pl.BlockSpec((nb, N_HEADS, t), lambda i, j: (i, 0, 0)),
                  cache_spec, cache_spec,
                  pl.BlockSpec((nb, N_HEADS, tkp), lambda i, j: (i, 0, jp(j)))],
        out_specs=pl.BlockSpec((nb, t, GROUP_W), lambda i, j: (i, 0, 0)),
        out_shape=jax.ShapeDtypeStruct((b, t, GROUP_W), F32 if precise else BF16),
        scratch_shapes=[pltpu.VMEM((nb * N_HEADS, t, 1), F32), pltpu.VMEM((nb * N_HEADS, t, 1), F32),
                        pltpu.VMEM((nb * N_HEADS, t, HEAD_DIM), F32)],
        compiler_params=_cp(("parallel", "arbitrary")),
        name="fox_attn_past",
    )(zfox, ck_new, kt_all, vt_all, ck_past)


def _out_kernel(x_ref, ya_ref, yb_ref, yc_ref, yd_ref, *refs, precise):
    w_refs, o_ref = refs[:-1], refs[-1]
    dot = lambda a, b: jnp.dot(a, b, preferred_element_type=F32)
    acc = x_ref[...]
    for i, y_ref in enumerate((ya_ref, yb_ref, yc_ref, yd_ref)):
        if precise:
            yh, yl = _split(y_ref[...])
            acc = acc + (dot(yh, w_refs[0][i]) + (dot(yh, w_refs[1][i]) + dot(yl, w_refs[0][i])))
        else:
            acc = acc + dot(y_ref[...].astype(BF16), w_refs[0][i])
    o_ref[...] = acc


def _out_proj(x, ys, ws):
    n = x.shape[0]
    tm = 1024
    y_spec = pl.BlockSpec((tm, GROUP_W), lambda i: (i, 0))
    x_spec = pl.BlockSpec((tm, D_MODEL), lambda i: (i, 0))
    return pl.pallas_call(
        functools.partial(_out_kernel, precise=len(ws) == 2),
        grid=(n // tm,),
        in_specs=[x_spec, y_spec, y_spec, y_spec, y_spec]
                 + [pl.BlockSpec(w.shape, lambda i: (0, 0, 0)) for w in ws],
        out_specs=x_spec,
        out_shape=jax.ShapeDtypeStruct((n, D_MODEL), F32),
        compiler_params=_cp(("parallel",)),
        name="out_proj",
    )(x, *ys, *ws)


def _router_kernel(x_ref, g_ref, w_ref, b_ref, tril_ref, xs_ref, gs_ref, dest_ref, meta_ref):
    h = _rmsnorm(x_ref[...], g_ref[...])
    logits = _dot_hi(h, w_ref[...]) + b_ref[...]
    tm = logits.shape[0]
    lane_i = lax.broadcasted_iota(jnp.int32, (tm, ROUTER_PAD), 1)
    lane = lane_i.astype(F32)
    big = float(ROUTER_PAD)
    rmax = lambda a: jnp.max(a, axis=1, keepdims=True)
    rsum = lambda a: jnp.sum(a, axis=1, keepdims=True)
    first = lambda m: jnp.min(jnp.where(m, lane, big), axis=1, keepdims=True)

    is_c = jnp.logical_and(lane_i >= N_EXPERTS, lane_i < N_EXPERTS + N_GROUPS)
    lc = jnp.where(is_c, logits, -jnp.inf)
    ec = jnp.exp(lc - rmax(lc))
    pc_all = ec / rsum(ec)
    pc = rmax(pc_all)
    gi = first(jnp.logical_and(is_c, pc_all == pc)) - float(N_EXPERTS)

    is_f = (lane_i // EXPERTS_PER_GROUP).astype(F32) == gi
    lf = jnp.where(is_f, logits, -jnp.inf)
    ef = jnp.exp(lf - rmax(lf))
    pf = jnp.where(is_f, ef / rsum(ef), -1.0)
    t1 = rmax(pf)
    i1 = first(pf == t1)
    pf2 = jnp.where(lane == i1, -1.0, pf)
    t2 = rmax(pf2)
    i2 = first(pf2 == t2)
    den = t1 + t2
    gate = jnp.where(lane == i1, pc * (t1 / den), jnp.where(lane == i2, pc * (t2 / den), 0.0))

    onehot = jnp.where(lane == gi, 1.0, 0.0)
    rank = jnp.dot(tril_ref[...], onehot.astype(BF16), preferred_element_type=F32)
    cnt = jnp.sum(onehot, axis=0, keepdims=True)
    padded = jnp.floor((cnt + (MOE_PAD - 1.0)) * (1.0 / MOE_PAD)) * MOE_PAD
    lane1 = lax.broadcasted_iota(jnp.int32, (1, ROUTER_PAD), 1)
    off = jnp.zeros((1, ROUTER_PAD), F32)
    meta = jnp.zeros((1, ROUTER_PAD), F32)
    start = jnp.zeros((1, 1), F32)
    for g in range(N_GROUPS):
        size_g = jnp.sum(jnp.where(lane1 == g, padded, 0.0), axis=1, keepdims=True)
        off = off + jnp.where(lane1 == g, start, 0.0)
        meta = meta + jnp.where(lane1 == g, start, 0.0) + jnp.where(lane1 == N_GROUPS + g, size_g, 0.0)
        start = start + size_g
    dest = rsum(onehot * (off + rank))
    rows = lax.broadcasted_iota(jnp.int32, (tm, MOE_ROWS), 1).astype(F32)
    perm_t = jnp.where(rows == dest, 1.0, 0.0).astype(BF16)
    rhs = jnp.concatenate([h.astype(BF16)] + _pieces(gate, 3), axis=1)
    srt = lax.dot_general(perm_t, rhs, (((0,), (0,)), ((), ())), preferred_element_type=F32)
    xs_ref[...] = srt[:, :D_MODEL].astype(BF16)
    gs_ref[...] = (srt[:, D_MODEL:D_MODEL + ROUTER_PAD] + srt[:, D_MODEL + ROUTER_PAD:D_MODEL + 2 * ROUTER_PAD]
                   + srt[:, D_MODEL + 2 * ROUTER_PAD:])
    dest_ref[...] = dest
    meta_ref[...] = meta


def _router(x, g, wr, br, tril):
    n = x.shape[0]
    tm = MOE_TILE
    nt = n // tm
    return pl.pallas_call(
        _router_kernel,
        grid=(nt,),
        in_specs=[pl.BlockSpec((tm, D_MODEL), lambda i: (i, 0)),
                  pl.BlockSpec((1, D_MODEL), lambda i: (0, 0)),
                  pl.BlockSpec((D_MODEL, ROUTER_PAD), lambda i: (0, 0)),
                  pl.BlockSpec((1, ROUTER_PAD), lambda i: (0, 0)),
                  pl.BlockSpec((tm, tm), lambda i: (0, 0))],
        out_specs=[pl.BlockSpec((MOE_ROWS, D_MODEL), lambda i: (i, 0)),
                   pl.BlockSpec((MOE_ROWS, ROUTER_PAD), lambda i: (i, 0)),
                   pl.BlockSpec((tm, 1), lambda i: (i, 0)),
                   pl.BlockSpec((None, 1, ROUTER_PAD), lambda i: (i, 0, 0))],
        out_shape=[jax.ShapeDtypeStruct((nt * MOE_ROWS, D_MODEL), BF16),
                   jax.ShapeDtypeStruct((nt * MOE_ROWS, ROUTER_PAD), F32),
                   jax.ShapeDtypeStruct((n, 1), F32),
                   jax.ShapeDtypeStruct((nt, 1, ROUTER_PAD), F32)],
        compiler_params=_cp(("parallel",)),
        name="moe_router",
    )(x, g, wr, br, tril)


def _moe_kernel(meta_ref, xs_ref, gs_ref, w1_ref, w3_ref, w2_ref, o_ref, *, tps):
    i, e = pl.program_id(0), pl.program_id(1)

    @pl.when(e == 0)
    def _():
        o_ref[...] = jnp.zeros(o_ref.shape, F32)

    g = e // EXPERTS_PER_GROUP
    for sub in range(tps):
        tile = i * tps + sub
        off = meta_ref[tile, g] + sub * MOE_ROWS
        size = meta_ref[tile, N_GROUPS + g]
        for w in range(MOE_TILE // MOE_WIN + 1):
            @pl.when(w * MOE_WIN < size)
            def _():
                rows = pl.ds(pl.multiple_of(off + w * MOE_WIN, MOE_PAD), MOE_WIN)
                xw = xs_ref[rows, :]
                a = jnp.dot(xw, w1_ref[...], preferred_element_type=F32)
                b = jnp.dot(xw, w3_ref[...], preferred_element_type=F32)
                he = (a * _sigmoid(a) * b).astype(BF16)
                ye = jnp.dot(he, w2_ref[...], preferred_element_type=F32)
                gw = gs_ref[rows, :]
                lane = lax.broadcasted_iota(jnp.int32, gw.shape, 1)
                gcol = jnp.sum(jnp.where(lane == e, gw, 0.0), axis=1, keepdims=True)
                o_ref[rows, :] += gcol * ye


def _moe(meta, xs, gs, w1, w3, w2, layer):
    nt = meta.shape[0]
    tps = MOE_TILES_PER_STEP if nt % MOE_TILES_PER_STEP == 0 else 1
    rows = tps * MOE_ROWS
    return pl.pallas_call(
        functools.partial(_moe_kernel, tps=tps),
        grid_spec=pltpu.PrefetchScalarGridSpec(
            num_scalar_prefetch=1,
            grid=(nt // tps, N_EXPERTS),
            in_specs=[pl.BlockSpec((rows, D_MODEL), lambda i, e, m: (i, 0)),
                      pl.BlockSpec((rows, ROUTER_PAD), lambda i, e, m: (i, 0)),
                      pl.BlockSpec((None, None, D_MODEL, D_EXPERT), lambda i, e, m: (layer, e, 0, 0)),
                      pl.BlockSpec((None, None, D_MODEL, D_EXPERT), lambda i, e, m: (layer, e, 0, 0)),
                      pl.BlockSpec((None, None, D_EXPERT, D_MODEL), lambda i, e, m: (layer, e, 0, 0))],
            out_specs=pl.BlockSpec((rows, D_MODEL), lambda i, e, m: (i, 0))),
        out_shape=jax.ShapeDtypeStruct((nt * MOE_ROWS, D_MODEL), F32),
        compiler_params=pltpu.CompilerParams(dimension_semantics=("parallel", "arbitrary"),
                                             vmem_limit_bytes=MOE_VMEM_LIMIT),
        name="moe_experts",
    )(meta, xs, gs, w1, w3, w2)


def _combine_kernel(x_ref, os_ref, dest_ref, nf_ref, o_ref, *, final):
    tm = x_ref.shape[0]
    rows = lax.broadcasted_iota(jnp.int32, (tm, MOE_ROWS), 1).astype(F32)
    perm_t = jnp.where(rows == dest_ref[...], 1.0, 0.0).astype(BF16)
    y = x_ref[...] + _dot_pieces(perm_t, os_ref[...], 2, lhs_exact=True)
    o_ref[...] = _rmsnorm(y, nf_ref[...]) if final else y


def _moe_combine(x, os, dest, nf, final):
    n = x.shape[0]
    tm = MOE_TILE
    return pl.pallas_call(
        functools.partial(_combine_kernel, final=final),
        grid=(n // tm,),
        in_specs=[pl.BlockSpec((tm, D_MODEL), lambda i: (i, 0)),
                  pl.BlockSpec((MOE_ROWS, D_MODEL), lambda i: (i, 0)),
                  pl.BlockSpec((tm, 1), lambda i: (i, 0)),
                  pl.BlockSpec((1, D_MODEL), lambda i: (0, 0))],
        out_specs=pl.BlockSpec((tm, D_MODEL), lambda i: (i, 0)),
        out_shape=jax.ShapeDtypeStruct((n, D_MODEL), F32),
        compiler_params=_cp(("parallel",)),
        name="moe_combine",
    )(x, os, dest, nf)


def _prep_layer(l, p):
    w_in = p["w_in"][l]
    o_rwkv, o_ret, o_fox = GROUP_W, GROUP_W + RWKV_COLS, GROUP_W + RWKV_COLS + 4 * GROUP_W
    ff = jnp.pad(w_in[:, o_fox + 4 * GROUP_W:], ((0, 0), (0, FF_PAD - N_HEADS)))
    w_in_r = jnp.concatenate([w_in[:, o_rwkv:o_ret], w_in[:, o_ret:o_fox],
                              w_in[:, o_fox:o_fox + 4 * GROUP_W], w_in[:, :GROUP_W], ff], axis=1)
    pw = p["pool_w"][l]
    pg = GROUP_W // len(POOL_WINDOWS)
    w_bd = jnp.zeros((GROUP_W, GROUP_W), F32)
    for gi in range(len(POOL_WINDOWS)):
        w_bd = w_bd.at[gi * pg:(gi + 1) * pg, gi * pg:(gi + 1) * pg].set(pw[gi])
    row = lambda a: a.reshape(1, -1)
    rwkv = (row(p["rwkv_mu"][l]), row(p["rwkv_w0"][l]), p["rwkv_w2"][l], row(p["rwkv_a0"][l]),
            p["rwkv_a2"][l], p["rwkv_g2"][l], row(p["rwkv_kk"][l]), row(p["rwkv_ka"][l]),
            row(p["rwkv_rk"][l]), row(p["rwkv_lnx"][l]))
    wf = jnp.transpose(p["moe_wf"][l], (1, 0, 2)).reshape(D_MODEL, N_EXPERTS)
    wr = jnp.pad(jnp.concatenate([wf, p["moe_wc"][l]], axis=1),
                 ((0, 0), (0, ROUTER_PAD - N_EXPERTS - N_GROUPS)))
    br = jnp.pad(jnp.concatenate([p["moe_bf"][l].reshape(-1), p["moe_bc"][l]]),
                 (0, ROUTER_PAD - N_EXPERTS - N_GROUPS)).reshape(1, ROUTER_PAD)
    w_out4 = p["w_out"][l].reshape(4, GROUP_W, D_MODEL)
    precise = l < p["w_in"].shape[0] - 1
    split = lambda w: _split_bits(w) if precise else (w.astype(BF16),)
    return dict(
        precise=precise, n1=row(p["norm1_g"][l]), w_in=split(w_in_r), pool_w=w_bd,
        w_in_single=(w_in_r.astype(BF16),), w_out_single=(w_out4.astype(BF16),),
        pool_scale=row(p["pool_scale"][l]),
        rwkv=rwkv, ret_gn=row(p["ret_gn"][l]), fox_bf=p["fox_bf"][l],
        w_out=split(w_out4), n2=row(p["norm2_g"][l]),
        wr=wr, br=br, layer=l,
        w1=p["moe_w1"].astype(BF16), w3=p["moe_w3"].astype(BF16), w2=p["moe_w2"].astype(BF16))


def _tables(t, pos0):
    half = HEAD_DIM // 2
    inv = ROPE_BASE ** (-np.arange(half, dtype=np.float64) / half)
    ang = (pos0 + np.arange(t, dtype=np.float64))[:, None] * inv[None, :]
    cos, sin = np.cos(ang), np.sin(ang)
    cos_t = np.tile(np.concatenate([cos, cos], axis=1), (1, N_HEADS))
    sin_t = np.tile(np.concatenate([-sin, sin], axis=1), (1, N_HEADS))
    L = CHUNK
    log_g = np.log(np.array(RET_GAMMA, np.float64))
    idx = np.arange(L, dtype=np.float64)
    dmask = np.exp(log_g[:, None, None] * np.abs(idx[:, None] - idx[None, :]))
    lanes = lambda a: np.repeat(a, HEAD_DIM, axis=1)
    kdec = lanes(np.exp(log_g[None, :] * (L - 1.0 - idx)[:, None]))
    qdec = lanes(np.exp(log_g[None, :] * (idx + 1.0)[:, None]))
    cdec = np.broadcast_to(np.exp(log_g * L)[:, None, None], (N_HEADS, HEAD_DIM, HEAD_DIM))
    hid = np.arange(GROUP_W) // HEAD_DIM
    seg = np.where(hid[:, None] == hid[None, :], 1.0 / HEAD_DIM, 0.0)
    tril = np.tril(np.ones((L, L)))
    triu = np.triu(np.ones((128, 128)))
    tril_moe = np.tril(np.ones((MOE_TILE, MOE_TILE)), -1)
    f32 = lambda a: jnp.asarray(np.asarray(a, np.float32))
    return dict(cos=f32(cos_t), sin=f32(sin_t), dmask=f32(dmask), kdec=f32(kdec), qdec=f32(qdec),
                cdec=f32(cdec), seg=f32(seg), tril=f32(tril), triu=f32(triu),
                tril_moe=f32(tril_moe).astype(BF16))


def _trunk(x, pos0, states, layers, norm_f, cache=None):
    b, t, _ = x.shape
    n = b * t
    tb = _tables(t, pos0)
    xf = x.reshape(n, D_MODEL)
    nf = norm_f.reshape(1, D_MODEL)
    outs = [[] for _ in range(7)]
    for l, lp in enumerate(layers):
        pool_buf, shift, wkv, ret, lf_past = states[l]
        precise = lp["precise"]
        tail = precise and t > PRECISE_TAIL and (t - PRECISE_TAIL) % ATTN_TILE == 0
        tail_rows = lambda a, rows=PRECISE_TAIL: a.reshape(b, t, -1)[:, t - rows:].reshape(b * rows, -1)
        put_tail = lambda full, part: lax.dynamic_update_slice(
            full.reshape(b, t, -1), part.reshape(b, PRECISE_TAIL, -1), (0, t - PRECISE_TAIL, 0)).reshape(n, -1)
        if tail:
            zs = _in_proj(xf, lp["n1"], lp["w_in_single"])
            zt = _in_proj(tail_rows(xf), lp["n1"], lp["w_in"])
            z_rwkv, z_ret, z_fox, z_pool, z_ff = [put_tail(a, c) for a, c in zip(zs, zt)]
        else:
            z_rwkv, z_ret, z_fox, z_pool, z_ff = _in_proj(xf, lp["n1"], lp["w_in"])
        buf16 = jnp.pad(pool_buf, ((0, 0), (POOL_PAD - POOL_BUF, 0), (0, 0)))
        y_a, pool_new = _pool_mixer(z_pool.reshape(b, t, GROUP_W), buf16, lp["pool_w"], lp["pool_scale"], pos0,
                                    precise)
        y_b, wkv_new, shift_new = _rwkv_mixer(z_rwkv.reshape(b, t, RWKV_COLS), shift, wkv, lp["rwkv"],
                                              tb["seg"], tb["tril"], precise)
        y_c, ret_new = _ret_mixer(z_ret.reshape(b, t, 4 * GROUP_W), ret, tb["cos"], tb["sin"], tb["dmask"],
                                  tb["kdec"], tb["qdec"], tb["cdec"], lp["ret_gn"], tb["seg"], precise)
        zf_t = jnp.transpose(z_ff[:, :N_HEADS].reshape(b, t, N_HEADS), (0, 2, 1)).reshape(b * N_HEADS, t)
        bias_rows = jnp.tile(lp["fox_bf"], b).reshape(b * N_HEADS, 1)
        zfox3 = z_fox.reshape(b, t, 4 * GROUP_W)
        if cache is None:
            lf_t, c_new = _fox_gate(zf_t, bias_rows, tb["triu"], None)
            cn3 = c_new.reshape(b, N_HEADS, t)
            first_precise = ((t - PRECISE_TAIL) // ATTN_TILE if tail else 0) if precise else None
            y_d = _fox_attn(zfox3, cn3, precise, first_precise)
        else:
            p = lf_past.shape[1]
            lfp_t = jnp.transpose(lf_past, (0, 2, 1)).reshape(b * N_HEADS, p)
            lf_t, c_new, c_past = _fox_gate(zf_t, bias_rows, tb["triu"], lfp_t)
            cn3 = c_new.reshape(b, N_HEADS, t)
            y_d = _fox_attn_past(zfox3, cn3, cache[0], cache[1], c_past.reshape(b, N_HEADS, p), l, precise)
        ys4 = [y.reshape(n, GROUP_W) for y in (y_a, y_b, y_c, y_d)]
        if tail:
            x1 = put_tail(_out_proj(xf, ys4, lp["w_out_single"]),
                          _out_proj(tail_rows(xf), [tail_rows(y) for y in ys4], lp["w_out"]))
        else:
            x1 = _out_proj(xf, ys4, lp["w_out"])
        xs, gs, dest, meta = _router(x1, lp["n2"], lp["wr"], lp["br"], tb["tril_moe"])
        meta = meta[:, 0, :2 * N_GROUPS].astype(jnp.int32)
        ys = _moe(meta, xs, gs, lp["w1"], lp["w3"], lp["w2"], lp["layer"])
        xf = _moe_combine(x1, ys, dest, nf, final=(l == len(layers) - 1))
        k_new = zfox3[:, :, GROUP_W:2 * GROUP_W].reshape(b, t, N_HEADS, HEAD_DIM)
        v_new = zfox3[:, :, 2 * GROUP_W:3 * GROUP_W].reshape(b, t, N_HEADS, HEAD_DIM)
        lf_new = jnp.transpose(lf_t.reshape(b, N_HEADS, t), (0, 2, 1))
        for lst, s in zip(outs, (pool_new, shift_new, wkv_new, ret_new, k_new, v_new, lf_new)):
            lst.append(s)
    return xf.reshape(b, t, D_MODEL), [jnp.stack(lst) for lst in outs]


def kernel(x_prompt, x_sample, state_pool, state_shift, state_wkv, state_ret, cache_fox_k, cache_fox_v, cache_fox_logf, norm1_g, w_in, pool_w, pool_scale, rwkv_mu, rwkv_w0, rwkv_w2, rwkv_a0, rwkv_a2, rwkv_g2, rwkv_kk, rwkv_ka, rwkv_rk, rwkv_lnx, ret_gn, fox_bf, w_out, norm2_g, moe_wc, moe_bc, moe_wf, moe_bf, moe_w1, moe_w3, moe_w2, norm_f):
    p = dict(norm1_g=norm1_g, w_in=w_in, pool_w=pool_w, pool_scale=pool_scale, rwkv_mu=rwkv_mu,
             rwkv_w0=rwkv_w0, rwkv_w2=rwkv_w2, rwkv_a0=rwkv_a0, rwkv_a2=rwkv_a2, rwkv_g2=rwkv_g2,
             rwkv_kk=rwkv_kk, rwkv_ka=rwkv_ka, rwkv_rk=rwkv_rk, rwkv_lnx=rwkv_lnx, ret_gn=ret_gn,
             fox_bf=fox_bf, w_out=w_out, norm2_g=norm2_g, moe_wc=moe_wc, moe_bc=moe_bc, moe_wf=moe_wf,
             moe_bf=moe_bf, moe_w1=moe_w1, moe_w3=moe_w3, moe_w2=moe_w2)
    depth = w_in.shape[0]
    layers = [_prep_layer(l, p) for l in range(depth)]
    b = x_prompt.shape[0]
    dt = x_prompt.dtype
    prompt_init = [(jnp.zeros((b, POOL_BUF, GROUP_W), dt), jnp.zeros((b, 1, RWKV_COLS), dt),
                    jnp.zeros((b, N_HEADS, HEAD_DIM, HEAD_DIM), dt),
                    jnp.zeros((b, N_HEADS, HEAD_DIM, HEAD_DIM), dt), None)
                   for _ in range(depth)]
    sample_init = [(state_pool[l], state_shift[l], state_wkv[l], state_ret[l], cache_fox_logf[l])
                   for l in range(depth)]
    past = cache_fox_k.shape[2]
    cache = (jnp.transpose(cache_fox_k, (0, 1, 3, 4, 2)), jnp.transpose(cache_fox_v, (0, 1, 3, 4, 2)))
    y_prompt, new_p = _trunk(x_prompt, 0, prompt_init, layers, norm_f)
    y_sample, new_s = _trunk(x_sample, past, sample_init, layers, norm_f, cache)
    return (y_prompt, y_sample, *new_p, *new_s)
```

```python
import functools

import jax
import jax.numpy as jnp
import numpy as np
from jax import lax
from jax.experimental import pallas as pl
from jax.experimental.pallas import tpu as pltpu

F32 = jnp.float32
BF16 = jnp.bfloat16
HIGHEST = lax.Precision.HIGHEST

D_MODEL = 1024
DEPTH = 2
CHUNK = 64
GROUP_W = 256
HEAD_DIM = 64
N_HEADS = 4
POOL_WINDOWS = (2, 4, 8, 16)
POOL_BUF = 15
POOL_PAD = 16
RWKV_COLS = 1024
N_IN = 3332
FF_PAD = 128
RET_GAMMA = tuple(1.0 - 2.0 ** (-5 - h) for h in range(N_HEADS))
ROPE_BASE = 10000.0
N_GROUPS = 4
EXPERTS_PER_GROUP = 4
N_EXPERTS = 16
D_EXPERT = 512
LOG2E = 1.4426950408889634
RMS_EPS = 1e-6
RWKV_GN_EPS = 64e-5
RET_GN_EPS = 1e-5
SUB = 16
ROUTER_PAD = 128
MOE_TILE = 1024
MOE_PAD = 64
MOE_WIN = 320
MOE_ROWS = MOE_TILE + N_GROUPS * MOE_PAD + (MOE_WIN - MOE_PAD)
MOE_TILES_PER_STEP = 2
ATTN_TILE = 512
PRECISE_TAIL = 512
VMEM_LIMIT = 48 * 1024 * 1024
MOE_VMEM_LIMIT = 56 * 1024 * 1024


def _cp(sem):
    return pltpu.CompilerParams(dimension_semantics=sem, vmem_limit_bytes=VMEM_LIMIT)


def _dot(a, b):
    return jnp.dot(a.astype(BF16), b.astype(BF16), preferred_element_type=F32)


def _dot_nt(a, b):
    return lax.dot_general(a.astype(BF16), b.astype(BF16), (((1,), (1,)), ((), ())),
                           preferred_element_type=F32)


def _dot_tn(a, b):
    return lax.dot_general(a.astype(BF16), b.astype(BF16), (((0,), (0,)), ((), ())),
                           preferred_element_type=F32)


def _split(a):
    hi = a.astype(BF16)
    return hi, (a - hi.astype(F32)).astype(BF16)


def _split_bits(w):
    bits = lax.bitcast_convert_type(w, jnp.uint32) & jnp.uint32(0xFFFF0000)
    hi = lax.bitcast_convert_type(bits, F32)
    return hi.astype(BF16), (w - hi).astype(BF16)


def _dg3(a, b, dims):
    ah, al = _split(a)
    bh, bl = _split(b)
    dg = lambda x, y: lax.dot_general(x, y, (dims, ((), ())), preferred_element_type=F32)
    return dg(ah, bh) + (dg(ah, bl) + dg(al, bh))


class _MM:
    def __init__(self, precise):
        if precise:
            self.nn = lambda a, b: _dg3(a, b, ((1,), (0,)))
            self.nt = lambda a, b: _dg3(a, b, ((1,), (1,)))
            self.tn = lambda a, b: _dg3(a, b, ((0,), (0,)))
        else:
            self.nn, self.nt, self.tn = _dot, _dot_nt, _dot_tn


def _pieces(a, n):
    out = []
    for i in range(n):
        p = a.astype(BF16)
        out.append(p)
        if i + 1 < n:
            a = a - p.astype(F32)
    return out


def _dot_pieces(a, b, n, lhs_exact):
    dot = lambda x, y: jnp.dot(x, y, preferred_element_type=F32)
    terms = [dot(a, p) for p in _pieces(b, n)] if lhs_exact else [dot(p, b) for p in _pieces(a, n)]
    acc = terms[-1]
    for t in terms[-2::-1]:
        acc = acc + t
    return acc


def _dot_hi(a, b):
    return jnp.dot(a, b, precision=HIGHEST, preferred_element_type=F32)


def _dot_nt_hi(a, b):
    return lax.dot_general(a, b, (((1,), (1,)), ((), ())), precision=HIGHEST,
                           preferred_element_type=F32)


def _sigmoid(x):
    return 1.0 / (1.0 + jnp.exp(-x))


def _softplus(x):
    return jnp.maximum(x, 0.0) + jnp.log(1.0 + jnp.exp(-jnp.abs(x)))


def _rmsnorm(x, g):
    return x * lax.rsqrt(jnp.mean(x * x, axis=-1, keepdims=True) + RMS_EPS) * g


_IN_SPLITS = ((0, 1024), (1024, 2048), (2048, 3072), (3072, 3328), (3328, 3456))


def _in_kernel(x_ref, g_ref, *refs, precise):
    nw = 2 if precise else 1
    w_refs, o_refs = refs[:nw], refs[len(refs) - len(_IN_SPLITS):]
    h = _rmsnorm(x_ref[...], g_ref[...])
    tm = h.shape[0]
    dot = lambda a, b: jnp.dot(a, b, preferred_element_type=F32)
    if precise:
        hh, hl = _split(h)
        lhs = jnp.concatenate([hh, hl], axis=0)
    else:
        lhs = h.astype(BF16)
    for o_ref, (c0, c1) in zip(o_refs, _IN_SPLITS):
        acc = dot(lhs, w_refs[0][:, c0:c1])
        if precise:
            acc = acc[:tm] + (acc[tm:] + dot(hh, w_refs[1][:, c0:c1]))
        o_ref[...] = acc


def _tail_blocks(n, seqs, tm):
    assert PRECISE_TAIL == tm
    per_seq = n // seqs // tm
    return seqs, (lambda i: (i * per_seq + per_seq - 1, 0))


def _in_proj(x, g, ws, into=None, seqs=None):
    n = x.shape[0]
    tm = 512
    ncol = ws[0].shape[1]
    widths = [c1 - c0 for c0, c1 in _IN_SPLITS]
    steps, rows = (n // tm, lambda i: (i, 0)) if into is None else _tail_blocks(n, seqs, tm)
    alias = [] if into is None else list(into)
    return pl.pallas_call(
        functools.partial(_in_kernel, precise=len(ws) == 2),
        grid=(steps,),
        in_specs=[pl.BlockSpec((tm, D_MODEL), rows),
                  pl.BlockSpec((1, D_MODEL), lambda i: (0, 0))]
                 + [pl.BlockSpec((D_MODEL, ncol), lambda i: (0, 0), pipeline_mode=pl.Buffered(1)) for _ in ws]
                 + [pl.BlockSpec(memory_space=pl.ANY) for _ in alias],
        out_specs=[pl.BlockSpec((tm, wd), rows) for wd in widths],
        out_shape=[jax.ShapeDtypeStruct((n, wd), F32) for wd in widths],
        input_output_aliases={2 + len(ws) + k: k for k in range(len(alias))},
        compiler_params=_cp(("parallel",)),
        name="in_proj",
    )(x, g, *ws, *alias)


def _pool_kernel(u_ref, buf_ref, w_ref, sc_ref, y_ref, nb_ref, ext_ref, *, tt, pos0, precise):
    mm = _MM(precise)
    j = pl.program_id(1)

    @pl.when(j == 0)
    def _():
        ext_ref[0:POOL_PAD, :] = buf_ref[...]

    u = u_ref[...]
    ext_ref[POOL_PAD:POOL_PAD + tt, :] = u
    acc = u
    sums = []
    for k in range(1, POOL_PAD):
        acc = acc + ext_ref[POOL_PAD - k:POOL_PAD - k + tt, :]
        if k + 1 in POOL_WINDOWS:
            sums.append(acc)
    lane = lax.broadcasted_iota(jnp.int32, (tt, GROUP_W), 1)
    grp = lane // (GROUP_W // len(POOL_WINDOWS))
    pos = (pos0 + j * tt + lax.broadcasted_iota(jnp.int32, (tt, GROUP_W), 0)).astype(F32)
    mean = jnp.zeros((tt, GROUP_W), F32)
    for gi, w in enumerate(POOL_WINDOWS):
        cnt = jnp.minimum(float(w), pos + 1.0)
        mean = jnp.where(grp == gi, sums[gi] / cnt, mean)
    d = mean - u
    y_ref[...] = (mm.nn(d, w_ref[...]) * sc_ref[...]).astype(y_ref.dtype)
    nb_ref[...] = ext_ref[tt + 1:tt + POOL_PAD, :]
    ext_ref[0:POOL_PAD, :] = ext_ref[tt:tt + POOL_PAD, :]


def _pool_mixer(u, buf16, w_bd, scale, pos0, precise):
    b, t, _ = u.shape
    tt = min(t, 512)
    return pl.pallas_call(
        functools.partial(_pool_kernel, tt=tt, pos0=pos0, precise=precise),
        grid=(b, t // tt),
        in_specs=[pl.BlockSpec((None, tt, GROUP_W), lambda i, j: (i, j, 0)),
                  pl.BlockSpec((None, POOL_PAD, GROUP_W), lambda i, j: (i, 0, 0)),
                  pl.BlockSpec((GROUP_W, GROUP_W), lambda i, j: (0, 0)),
                  pl.BlockSpec((1, GROUP_W), lambda i, j: (0, 0))],
        out_specs=[pl.BlockSpec((None, tt, GROUP_W), lambda i, j: (i, j, 0)),
                   pl.BlockSpec((None, POOL_BUF, GROUP_W), lambda i, j: (i, 0, 0))],
        out_shape=[jax.ShapeDtypeStruct((b, t, GROUP_W), F32 if precise else BF16),
                   jax.ShapeDtypeStruct((b, POOL_BUF, GROUP_W), F32)],
        scratch_shapes=[pltpu.VMEM((POOL_PAD + tt, GROUP_W), F32)],
        compiler_params=_cp(("parallel", "arbitrary")),
        name="pool_mixer",
    )(u, buf16, w_bd, scale)


def _rwkv_kernel(z_ref, sh_ref, s0_ref, mu_ref, w0_ref, w2_ref, a0_ref, a2_ref, g2_ref, kk_ref,
                 ka_ref, rk_ref, lnx_ref, seg_ref, tril_ref,
                 y_ref, so_ref, sho_ref, zext_ref, s_ref, ybuf_ref, pre_ref, *, precise, nb, nc):
    mm = _MM(precise)
    lora = _dot_hi if precise else _dot
    j = pl.program_id(1)
    L = CHUNK
    rows = nc * L
    n = nb * rows

    @pl.when(j == 0)
    def _():
        zext_ref[:, 7:8, :] = sh_ref[...]
        s_ref[...] = s0_ref[...]

    z3 = z_ref[...]
    zext_ref[:, 8:8 + rows, :] = z3
    prev3 = zext_ref[:, 7:7 + rows, :]
    zs = (z3 + (prev3 - z3) * mu_ref[...]).reshape(n, RWKV_COLS)
    c = GROUP_W
    r, k, v = zs[:, 0:c], zs[:, c:2 * c], zs[:, 2 * c:3 * c]
    wl, al, gl = zs[:, 768:832], zs[:, 832:896], zs[:, 896:1024]
    w = -_softplus(-(w0_ref[...] + lora(jnp.tanh(wl), w2_ref[...]))) - 0.5
    lw = -jnp.exp(w)
    a = _sigmoid(a0_ref[...] + lora(al, a2_ref[...]))
    g = lora(_sigmoid(gl), g2_ref[...])
    seg = seg_ref[...]
    segsum = lambda x: _dot_pieces(x, seg, 2 if precise else 1, lhs_exact=False)
    kk = k * kk_ref[...]
    kk = kk / jnp.maximum(jnp.sqrt(segsum(kk * kk) * float(HEAD_DIM)), 1e-12)
    k2 = k * (1.0 + (a - 1.0) * ka_ref[...])
    bonus = segsum(r * k2 * rk_ref[...]) * float(HEAD_DIM) * v
    b = kk * a

    ri = lax.broadcasted_iota(jnp.int32, (L, L), 0)
    ci = lax.broadcasted_iota(jnp.int32, (L, L), 1)
    strict, incl, eye = ri > ci, ri >= ci, ri == ci
    blk = (ri // SUB) == (ci // SUB)
    eye_f = jnp.where(eye, 1.0, 0.0)
    tril = tril_ref[...]

    nch = nb * nc
    chains = [(ch, h) for ch in range(nch) for h in range(N_HEADS)]
    each = lambda f, *lists: [f(*vals) for vals in zip(*lists)]
    at, rt, bt, kt, bh, kh, dend, vv = [], [], [], [], [], [], [], []
    for ch in range(nch):
        rs = slice(ch * L, (ch + 1) * L)
        lwc = lw[rs]
        cum = _dot_pieces(tril, lwc, 3 if precise else 2, lhs_exact=True)
        pend = cum[L - 1:L, :]
        e_neg = jnp.exp(-cum)
        e_end = jnp.exp(pend - cum)
        at_c, rt_c = -kk[rs] * jnp.exp(cum - lwc), r[rs] * jnp.exp(cum)
        bt_c, kt_c, bh_c, kh_c = b[rs] * e_neg, k2[rs] * e_neg, b[rs] * e_end, k2[rs] * e_end
        dend_c, v_c = jnp.exp(pend), v[rs]
        for h in range(N_HEADS):
            hs = slice(h * HEAD_DIM, (h + 1) * HEAD_DIM)
            for lst, val in ((at, at_c), (rt, rt_c), (bt, bt_c), (kt, kt_c), (bh, bh_c), (kh, kh_c),
                             (dend, dend_c), (vv, v_c)):
                lst.append(val[:, hs])
    gm = each(lambda a_, r_, b_, k_: mm.nt(jnp.concatenate([a_, r_], axis=0),
                                           jnp.concatenate([b_, k_], axis=0)), at, rt, bt, kt)
    gkv = each(lambda g_, v_: mm.nn(jnp.concatenate([jnp.where(strict, g_[:L, L:], 0.0),
                                                     jnp.where(incl, g_[L:, L:], 0.0)], axis=0), v_),
               gm, vv)
    a_ab = each(lambda g_: jnp.where(strict, g_[:L, :L], 0.0), gm)
    g_b = each(lambda g_: jnp.where(incl, g_[L:, :L], 0.0), gm)
    dp = each(lambda a_: jnp.where(blk, a_, 0.0), a_ab)
    aoff = each(lambda a_: jnp.where(blk, 0.0, a_), a_ab)
    md = each(lambda d_: eye_f + d_, dp)
    for _ in range(3):
        dp = each(lambda d_: mm.nn(d_, d_), dp)
        md = each(lambda m_, d_: m_ + mm.nn(m_, d_), md, dp)
    nm = each(mm.nn, md, aoff)
    nm2 = each(lambda n_: mm.nn(n_, n_), nm)
    x = each(lambda m_, a_, g_: mm.nn(m_, jnp.concatenate([a_, g_[:L]], axis=1)), md, at, gkv)
    x = each(lambda n_, x_: x_ + mm.nn(n_, x_), nm2, x)
    x = each(lambda n_, x_: x_ + mm.nn(n_, x_), nm, x)
    gx = each(mm.nn, g_b, x)
    xtb = each(mm.tn, x, bh)
    vtk = each(mm.tn, vv, kh)
    for i, (ch, h) in enumerate(chains):
        pre_ref[ch, h, 0] = rt[i] + gx[i][:, :HEAD_DIM]
        pre_ref[ch, h, 1] = gx[i][:, HEAD_DIM:] + gkv[i][L:]
        pre_ref[ch, h, 2] = jnp.where(eye, jnp.broadcast_to(dend[i], (L, L)), 0.0) + xtb[i][:HEAD_DIM]
        pre_ref[ch, h, 3] = xtb[i][HEAD_DIM:] + vtk[i]

    st_nt = _dot_nt_hi if precise else (lambda a_, b_: _dg3(a_, b_, ((1,), (1,))))
    st_nn = _dot_hi if precise else (lambda a_, b_: _dg3(a_, b_, ((1,), (0,))))
    seqs = [(bi, h) for bi in range(nb) for h in range(N_HEADS)]
    s = [s_ref[bi, h] for bi, h in seqs]
    for ci_ in range(nc):
        for i, (bi, h) in enumerate(seqs):
            ch = bi * nc + ci_
            ybuf_ref[ch * L:(ch + 1) * L, h * HEAD_DIM:(h + 1) * HEAD_DIM] = (
                st_nt(pre_ref[ch, h, 0], s[i]) + pre_ref[ch, h, 1])
        s = [st_nn(s[i], pre_ref[bi * nc + ci_, h, 2]) + pre_ref[bi * nc + ci_, h, 3]
             for i, (bi, h) in enumerate(seqs)]
    for i, (bi, h) in enumerate(seqs):
        s_ref[bi, h] = s[i]

    yb = ybuf_ref[...]
    mu = segsum(yb)
    yc = yb - mu
    y = yc * lax.rsqrt(segsum(yc * yc) + RWKV_GN_EPS) * lnx_ref[...]
    y_ref[...] = ((y + bonus) * g).astype(y_ref.dtype).reshape(nb, rows, GROUP_W)
    so_ref[...] = s_ref[...]
    sho_ref[...] = z3[:, rows - 1:rows, :]
    zext_ref[:, 7:8, :] = z3[:, rows - 1:rows, :]


def _rwkv_mixer(z, shift, s0, prm, seg, tril, precise):
    b, t, _ = z.shape
    L = CHUNK
    nc = min(t // L, 4)
    nb = 1 if nc > 1 else min(b, 4)
    rows = nc * L
    full = lambda a: pl.BlockSpec(a.shape, lambda i, j: (0,) * a.ndim)
    st_spec = pl.BlockSpec((nb, N_HEADS, HEAD_DIM, HEAD_DIM), lambda i, j: (i, 0, 0, 0))
    sh_spec = pl.BlockSpec((nb, 1, RWKV_COLS), lambda i, j: (i, 0, 0))
    consts = list(prm) + [seg.astype(BF16), tril.astype(BF16)]
    return pl.pallas_call(
        functools.partial(_rwkv_kernel, precise=precise, nb=nb, nc=nc),
        grid=(b // nb, t // rows),
        in_specs=[pl.BlockSpec((nb, rows, RWKV_COLS), lambda i, j: (i, j, 0)), sh_spec, st_spec]
                 + [full(a) for a in consts],
        out_specs=[pl.BlockSpec((nb, rows, GROUP_W), lambda i, j: (i, j, 0)), st_spec, sh_spec],
        out_shape=[jax.ShapeDtypeStruct((b, t, GROUP_W), F32 if precise else BF16),
                   jax.ShapeDtypeStruct((b, N_HEADS, HEAD_DIM, HEAD_DIM), F32),
                   jax.ShapeDtypeStruct((b, 1, RWKV_COLS), F32)],
        scratch_shapes=[pltpu.VMEM((nb, 8 + rows, RWKV_COLS), F32),
                        pltpu.VMEM((nb, N_HEADS, HEAD_DIM, HEAD_DIM), F32),
                        pltpu.VMEM((nb * rows, GROUP_W), F32),
                        pltpu.VMEM((nb * nc, N_HEADS, 4, HEAD_DIM, HEAD_DIM), F32)],
        compiler_params=_cp(("parallel", "arbitrary")),
        name="rwkv_mixer",
    )(z, shift, s0, *consts)


def _ret_kernel(z_ref, cos_ref, sin_ref, dm_ref, kdec_ref, qdec_ref, cdec_ref, gn_ref, seg_ref, s0_ref,
                y_ref, so_ref, s_ref, obuf_ref, *, precise, nb, nc):
    mm = _MM(precise)
    j = pl.program_id(1)
    L = CHUNK
    rows = nc * L

    @pl.when(j == 0)
    def _():
        s_ref[...] = s0_ref[...]

    c = GROUP_W
    lane = lax.broadcasted_iota(jnp.int32, (rows, c), 1)
    first_half = (lane % HEAD_DIM) < (HEAD_DIM // 2)
    cs, sn = cos_ref[...], sin_ref[...]
    kdec, qdec = kdec_ref[...], qdec_ref[...]

    def rope(x):
        partner = jnp.where(first_half, pltpu.roll(x, c - HEAD_DIM // 2, 1), pltpu.roll(x, HEAD_DIM // 2, 1))
        return x * cs + partner * sn

    each = lambda f, *lists: [f(*vals) for vals in zip(*lists)]
    keys = [(bi, ci, h) for bi in range(nb) for ci in range(nc) for h in range(N_HEADS)]
    qs, ks, vs, kds, qds, g_all = [], [], [], [], [], []
    for bi in range(nb):
        q = rope(z_ref[bi, :, 0:c])
        k = rope(z_ref[bi, :, c:2 * c]) * HEAD_DIM ** -0.5
        v = z_ref[bi, :, 2 * c:3 * c]
        g_all.append(z_ref[bi, :, 3 * c:4 * c])
        for ci in range(nc):
            rs = slice(ci * L, (ci + 1) * L)
            kd, qd = k[rs] * kdec, q[rs] * qdec
            for h in range(N_HEADS):
                hs = slice(h * HEAD_DIM, (h + 1) * HEAD_DIM)
                for lst, val in ((qs, q[rs]), (ks, k[rs]), (vs, v[rs]), (kds, kd), (qds, qd)):
                    lst.append(val[:, hs])
    att = [mm.nt(q_, k_) * dm_ref[h] for q_, k_, (_, _, h) in zip(qs, ks, keys)]
    o_in = each(mm.nn, att, vs)
    kv = each(mm.tn, kds, vs)
    s_prev = {}
    for bi in range(nb):
        for h in range(N_HEADS):
            s = s_ref[bi, h]
            for ci in range(nc):
                s_prev[(bi, ci, h)] = s
                s = s * cdec_ref[h] + kv[keys.index((bi, ci, h))]
            s_ref[bi, h] = s
    o_x = [mm.nn(qd_, s_prev[key]) for qd_, key in zip(qds, keys)]
    for i, (bi, ci, h) in enumerate(keys):
        r0 = bi * rows + ci * L
        obuf_ref[r0:r0 + L, h * HEAD_DIM:(h + 1) * HEAD_DIM] = o_in[i] + o_x[i]
    seg = seg_ref[...]
    segsum = lambda x: _dot_pieces(x, seg, 2 if precise else 1, lhs_exact=False)
    ob = obuf_ref[...]
    oc = ob - segsum(ob)
    o = oc * lax.rsqrt(segsum(oc * oc) + RET_GN_EPS) * gn_ref[...]
    g = jnp.concatenate(g_all, axis=0) if nb > 1 else g_all[0]
    y_ref[...] = (g * _sigmoid(g) * o).astype(y_ref.dtype).reshape(nb, rows, c)
    so_ref[...] = s_ref[...]


def _ret_mixer(z, s0, cos, sin, dmask, kdec, qdec, cdec, gn, seg, precise):
    b, t, _ = z.shape
    L = CHUNK
    nc = min(t // L, 4)
    nb = 1 if nc > 1 else min(b, 4)
    rows = nc * L
    full = lambda a: pl.BlockSpec(a.shape, lambda i, j: (0,) * a.ndim)
    st_spec = pl.BlockSpec((nb, N_HEADS, HEAD_DIM, HEAD_DIM), lambda i, j: (i, 0, 0, 0))
    tab_spec = pl.BlockSpec((rows, GROUP_W), lambda i, j: (j, 0))
    seg = seg.astype(BF16)
    return pl.pallas_call(
        functools.partial(_ret_kernel, precise=precise, nb=nb, nc=nc),
        grid=(b // nb, t // rows),
        in_specs=[pl.BlockSpec((nb, rows, 4 * GROUP_W), lambda i, j: (i, j, 0)), tab_spec, tab_spec,
                  full(dmask), full(kdec), full(qdec), full(cdec), full(gn), full(seg), st_spec],
        out_specs=[pl.BlockSpec((nb, rows, GROUP_W), lambda i, j: (i, j, 0)), st_spec],
        out_shape=[jax.ShapeDtypeStruct((b, t, GROUP_W), F32 if precise else BF16),
                   jax.ShapeDtypeStruct((b, N_HEADS, HEAD_DIM, HEAD_DIM), F32)],
        scratch_shapes=[pltpu.VMEM((nb, N_HEADS, HEAD_DIM, HEAD_DIM), F32),
                        pltpu.VMEM((nb * rows, GROUP_W), F32)],
        compiler_params=_cp(("parallel", "arbitrary")),
        name="ret_mixer",
    )(z, cos, sin, dmask, kdec, qdec, cdec, gn, seg, s0)


def _cumsum_lanes(x_ref, o_ref, tri, carry, width):
    blk = 128
    for c0 in range(0, width, blk):
        wd = min(blk, width - c0)
        cs = _dot_hi(x_ref[:, c0:c0 + wd], tri[:wd, :wd]) + carry
        o_ref[:, c0:c0 + wd] = cs
        carry = cs[:, wd - 1:wd]
    return carry


def _fox_gate_kernel(*refs, past, t):
    if past:
        zf_ref, b_ref, tri_ref, lfp_ref, lf_ref, cn_ref, cp_ref = refs
    else:
        zf_ref, b_ref, tri_ref, lf_ref, cn_ref = refs
    tri = tri_ref[...]
    x = zf_ref[...] + b_ref[...]
    lf_ref[...] = jnp.minimum(x, 0.0) - jnp.log(1.0 + jnp.exp(-jnp.abs(x)))
    carry = jnp.zeros((zf_ref.shape[0], 1), F32)
    if past:
        carry = _cumsum_lanes(lfp_ref, cp_ref, tri, carry, past)
    _cumsum_lanes(lf_ref, cn_ref, tri, carry, t)


def _fox_gate(zf_t, bias_rows, tri, lf_past_t):
    rows, t = zf_t.shape
    rt = min(rows, 32)
    past = 0 if lf_past_t is None else lf_past_t.shape[1]
    row_spec = lambda wd: pl.BlockSpec((rt, wd), lambda i: (i, 0))
    in_specs = [row_spec(t), row_spec(1), pl.BlockSpec(tri.shape, lambda i: (0, 0))]
    out_specs = [row_spec(t), row_spec(t)]
    out_shape = [jax.ShapeDtypeStruct((rows, t), F32), jax.ShapeDtypeStruct((rows, t), F32)]
    args = [zf_t, bias_rows, tri]
    if past:
        in_specs.append(row_spec(past))
        out_specs.append(row_spec(past))
        out_shape.append(jax.ShapeDtypeStruct((rows, past), F32))
        args.append(lf_past_t)
    return pl.pallas_call(
        functools.partial(_fox_gate_kernel, past=past, t=t),
        grid=(rows // rt,),
        in_specs=in_specs, out_specs=out_specs, out_shape=out_shape,
        compiler_params=_cp(("parallel",)),
        name="fox_gate",
    )(*args)


def _per_head_lanes(cols, tq):
    lane_head = lax.broadcasted_iota(jnp.int32, (tq, GROUP_W), 1) // HEAD_DIM
    out = jnp.broadcast_to(cols[N_HEADS - 1], (tq, GROUP_W))
    for h in range(N_HEADS - 2, -1, -1):
        out = jnp.where(lane_head == h, cols[h], out)
    return out


def _fox_attn_tile(q_ref, kn_ref, vn_ref, ckn_ref, m_ref, l_ref, acc_ref, *, i, j, tq, mm):
    heads = range(N_HEADS)
    per_head_lanes = lambda cols: _per_head_lanes(cols, tq)

    def block(k_ref, v_ref, ck_ref, diagonal, mm):
        q = q_ref[...] * (HEAD_DIM ** -0.5 * LOG2E)
        k = k_ref[...]
        v = v_ref[...]
        tk = k.shape[0]
        ck = ck_ref[...] * LOG2E
        lane_head = lax.broadcasted_iota(jnp.int32, (tk, GROUP_W), 1) // HEAD_DIM
        kbd = jnp.concatenate([jnp.where(lane_head == h, k, 0.0) for h in heads], axis=0)
        vbd = jnp.concatenate([jnp.where(lane_head == h, v, 0.0) for h in heads], axis=0)
        s_all = mm.nt(q, kbd)
        s = [s_all[:, h * tk:(h + 1) * tk] - ck[h:h + 1, :] for h in heads]
        if diagonal:
            keep = (lax.broadcasted_iota(jnp.int32, (tq, tk), 1)
                    <= lax.broadcasted_iota(jnp.int32, (tq, tk), 0))
            s = [jnp.where(keep, s_h, -jnp.inf) for s_h in s]
        m_prev = [m_ref[h] for h in heads]
        m_new = [jnp.maximum(m_prev[h], jnp.max(s[h], axis=1, keepdims=True)) for h in heads]
        alpha = [jnp.exp2(m_prev[h] - m_new[h]) for h in heads]
        p = [jnp.exp2(s[h] - m_new[h]) for h in heads]
        pv = mm.nn(jnp.concatenate(p, axis=1), vbd)
        for h in heads:
            l_ref[h] = alpha[h] * l_ref[h] + jnp.sum(p[h], axis=1, keepdims=True)
            m_ref[h] = m_new[h]
        acc_ref[...] = per_head_lanes(alpha) * acc_ref[...] + pv

    @pl.when(j < i)
    def _():
        block(kn_ref, vn_ref, ckn_ref, False, mm)

    @pl.when(j == i)
    def _():
        block(kn_ref, vn_ref, ckn_ref, True, mm)


def _fox_attn_kernel(q_ref, kn_ref, vn_ref, og_ref, ckn_ref, y_ref, m_ref, l_ref, acc_ref, *,
                     tq, precise_from):
    i, j = pl.program_id(1), pl.program_id(2)
    nk = pl.num_programs(2)
    refs = (q_ref, kn_ref, vn_ref, ckn_ref, m_ref, l_ref, acc_ref)

    @pl.when(j == 0)
    def _():
        m_ref[...] = jnp.full(m_ref.shape, -jnp.inf, F32)
        l_ref[...] = jnp.zeros(l_ref.shape, F32)
        acc_ref[...] = jnp.zeros(acc_ref.shape, F32)

    if precise_from is None:
        _fox_attn_tile(*refs, i=i, j=j, tq=tq, mm=_MM(False))
    elif precise_from <= 0:
        _fox_attn_tile(*refs, i=i, j=j, tq=tq, mm=_MM(True))
    else:
        @pl.when(i >= precise_from)
        def _():
            _fox_attn_tile(*refs, i=i, j=j, tq=tq, mm=_MM(True))

        @pl.when(i < precise_from)
        def _():
            _fox_attn_tile(*refs, i=i, j=j, tq=tq, mm=_MM(False))

    @pl.when(j == nk - 1)
    def _():
        o = acc_ref[...] / _per_head_lanes([l_ref[h] for h in range(N_HEADS)], tq)
        y_ref[...] = (_sigmoid(og_ref[...]) * o).astype(y_ref.dtype)


def _fox_attn(zfox, ck_new, precise, precise_from):
    b, t, _ = zfox.shape
    tq = min(t, ATTN_TILE)
    nq = t // tq
    jc = lambda i, j: jnp.minimum(j, i)
    in_specs = [pl.BlockSpec((None, tq, GROUP_W), lambda bb, i, j: (bb, i, 0)),
                pl.BlockSpec((None, tq, GROUP_W), lambda bb, i, j: (bb, jc(i, j), 1)),
                pl.BlockSpec((None, tq, GROUP_W), lambda bb, i, j: (bb, jc(i, j), 2)),
                pl.BlockSpec((None, tq, GROUP_W), lambda bb, i, j: (bb, i, 3)),
                pl.BlockSpec((None, N_HEADS, tq), lambda bb, i, j: (bb, 0, jc(i, j)))]
    return pl.pallas_call(
        functools.partial(_fox_attn_kernel, tq=tq, precise_from=precise_from),
        grid=(b, nq, nq),
        in_specs=in_specs,
        out_specs=pl.BlockSpec((None, tq, GROUP_W), lambda bb, i, j: (bb, i, 0)),
        out_shape=jax.ShapeDtypeStruct((b, t, GROUP_W), F32 if precise else BF16),
        scratch_shapes=[pltpu.VMEM((N_HEADS, tq, 1), F32), pltpu.VMEM((N_HEADS, tq, 1), F32),
                        pltpu.VMEM((tq, GROUP_W), F32)],
        compiler_params=_cp(("parallel", "parallel", "arbitrary")),
        name="fox_attn",
    )(zfox, zfox, zfox, zfox, ck_new)


def _fox_past_kernel(zf_ref, ckn_ref, kt_ref, vt_ref, ckp_ref, y_ref, m_ref, l_ref, acc_ref, *,
                     nkp, nb, precise):
    mm = _MM(precise)
    j = pl.program_id(1)
    t = zf_ref.shape[1]
    c = GROUP_W
    chains = [(bi, h) for bi in range(nb) for h in range(N_HEADS)]
    hs = lambda h: slice(h * HEAD_DIM, (h + 1) * HEAD_DIM)

    @pl.when(j == 0)
    def _():
        m_ref[...] = jnp.full(m_ref.shape, -jnp.inf, F32)
        l_ref[...] = jnp.zeros(l_ref.shape, F32)
        acc_ref[...] = jnp.zeros(acc_ref.shape, F32)

    def online_update(s, pv_of):
        m_prev = [m_ref[i] for i in range(len(chains))]
        m_new = [jnp.maximum(mp, jnp.max(s_i, axis=1, keepdims=True)) for mp, s_i in zip(m_prev, s)]
        alpha = [jnp.exp2(mp - mn) for mp, mn in zip(m_prev, m_new)]
        p = [jnp.exp2(s_i - mn) for s_i, mn in zip(s, m_new)]
        pv = pv_of(p)
        for i in range(len(chains)):
            l_ref[i] = alpha[i] * l_ref[i] + jnp.sum(p[i], axis=1, keepdims=True)
            acc_ref[i] = alpha[i] * acc_ref[i] + pv[i]
            m_ref[i] = m_new[i]

    def queries():
        return [zf_ref[bi, :, hs(h)] * (HEAD_DIM ** -0.5 * LOG2E) for bi, h in chains]

    @pl.when(j < nkp)
    def _():
        q = queries()
        s = [mm.nn(q[i], kt_ref[bi, h]) - ckp_ref[bi, h:h + 1, :] * LOG2E
             for i, (bi, h) in enumerate(chains)]
        online_update(s, lambda p: [mm.nt(p[i], vt_ref[bi, h]) for i, (bi, h) in enumerate(chains)])

    @pl.when(j == nkp)
    def _():
        q = queries()
        keep = lax.broadcasted_iota(jnp.int32, (t, t), 1) <= lax.broadcasted_iota(jnp.int32, (t, t), 0)
        s = [jnp.where(keep, mm.nt(q[i], zf_ref[bi, :, c + h * HEAD_DIM:c + (h + 1) * HEAD_DIM])
                       - ckn_ref[bi, h:h + 1, :] * LOG2E, -jnp.inf)
             for i, (bi, h) in enumerate(chains)]
        online_update(s, lambda p: [mm.nn(p[i], zf_ref[bi, :, 2 * c + h * HEAD_DIM:2 * c + (h + 1) * HEAD_DIM])
                                    for i, (bi, h) in enumerate(chains)])
        for i, (bi, h) in enumerate(chains):
            og = zf_ref[bi, :, 3 * c + h * HEAD_DIM:3 * c + (h + 1) * HEAD_DIM]
            y_ref[bi, :, hs(h)] = (_sigmoid(og) * (acc_ref[i] / l_ref[i])).astype(y_ref.dtype)


def _fox_attn_past(zfox, ck_new, kt_all, vt_all, ck_past, layer, precise):
    b, t, _ = zfox.shape
    p = kt_all.shape[-1]
    tkp = 512
    nkp = p // tkp
    nb = min(b, 4)
    jp = lambda j: jnp.minimum(j, nkp - 1)
    cache_spec = pl.BlockSpec((None, nb, N_HEADS, HEAD_DIM, tkp), lambda i, j: (layer, i, 0, 0, jp(j)))
    return pl.pallas_call(
        functools.partial(_fox_past_kernel, nkp=nkp, nb=nb, precise=precise),
        grid=(b // nb, nkp + 1),
        in_specs=[pl.BlockSpec((nb, t, 4 * GROUP_W), lambda i, j: (i, 0, 0)),
                  pl.BlockSpec((nb, N_HEADS, t), lambda i, j: (i, 0, 0)),
                  cache_spec, cache_spec,
                  pl.BlockSpec((nb, N_HEADS, tkp), lambda i, j: (i, 0, jp(j)))],
        out_specs=pl.BlockSpec((nb, t, GROUP_W), lambda i, j: (i, 0, 0)),
        out_shape=jax.ShapeDtypeStruct((b, t, GROUP_W), F32 if precise else BF16),
        scratch_shapes=[pltpu.VMEM((nb * N_HEADS, t, 1), F32), pltpu.VMEM((nb * N_HEADS, t, 1), F32),
                        pltpu.VMEM((nb * N_HEADS, t, HEAD_DIM), F32)],
        compiler_params=_cp(("parallel", "arbitrary")),
        name="fox_attn_past",
    )(zfox, ck_new, kt_all, vt_all, ck_past)


def _out_kernel(x_ref, ya_ref, yb_ref, yc_ref, yd_ref, *refs, precise):
    w_refs, o_ref = refs[:2 if precise else 1], refs[-1]
    dot = lambda a, b: jnp.dot(a, b, preferred_element_type=F32)
    acc = x_ref[...]
    for i, y_ref in enumerate((ya_ref, yb_ref, yc_ref, yd_ref)):
        if precise:
            yh, yl = _split(y_ref[...])
            acc = acc + (dot(yh, w_refs[0][i]) + (dot(yh, w_refs[1][i]) + dot(yl, w_refs[0][i])))
        else:
            acc = acc + dot(y_ref[...].astype(BF16), w_refs[0][i])
    o_ref[...] = acc


def _out_proj(x, ys, ws, into=None, seqs=None):
    n = x.shape[0]
    tm = 1024 if into is None else PRECISE_TAIL
    steps, rows = (n // tm, lambda i: (i, 0)) if into is None else _tail_blocks(n, seqs, tm)
    alias = [] if into is None else [into]
    y_spec = pl.BlockSpec((tm, GROUP_W), rows)
    x_spec = pl.BlockSpec((tm, D_MODEL), rows)
    return pl.pallas_call(
        functools.partial(_out_kernel, precise=len(ws) == 2),
        grid=(steps,),
        in_specs=[x_spec, y_spec, y_spec, y_spec, y_spec]
                 + [pl.BlockSpec(w.shape, lambda i: (0, 0, 0)) for w in ws]
                 + [pl.BlockSpec(memory_space=pl.ANY) for _ in alias],
        out_specs=x_spec,
        out_shape=jax.ShapeDtypeStruct((n, D_MODEL), F32),
        input_output_aliases={5 + len(ws): 0} if alias else {},
        compiler_params=_cp(("parallel",)),
        name="out_proj",
    )(x, *ys, *ws, *alias)


def _router_kernel(x_ref, g_ref, w_ref, b_ref, tril_ref, xs_ref, gs_ref, dest_ref, meta_ref):
    h = _rmsnorm(x_ref[...], g_ref[...])
    logits = _dot_hi(h, w_ref[...]) + b_ref[...]
    tm = logits.shape[0]
    lane_i = lax.broadcasted_iota(jnp.int32, (tm, ROUTER_PAD), 1)
    lane = lane_i.astype(F32)
    big = float(ROUTER_PAD)
    rmax = lambda a: jnp.max(a, axis=1, keepdims=True)
    rsum = lambda a: jnp.sum(a, axis=1, keepdims=True)
    first = lambda m: jnp.min(jnp.where(m, lane, big), axis=1, keepdims=True)

    is_c = jnp.logical_and(lane_i >= N_EXPERTS, lane_i < N_EXPERTS + N_GROUPS)
    lc = jnp.where(is_c, logits, -jnp.inf)
    ec = jnp.exp(lc - rmax(lc))
    pc_all = ec / rsum(ec)
    pc = rmax(pc_all)
    gi = first(jnp.logical_and(is_c, pc_all == pc)) - float(N_EXPERTS)

    is_f = (lane_i // EXPERTS_PER_GROUP).astype(F32) == gi
    lf = jnp.where(is_f, logits, -jnp.inf)
    ef = jnp.exp(lf - rmax(lf))
    pf = jnp.where(is_f, ef / rsum(ef), -1.0)
    t1 = rmax(pf)
    i1 = first(pf == t1)
    pf2 = jnp.where(lane == i1, -1.0, pf)
    t2 = rmax(pf2)
    i2 = first(pf2 == t2)
    den = t1 + t2
    gate = jnp.where(lane == i1, pc * (t1 / den), jnp.where(lane == i2, pc * (t2 / den), 0.0))

    onehot = jnp.where(lane == gi, 1.0, 0.0)
    rank = jnp.dot(tril_ref[...], onehot.astype(BF16), preferred_element_type=F32)
    cnt = jnp.sum(onehot, axis=0, keepdims=True)
    padded = jnp.floor((cnt + (MOE_PAD - 1.0)) * (1.0 / MOE_PAD)) * MOE_PAD
    lane1 = lax.broadcasted_iota(jnp.int32, (1, ROUTER_PAD), 1)
    off = jnp.zeros((1, ROUTER_PAD), F32)
    meta = jnp.zeros((1, ROUTER_PAD), F32)
    start = jnp.zeros((1, 1), F32)
    for g in range(N_GROUPS):
        size_g = jnp.sum(jnp.where(lane1 == g, padded, 0.0), axis=1, keepdims=True)
        off = off + jnp.where(lane1 == g, start, 0.0)
        meta = meta + jnp.where(lane1 == g, start, 0.0) + jnp.where(lane1 == N_GROUPS + g, size_g, 0.0)
        start = start + size_g
    dest = rsum(onehot * (off + rank))
    rows = lax.broadcasted_iota(jnp.int32, (tm, MOE_ROWS), 1).astype(F32)
    perm_t = jnp.where(rows == dest, 1.0, 0.0).astype(BF16)
    rhs = jnp.concatenate([h.astype(BF16)] + _pieces(gate, 3), axis=1)
    srt = lax.dot_general(perm_t, rhs, (((0,), (0,)), ((), ())), preferred_element_type=F32)
    xs_ref[...] = srt[:, :D_MODEL].astype(BF16)
    gs_ref[...] = (srt[:, D_MODEL:D_MODEL + ROUTER_PAD] + srt[:, D_MODEL + ROUTER_PAD:D_MODEL + 2 * ROUTER_PAD]
                   + srt[:, D_MODEL + 2 * ROUTER_PAD:])
    dest_ref[...] = dest
    meta_ref[...] = meta


def _router(x, g, wr, br, tril):
    n = x.shape[0]
    tm = MOE_TILE
    nt = n // tm
    return pl.pallas_call(
        _router_kernel,
        grid=(nt,),
        in_specs=[pl.BlockSpec((tm, D_MODEL), lambda i: (i, 0)),
                  pl.BlockSpec((1, D_MODEL), lambda i: (0, 0)),
                  pl.BlockSpec((D_MODEL, ROUTER_PAD), lambda i: (0, 0)),
                  pl.BlockSpec((1, ROUTER_PAD), lambda i: (0, 0)),
                  pl.BlockSpec((tm, tm), lambda i: (0, 0))],
        out_specs=[pl.BlockSpec((MOE_ROWS, D_MODEL), lambda i: (i, 0)),
                   pl.BlockSpec((MOE_ROWS, ROUTER_PAD), lambda i: (i, 0)),
                   pl.BlockSpec((tm, 1), lambda i: (i, 0)),
                   pl.BlockSpec((None, 1, ROUTER_PAD), lambda i: (i, 0, 0))],
        out_shape=[jax.ShapeDtypeStruct((nt * MOE_ROWS, D_MODEL), BF16),
                   jax.ShapeDtypeStruct((nt * MOE_ROWS, ROUTER_PAD), F32),
                   jax.ShapeDtypeStruct((n, 1), F32),
                   jax.ShapeDtypeStruct((nt, 1, ROUTER_PAD), F32)],
        compiler_params=_cp(("parallel",)),
        name="moe_router",
    )(x, g, wr, br, tril)


def _moe_kernel(meta_ref, xs_ref, gs_ref, w1_ref, w3_ref, w2_ref, o_ref, *, tps):
    i, e = pl.program_id(0), pl.program_id(1)

    @pl.when(e == 0)
    def _():
        o_ref[...] = jnp.zeros(o_ref.shape, F32)

    g = e // EXPERTS_PER_GROUP
    for sub in range(tps):
        tile = i * tps + sub
        off = meta_ref[tile, g] + sub * MOE_ROWS
        size = meta_ref[tile, N_GROUPS + g]
        for w in range(MOE_TILE // MOE_WIN + 1):
            @pl.when(w * MOE_WIN < size)
            def _():
                rows = pl.ds(pl.multiple_of(off + w * MOE_WIN, MOE_PAD), MOE_WIN)
                xw = xs_ref[rows, :]
                a = jnp.dot(xw, w1_ref[...], preferred_element_type=F32)
                b = jnp.dot(xw, w3_ref[...], preferred_element_type=F32)
                he = (a * _sigmoid(a) * b).astype(BF16)
                ye = jnp.dot(he, w2_ref[...], preferred_element_type=F32)
                gw = gs_ref[rows, :]
                lane = lax.broadcasted_iota(jnp.int32, gw.shape, 1)
                gcol = jnp.sum(jnp.where(lane == e, gw, 0.0), axis=1, keepdims=True)
                o_ref[rows, :] += gcol * ye


def _moe(meta, xs, gs, w1, w3, w2, layer):
    nt = meta.shape[0]
    tps = MOE_TILES_PER_STEP if nt % MOE_TILES_PER_STEP == 0 else 1
    rows = tps * MOE_ROWS
    return pl.pallas_call(
        functools.partial(_moe_kernel, tps=tps),
        grid_spec=pltpu.PrefetchScalarGridSpec(
            num_scalar_prefetch=1,
            grid=(nt // tps, N_EXPERTS),
            in_specs=[pl.BlockSpec((rows, D_MODEL), lambda i, e, m: (i, 0)),
                      pl.BlockSpec((rows, ROUTER_PAD), lambda i, e, m: (i, 0)),
                      pl.BlockSpec((None, None, D_MODEL, D_EXPERT), lambda i, e, m: (layer, e, 0, 0)),
                      pl.BlockSpec((None, None, D_MODEL, D_EXPERT), lambda i, e, m: (layer, e, 0, 0)),
                      pl.BlockSpec((None, None, D_EXPERT, D_MODEL), lambda i, e, m: (layer, e, 0, 0))],
            out_specs=pl.BlockSpec((rows, D_MODEL), lambda i, e, m: (i, 0))),
        out_shape=jax.ShapeDtypeStruct((nt * MOE_ROWS, D_MODEL), F32),
        compiler_params=pltpu.CompilerParams(dimension_semantics=("parallel", "arbitrary"),
                                             vmem_limit_bytes=MOE_VMEM_LIMIT),
        name="moe_experts",
    )(meta, xs, gs, w1, w3, w2)


def _combine_kernel(x_ref, os_ref, dest_ref, nf_ref, o_ref, *, final):
    tm = x_ref.shape[0]
    rows = lax.broadcasted_iota(jnp.int32, (tm, MOE_ROWS), 1).astype(F32)
    perm_t = jnp.where(rows == dest_ref[...], 1.0, 0.0).astype(BF16)
    y = x_ref[...] + _dot_pieces(perm_t, os_ref[...], 2, lhs_exact=True)
    o_ref[...] = _rmsnorm(y, nf_ref[...]) if final else y


def _moe_combine(x, os, dest, nf, final):
    n = x.shape[0]
    tm = MOE_TILE
    return pl.pallas_call(
        functools.partial(_combine_kernel, final=final),
        grid=(n // tm,),
        in_specs=[pl.BlockSpec((tm, D_MODEL), lambda i: (i, 0)),
                  pl.BlockSpec((MOE_ROWS, D_MODEL), lambda i: (i, 0)),
                  pl.BlockSpec((tm, 1), lambda i: (i, 0)),
                  pl.BlockSpec((1, D_MODEL), lambda i: (0, 0))],
        out_specs=pl.BlockSpec((tm, D_MODEL), lambda i: (i, 0)),
        out_shape=jax.ShapeDtypeStruct((n, D_MODEL), F32),
        compiler_params=_cp(("parallel",)),
        name="moe_combine",
    )(x, os, dest, nf)


def _prep_layer(l, p):
    w_in = p["w_in"][l]
    o_rwkv, o_ret, o_fox = GROUP_W, GROUP_W + RWKV_COLS, GROUP_W + RWKV_COLS + 4 * GROUP_W
    ff = jnp.pad(w_in[:, o_fox + 4 * GROUP_W:], ((0, 0), (0, FF_PAD - N_HEADS)))
    w_in_r = jnp.concatenate([w_in[:, o_rwkv:o_ret], w_in[:, o_ret:o_fox],
                              w_in[:, o_fox:o_fox + 4 * GROUP_W], w_in[:, :GROUP_W], ff], axis=1)
    pw = p["pool_w"][l]
    pg = GROUP_W // len(POOL_WINDOWS)
    w_bd = jnp.zeros((GROUP_W, GROUP_W), F32)
    for gi in range(len(POOL_WINDOWS)):
        w_bd = w_bd.at[gi * pg:(gi + 1) * pg, gi * pg:(gi + 1) * pg].set(pw[gi])
    row = lambda a: a.reshape(1, -1)
    rwkv = (row(p["rwkv_mu"][l]), row(p["rwkv_w0"][l]), p["rwkv_w2"][l], row(p["rwkv_a0"][l]),
            p["rwkv_a2"][l], p["rwkv_g2"][l], row(p["rwkv_kk"][l]), row(p["rwkv_ka"][l]),
            row(p["rwkv_rk"][l]), row(p["rwkv_lnx"][l]))
    wf = jnp.transpose(p["moe_wf"][l], (1, 0, 2)).reshape(D_MODEL, N_EXPERTS)
    wr = jnp.pad(jnp.concatenate([wf, p["moe_wc"][l]], axis=1),
                 ((0, 0), (0, ROUTER_PAD - N_EXPERTS - N_GROUPS)))
    br = jnp.pad(jnp.concatenate([p["moe_bf"][l].reshape(-1), p["moe_bc"][l]]),
                 (0, ROUTER_PAD - N_EXPERTS - N_GROUPS)).reshape(1, ROUTER_PAD)
    w_out4 = p["w_out"][l].reshape(4, GROUP_W, D_MODEL)
    precise = l < p["w_in"].shape[0] - 1
    split = lambda w: _split_bits(w) if precise else (w.astype(BF16),)
    return dict(
        precise=precise, n1=row(p["norm1_g"][l]), w_in=split(w_in_r), pool_w=w_bd,
        w_in_single=(w_in_r.astype(BF16),), w_out_single=(w_out4.astype(BF16),),
        pool_scale=row(p["pool_scale"][l]),
        rwkv=rwkv, ret_gn=row(p["ret_gn"][l]), fox_bf=p["fox_bf"][l],
        w_out=split(w_out4), n2=row(p["norm2_g"][l]),
        wr=wr, br=br, layer=l,
        w1=p["moe_w1"].astype(BF16), w3=p["moe_w3"].astype(BF16), w2=p["moe_w2"].astype(BF16))


def _tables(t, pos0):
    half = HEAD_DIM // 2
    inv = ROPE_BASE ** (-np.arange(half, dtype=np.float64) / half)
    ang = (pos0 + np.arange(t, dtype=np.float64))[:, None] * inv[None, :]
    cos, sin = np.cos(ang), np.sin(ang)
    cos_t = np.tile(np.concatenate([cos, cos], axis=1), (1, N_HEADS))
    sin_t = np.tile(np.concatenate([-sin, sin], axis=1), (1, N_HEADS))
    L = CHUNK
    log_g = np.log(np.array(RET_GAMMA, np.float64))
    idx = np.arange(L, dtype=np.float64)
    dmask = np.exp(log_g[:, None, None] * np.abs(idx[:, None] - idx[None, :]))
    lanes = lambda a: np.repeat(a, HEAD_DIM, axis=1)
    kdec = lanes(np.exp(log_g[None, :] * (L - 1.0 - idx)[:, None]))
    qdec = lanes(np.exp(log_g[None, :] * (idx + 1.0)[:, None]))
    cdec = np.broadcast_to(np.exp(log_g * L)[:, None, None], (N_HEADS, HEAD_DIM, HEAD_DIM))
    hid = np.arange(GROUP_W) // HEAD_DIM
    seg = np.where(hid[:, None] == hid[None, :], 1.0 / HEAD_DIM, 0.0)
    tril = np.tril(np.ones((L, L)))
    triu = np.triu(np.ones((128, 128)))
    tril_moe = np.tril(np.ones((MOE_TILE, MOE_TILE)), -1)
    f32 = lambda a: jnp.asarray(np.asarray(a, np.float32))
    return dict(cos=f32(cos_t), sin=f32(sin_t), dmask=f32(dmask), kdec=f32(kdec), qdec=f32(qdec),
                cdec=f32(cdec), seg=f32(seg), tril=f32(tril), triu=f32(triu),
                tril_moe=f32(tril_moe).astype(BF16))


def _trunk(x, pos0, states, layers, norm_f, cache=None):
    b, t, _ = x.shape
    n = b * t
    tb = _tables(t, pos0)
    xf = x.reshape(n, D_MODEL)
    nf = norm_f.reshape(1, D_MODEL)
    outs = [[] for _ in range(7)]
    for l, lp in enumerate(layers):
        pool_buf, shift, wkv, ret, lf_past = states[l]
        precise = lp["precise"]
        tail = precise and t > PRECISE_TAIL and (t - PRECISE_TAIL) % ATTN_TILE == 0
        if tail:
            zs = _in_proj(xf, lp["n1"], lp["w_in_single"])
            z_rwkv, z_ret, z_fox, z_pool, z_ff = _in_proj(xf, lp["n1"], lp["w_in"], into=zs, seqs=b)
        else:
            z_rwkv, z_ret, z_fox, z_pool, z_ff = _in_proj(xf, lp["n1"], lp["w_in"])
        buf16 = jnp.pad(pool_buf, ((0, 0), (POOL_PAD - POOL_BUF, 0), (0, 0)))
        y_a, pool_new = _pool_mixer(z_pool.reshape(b, t, GROUP_W), buf16, lp["pool_w"], lp["pool_scale"], pos0,
                                    precise)
        y_b, wkv_new, shift_new = _rwkv_mixer(z_rwkv.reshape(b, t, RWKV_COLS), shift, wkv, lp["rwkv"],
                                              tb["seg"], tb["tril"], precise)
        y_c, ret_new = _ret_mixer(z_ret.reshape(b, t, 4 * GROUP_W), ret, tb["cos"], tb["sin"], tb["dmask"],
                                  tb["kdec"], tb["qdec"], tb["cdec"], lp["ret_gn"], tb["seg"], precise)
        zf_t = jnp.transpose(z_ff[:, :N_HEADS].reshape(b, t, N_HEADS), (0, 2, 1)).reshape(b * N_HEADS, t)
        bias_rows = jnp.tile(lp["fox_bf"], b).reshape(b * N_HEADS, 1)
        zfox3 = z_fox.reshape(b, t, 4 * GROUP_W)
        if cache is None:
            lf_t, c_new = _fox_gate(zf_t, bias_rows, tb["triu"], None)
            cn3 = c_new.reshape(b, N_HEADS, t)
            first_precise = ((t - PRECISE_TAIL) // ATTN_TILE if tail else 0) if precise else None
            y_d = _fox_attn(zfox3, cn3, precise, first_precise)
        else:
            p = lf_past.shape[1]
            lfp_t = jnp.transpose(lf_past, (0, 2, 1)).reshape(b * N_HEADS, p)
            lf_t, c_new, c_past = _fox_gate(zf_t, bias_rows, tb["triu"], lfp_t)
            cn3 = c_new.reshape(b, N_HEADS, t)
            y_d = _fox_attn_past(zfox3, cn3, cache[0], cache[1], c_past.reshape(b, N_HEADS, p), l, precise)
        ys4 = [y.reshape(n, GROUP_W) for y in (y_a, y_b, y_c, y_d)]
        if tail:
            x1 = _out_proj(xf, ys4, lp["w_out"], into=_out_proj(xf, ys4, lp["w_out_single"]), seqs=b)
        else:
            x1 = _out_proj(xf, ys4, lp["w_out"])
        xs, gs, dest, meta = _router(x1, lp["n2"], lp["wr"], lp["br"], tb["tril_moe"])
        meta = meta[:, 0, :2 * N_GROUPS].astype(jnp.int32)
        ys = _moe(meta, xs, gs, lp["w1"], lp["w3"], lp["w2"], lp["layer"])
        xf = _moe_combine(x1, ys, dest, nf, final=(l == len(layers) - 1))
        k_new = zfox3[:, :, GROUP_W:2 * GROUP_W].reshape(b, t, N_HEADS, HEAD_DIM)
        v_new = zfox3[:, :, 2 * GROUP_W:3 * GROUP_W].reshape(b, t, N_HEADS, HEAD_DIM)
        lf_new = jnp.transpose(lf_t.reshape(b, N_HEADS, t), (0, 2, 1))
        for lst, s in zip(outs, (pool_new, shift_new, wkv_new, ret_new, k_new, v_new, lf_new)):
            lst.append(s)
    return xf.reshape(b, t, D_MODEL), [jnp.stack(lst) for lst in outs]


def kernel(x_prompt, x_sample, state_pool, state_shift, state_wkv, state_ret, cache_fox_k, cache_fox_v, cache_fox_logf, norm1_g, w_in, pool_w, pool_scale, rwkv_mu, rwkv_w0, rwkv_w2, rwkv_a0, rwkv_a2, rwkv_g2, rwkv_kk, rwkv_ka, rwkv_rk, rwkv_lnx, ret_gn, fox_bf, w_out, norm2_g, moe_wc, moe_bc, moe_wf, moe_bf, moe_w1, moe_w3, moe_w2, norm_f):
    p = dict(norm1_g=norm1_g, w_in=w_in, pool_w=pool_w, pool_scale=pool_scale, rwkv_mu=rwkv_mu,
             rwkv_w0=rwkv_w0, rwkv_w2=rwkv_w2, rwkv_a0=rwkv_a0, rwkv_a2=rwkv_a2, rwkv_g2=rwkv_g2,
             rwkv_kk=rwkv_kk, rwkv_ka=rwkv_ka, rwkv_rk=rwkv_rk, rwkv_lnx=rwkv_lnx, ret_gn=ret_gn,
             fox_bf=fox_bf, w_out=w_out, norm2_g=norm2_g, moe_wc=moe_wc, moe_bc=moe_bc, moe_wf=moe_wf,
             moe_bf=moe_bf, moe_w1=moe_w1, moe_w3=moe_w3, moe_w2=moe_w2)
    depth = w_in.shape[0]
    layers = [_prep_layer(l, p) for l in range(depth)]
    b = x_prompt.shape[0]
    dt = x_prompt.dtype
    prompt_init = [(jnp.zeros((b, POOL_BUF, GROUP_W), dt), jnp.zeros((b, 1, RWKV_COLS), dt),
                    jnp.zeros((b, N_HEADS, HEAD_DIM, HEAD_DIM), dt),
                    jnp.zeros((b, N_HEADS, HEAD_DIM, HEAD_DIM), dt), None)
                   for _ in range(depth)]
    sample_init = [(state_pool[l], state_shift[l], state_wkv[l], state_ret[l], cache_fox_logf[l])
                   for l in range(depth)]
    past = cache_fox_k.shape[2]
    cache = (jnp.transpose(cache_fox_k, (0, 1, 3, 4, 2)), jnp.transpose(cache_fox_v, (0, 1, 3, 4, 2)))
    y_prompt, new_p = _trunk(x_prompt, 0, prompt_init, layers, norm_f)
    y_sample, new_s = _trunk(x_sample, past, sample_init, layers, norm_f, cache)
    return (y_prompt, y_sample, *new_p, *new_s)
```

```python
import functools

import jax
import jax.numpy as jnp
import numpy as np
from jax import lax
from jax.experimental import pallas as pl
from jax.experimental.pallas import tpu as pltpu

F32 = jnp.float32
BF16 = jnp.bfloat16
HIGHEST = lax.Precision.HIGHEST

D_MODEL = 1024
DEPTH = 2
CHUNK = 64
GROUP_W = 256
HEAD_DIM = 64
N_HEADS = 4
POOL_WINDOWS = (2, 4, 8, 16)
POOL_BUF = 15
POOL_PAD = 16
RWKV_COLS = 1024
N_IN = 3332
FF_PAD = 128
RET_GAMMA = tuple(1.0 - 2.0 ** (-5 - h) for h in range(N_HEADS))
ROPE_BASE = 10000.0
N_GROUPS = 4
EXPERTS_PER_GROUP = 4
N_EXPERTS = 16
D_EXPERT = 512
LOG2E = 1.4426950408889634
RMS_EPS = 1e-6
RWKV_GN_EPS = 64e-5
RET_GN_EPS = 1e-5
SUB = 16
ROUTER_PAD = 128
MOE_TILE = 1024
MOE_PAD = 64
MOE_WIN = 320
MOE_ROWS = MOE_TILE + N_GROUPS * MOE_PAD + (MOE_WIN - MOE_PAD)
MOE_TILES_PER_STEP = 2
ATTN_TILE = 512
PRECISE_TAIL = 512
VMEM_LIMIT = 48 * 1024 * 1024
MOE_VMEM_LIMIT = 56 * 1024 * 1024


def _cp(sem):
    return pltpu.CompilerParams(dimension_semantics=sem, vmem_limit_bytes=VMEM_LIMIT)


def _dot(a, b):
    return jnp.dot(a.astype(BF16), b.astype(BF16), preferred_element_type=F32)


def _dot_nt(a, b):
    return lax.dot_general(a.astype(BF16), b.astype(BF16), (((1,), (1,)), ((), ())),
                           preferred_element_type=F32)


def _dot_tn(a, b):
    return lax.dot_general(a.astype(BF16), b.astype(BF16), (((0,), (0,)), ((), ())),
                           preferred_element_type=F32)


def _split(a):
    hi = a.astype(BF16)
    return hi, (a - hi.astype(F32)).astype(BF16)


def _split_bits(w):
    bits = lax.bitcast_convert_type(w, jnp.uint32) & jnp.uint32(0xFFFF0000)
    hi = lax.bitcast_convert_type(bits, F32)
    return hi.astype(BF16), (w - hi).astype(BF16)


def _dg3(a, b, dims):
    ah, al = _split(a)
    bh, bl = _split(b)
    dg = lambda x, y: lax.dot_general(x, y, (dims, ((), ())), preferred_element_type=F32)
    return dg(ah, bh) + (dg(ah, bl) + dg(al, bh))


class _MM:
    def __init__(self, precise):
        if precise:
            self.nn = lambda a, b: _dg3(a, b, ((1,), (0,)))
            self.nt = lambda a, b: _dg3(a, b, ((1,), (1,)))
            self.tn = lambda a, b: _dg3(a, b, ((0,), (0,)))
        else:
            self.nn, self.nt, self.tn = _dot, _dot_nt, _dot_tn


def _pieces(a, n):
    out = []
    for i in range(n):
        p = a.astype(BF16)
        out.append(p)
        if i + 1 < n:
            a = a - p.astype(F32)
    return out


def _dot_pieces(a, b, n, lhs_exact):
    dot = lambda x, y: jnp.dot(x, y, preferred_element_type=F32)
    terms = [dot(a, p) for p in _pieces(b, n)] if lhs_exact else [dot(p, b) for p in _pieces(a, n)]
    acc = terms[-1]
    for t in terms[-2::-1]:
        acc = acc + t
    return acc


def _dot_hi(a, b):
    return jnp.dot(a, b, precision=HIGHEST, preferred_element_type=F32)


def _dot_nt_hi(a, b):
    return lax.dot_general(a, b, (((1,), (1,)), ((), ())), precision=HIGHEST,
                           preferred_element_type=F32)


def _sigmoid(x):
    return 1.0 / (1.0 + jnp.exp(-x))


def _softplus(x):
    return jnp.maximum(x, 0.0) + jnp.log(1.0 + jnp.exp(-jnp.abs(x)))


def _rmsnorm(x, g):
    return x * lax.rsqrt(jnp.mean(x * x, axis=-1, keepdims=True) + RMS_EPS) * g


_IN_SPLITS = ((0, 1024), (1024, 2048), (2048, 3072), (3072, 3328), (3328, 3456))


def _in_kernel(x_ref, g_ref, *refs, precise):
    nw = 2 if precise else 1
    w_refs, o_refs = refs[:nw], refs[len(refs) - len(_IN_SPLITS):]
    h = _rmsnorm(x_ref[...], g_ref[...])
    tm = h.shape[0]
    dot = lambda a, b: lax.dot_general(a, b, (((1,), (1,)), ((), ())), preferred_element_type=F32)
    if precise:
        hh, hl = _split(h)
        lhs = jnp.concatenate([hh, hl], axis=0)
    else:
        lhs = h.astype(BF16)
    for o_ref, (c0, c1) in zip(o_refs, _IN_SPLITS):
        acc = dot(lhs, w_refs[0][c0:c1, :])
        if precise:
            acc = acc[:tm] + (acc[tm:] + dot(hh, w_refs[1][c0:c1, :]))
        o_ref[...] = acc


def _tail_blocks(n, seqs, tm):
    assert PRECISE_TAIL == tm
    per_seq = n // seqs // tm
    return seqs, (lambda i: (i * per_seq + per_seq - 1, 0))


def _in_proj(x, g, ws, into=None, seqs=None):
    n = x.shape[0]
    tm = 512
    ncol = ws[0].shape[0]
    widths = [c1 - c0 for c0, c1 in _IN_SPLITS]
    steps, rows = (n // tm, lambda i: (i, 0)) if into is None else _tail_blocks(n, seqs, tm)
    alias = [] if into is None else list(into)
    return pl.pallas_call(
        functools.partial(_in_kernel, precise=len(ws) == 2),
        grid=(steps,),
        in_specs=[pl.BlockSpec((tm, D_MODEL), rows),
                  pl.BlockSpec((1, D_MODEL), lambda i: (0, 0))]
                 + [pl.BlockSpec((ncol, D_MODEL), lambda i: (0, 0), pipeline_mode=pl.Buffered(1)) for _ in ws]
                 + [pl.BlockSpec(memory_space=pl.ANY) for _ in alias],
        out_specs=[pl.BlockSpec((tm, wd), rows) for wd in widths],
        out_shape=[jax.ShapeDtypeStruct((n, wd), F32) for wd in widths],
        input_output_aliases={2 + len(ws) + k: k for k in range(len(alias))},
        compiler_params=_cp(("parallel",)),
        name="in_proj",
    )(x, g, *ws, *alias)


def _pool_kernel(u_ref, buf_ref, w_ref, sc_ref, y_ref, nb_ref, ext_ref, *, tt, pos0, precise):
    mm = _MM(precise)
    j = pl.program_id(1)

    @pl.when(j == 0)
    def _():
        ext_ref[0:POOL_PAD, :] = buf_ref[...]

    u = u_ref[...]
    ext_ref[POOL_PAD:POOL_PAD + tt, :] = u
    acc = u
    sums = []
    for k in range(1, POOL_PAD):
        acc = acc + ext_ref[POOL_PAD - k:POOL_PAD - k + tt, :]
        if k + 1 in POOL_WINDOWS:
            sums.append(acc)
    lane = lax.broadcasted_iota(jnp.int32, (tt, GROUP_W), 1)
    grp = lane // (GROUP_W // len(POOL_WINDOWS))
    pos = (pos0 + j * tt + lax.broadcasted_iota(jnp.int32, (tt, GROUP_W), 0)).astype(F32)
    mean = jnp.zeros((tt, GROUP_W), F32)
    for gi, w in enumerate(POOL_WINDOWS):
        cnt = jnp.minimum(float(w), pos + 1.0)
        mean = jnp.where(grp == gi, sums[gi] / cnt, mean)
    d = mean - u
    y_ref[...] = (mm.nn(d, w_ref[...]) * sc_ref[...]).astype(y_ref.dtype)
    nb_ref[...] = ext_ref[tt + 1:tt + POOL_PAD, :]
    ext_ref[0:POOL_PAD, :] = ext_ref[tt:tt + POOL_PAD, :]


def _pool_mixer(u, buf16, w_bd, scale, pos0, precise):
    b, t, _ = u.shape
    tt = min(t, 512)
    return pl.pallas_call(
        functools.partial(_pool_kernel, tt=tt, pos0=pos0, precise=precise),
        grid=(b, t // tt),
        in_specs=[pl.BlockSpec((None, tt, GROUP_W), lambda i, j: (i, j, 0)),
                  pl.BlockSpec((None, POOL_PAD, GROUP_W), lambda i, j: (i, 0, 0)),
                  pl.BlockSpec((GROUP_W, GROUP_W), lambda i, j: (0, 0)),
                  pl.BlockSpec((1, GROUP_W), lambda i, j: (0, 0))],
        out_specs=[pl.BlockSpec((None, tt, GROUP_W), lambda i, j: (i, j, 0)),
                   pl.BlockSpec((None, POOL_BUF, GROUP_W), lambda i, j: (i, 0, 0))],
        out_shape=[jax.ShapeDtypeStruct((b, t, GROUP_W), F32 if precise else BF16),
                   jax.ShapeDtypeStruct((b, POOL_BUF, GROUP_W), F32)],
        scratch_shapes=[pltpu.VMEM((POOL_PAD + tt, GROUP_W), F32)],
        compiler_params=_cp(("parallel", "arbitrary")),
        name="pool_mixer",
    )(u, buf16, w_bd, scale)


def _rwkv_kernel(z_ref, sh_ref, s0_ref, mu_ref, w0_ref, w2_ref, a0_ref, a2_ref, g2_ref, kk_ref,
                 ka_ref, rk_ref, lnx_ref, seg_ref, tril_ref,
                 y_ref, so_ref, sho_ref, zext_ref, s_ref, ybuf_ref, pre_ref, *, precise, nb, nc):
    mm = _MM(precise)
    lora = _dot_hi if precise else _dot
    j = pl.program_id(1)
    L = CHUNK
    rows = nc * L
    n = nb * rows

    @pl.when(j == 0)
    def _():
        zext_ref[:, 7:8, :] = sh_ref[...]
        s_ref[...] = s0_ref[...]

    z3 = z_ref[...]
    zext_ref[:, 8:8 + rows, :] = z3
    prev3 = zext_ref[:, 7:7 + rows, :]
    zs = (z3 + (prev3 - z3) * mu_ref[...]).reshape(n, RWKV_COLS)
    c = GROUP_W
    r, k, v = zs[:, 0:c], zs[:, c:2 * c], zs[:, 2 * c:3 * c]
    wl, al, gl = zs[:, 768:832], zs[:, 832:896], zs[:, 896:1024]
    w = -_softplus(-(w0_ref[...] + lora(jnp.tanh(wl), w2_ref[...]))) - 0.5
    lw = -jnp.exp(w)
    a = _sigmoid(a0_ref[...] + lora(al, a2_ref[...]))
    g = lora(_sigmoid(gl), g2_ref[...])
    seg = seg_ref[...]
    segsum = lambda x: _dot_pieces(x, seg, 2 if precise else 1, lhs_exact=False)
    kk = k * kk_ref[...]
    kk = kk / jnp.maximum(jnp.sqrt(segsum(kk * kk) * float(HEAD_DIM)), 1e-12)
    k2 = k * (1.0 + (a - 1.0) * ka_ref[...])
    bonus = segsum(r * k2 * rk_ref[...]) * float(HEAD_DIM) * v
    b = kk * a

    ri = lax.broadcasted_iota(jnp.int32, (L, L), 0)
    ci = lax.broadcasted_iota(jnp.int32, (L, L), 1)
    strict, incl, eye = ri > ci, ri >= ci, ri == ci
    blk = (ri // SUB) == (ci // SUB)
    eye_f = jnp.where(eye, 1.0, 0.0)
    tril = tril_ref[...]

    nch = nb * nc
    chains = [(ch, h) for ch in range(nch) for h in range(N_HEADS)]
    each = lambda f, *lists: [f(*vals) for vals in zip(*lists)]
    at, rt, bt, kt, bh, kh, dend, vv = [], [], [], [], [], [], [], []
    for ch in range(nch):
        rs = slice(ch * L, (ch + 1) * L)
        lwc = lw[rs]
        cum = _dot_pieces(tril, lwc, 3 if precise else 2, lhs_exact=True)
        pend = cum[L - 1:L, :]
        e_neg = jnp.exp(-cum)
        e_end = jnp.exp(pend - cum)
        at_c, rt_c = -kk[rs] * jnp.exp(cum - lwc), r[rs] * jnp.exp(cum)
        bt_c, kt_c, bh_c, kh_c = b[rs] * e_neg, k2[rs] * e_neg, b[rs] * e_end, k2[rs] * e_end
        dend_c, v_c = jnp.exp(pend), v[rs]
        for h in range(N_HEADS):
            hs = slice(h * HEAD_DIM, (h + 1) * HEAD_DIM)
            for lst, val in ((at, at_c), (rt, rt_c), (bt, bt_c), (kt, kt_c), (bh, bh_c), (kh, kh_c),
                             (dend, dend_c), (vv, v_c)):
                lst.append(val[:, hs])
    gm = each(lambda a_, r_, b_, k_: mm.nt(jnp.concatenate([a_, r_], axis=0),
                                           jnp.concatenate([b_, k_], axis=0)), at, rt, bt, kt)
    gkv = each(lambda g_, v_: mm.nn(jnp.concatenate([jnp.where(strict, g_[:L, L:], 0.0),
                                                     jnp.where(incl, g_[L:, L:], 0.0)], axis=0), v_),
               gm, vv)
    a_ab = each(lambda g_: jnp.where(strict, g_[:L, :L], 0.0), gm)
    g_b = each(lambda g_: jnp.where(incl, g_[L:, :L], 0.0), gm)
    dp = each(lambda a_: jnp.where(blk, a_, 0.0), a_ab)
    aoff = each(lambda a_: jnp.where(blk, 0.0, a_), a_ab)
    md = each(lambda d_: eye_f + d_, dp)
    for _ in range(3):
        dp = each(lambda d_: mm.nn(d_, d_), dp)
        md = each(lambda m_, d_: m_ + mm.nn(m_, d_), md, dp)
    nm = each(mm.nn, md, aoff)
    nm2 = each(lambda n_: mm.nn(n_, n_), nm)
    x = each(lambda m_, a_, g_: mm.nn(m_, jnp.concatenate([a_, g_[:L]], axis=1)), md, at, gkv)
    x = each(lambda n_, x_: x_ + mm.nn(n_, x_), nm2, x)
    x = each(lambda n_, x_: x_ + mm.nn(n_, x_), nm, x)
    gx = each(mm.nn, g_b, x)
    xtb = each(mm.tn, x, bh)
    vtk = each(mm.tn, vv, kh)
    for i, (ch, h) in enumerate(chains):
        pre_ref[ch, h, 0] = rt[i] + gx[i][:, :HEAD_DIM]
        pre_ref[ch, h, 1] = gx[i][:, HEAD_DIM:] + gkv[i][L:]
        pre_ref[ch, h, 2] = jnp.where(eye, jnp.broadcast_to(dend[i], (L, L)), 0.0) + xtb[i][:HEAD_DIM]
        pre_ref[ch, h, 3] = xtb[i][HEAD_DIM:] + vtk[i]

    st_nt = _dot_nt_hi if precise else (lambda a_, b_: _dg3(a_, b_, ((1,), (1,))))
    st_nn = _dot_hi if precise else (lambda a_, b_: _dg3(a_, b_, ((1,), (0,))))
    seqs = [(bi, h) for bi in range(nb) for h in range(N_HEADS)]
    s = [s_ref[bi, h] for bi, h in seqs]
    for ci_ in range(nc):
        for i, (bi, h) in enumerate(seqs):
            ch = bi * nc + ci_
            ybuf_ref[ch * L:(ch + 1) * L, h * HEAD_DIM:(h + 1) * HEAD_DIM] = (
                st_nt(pre_ref[ch, h, 0], s[i]) + pre_ref[ch, h, 1])
        s = [st_nn(s[i], pre_ref[bi * nc + ci_, h, 2]) + pre_ref[bi * nc + ci_, h, 3]
             for i, (bi, h) in enumerate(seqs)]
    for i, (bi, h) in enumerate(seqs):
        s_ref[bi, h] = s[i]

    yb = ybuf_ref[...]
    mu = segsum(yb)
    yc = yb - mu
    y = yc * lax.rsqrt(segsum(yc * yc) + RWKV_GN_EPS) * lnx_ref[...]
    y_ref[...] = ((y + bonus) * g).astype(y_ref.dtype).reshape(nb, rows, GROUP_W)
    so_ref[...] = s_ref[...]
    sho_ref[...] = z3[:, rows - 1:rows, :]
    zext_ref[:, 7:8, :] = z3[:, rows - 1:rows, :]


def _rwkv_mixer(z, shift, s0, prm, seg, tril, precise):
    b, t, _ = z.shape
    L = CHUNK
    nc = min(t // L, 4)
    nb = 1 if nc > 1 else min(b, 4)
    rows = nc * L
    full = lambda a: pl.BlockSpec(a.shape, lambda i, j: (0,) * a.ndim)
    st_spec = pl.BlockSpec((nb, N_HEADS, HEAD_DIM, HEAD_DIM), lambda i, j: (i, 0, 0, 0))
    sh_spec = pl.BlockSpec((nb, 1, RWKV_COLS), lambda i, j: (i, 0, 0))
    consts = list(prm) + [seg.astype(BF16), tril.astype(BF16)]
    return pl.pallas_call(
        functools.partial(_rwkv_kernel, precise=precise, nb=nb, nc=nc),
        grid=(b // nb, t // rows),
        in_specs=[pl.BlockSpec((nb, rows, RWKV_COLS), lambda i, j: (i, j, 0)), sh_spec, st_spec]
                 + [full(a) for a in consts],
        out_specs=[pl.BlockSpec((nb, rows, GROUP_W), lambda i, j: (i, j, 0)), st_spec, sh_spec],
        out_shape=[jax.ShapeDtypeStruct((b, t, GROUP_W), F32 if precise else BF16),
                   jax.ShapeDtypeStruct((b, N_HEADS, HEAD_DIM, HEAD_DIM), F32),
                   jax.ShapeDtypeStruct((b, 1, RWKV_COLS), F32)],
        scratch_shapes=[pltpu.VMEM((nb, 8 + rows, RWKV_COLS), F32),
                        pltpu.VMEM((nb, N_HEADS, HEAD_DIM, HEAD_DIM), F32),
                        pltpu.VMEM((nb * rows, GROUP_W), F32),
                        pltpu.VMEM((nb * nc, N_HEADS, 4, HEAD_DIM, HEAD_DIM), F32)],
        compiler_params=_cp(("parallel", "arbitrary")),
        name="rwkv_mixer",
    )(z, shift, s0, *consts)


def _ret_kernel(z_ref, cos_ref, sin_ref, dm_ref, kdec_ref, qdec_ref, cdec_ref, gn_ref, seg_ref, s0_ref,
                y_ref, so_ref, s_ref, obuf_ref, *, precise, nb, nc):
    mm = _MM(precise)
    j = pl.program_id(1)
    L = CHUNK
    rows = nc * L

    @pl.when(j == 0)
    def _():
        s_ref[...] = s0_ref[...]

    c = GROUP_W
    lane = lax.broadcasted_iota(jnp.int32, (rows, c), 1)
    first_half = (lane % HEAD_DIM) < (HEAD_DIM // 2)
    cs, sn = cos_ref[...], sin_ref[...]
    kdec, qdec = kdec_ref[...], qdec_ref[...]

    def rope(x):
        partner = jnp.where(first_half, pltpu.roll(x, c - HEAD_DIM // 2, 1), pltpu.roll(x, HEAD_DIM // 2, 1))
        return x * cs + partner * sn

    each = lambda f, *lists: [f(*vals) for vals in zip(*lists)]
    keys = [(bi, ci, h) for bi in range(nb) for ci in range(nc) for h in range(N_HEADS)]
    qs, ks, vs, kds, qds, g_all = [], [], [], [], [], []
    for bi in range(nb):
        q = rope(z_ref[bi, :, 0:c])
        k = rope(z_ref[bi, :, c:2 * c]) * HEAD_DIM ** -0.5
        v = z_ref[bi, :, 2 * c:3 * c]
        g_all.append(z_ref[bi, :, 3 * c:4 * c])
        for ci in range(nc):
            rs = slice(ci * L, (ci + 1) * L)
            kd, qd = k[rs] * kdec, q[rs] * qdec
            for h in range(N_HEADS):
                hs = slice(h * HEAD_DIM, (h + 1) * HEAD_DIM)
                for lst, val in ((qs, q[rs]), (ks, k[rs]), (vs, v[rs]), (kds, kd), (qds, qd)):
                    lst.append(val[:, hs])
    att = [mm.nt(q_, k_) * dm_ref[h] for q_, k_, (_, _, h) in zip(qs, ks, keys)]
    o_in = each(mm.nn, att, vs)
    kv = each(mm.tn, kds, vs)
    s_prev = {}
    for bi in range(nb):
        for h in range(N_HEADS):
            s = s_ref[bi, h]
            for ci in range(nc):
                s_prev[(bi, ci, h)] = s
                s = s * cdec_ref[h] + kv[keys.index((bi, ci, h))]
            s_ref[bi, h] = s
    o_x = [mm.nn(qd_, s_prev[key]) for qd_, key in zip(qds, keys)]
    for i, (bi, ci, h) in enumerate(keys):
        r0 = bi * rows + ci * L
        obuf_ref[r0:r0 + L, h * HEAD_DIM:(h + 1) * HEAD_DIM] = o_in[i] + o_x[i]
    seg = seg_ref[...]
    segsum = lambda x: _dot_pieces(x, seg, 2 if precise else 1, lhs_exact=False)
    ob = obuf_ref[...]
    oc = ob - segsum(ob)
    o = oc * lax.rsqrt(segsum(oc * oc) + RET_GN_EPS) * gn_ref[...]
    g = jnp.concatenate(g_all, axis=0) if nb > 1 else g_all[0]
    y_ref[...] = (g * _sigmoid(g) * o).astype(y_ref.dtype).reshape(nb, rows, c)
    so_ref[...] = s_ref[...]


def _ret_mixer(z, s0, cos, sin, dmask, kdec, qdec, cdec, gn, seg, precise):
    b, t, _ = z.shape
    L = CHUNK
    nc = min(t // L, 4)
    nb = 1 if nc > 1 else min(b, 4)
    rows = nc * L
    full = lambda a: pl.BlockSpec(a.shape, lambda i, j: (0,) * a.ndim)
    st_spec = pl.BlockSpec((nb, N_HEADS, HEAD_DIM, HEAD_DIM), lambda i, j: (i, 0, 0, 0))
    tab_spec = pl.BlockSpec((rows, GROUP_W), lambda i, j: (j, 0))
    seg = seg.astype(BF16)
    return pl.pallas_call(
        functools.partial(_ret_kernel, precise=precise, nb=nb, nc=nc),
        grid=(b // nb, t // rows),
        in_specs=[pl.BlockSpec((nb, rows, 4 * GROUP_W), lambda i, j: (i, j, 0)), tab_spec, tab_spec,
                  full(dmask), full(kdec), full(qdec), full(cdec), full(gn), full(seg), st_spec],
        out_specs=[pl.BlockSpec((nb, rows, GROUP_W), lambda i, j: (i, j, 0)), st_spec],
        out_shape=[jax.ShapeDtypeStruct((b, t, GROUP_W), F32 if precise else BF16),
                   jax.ShapeDtypeStruct((b, N_HEADS, HEAD_DIM, HEAD_DIM), F32)],
        scratch_shapes=[pltpu.VMEM((nb, N_HEADS, HEAD_DIM, HEAD_DIM), F32),
                        pltpu.VMEM((nb * rows, GROUP_W), F32)],
        compiler_params=_cp(("parallel", "arbitrary")),
        name="ret_mixer",
    )(z, cos, sin, dmask, kdec, qdec, cdec, gn, seg, s0)


def _cumsum_lanes(x_ref, o_ref, tri, carry, width):
    blk = 128
    for c0 in range(0, width, blk):
        wd = min(blk, width - c0)
        cs = _dot_hi(x_ref[:, c0:c0 + wd], tri[:wd, :wd]) + carry
        o_ref[:, c0:c0 + wd] = cs
        carry = cs[:, wd - 1:wd]
    return carry


def _fox_gate_kernel(*refs, past, t):
    if past:
        zf_ref, b_ref, tri_ref, lfp_ref, lf_ref, cn_ref, cp_ref = refs
    else:
        zf_ref, b_ref, tri_ref, lf_ref, cn_ref = refs
    tri = tri_ref[...]
    x = zf_ref[...] + b_ref[...]
    lf_ref[...] = jnp.minimum(x, 0.0) - jnp.log(1.0 + jnp.exp(-jnp.abs(x)))
    carry = jnp.zeros((zf_ref.shape[0], 1), F32)
    if past:
        carry = _cumsum_lanes(lfp_ref, cp_ref, tri, carry, past)
    _cumsum_lanes(lf_ref, cn_ref, tri, carry, t)


def _fox_gate(zf_t, bias_rows, tri, lf_past_t):
    rows, t = zf_t.shape
    rt = min(rows, 32)
    past = 0 if lf_past_t is None else lf_past_t.shape[1]
    row_spec = lambda wd: pl.BlockSpec((rt, wd), lambda i: (i, 0))
    in_specs = [row_spec(t), row_spec(1), pl.BlockSpec(tri.shape, lambda i: (0, 0))]
    out_specs = [row_spec(t), row_spec(t)]
    out_shape = [jax.ShapeDtypeStruct((rows, t), F32), jax.ShapeDtypeStruct((rows, t), F32)]
    args = [zf_t, bias_rows, tri]
    if past:
        in_specs.append(row_spec(past))
        out_specs.append(row_spec(past))
        out_shape.append(jax.ShapeDtypeStruct((rows, past), F32))
        args.append(lf_past_t)
    return pl.pallas_call(
        functools.partial(_fox_gate_kernel, past=past, t=t),
        grid=(rows // rt,),
        in_specs=in_specs, out_specs=out_specs, out_shape=out_shape,
        compiler_params=_cp(("parallel",)),
        name="fox_gate",
    )(*args)


def _per_head_lanes(cols, tq):
    lane_head = lax.broadcasted_iota(jnp.int32, (tq, GROUP_W), 1) // HEAD_DIM
    out = jnp.broadcast_to(cols[N_HEADS - 1], (tq, GROUP_W))
    for h in range(N_HEADS - 2, -1, -1):
        out = jnp.where(lane_head == h, cols[h], out)
    return out


def _fox_attn_tile(q_ref, kn_ref, vn_ref, ckn_ref, m_ref, l_ref, acc_ref, *, i, j, tq, mm):
    heads = range(N_HEADS)
    per_head_lanes = lambda cols: _per_head_lanes(cols, tq)

    def block(k_ref, v_ref, ck_ref, diagonal, mm):
        q = q_ref[...] * (HEAD_DIM ** -0.5 * LOG2E)
        k = k_ref[...]
        v = v_ref[...]
        tk = k.shape[0]
        ck = ck_ref[...] * LOG2E
        lane_head = lax.broadcasted_iota(jnp.int32, (tk, GROUP_W), 1) // HEAD_DIM
        kbd = jnp.concatenate([jnp.where(lane_head == h, k, 0.0) for h in heads], axis=0)
        vbd = jnp.concatenate([jnp.where(lane_head == h, v, 0.0) for h in heads], axis=0)
        s_all = mm.nt(q, kbd)
        s = [s_all[:, h * tk:(h + 1) * tk] - ck[h:h + 1, :] for h in heads]
        if diagonal:
            keep = (lax.broadcasted_iota(jnp.int32, (tq, tk), 1)
                    <= lax.broadcasted_iota(jnp.int32, (tq, tk), 0))
            s = [jnp.where(keep, s_h, -jnp.inf) for s_h in s]
        m_prev = [m_ref[h] for h in heads]
        m_new = [jnp.maximum(m_prev[h], jnp.max(s[h], axis=1, keepdims=True)) for h in heads]
        alpha = [jnp.exp2(m_prev[h] - m_new[h]) for h in heads]
        p = [jnp.exp2(s[h] - m_new[h]) for h in heads]
        pv = mm.nn(jnp.concatenate(p, axis=1), vbd)
        for h in heads:
            l_ref[h] = alpha[h] * l_ref[h] + jnp.sum(p[h], axis=1, keepdims=True)
            m_ref[h] = m_new[h]
        acc_ref[...] = per_head_lanes(alpha) * acc_ref[...] + pv

    @pl.when(j < i)
    def _():
        block(kn_ref, vn_ref, ckn_ref, False, mm)

    @pl.when(j == i)
    def _():
        block(kn_ref, vn_ref, ckn_ref, True, mm)


def _fox_attn_kernel(q_ref, kn_ref, vn_ref, og_ref, ckn_ref, y_ref, m_ref, l_ref, acc_ref, *,
                     tq, precise_from):
    i, j = pl.program_id(1), pl.program_id(2)
    nk = pl.num_programs(2)
    refs = (q_ref, kn_ref, vn_ref, ckn_ref, m_ref, l_ref, acc_ref)

    @pl.when(j == 0)
    def _():
        m_ref[...] = jnp.full(m_ref.shape, -jnp.inf, F32)
        l_ref[...] = jnp.zeros(l_ref.shape, F32)
        acc_ref[...] = jnp.zeros(acc_ref.shape, F32)

    if precise_from is None:
        _fox_attn_tile(*refs, i=i, j=j, tq=tq, mm=_MM(False))
    elif precise_from <= 0:
        _fox_attn_tile(*refs, i=i, j=j, tq=tq, mm=_MM(True))
    else:
        @pl.when(i >= precise_from)
        def _():
            _fox_attn_tile(*refs, i=i, j=j, tq=tq, mm=_MM(True))

        @pl.when(i < precise_from)
        def _():
            _fox_attn_tile(*refs, i=i, j=j, tq=tq, mm=_MM(False))

    @pl.when(j == nk - 1)
    def _():
        o = acc_ref[...] / _per_head_lanes([l_ref[h] for h in range(N_HEADS)], tq)
        y_ref[...] = (_sigmoid(og_ref[...]) * o).astype(y_ref.dtype)


def _fox_attn(zfox, ck_new, precise, precise_from):
    b, t, _ = zfox.shape
    tq = min(t, ATTN_TILE)
    nq = t // tq
    jc = lambda i, j: jnp.minimum(j, i)
    in_specs = [pl.BlockSpec((None, tq, GROUP_W), lambda bb, i, j: (bb, i, 0)),
                pl.BlockSpec((None, tq, GROUP_W), lambda bb, i, j: (bb, jc(i, j), 1)),
                pl.BlockSpec((None, tq, GROUP_W), lambda bb, i, j: (bb, jc(i, j), 2)),
                pl.BlockSpec((None, tq, GROUP_W), lambda bb, i, j: (bb, i, 3)),
                pl.BlockSpec((None, N_HEADS, tq), lambda bb, i, j: (bb, 0, jc(i, j)))]
    return pl.pallas_call(
        functools.partial(_fox_attn_kernel, tq=tq, precise_from=precise_from),
        grid=(b, nq, nq),
        in_specs=in_specs,
        out_specs=pl.BlockSpec((None, tq, GROUP_W), lambda bb, i, j: (bb, i, 0)),
        out_shape=jax.ShapeDtypeStruct((b, t, GROUP_W), F32 if precise else BF16),
        scratch_shapes=[pltpu.VMEM((N_HEADS, tq, 1), F32), pltpu.VMEM((N_HEADS, tq, 1), F32),
                        pltpu.VMEM((tq, GROUP_W), F32)],
        compiler_params=_cp(("parallel", "parallel", "arbitrary")),
        name="fox_attn",
    )(zfox, zfox, zfox, zfox, ck_new)


def _fox_past_kernel(zf_ref, ckn_ref, kt_ref, vt_ref, ckp_ref, y_ref, m_ref, l_ref, acc_ref, *,
                     nkp, nb, precise):
    mm = _MM(precise)
    j = pl.program_id(1)
    t = zf_ref.shape[1]
    c = GROUP_W
    chains = [(bi, h) for bi in range(nb) for h in range(N_HEADS)]
    hs = lambda h: slice(h * HEAD_DIM, (h + 1) * HEAD_DIM)

    @pl.when(j == 0)
    def _():
        m_ref[...] = jnp.full(m_ref.shape, -jnp.inf, F32)
        l_ref[...] = jnp.zeros(l_ref.shape, F32)
        acc_ref[...] = jnp.zeros(acc_ref.shape, F32)

    def online_update(s, pv_of):
        m_prev = [m_ref[i] for i in range(len(chains))]
        m_new = [jnp.maximum(mp, jnp.max(s_i, axis=1, keepdims=True)) for mp, s_i in zip(m_prev, s)]
        alpha = [jnp.exp2(mp - mn) for mp, mn in zip(m_prev, m_new)]
        p = [jnp.exp2(s_i - mn) for s_i, mn in zip(s, m_new)]
        pv = pv_of(p)
        for i in range(len(chains)):
            l_ref[i] = alpha[i] * l_ref[i] + jnp.sum(p[i], axis=1, keepdims=True)
            acc_ref[i] = alpha[i] * acc_ref[i] + pv[i]
            m_ref[i] = m_new[i]

    def queries():
        return [zf_ref[bi, :, hs(h)] * (HEAD_DIM ** -0.5 * LOG2E) for bi, h in chains]

    @pl.when(j < nkp)
    def _():
        q = queries()
        s = [mm.nn(q[i], kt_ref[bi, h]) - ckp_ref[bi, h:h + 1, :] * LOG2E
             for i, (bi, h) in enumerate(chains)]
        online_update(s, lambda p: [mm.nt(p[i], vt_ref[bi, h]) for i, (bi, h) in enumerate(chains)])

    @pl.when(j == nkp)
    def _():
        q = queries()
        keep = lax.broadcasted_iota(jnp.int32, (t, t), 1) <= lax.broadcasted_iota(jnp.int32, (t, t), 0)
        s = [jnp.where(keep, mm.nt(q[i], zf_ref[bi, :, c + h * HEAD_DIM:c + (h + 1) * HEAD_DIM])
                       - ckn_ref[bi, h:h + 1, :] * LOG2E, -jnp.inf)
             for i, (bi, h) in enumerate(chains)]
        online_update(s, lambda p: [mm.nn(p[i], zf_ref[bi, :, 2 * c + h * HEAD_DIM:2 * c + (h + 1) * HEAD_DIM])
                                    for i, (bi, h) in enumerate(chains)])
        for i, (bi, h) in enumerate(chains):
            og = zf_ref[bi, :, 3 * c + h * HEAD_DIM:3 * c + (h + 1) * HEAD_DIM]
            y_ref[bi, :, hs(h)] = (_sigmoid(og) * (acc_ref[i] / l_ref[i])).astype(y_ref.dtype)


def _fox_attn_past(zfox, ck_new, kt_all, vt_all, ck_past, layer, precise):
    b, t, _ = zfox.shape
    p = kt_all.shape[-1]
    tkp = 512
    nkp = p // tkp
    nb = min(b, 4)
    jp = lambda j: jnp.minimum(j, nkp - 1)
    cache_spec = pl.BlockSpec((None, nb, N_HEADS, HEAD_DIM, tkp), lambda i, j: (layer, i, 0, 0, jp(j)))
    return pl.pallas_call(
        functools.partial(_fox_past_kernel, nkp=nkp, nb=nb, precise=precise),
        grid=(b // nb, nkp + 1),
        in_specs=[pl.BlockSpec((nb, t, 4 * GROUP_W), lambda i, j: (i, 0, 0)),
                  pl.BlockSpec((nb, N_HEADS, t), lambda i, j: (i, 0, 0)),
                  cache_spec, cache_spec,
                  pl.BlockSpec((nb, N_HEADS, tkp), lambda i, j: (i, 0, jp(j)))],
        out_specs=pl.BlockSpec((nb, t, GROUP_W), lambda i, j: (i, 0, 0)),
        out_shape=jax.ShapeDtypeStruct((b, t, GROUP_W), F32 if precise else BF16),
        scratch_shapes=[pltpu.VMEM((nb * N_HEADS, t, 1), F32), pltpu.VMEM((nb * N_HEADS, t, 1), F32),
                        pltpu.VMEM((nb * N_HEADS, t, HEAD_DIM), F32)],
        compiler_params=_cp(("parallel", "arbitrary")),
        name="fox_attn_past",
    )(zfox, ck_new, kt_all, vt_all, ck_past)


def _out_kernel(x_ref, ya_ref, yb_ref, yc_ref, yd_ref, *refs, precise):
    w_refs, o_ref = refs[:2 if precise else 1], refs[-1]
    dot = lambda a, b: jnp.dot(a, b, preferred_element_type=F32)
    acc = x_ref[...]
    for i, y_ref in enumerate((ya_ref, yb_ref, yc_ref, yd_ref)):
        if precise:
            yh, yl = _split(y_ref[...])
            acc = acc + (dot(yh, w_refs[0][i]) + (dot(yh, w_refs[1][i]) + dot(yl, w_refs[0][i])))
        else:
            acc = acc + dot(y_ref[...].astype(BF16), w_refs[0][i])
    o_ref[...] = acc


def _out_proj(x, ys, ws, into=None, seqs=None):
    n = x.shape[0]
    tm = 1024 if into is None else PRECISE_TAIL
    steps, rows = (n // tm, lambda i: (i, 0)) if into is None else _tail_blocks(n, seqs, tm)
    alias = [] if into is None else [into]
    y_spec = pl.BlockSpec((tm, GROUP_W), rows)
    x_spec = pl.BlockSpec((tm, D_MODEL), rows)
    return pl.pallas_call(
        functools.partial(_out_kernel, precise=len(ws) == 2),
        grid=(steps,),
        in_specs=[x_spec, y_spec, y_spec, y_spec, y_spec]
                 + [pl.BlockSpec(w.shape, lambda i: (0, 0, 0)) for w in ws]
                 + [pl.BlockSpec(memory_space=pl.ANY) for _ in alias],
        out_specs=x_spec,
        out_shape=jax.ShapeDtypeStruct((n, D_MODEL), F32),
        input_output_aliases={5 + len(ws): 0} if alias else {},
        compiler_params=_cp(("parallel",)),
        name="out_proj",
    )(x, *ys, *ws, *alias)


def _router_kernel(x_ref, g_ref, w_ref, b_ref, tril_ref, xs_ref, gs_ref, dest_ref, meta_ref):
    h = _rmsnorm(x_ref[...], g_ref[...])
    logits = _dot_hi(h, w_ref[...]) + b_ref[...]
    tm = logits.shape[0]
    lane_i = lax.broadcasted_iota(jnp.int32, (tm, ROUTER_PAD), 1)
    lane = lane_i.astype(F32)
    big = float(ROUTER_PAD)
    rmax = lambda a: jnp.max(a, axis=1, keepdims=True)
    rsum = lambda a: jnp.sum(a, axis=1, keepdims=True)
    first = lambda m: jnp.min(jnp.where(m, lane, big), axis=1, keepdims=True)

    is_c = jnp.logical_and(lane_i >= N_EXPERTS, lane_i < N_EXPERTS + N_GROUPS)
    lc = jnp.where(is_c, logits, -jnp.inf)
    ec = jnp.exp(lc - rmax(lc))
    pc_all = ec / rsum(ec)
    pc = rmax(pc_all)
    gi = first(jnp.logical_and(is_c, pc_all == pc)) - float(N_EXPERTS)

    is_f = (lane_i // EXPERTS_PER_GROUP).astype(F32) == gi
    lf = jnp.where(is_f, logits, -jnp.inf)
    ef = jnp.exp(lf - rmax(lf))
    pf = jnp.where(is_f, ef / rsum(ef), -1.0)
    t1 = rmax(pf)
    i1 = first(pf == t1)
    pf2 = jnp.where(lane == i1, -1.0, pf)
    t2 = rmax(pf2)
    i2 = first(pf2 == t2)
    den = t1 + t2
    gate = jnp.where(lane == i1, pc * (t1 / den), jnp.where(lane == i2, pc * (t2 / den), 0.0))

    onehot = jnp.where(lane == gi, 1.0, 0.0)
    rank = jnp.dot(tril_ref[...], onehot.astype(BF16), preferred_element_type=F32)
    cnt = jnp.sum(onehot, axis=0, keepdims=True)
    padded = jnp.floor((cnt + (MOE_PAD - 1.0)) * (1.0 / MOE_PAD)) * MOE_PAD
    lane1 = lax.broadcasted_iota(jnp.int32, (1, ROUTER_PAD), 1)
    off = jnp.zeros((1, ROUTER_PAD), F32)
    meta = jnp.zeros((1, ROUTER_PAD), F32)
    start = jnp.zeros((1, 1), F32)
    for g in range(N_GROUPS):
        size_g = jnp.sum(jnp.where(lane1 == g, padded, 0.0), axis=1, keepdims=True)
        off = off + jnp.where(lane1 == g, start, 0.0)
        meta = meta + jnp.where(lane1 == g, start, 0.0) + jnp.where(lane1 == N_GROUPS + g, size_g, 0.0)
        start = start + size_g
    dest = rsum(onehot * (off + rank))
    rows = lax.broadcasted_iota(jnp.int32, (tm, MOE_ROWS), 1).astype(F32)
    perm_t = jnp.where(rows == dest, 1.0, 0.0).astype(BF16)
    rhs = jnp.concatenate([h.astype(BF16)] + _pieces(gate, 3), axis=1)
    srt = lax.dot_general(perm_t, rhs, (((0,), (0,)), ((), ())), preferred_element_type=F32)
    xs_ref[...] = srt[:, :D_MODEL].astype(BF16)
    gs_ref[...] = (srt[:, D_MODEL:D_MODEL + ROUTER_PAD] + srt[:, D_MODEL + ROUTER_PAD:D_MODEL + 2 * ROUTER_PAD]
                   + srt[:, D_MODEL + 2 * ROUTER_PAD:])
    dest_ref[...] = dest
    meta_ref[...] = meta


def _router(x, g, wr, br, tril):
    n = x.shape[0]
    tm = MOE_TILE
    nt = n // tm
    return pl.pallas_call(
        _router_kernel,
        grid=(nt,),
        in_specs=[pl.BlockSpec((tm, D_MODEL), lambda i: (i, 0)),
                  pl.BlockSpec((1, D_MODEL), lambda i: (0, 0)),
                  pl.BlockSpec((D_MODEL, ROUTER_PAD), lambda i: (0, 0)),
                  pl.BlockSpec((1, ROUTER_PAD), lambda i: (0, 0)),
                  pl.BlockSpec((tm, tm), lambda i: (0, 0))],
        out_specs=[pl.BlockSpec((MOE_ROWS, D_MODEL), lambda i: (i, 0)),
                   pl.BlockSpec((MOE_ROWS, ROUTER_PAD), lambda i: (i, 0)),
                   pl.BlockSpec((tm, 1), lambda i: (i, 0)),
                   pl.BlockSpec((None, 1, ROUTER_PAD), lambda i: (i, 0, 0))],
        out_shape=[jax.ShapeDtypeStruct((nt * MOE_ROWS, D_MODEL), BF16),
                   jax.ShapeDtypeStruct((nt * MOE_ROWS, ROUTER_PAD), F32),
                   jax.ShapeDtypeStruct((n, 1), F32),
                   jax.ShapeDtypeStruct((nt, 1, ROUTER_PAD), F32)],
        compiler_params=_cp(("parallel",)),
        name="moe_router",
    )(x, g, wr, br, tril)


def _moe_kernel(meta_ref, xs_ref, gs_ref, w1_ref, w3_ref, w2_ref, o_ref, *, tps):
    i, e = pl.program_id(0), pl.program_id(1)

    @pl.when(e == 0)
    def _():
        o_ref[...] = jnp.zeros(o_ref.shape, F32)

    g = e // EXPERTS_PER_GROUP
    for sub in range(tps):
        tile = i * tps + sub
        off = meta_ref[tile, g] + sub * MOE_ROWS
        size = meta_ref[tile, N_GROUPS + g]
        for w in range(MOE_TILE // MOE_WIN + 1):
            @pl.when(w * MOE_WIN < size)
            def _():
                rows = pl.ds(pl.multiple_of(off + w * MOE_WIN, MOE_PAD), MOE_WIN)
                xw = xs_ref[rows, :]
                a = jnp.dot(xw, w1_ref[...], preferred_element_type=F32)
                b = jnp.dot(xw, w3_ref[...], preferred_element_type=F32)
                he = (a * _sigmoid(a) * b).astype(BF16)
                ye = jnp.dot(he, w2_ref[...], preferred_element_type=F32)
                gw = gs_ref[rows, :]
                lane = lax.broadcasted_iota(jnp.int32, gw.shape, 1)
                gcol = jnp.sum(jnp.where(lane == e, gw, 0.0), axis=1, keepdims=True)
                o_ref[rows, :] += gcol * ye


def _moe(meta, xs, gs, w1, w3, w2, layer):
    nt = meta.shape[0]
    tps = MOE_TILES_PER_STEP if nt % MOE_TILES_PER_STEP == 0 else 1
    rows = tps * MOE_ROWS
    return pl.pallas_call(
        functools.partial(_moe_kernel, tps=tps),
        grid_spec=pltpu.PrefetchScalarGridSpec(
            num_scalar_prefetch=1,
            grid=(nt // tps, N_EXPERTS),
            in_specs=[pl.BlockSpec((rows, D_MODEL), lambda i, e, m: (i, 0)),
                      pl.BlockSpec((rows, ROUTER_PAD), lambda i, e, m: (i, 0)),
                      pl.BlockSpec((None, None, D_MODEL, D_EXPERT), lambda i, e, m: (layer, e, 0, 0)),
                      pl.BlockSpec((None, None, D_MODEL, D_EXPERT), lambda i, e, m: (layer, e, 0, 0)),
                      pl.BlockSpec((None, None, D_EXPERT, D_MODEL), lambda i, e, m: (layer, e, 0, 0))],
            out_specs=pl.BlockSpec((rows, D_MODEL), lambda i, e, m: (i, 0))),
        out_shape=jax.ShapeDtypeStruct((nt * MOE_ROWS, D_MODEL), F32),
        compiler_params=pltpu.CompilerParams(dimension_semantics=("parallel", "arbitrary"),
                                             vmem_limit_bytes=MOE_VMEM_LIMIT),
        name="moe_experts",
    )(meta, xs, gs, w1, w3, w2)


def _combine_kernel(x_ref, os_ref, dest_ref, nf_ref, o_ref, *, final):
    tm = x_ref.shape[0]
    rows = lax.broadcasted_iota(jnp.int32, (tm, MOE_ROWS), 1).astype(F32)
    perm_t = jnp.where(rows == dest_ref[...], 1.0, 0.0).astype(BF16)
    y = x_ref[...] + _dot_pieces(perm_t, os_ref[...], 2, lhs_exact=True)
    o_ref[...] = _rmsnorm(y, nf_ref[...]) if final else y


def _moe_combine(x, os, dest, nf, final):
    n = x.shape[0]
    tm = MOE_TILE
    return pl.pallas_call(
        functools.partial(_combine_kernel, final=final),
        grid=(n // tm,),
        in_specs=[pl.BlockSpec((tm, D_MODEL), lambda i: (i, 0)),
                  pl.BlockSpec((MOE_ROWS, D_MODEL), lambda i: (i, 0)),
                  pl.BlockSpec((tm, 1), lambda i: (i, 0)),
                  pl.BlockSpec((1, D_MODEL), lambda i: (0, 0))],
        out_specs=pl.BlockSpec((tm, D_MODEL), lambda i: (i, 0)),
        out_shape=jax.ShapeDtypeStruct((n, D_MODEL), F32),
        compiler_params=_cp(("parallel",)),
        name="moe_combine",
    )(x, os, dest, nf)


def _prep_layer(l, p):
    w_in = jnp.transpose(p["w_in"][l], (1, 0))
    o_rwkv, o_ret, o_fox = GROUP_W, GROUP_W + RWKV_COLS, GROUP_W + RWKV_COLS + 4 * GROUP_W
    ff = jnp.pad(w_in[o_fox + 4 * GROUP_W:], ((0, FF_PAD - N_HEADS), (0, 0)))
    w_in_r = jnp.concatenate([w_in[o_rwkv:o_ret], w_in[o_ret:o_fox],
                              w_in[o_fox:o_fox + 4 * GROUP_W], w_in[:GROUP_W], ff], axis=0)
    pw = p["pool_w"][l]
    pg = GROUP_W // len(POOL_WINDOWS)
    w_bd = jnp.zeros((GROUP_W, GROUP_W), F32)
    for gi in range(len(POOL_WINDOWS)):
        w_bd = w_bd.at[gi * pg:(gi + 1) * pg, gi * pg:(gi + 1) * pg].set(pw[gi])
    row = lambda a: a.reshape(1, -1)
    rwkv = (row(p["rwkv_mu"][l]), row(p["rwkv_w0"][l]), p["rwkv_w2"][l], row(p["rwkv_a0"][l]),
            p["rwkv_a2"][l], p["rwkv_g2"][l], row(p["rwkv_kk"][l]), row(p["rwkv_ka"][l]),
            row(p["rwkv_rk"][l]), row(p["rwkv_lnx"][l]))
    wf = jnp.transpose(p["moe_wf"][l], (1, 0, 2)).reshape(D_MODEL, N_EXPERTS)
    wr = jnp.pad(jnp.concatenate([wf, p["moe_wc"][l]], axis=1),
                 ((0, 0), (0, ROUTER_PAD - N_EXPERTS - N_GROUPS)))
    br = jnp.pad(jnp.concatenate([p["moe_bf"][l].reshape(-1), p["moe_bc"][l]]),
                 (0, ROUTER_PAD - N_EXPERTS - N_GROUPS)).reshape(1, ROUTER_PAD)
    w_out4 = p["w_out"][l].reshape(4, GROUP_W, D_MODEL)
    precise = l < p["w_in"].shape[0] - 1
    split = lambda w: _split_bits(w) if precise else (w.astype(BF16),)
    return dict(
        precise=precise, n1=row(p["norm1_g"][l]), w_in=split(w_in_r), pool_w=w_bd,
        w_in_single=(w_in_r.astype(BF16),), w_out_single=(w_out4.astype(BF16),),
        pool_scale=row(p["pool_scale"][l]),
        rwkv=rwkv, ret_gn=row(p["ret_gn"][l]), fox_bf=p["fox_bf"][l],
        w_out=split(w_out4), n2=row(p["norm2_g"][l]),
        wr=wr, br=br, layer=l,
        w1=p["moe_w1"].astype(BF16), w3=p["moe_w3"].astype(BF16), w2=p["moe_w2"].astype(BF16))


def _tables(t, pos0):
    half = HEAD_DIM // 2
    inv = ROPE_BASE ** (-np.arange(half, dtype=np.float64) / half)
    ang = (pos0 + np.arange(t, dtype=np.float64))[:, None] * inv[None, :]
    cos, sin = np.cos(ang), np.sin(ang)
    cos_t = np.tile(np.concatenate([cos, cos], axis=1), (1, N_HEADS))
    sin_t = np.tile(np.concatenate([-sin, sin], axis=1), (1, N_HEADS))
    L = CHUNK
    log_g = np.log(np.array(RET_GAMMA, np.float64))
    idx = np.arange(L, dtype=np.float64)
    dmask = np.exp(log_g[:, None, None] * np.abs(idx[:, None] - idx[None, :]))
    lanes = lambda a: np.repeat(a, HEAD_DIM, axis=1)
    kdec = lanes(np.exp(log_g[None, :] * (L - 1.0 - idx)[:, None]))
    qdec = lanes(np.exp(log_g[None, :] * (idx + 1.0)[:, None]))
    cdec = np.broadcast_to(np.exp(log_g * L)[:, None, None], (N_HEADS, HEAD_DIM, HEAD_DIM))
    hid = np.arange(GROUP_W) // HEAD_DIM
    seg = np.where(hid[:, None] == hid[None, :], 1.0 / HEAD_DIM, 0.0)
    tril = np.tril(np.ones((L, L)))
    triu = np.triu(np.ones((128, 128)))
    tril_moe = np.tril(np.ones((MOE_TILE, MOE_TILE)), -1)
    f32 = lambda a: jnp.asarray(np.asarray(a, np.float32))
    return dict(cos=f32(cos_t), sin=f32(sin_t), dmask=f32(dmask), kdec=f32(kdec), qdec=f32(qdec),
                cdec=f32(cdec), seg=f32(seg), tril=f32(tril), triu=f32(triu),
                tril_moe=f32(tril_moe).astype(BF16))


def _trunk(x, pos0, states, layers, norm_f, cache=None):
    b, t, _ = x.shape
    n = b * t
    tb = _tables(t, pos0)
    xf = x.reshape(n, D_MODEL)
    nf = norm_f.reshape(1, D_MODEL)
    outs = [[] for _ in range(7)]
    for l, lp in enumerate(layers):
        pool_buf, shift, wkv, ret, lf_past = states[l]
        precise = lp["precise"]
        tail = precise and t > PRECISE_TAIL and (t - PRECISE_TAIL) % ATTN_TILE == 0
        if tail:
            zs = _in_proj(xf, lp["n1"], lp["w_in_single"])
            z_rwkv, z_ret, z_fox, z_pool, z_ff = _in_proj(xf, lp["n1"], lp["w_in"], into=zs, seqs=b)
        else:
            z_rwkv, z_ret, z_fox, z_pool, z_ff = _in_proj(xf, lp["n1"], lp["w_in"])
        buf16 = jnp.pad(pool_buf, ((0, 0), (POOL_PAD - POOL_BUF, 0), (0, 0)))
        y_a, pool_new = _pool_mixer(z_pool.reshape(b, t, GROUP_W), buf16, lp["pool_w"], lp["pool_scale"], pos0,
                                    precise)
        y_b, wkv_new, shift_new = _rwkv_mixer(z_rwkv.reshape(b, t, RWKV_COLS), shift, wkv, lp["rwkv"],
                                              tb["seg"], tb["tril"], precise)
        y_c, ret_new = _ret_mixer(z_ret.reshape(b, t, 4 * GROUP_W), ret, tb["cos"], tb["sin"], tb["dmask"],
                                  tb["kdec"], tb["qdec"], tb["cdec"], lp["ret_gn"], tb["seg"], precise)
        zf_t = jnp.transpose(z_ff[:, :N_HEADS].reshape(b, t, N_HEADS), (0, 2, 1)).reshape(b * N_HEADS, t)
        bias_rows = jnp.tile(lp["fox_bf"], b).reshape(b * N_HEADS, 1)
        zfox3 = z_fox.reshape(b, t, 4 * GROUP_W)
        if cache is None:
            lf_t, c_new = _fox_gate(zf_t, bias_rows, tb["triu"], None)
            cn3 = c_new.reshape(b, N_HEADS, t)
            first_precise = ((t - PRECISE_TAIL) // ATTN_TILE if tail else 0) if precise else None
            y_d = _fox_attn(zfox3, cn3, precise, first_precise)
        else:
            p = lf_past.shape[1]
            lfp_t = jnp.transpose(lf_past, (0, 2, 1)).reshape(b * N_HEADS, p)
            lf_t, c_new, c_past = _fox_gate(zf_t, bias_rows, tb["triu"], lfp_t)
            cn3 = c_new.reshape(b, N_HEADS, t)
            y_d = _fox_attn_past(zfox3, cn3, cache[0], cache[1], c_past.reshape(b, N_HEADS, p), l, precise)
        ys4 = [y.reshape(n, GROUP_W) for y in (y_a, y_b, y_c, y_d)]
        if tail:
            x1 = _out_proj(xf, ys4, lp["w_out"], into=_out_proj(xf, ys4, lp["w_out_single"]), seqs=b)
        else:
            x1 = _out_proj(xf, ys4, lp["w_out"])
        xs, gs, dest, meta = _router(x1, lp["n2"], lp["wr"], lp["br"], tb["tril_moe"])
        meta = meta[:, 0, :2 * N_GROUPS].astype(jnp.int32)
        ys = _moe(meta, xs, gs, lp["w1"], lp["w3"], lp["w2"], lp["layer"])
        xf = _moe_combine(x1, ys, dest, nf, final=(l == len(layers) - 1))
        k_new = zfox3[:, :, GROUP_W:2 * GROUP_W].reshape(b, t, N_HEADS, HEAD_DIM)
        v_new = zfox3[:, :, 2 * GROUP_W:3 * GROUP_W].reshape(b, t, N_HEADS, HEAD_DIM)
        lf_new = jnp.transpose(lf_t.reshape(b, N_HEADS, t), (0, 2, 1))
        for lst, s in zip(outs, (pool_new, shift_new, wkv_new, ret_new, k_new, v_new, lf_new)):
            lst.append(s)
    return xf.reshape(b, t, D_MODEL), [jnp.stack(lst) for lst in outs]


def kernel(x_prompt, x_sample, state_pool, state_shift, state_wkv, state_ret, cache_fox_k, cache_fox_v, cache_fox_logf, norm1_g, w_in, pool_w, pool_scale, rwkv_mu, rwkv_w0, rwkv_w2, rwkv_a0, rwkv_a2, rwkv_g2, rwkv_kk, rwkv_ka, rwkv_rk, rwkv_lnx, ret_gn, fox_bf, w_out, norm2_g, moe_wc, moe_bc, moe_wf, moe_bf, moe_w1, moe_w3, moe_w2, norm_f):
    p = dict(norm1_g=norm1_g, w_in=w_in, pool_w=pool_w, pool_scale=pool_scale, rwkv_mu=rwkv_mu,
             rwkv_w0=rwkv_w0, rwkv_w2=rwkv_w2, rwkv_a0=rwkv_a0, rwkv_a2=rwkv_a2, rwkv_g2=rwkv_g2,
             rwkv_kk=rwkv_kk, rwkv_ka=rwkv_ka, rwkv_rk=rwkv_rk, rwkv_lnx=rwkv_lnx, ret_gn=ret_gn,
             fox_bf=fox_bf, w_out=w_out, norm2_g=norm2_g, moe_wc=moe_wc, moe_bc=moe_bc, moe_wf=moe_wf,
             moe_bf=moe_bf, moe_w1=moe_w1, moe_w3=moe_w3, moe_w2=moe_w2)
    depth = w_in.shape[0]
    layers = [_prep_layer(l, p) for l in range(depth)]
    b = x_prompt.shape[0]
    dt = x_prompt.dtype
    prompt_init = [(jnp.zeros((b, POOL_BUF, GROUP_W), dt), jnp.zeros((b, 1, RWKV_COLS), dt),
                    jnp.zeros((b, N_HEADS, HEAD_DIM, HEAD_DIM), dt),
                    jnp.zeros((b, N_HEADS, HEAD_DIM, HEAD_DIM), dt), None)
                   for _ in range(depth)]
    sample_init = [(state_pool[l], state_shift[l], state_wkv[l], state_ret[l], cache_fox_logf[l])
                   for l in range(depth)]
    past = cache_fox_k.shape[2]
    cache = (jnp.transpose(cache_fox_k, (0, 1, 3, 4, 2)), jnp.transpose(cache_fox_v, (0, 1, 3, 4, 2)))
    y_prompt, new_p = _trunk(x_prompt, 0, prompt_init, layers, norm_f)
    y_sample, new_s = _trunk(x_sample, past, sample_init, layers, norm_f, cache)
    return (y_prompt, y_sample, *new_p, *new_s)
```
